```python
import math
import jax
import jax.numpy as jnp
from jax import lax
import numpy as np

D_MODEL = 1024
BATCH = 4
SEQ = 8192
DEPTH = 2
DEC_BATCH = 32
DEC_SEQ = 4
PAST_LEN = 16384
PAGE_SIZE = 128

HEAD_DIM = 64
MIX_WIDTH = D_MODEL // 2
NSA_HEADS = MIX_WIDTH // HEAD_DIM
NSA_KV_HEADS = NSA_HEADS // 4
NSA_CMP_BLOCK = 32
NSA_SEL_BLOCK = 64
NSA_TOPN = 16
NSA_WINDOW = 512
FORCE_SCORE = 1e4
LRU_WIDTH = MIX_WIDTH
LRU_BLOCKS = 8
LRU_C = 8.0
CONV_WIDTH = 4
DIFF_HALF = HEAD_DIM
DIFF_HEADS = MIX_WIDTH // (2 * DIFF_HALF)
DIFF_KV_HEADS = DIFF_HEADS // 2
DIFF_LAYER_INDEX = 1
DIFF_LAMBDA_INIT = 0.8 - 0.6 * math.exp(-0.3 * DIFF_LAYER_INDEX)
DSA_HEADS = MIX_WIDTH // HEAD_DIM
DSA_KV_HEADS = DSA_HEADS // 4
IDX_HEADS = 4
IDX_DIM = HEAD_DIM
DSA_TOPK_MAX = 256
Q_BLOCK = 128
NORM_EPS = 1e-6
NEG = -1e30

L0_SIZES = (NSA_HEADS * HEAD_DIM, 6 * NSA_KV_HEADS * HEAD_DIM, 3 * NSA_HEADS, MIX_WIDTH, LRU_WIDTH, LRU_WIDTH)
L1_SIZES = (DIFF_HEADS * 2 * DIFF_HALF, DIFF_KV_HEADS * 2 * DIFF_HALF, DIFF_KV_HEADS * 2 * DIFF_HALF, MIX_WIDTH,
            DSA_HEADS * HEAD_DIM, DSA_KV_HEADS * HEAD_DIM, DSA_KV_HEADS * HEAD_DIM, IDX_HEADS * IDX_DIM, IDX_DIM,
            IDX_HEADS, MIX_WIDTH)

kernel_name = 'hybrid_nsa_rglru_diff_dsa_step'


def _rmsnorm(x, g):
    xf = x.astype(jnp.float32)
    y = xf * lax.rsqrt(jnp.mean(xf * xf, axis=-1, keepdims=True) + NORM_EPS)
    return (y * g.astype(jnp.float32)).astype(x.dtype)


def _split(h, sizes):
    return jnp.split(h, np.cumsum(sizes)[:-1].tolist(), axis=-1)


def _masked_softmax(s, mask):
    s = jnp.where(mask, s.astype(jnp.float32), NEG)
    m = jnp.max(s, axis=-1, keepdims=True)
    p = jnp.exp(s - m) * mask
    return p / jnp.maximum(jnp.sum(p, axis=-1, keepdims=True), 1e-30)


def _map_query_blocks(fn, q_pos, *q_arrays):
    lq = q_pos.shape[0]
    blk = Q_BLOCK if lq % Q_BLOCK == 0 else lq
    nb = lq // blk
    if nb == 1:
        return fn(q_pos, *q_arrays)
    bsz = q_arrays[0].shape[0]
    qp = q_pos.reshape(nb, blk)
    qs = tuple(jnp.moveaxis(a.reshape(bsz, nb, blk, *a.shape[2:]), 1, 0) for a in q_arrays)
    out = lax.map(lambda args: fn(*args), (qp,) + qs)
    return jnp.moveaxis(out, 0, 1).reshape(bsz, lq, *out.shape[3:])


def _modulation(c, w_ada, b_ada):
    shift, scale, gate = jnp.split(c @ w_ada + b_ada, 3, axis=-1)
    return shift[:, None], scale[:, None], gate[:, None]


def _gather_pages(pool, page_table):
    g = pool[page_table]
    return g.reshape(g.shape[0], g.shape[1] * g.shape[2], *g.shape[3:])


def _nsa_attend(q, gates, q_pos, k_cmp, v_cmp, k_sel, v_sel, k_win, v_win, win_pos0, w_cmp_k, w_cmp_v):
    bsz, lk = k_cmp.shape[0], k_cmp.shape[1]
    G = NSA_KV_HEADS
    hpg = NSA_HEADS // G
    scale = HEAD_DIM ** -0.5
    nc = lk // NSA_CMP_BLOCK

    def compress(k, w):
        kb = k[:, :nc * NSA_CMP_BLOCK].reshape(bsz, nc, NSA_CMP_BLOCK, G, HEAD_DIM)
        return jnp.einsum('bnjgd,jd->bngd', kb, w)

    kc = compress(k_cmp, w_cmp_k)
    vc = compress(v_cmp, w_cmp_v)
    cmp_end = (jnp.arange(nc) + 1) * NSA_CMP_BLOCK - 1
    ns = -(-lk // NSA_SEL_BLOCK)
    ratio = NSA_SEL_BLOCK // NSA_CMP_BLOCK
    pad = ns * NSA_SEL_BLOCK - lk

    def sel_blocks(k):
        k = jnp.pad(k, ((0, 0), (0, pad), (0, 0), (0, 0)))
        return jnp.moveaxis(k.reshape(bsz, ns, NSA_SEL_BLOCK, G, HEAD_DIM), 3, 1)

    ksb = sel_blocks(k_sel)
    vsb = sel_blocks(v_sel)
    n_top = min(NSA_TOPN, ns)
    kw = jnp.pad(k_win, ((0, 0), (NSA_WINDOW, 0), (0, 0), (0, 0)))
    vw = jnp.pad(v_win, ((0, 0), (NSA_WINDOW, 0), (0, 0), (0, 0)))
    b_ix = jnp.arange(bsz)[:, None, None, None]
    g_ix = jnp.arange(G)[None, None, :, None]
    blk_ids = jnp.arange(ns)

    def block_fn(qp, qb, gb):
        lb = qp.shape[0]
        qg = qb.reshape(bsz, lb, G, hpg, HEAD_DIM)
        s = jnp.einsum('btghd,bngd->btghn', qg, kc) * scale
        mask_c = (cmp_end[None, :] <= qp[:, None])[None, :, None, None, :]
        p_c = _masked_softmax(s, mask_c)
        o_c = jnp.einsum('btghn,bngd->btghd', p_c.astype(vc.dtype), vc)
        imp = jnp.pad(jnp.sum(p_c, axis=3), ((0, 0), (0, 0), (0, 0), (0, ns * ratio - nc)))
        imp = imp.reshape(bsz, lb, G, ns, ratio).sum(-1)
        cur = qp // NSA_SEL_BLOCK
        forced = (blk_ids[None, :] == cur[:, None]) | (blk_ids[None, :] == 0)
        allowed = blk_ids[None, :] <= cur[:, None]
        imp = jnp.where(forced[None, :, None, :], FORCE_SCORE, imp)
        imp = jnp.where(allowed[None, :, None, :], imp, NEG)
        top_s, top_i = lax.top_k(imp, n_top)
        sel_ok = top_s > 0.5 * NEG
        ks = ksb[b_ix, g_ix, top_i]
        vs = vsb[b_ix, g_ix, top_i]
        kpos = top_i[..., None] * NSA_SEL_BLOCK + jnp.arange(NSA_SEL_BLOCK)
        mask_s = (sel_ok[..., None] & (kpos <= qp[None, :, None, None, None]))
        mask_s = mask_s.reshape(bsz, lb, G, 1, n_top * NSA_SEL_BLOCK)
        s = jnp.einsum('btghd,btgnjd->btghnj', qg, ks).reshape(bsz, lb, G, hpg, n_top * NSA_SEL_BLOCK) * scale
        p_s = _masked_softmax(s, mask_s)
        o_s = jnp.einsum('btghm,btgmd->btghd', p_s.astype(vs.dtype),
                         vs.reshape(bsz, lb, G, n_top * NSA_SEL_BLOCK, HEAD_DIM))
        q0 = qp[0]
        kwb = lax.dynamic_slice_in_dim(kw, q0 - win_pos0, NSA_WINDOW + lb, axis=1)
        vwb = lax.dynamic_slice_in_dim(vw, q0 - win_pos0, NSA_WINDOW + lb, axis=1)
        kpos_w = q0 - NSA_WINDOW + jnp.arange(NSA_WINDOW + lb)
        dlt = qp[:, None] - kpos_w[None, :]
        mask_w = ((kpos_w[None, :] >= win_pos0) & (dlt >= 0) & (dlt < NSA_WINDOW))[None, :, None, None, :]
        s = jnp.einsum('btghd,bsgd->btghs', qg, kwb) * scale
        p_w = _masked_softmax(s, mask_w)
        o_w = jnp.einsum('btghs,bsgd->btghd', p_w.astype(vwb.dtype), vwb)
        g = gb.reshape(bsz, lb, G, hpg, 3)
        o = g[..., 0:1] * o_c + g[..., 1:2] * o_s + g[..., 2:3] * o_w
        return o.reshape(bsz, lb, NSA_HEADS * HEAD_DIM)

    return _map_query_blocks(block_fn, q_pos, q, gates)


def _rglru(xc, w_r, b_r, w_i, b_i, lam, h0):
    bsz, L, W = xc.shape
    xb = xc.reshape(bsz, L, LRU_BLOCKS, W // LRU_BLOCKS)
    r = jax.nn.sigmoid(jnp.einsum('blhi,hij->blhj', xb, w_r).reshape(bsz, L, W) + b_r)
    i = jax.nn.sigmoid(jnp.einsum('blhi,hij->blhj', xb, w_i).reshape(bsz, L, W) + b_i)
    log_a = (-LRU_C * r * jax.nn.softplus(-lam)).astype(jnp.float32)
    a = jnp.exp(log_a)
    b = jnp.sqrt(-jnp.expm1(2.0 * log_a)) * (i * xc).astype(jnp.float32)

    def combine(left, right):
        a1, b1 = left
        a2, b2 = right
        return a1 * a2, a2 * b1 + b2

    a_cum, b_cum = lax.associative_scan(combine, (a, b), axis=1)
    h = a_cum * h0.astype(jnp.float32)[:, None] + b_cum
    return h.astype(xc.dtype), h[:, -1].astype(h0.dtype)


def _layer_nsa_lru(x, c, q_pos, past, norm_g, ada_w, ada_b, w_in, cmp_wk, cmp_wv, conv_w, conv_b,
                   lru_wr, lru_br, lru_wi, lru_bi, lru_lambda, w_out):
    bsz, L, _ = x.shape
    shift, scale, gate = _modulation(c, ada_w, ada_b)
    h = _rmsnorm(x, norm_g) * (1 + scale) + shift
    q, kv6, gl, z_a, x_b, z_b = _split(h @ w_in, L0_SIZES)
    q = q.reshape(bsz, L, NSA_HEADS, HEAD_DIM)
    kv6 = kv6.reshape(bsz, L, 6, NSA_KV_HEADS, HEAD_DIM)
    kv_paged_new = kv6[:, :, :4]
    kv_win_new = kv6[:, :, 4:]
    gates = jax.nn.sigmoid(gl).reshape(bsz, L, NSA_HEADS, 3)
    if past is None:
        kv_full = kv_paged_new
        kv_win = kv_win_new
        win_pos0 = 0
        conv_hist = jnp.zeros((bsz, CONV_WIDTH - 1, LRU_WIDTH), x.dtype)
        h0 = jnp.zeros((bsz, LRU_WIDTH), jnp.float32)
    else:
        pool, page_table, win_buf, conv_hist, h0 = past
        past_len = page_table.shape[1] * PAGE_SIZE
        kv_full = jnp.concatenate([_gather_pages(pool, page_table), kv_paged_new], axis=1)
        kv_win = jnp.concatenate([win_buf, kv_win_new], axis=1)
        win_pos0 = past_len - win_buf.shape[1]
    o_a = _nsa_attend(q, gates, q_pos, kv_full[:, :, 0], kv_full[:, :, 1], kv_full[:, :, 2], kv_full[:, :, 3],
                      kv_win[:, :, 0], kv_win[:, :, 1], win_pos0, cmp_wk, cmp_wv)
    xc = jnp.concatenate([conv_hist, x_b], axis=1)
    conv = conv_b + sum(xc[:, j:j + L] * conv_w[j] for j in range(CONV_WIDTH))
    o_b, h_last = _rglru(conv, lru_wr, lru_br, lru_wi, lru_bi, lru_lambda, h0)
    y = jnp.concatenate([o_a * jax.nn.silu(z_a), o_b * jax.nn.silu(z_b)], axis=-1) @ w_out
    win_keep = min(NSA_WINDOW, kv_win.shape[1])
    return x + gate * y, (kv_paged_new, kv_win[:, -win_keep:], xc[:, -(CONV_WIDTH - 1):], h_last)


def _diff_attend(q, kv, q_pos, lam):
    bsz, lk = kv.shape[0], kv.shape[1]
    G = DIFF_KV_HEADS
    hpg = DIFF_HEADS // G
    k = kv[:, :, 0].reshape(bsz, lk, G, 2, DIFF_HALF)
    v = kv[:, :, 1]
    k_pos = jnp.arange(lk)
    scale = DIFF_HALF ** -0.5

    def block_fn(qp, qb):
        lb = qp.shape[0]
        qg = qb.reshape(bsz, lb, G, hpg, 2, DIFF_HALF)
        s = jnp.einsum('btghmd,bsgmd->btghms', qg, k) * scale
        mask = (k_pos[None, :] <= qp[:, None])[None, :, None, None, None, :]
        p = _masked_softmax(s, mask)
        attn = p[..., 0, :] - lam * p[..., 1, :]
        o = jnp.einsum('btghs,bsgd->btghd', attn.astype(v.dtype), v)
        return o.reshape(bsz, lb, DIFF_HEADS, 2 * DIFF_HALF)

    return _map_query_blocks(block_fn, q_pos, q)


def _dsa_attend(q, qi, wi, kv, kidx, q_pos):
    bsz, lk = kv.shape[0], kv.shape[1]
    n_sel = min(DSA_TOPK_MAX, lk // 4)
    k_pos = jnp.arange(lk)
    G = DSA_KV_HEADS
    hpg = DSA_HEADS // G
    scale = HEAD_DIM ** -0.5
    b_ix = jnp.arange(bsz)[:, None, None]

    def block_fn(qp, qb, qib, wib):
        lb = qp.shape[0]
        causal = k_pos[None, :] <= qp[:, None]
        idx_s = jnp.einsum('bthd,bsd->bths', qib, kidx) * (IDX_DIM ** -0.5)
        score = jnp.einsum('bth,bths->bts', wib * (IDX_HEADS ** -0.5), jax.nn.relu(idx_s))
        score = jnp.where(causal[None], score.astype(jnp.float32), NEG)
        top_s, top_i = lax.top_k(score, n_sel)
        ok = top_s > 0.5 * NEG
        kvs = kv[b_ix, top_i]
        qg = qb.reshape(bsz, lb, G, hpg, HEAD_DIM)
        s = jnp.einsum('btghd,btngd->btghn', qg, kvs[:, :, :, 0]) * scale
        p = _masked_softmax(s, ok[:, :, None, None, :])
        o = jnp.einsum('btghn,btngd->btghd', p.astype(kvs.dtype), kvs[:, :, :, 1])
        return o.reshape(bsz, lb, DSA_HEADS * HEAD_DIM)

    return _map_query_blocks(block_fn, q_pos, q, qi, wi)


def _layer_diff_dsa(x, c, q_pos, past, norm_g, ada_w, ada_b, w_in, lam_q1, lam_k1, lam_q2, lam_k2, subln_g, w_out):
    bsz, L, _ = x.shape
    shift, scale, gate = _modulation(c, ada_w, ada_b)
    h = _rmsnorm(x, norm_g) * (1 + scale) + shift
    qc, kc, vc, z_c, qd, kd, vd, qi, ki, wi, z_d = _split(h @ w_in, L1_SIZES)
    qc = qc.reshape(bsz, L, DIFF_HEADS, 2, DIFF_HALF)
    diff_kv_new = jnp.stack([kc.reshape(bsz, L, DIFF_KV_HEADS, 2 * DIFF_HALF),
                             vc.reshape(bsz, L, DIFF_KV_HEADS, 2 * DIFF_HALF)], axis=2)
    dsa_kv_new = jnp.stack([kd.reshape(bsz, L, DSA_KV_HEADS, HEAD_DIM),
                            vd.reshape(bsz, L, DSA_KV_HEADS, HEAD_DIM)], axis=2)
    kidx_new = ki
    if past is None:
        diff_kv, dsa_kv, kidx = diff_kv_new, dsa_kv_new, kidx_new
    else:
        diff_pool, dsa_pool, kidx_pool, page_table = past
        diff_kv = jnp.concatenate([_gather_pages(diff_pool, page_table), diff_kv_new], axis=1)
        dsa_kv = jnp.concatenate([_gather_pages(dsa_pool, page_table), dsa_kv_new], axis=1)
        kidx = jnp.concatenate([_gather_pages(kidx_pool, page_table), kidx_new], axis=1)
    lam = (jnp.exp(jnp.sum(lam_q1 * lam_k1).astype(jnp.float32))
           - jnp.exp(jnp.sum(lam_q2 * lam_k2).astype(jnp.float32)) + DIFF_LAMBDA_INIT)
    o_c = _diff_attend(qc, diff_kv, q_pos, lam)
    o_c = (_rmsnorm(o_c, subln_g) * (1.0 - DIFF_LAMBDA_INIT)).reshape(bsz, L, DIFF_HEADS * 2 * DIFF_HALF)
    o_d = _dsa_attend(qd.reshape(bsz, L, DSA_HEADS, HEAD_DIM), qi.reshape(bsz, L, IDX_HEADS, IDX_DIM), wi,
                      dsa_kv, kidx, q_pos)
    y = jnp.concatenate([o_c * jax.nn.silu(z_c), o_d * jax.nn.silu(z_d)], axis=-1) @ w_out
    return x + gate * y, (diff_kv_new, dsa_kv_new, kidx_new)


def setup_inputs(seed: int = 0) -> dict:
    key = jax.random.key(seed)
    ks = iter(jax.random.split(key, 48))
    f32 = jnp.float32

    def nrm(shape, std):
        return jax.random.normal(next(ks), shape, f32) * std

    n_pages = PAST_LEN // PAGE_SIZE
    n_pool = (5 * DEC_BATCH * n_pages + 3) // 4
    win_buf = min(NSA_WINDOW, PAST_LEN)
    p0 = sum(L0_SIZES)
    p1 = sum(L1_SIZES)
    dsc = D_MODEL ** -0.5
    lru_bw = LRU_WIDTH // LRU_BLOCKS
    page_table = jax.random.permutation(next(ks), n_pool)[: DEC_BATCH * n_pages]
    page_table = page_table.reshape(DEC_BATCH, n_pages).astype(jnp.int32)
    a0 = jax.random.uniform(next(ks), (LRU_WIDTH,), f32, 0.9, 0.999)
    return {
        'x_prompt': nrm((BATCH, SEQ, D_MODEL), 1.0),
        'x_sample': nrm((DEC_BATCH, DEC_SEQ, D_MODEL), 1.0),
        'cache_l0_nsa_kv': nrm((n_pool, PAGE_SIZE, 4, NSA_KV_HEADS, HEAD_DIM), 1.0),
        'state_l0_win_kv': nrm((DEC_BATCH, win_buf, 2, NSA_KV_HEADS, HEAD_DIM), 1.0),
        'state_l0_conv': nrm((DEC_BATCH, CONV_WIDTH - 1, LRU_WIDTH), 1.0),
        'state_l0_lru_h': nrm((DEC_BATCH, LRU_WIDTH), 0.5),
        'cache_l1_diff_kv': nrm((n_pool, PAGE_SIZE, 2, DIFF_KV_HEADS, 2 * DIFF_HALF), 1.0),
        'cache_l1_dsa_kv': nrm((n_pool, PAGE_SIZE, 2, DSA_KV_HEADS, HEAD_DIM), 1.0),
        'cache_l1_dsa_kidx': nrm((n_pool, PAGE_SIZE, IDX_DIM), 1.0),
        'page_table': page_table,
        'c_prompt': nrm((BATCH, D_MODEL), 1.0),
        'c_sample': nrm((DEC_BATCH, D_MODEL), 1.0),
        'l0_norm_g': 1.0 + nrm((D_MODEL,), 0.02),
        'l0_ada_w': nrm((D_MODEL, 3 * D_MODEL), 0.3 * dsc),
        'l0_ada_b': nrm((3 * D_MODEL,), 0.02),
        'l0_w_in': nrm((D_MODEL, p0), dsc),
        'l0_cmp_wk': 1.0 / NSA_CMP_BLOCK + nrm((NSA_CMP_BLOCK, HEAD_DIM), 0.01),
        'l0_cmp_wv': 1.0 / NSA_CMP_BLOCK + nrm((NSA_CMP_BLOCK, HEAD_DIM), 0.01),
        'l0_conv_w': nrm((CONV_WIDTH, LRU_WIDTH), CONV_WIDTH ** -0.5),
        'l0_conv_b': nrm((LRU_WIDTH,), 0.02),
        'l0_lru_wr': nrm((LRU_BLOCKS, lru_bw, lru_bw), lru_bw ** -0.5),
        'l0_lru_br': nrm((LRU_WIDTH,), 0.02),
        'l0_lru_wi': nrm((LRU_BLOCKS, lru_bw, lru_bw), lru_bw ** -0.5),
        'l0_lru_bi': nrm((LRU_WIDTH,), 0.02),
        'l0_lru_lambda': jnp.log(a0) - jnp.log1p(-a0),
        'l0_w_out': nrm((2 * MIX_WIDTH, D_MODEL), (2 * MIX_WIDTH) ** -0.5),
        'l1_norm_g': 1.0 + nrm((D_MODEL,), 0.02),
        'l1_ada_w': nrm((D_MODEL, 3 * D_MODEL), 0.3 * dsc),
        'l1_ada_b': nrm((3 * D_MODEL,), 0.02),
        'l1_w_in': nrm((D_MODEL, p1), dsc),
        'l1_lam_q1': nrm((DIFF_HALF,), 0.1),
        'l1_lam_k1': nrm((DIFF_HALF,), 0.1),
        'l1_lam_q2': nrm((DIFF_HALF,), 0.1),
        'l1_lam_k2': nrm((DIFF_HALF,), 0.1),
        'l1_subln_g': 1.0 + nrm((2 * DIFF_HALF,), 0.02),
        'l1_w_out': nrm((2 * MIX_WIDTH, D_MODEL), (2 * MIX_WIDTH) ** -0.5),
        'final_norm_g': 1.0 + nrm((D_MODEL,), 0.02),
    }


def reference(x_prompt, x_sample, cache_l0_nsa_kv, state_l0_win_kv, state_l0_conv, state_l0_lru_h,
              cache_l1_diff_kv, cache_l1_dsa_kv, cache_l1_dsa_kidx, page_table, c_prompt, c_sample,
              l0_norm_g, l0_ada_w, l0_ada_b, l0_w_in, l0_cmp_wk, l0_cmp_wv, l0_conv_w, l0_conv_b,
              l0_lru_wr, l0_lru_br, l0_lru_wi, l0_lru_bi, l0_lru_lambda, l0_w_out,
              l1_norm_g, l1_ada_w, l1_ada_b, l1_w_in, l1_lam_q1, l1_lam_k1, l1_lam_q2, l1_lam_k2,
              l1_subln_g, l1_w_out, final_norm_g):
    past_len = page_table.shape[1] * PAGE_SIZE
    pos_p = jnp.arange(x_prompt.shape[1], dtype=jnp.int32)
    pos_s = past_len + jnp.arange(x_sample.shape[1], dtype=jnp.int32)
    xp, xs = x_prompt, x_sample
    for layer in range(DEPTH):
        if layer % 2 == 0:
            w0 = (l0_norm_g, l0_ada_w, l0_ada_b, l0_w_in, l0_cmp_wk, l0_cmp_wv, l0_conv_w, l0_conv_b,
                  l0_lru_wr, l0_lru_br, l0_lru_wi, l0_lru_bi, l0_lru_lambda, l0_w_out)
            xp, (nsa_kv_p, win_p, conv_p, h_p) = _layer_nsa_lru(xp, c_prompt, pos_p, None, *w0)
            xs, (nsa_kv_s, win_s, conv_s, h_s) = _layer_nsa_lru(
                xs, c_sample, pos_s, (cache_l0_nsa_kv, page_table, state_l0_win_kv, state_l0_conv, state_l0_lru_h), *w0)
        else:
            w1 = (l1_norm_g, l1_ada_w, l1_ada_b, l1_w_in, l1_lam_q1, l1_lam_k1, l1_lam_q2, l1_lam_k2,
                  l1_subln_g, l1_w_out)
            xp, (diff_kv_p, dsa_kv_p, kidx_p) = _layer_diff_dsa(xp, c_prompt, pos_p, None, *w1)
            xs, (diff_kv_s, dsa_kv_s, kidx_s) = _layer_diff_dsa(
                xs, c_sample, pos_s, (cache_l1_diff_kv, cache_l1_dsa_kv, cache_l1_dsa_kidx, page_table), *w1)
    y_prompt = _rmsnorm(xp, final_norm_g)
    y_sample = _rmsnorm(xs, final_norm_g)
    return (y_prompt, y_sample, nsa_kv_p, nsa_kv_s, win_p, win_s, conv_p, conv_s, h_p, h_s,
            diff_kv_p, diff_kv_s, dsa_kv_p, dsa_kv_s, kidx_p, kidx_s)
```

```python
import functools
import math

import jax
import jax.numpy as jnp
import numpy as np
from jax import lax
from jax.experimental import pallas as pl
from jax.experimental.pallas import tpu as pltpu

F32 = jnp.float32
BF16 = jnp.bfloat16
I32 = jnp.int32

PAGE_SIZE = 128
HEAD_DIM = 64
NSA_HEADS = 8
NSA_KV_HEADS = 2
NSA_CMP_BLOCK = 32
NSA_SEL_BLOCK = 64
NSA_TOPN = 16
NSA_WINDOW = 512
FORCE_SCORE = 1e4
LRU_BLOCKS = 8
LRU_C = 8.0
CONV_WIDTH = 4
DIFF_HALF = 64
DIFF_HEADS = 4
DIFF_KV_HEADS = 2
DIFF_LAMBDA_INIT = 0.8 - 0.6 * math.exp(-0.3 * 1)
DSA_HEADS = 8
DSA_KV_HEADS = 2
IDX_HEADS = 4
IDX_DIM = 64
DSA_TOPK_MAX = 256
NORM_EPS = 1e-6
NEG = -1e30
REMOVED = -3e38
INT_MIN = -2 ** 31

LANE = 128
VMEM_LIMIT = 56 * 1024 * 1024
KV_TILE = 512
WIN_SPAN = NSA_WINDOW + LANE
PAGES_PER_STEP = 16


def _cparams(sem):
    return pltpu.CompilerParams(dimension_semantics=sem, vmem_limit_bytes=VMEM_LIMIT)


def _dot(a, b):
    return jnp.dot(a, b, preferred_element_type=F32)


def _dot_nt(a, b):
    return lax.dot_general(a, b, (((1,), (1,)), ((), ())), preferred_element_type=F32)


def _round_up(x, m):
    return (x + m - 1) // m * m


def _mod_kernel(c_ref, w_ref, b_ref, o_ref):
    o_ref[...] = jnp.dot(c_ref[...], w_ref[...], preferred_element_type=F32,
                         precision=lax.Precision.HIGHEST) + b_ref[...]


def _modulation(c, w, b):
    bc, d = c.shape
    n = w.shape[1]
    tn = 512
    return pl.pallas_call(
        _mod_kernel,
        grid=(n // tn,),
        in_specs=[pl.BlockSpec((bc, d), lambda j: (0, 0)),
                  pl.BlockSpec((d, tn), lambda j: (0, j)),
                  pl.BlockSpec((1, tn), lambda j: (0, j))],
        out_specs=pl.BlockSpec((bc, tn), lambda j: (0, j)),
        out_shape=jax.ShapeDtypeStruct((bc, n), F32),
        compiler_params=_cparams(("arbitrary",)),
        name="modulation",
    )(c, w, b.reshape(1, n))


def _proj_kernel(x_ref, g_ref, sh_ref, sc_ref, w_ref, *o_refs, segs, sigmoid_seg):
    x = x_ref[0]
    y = x * lax.rsqrt(jnp.mean(x * x, axis=-1, keepdims=True) + NORM_EPS)
    h = (y * g_ref[...]) * (1.0 + sc_ref[0]) + sh_ref[0]
    hb = h.astype(BF16)
    for i, ((a, b), o_ref) in enumerate(zip(segs, o_refs)):
        r = _dot(hb, w_ref[:, a:b])
        if i == sigmoid_seg:
            r = jax.nn.sigmoid(r)
        o_ref[0] = r


def _project(x, g, shift, scale, w, segs, sigmoid_seg, tl):
    b, l, d = x.shape
    ts = shift.shape[1]
    tm = 1 if ts == 1 else tl
    mod_map = (lambda bi, li: (bi, 0, 0)) if ts == 1 else (lambda bi, li: (bi, li, 0))
    p = w.shape[1]
    kern = functools.partial(_proj_kernel, segs=tuple(segs), sigmoid_seg=sigmoid_seg)
    return pl.pallas_call(
        kern,
        grid=(b, l // tl),
        in_specs=[pl.BlockSpec((1, tl, d), lambda bi, li: (bi, li, 0)),
                  pl.BlockSpec((1, d), lambda bi, li: (0, 0)),
                  pl.BlockSpec((1, tm, d), mod_map),
                  pl.BlockSpec((1, tm, d), mod_map),
                  pl.BlockSpec((d, p), lambda bi, li: (0, 0))],
        out_specs=[pl.BlockSpec((1, tl, e - a), lambda bi, li: (bi, li, 0)) for a, e in segs],
        out_shape=[jax.ShapeDtypeStruct((b, l, e - a), F32) for a, e in segs],
        compiler_params=_cparams(("parallel", "arbitrary")),
        name="norm_mod_project",
    )(x, g.reshape(1, d), shift, scale, w)


def _gather_kernel(pt_ref, *refs, pp, n_page_steps, chunk_w, n_chunks, lane0, has_new, compress):
    page_refs = refs[:pp]
    pos = pp
    new_ref = None
    if has_new:
        new_ref = refs[pos]
        pos += 1
    w4_ref = None
    if compress:
        w4_ref = refs[pos]
        pos += 1
    out_ref = refs[pos]
    cmp_ref = refs[pos + 1] if compress else None
    j = pl.program_id(1)

    def emit(i, page):
        for c in range(n_chunks):
            a = lane0 + c * chunk_w
            out_ref[0, c, i * PAGE_SIZE:(i + 1) * PAGE_SIZE, :] = page[:, a:a + chunk_w].astype(BF16)
        if compress:
            nb = PAGE_SIZE // NSA_SEL_BLOCK
            prod = page[:, 0:4 * HEAD_DIM].reshape(nb, NSA_SEL_BLOCK, 4 * HEAD_DIM) * w4_ref[...][None]
            even = jnp.sum(prod[:, :NSA_CMP_BLOCK], axis=1)
            odd = jnp.sum(prod[:, NSA_CMP_BLOCK:], axis=1)
            for c in range(4):
                cmp_ref[0, c, 0, i * nb:(i + 1) * nb, :] = even[:, c * HEAD_DIM:(c + 1) * HEAD_DIM]
                cmp_ref[0, c, 1, i * nb:(i + 1) * nb, :] = odd[:, c * HEAD_DIM:(c + 1) * HEAD_DIM]

    if has_new:
        @pl.when(j < n_page_steps)
        def _():
            for i in range(pp):
                emit(i, page_refs[i][0])

        @pl.when(j >= n_page_steps)
        def _():
            new = new_ref[0]
            w = new.shape[-1]
            emit(0, jnp.concatenate([new, jnp.zeros((PAGE_SIZE - new.shape[0], w), F32)], axis=0))
            for i in range(1, pp):
                emit(i, jnp.zeros((PAGE_SIZE, w), F32))
    else:
        for i in range(pp):
            emit(i, page_refs[i][0])


def _gather_chunks(pool, page_table, new, chunk_w, lane0=0, n_chunks=None, cmp_w=None):
    npool, ps, w = pool.shape
    bk, n_pages = page_table.shape
    if n_chunks is None:
        n_chunks = (w - lane0) // chunk_w
    pp = min(PAGES_PER_STEP, n_pages)
    assert n_pages % pp == 0
    n_page_steps = n_pages // pp
    has_new = new is not None
    compress = cmp_w is not None
    n_steps = n_page_steps + (1 if has_new else 0)
    rows = pp * PAGE_SIZE
    lk_pad = n_steps * rows

    def page_map(i):
        def f(b, j, pt):
            return (pt[b, jnp.minimum(j * pp + i, n_pages - 1)], 0, 0)
        return f

    in_specs = [pl.BlockSpec((1, ps, w), page_map(i)) for i in range(pp)]
    args = [pool] * pp
    if has_new:
        in_specs.append(pl.BlockSpec((1, new.shape[1], w), lambda b, j, pt: (b, 0, 0)))
        args.append(new)
    if compress:
        in_specs.append(pl.BlockSpec(cmp_w.shape, lambda b, j, pt: (0, 0)))
        args.append(cmp_w)
    out_specs = [pl.BlockSpec((1, n_chunks, rows, chunk_w), lambda b, j, pt: (b, 0, j, 0))]
    out_shape = [jax.ShapeDtypeStruct((bk, n_chunks, lk_pad, chunk_w), BF16)]
    if compress:
        nb = rows // NSA_SEL_BLOCK
        out_specs.append(pl.BlockSpec((1, 4, 2, nb, HEAD_DIM), lambda b, j, pt: (b, 0, 0, j, 0)))
        out_shape.append(jax.ShapeDtypeStruct((bk, 4, 2, lk_pad // NSA_SEL_BLOCK, HEAD_DIM), F32))
    kern = functools.partial(_gather_kernel, pp=pp, n_page_steps=n_page_steps, chunk_w=chunk_w,
                             n_chunks=n_chunks, lane0=lane0, has_new=has_new, compress=compress)
    outs = pl.pallas_call(
        kern,
        grid_spec=pltpu.PrefetchScalarGridSpec(
            num_scalar_prefetch=1, grid=(bk, n_steps), in_specs=in_specs, out_specs=out_specs),
        out_shape=out_shape,
        compiler_params=_cparams(("parallel", "arbitrary")),
        name="gather_pages",
    )(page_table, *args)
    return outs if compress else outs[0]


def _identity_pages(x):
    b, l, w = x.shape
    n_pages = l // PAGE_SIZE
    pool = x.reshape(b * n_pages, PAGE_SIZE, w)
    table = jnp.arange(b * n_pages, dtype=I32).reshape(b, n_pages)
    return pool, table


def _flash_update(s, v_tile, m, l, acc):
    m_new = jnp.maximum(m, jnp.max(s, axis=-1, keepdims=True))
    alpha = jnp.exp(m - m_new)
    p = jnp.exp(s - m_new)
    l = alpha * l + jnp.sum(p, axis=-1, keepdims=True)
    acc = alpha * acc + _dot(p.astype(BF16), v_tile)
    return m_new, l, acc


def _flash_init(rows, dv):
    return (jnp.full((rows, 1), NEG, F32), jnp.zeros((rows, 1), F32), jnp.zeros((rows, dv), F32))


def _stack_heads(q, n, width):
    return jnp.concatenate([q[:, h * width:(h + 1) * width] for h in range(n)], axis=0)


def _nsa_kernel(q_ref, g_ref, cmp_ref, ksel_ref, vsel_ref, kwin_ref, vwin_ref, o_ref, *,
                tq, tk, q_pos0, win_pos0, nsp, n_top):
    hpg = NSA_HEADS // NSA_KV_HEADS
    qi = pl.program_id(2)
    q0 = q_pos0 + qi * tq
    qpos = q0 + lax.broadcasted_iota(I32, (tq, 1), 0)
    qpos_r = jnp.concatenate([qpos] * hpg, axis=0)
    qs = _stack_heads(q_ref[0] * (HEAD_DIM ** -0.5), hpg, HEAD_DIM).astype(BF16)
    rows = hpg * tq

    kce = cmp_ref[0, 0, 0].astype(BF16)
    kco = cmp_ref[0, 0, 1].astype(BF16)
    vce = cmp_ref[0, 1, 0].astype(BF16)
    vco = cmp_ref[0, 1, 1].astype(BF16)
    blk = lax.broadcasted_iota(I32, (1, nsp), 1)
    vis_e = (blk * NSA_SEL_BLOCK + (NSA_CMP_BLOCK - 1)) <= qpos_r
    vis_o = (blk * NSA_SEL_BLOCK + (NSA_SEL_BLOCK - 1)) <= qpos_r
    s_e = jnp.where(vis_e, _dot_nt(qs, kce), NEG)
    s_o = jnp.where(vis_o, _dot_nt(qs, kco), NEG)
    m = jnp.maximum(jnp.max(s_e, axis=-1, keepdims=True), jnp.max(s_o, axis=-1, keepdims=True))
    p_e = jnp.where(vis_e, jnp.exp(s_e - m), 0.0)
    p_o = jnp.where(vis_o, jnp.exp(s_o - m), 0.0)
    den = jnp.sum(p_e, axis=-1, keepdims=True) + jnp.sum(p_o, axis=-1, keepdims=True)
    inv = 1.0 / jnp.maximum(den, 1e-30)
    p_e = p_e * inv
    p_o = p_o * inv
    o_c = _dot(p_e.astype(BF16), vce) + _dot(p_o.astype(BF16), vco)

    pe_h = sum(p_e[h * tq:(h + 1) * tq] for h in range(hpg))
    po_h = sum(p_o[h * tq:(h + 1) * tq] for h in range(hpg))
    imp = pe_h + po_h
    cur = qpos // NSA_SEL_BLOCK
    imp = jnp.where((blk == cur) | (blk == 0), FORCE_SCORE, imp)
    imp = jnp.where(blk <= cur, imp, NEG)
    sel = jnp.zeros((tq, nsp), F32)
    for _ in range(n_top):
        mx = jnp.max(imp, axis=-1, keepdims=True)
        first = jnp.min(jnp.where(imp == mx, blk, nsp), axis=-1, keepdims=True)
        pick = blk == first
        sel = jnp.where(pick & (mx > 0.5 * NEG), 1.0, sel)
        imp = jnp.where(pick, REMOVED, imp)
    sel_b = sel.astype(BF16)

    blk_col = lax.broadcasted_iota(I32, (nsp, 1), 0)

    def sel_step(j, carry):
        k0 = pl.multiple_of(j * tk, tk)
        kpos = k0 + lax.broadcasted_iota(I32, (1, tk), 1)
        expand = jnp.where(blk_col == kpos // NSA_SEL_BLOCK, 1.0, 0.0).astype(BF16)
        picked = _dot(sel_b, expand)
        picked = jnp.where(kpos <= qpos, picked, 0.0)
        mask = jnp.concatenate([picked] * hpg, axis=0) > 0.5
        s = _dot_nt(qs, ksel_ref[0, 0, pl.ds(k0, tk), :])
        s = jnp.where(mask, s, NEG)
        return _flash_update(s, vsel_ref[0, 0, pl.ds(k0, tk), :], *carry)

    n_tiles = (q0 + tq - 1) // tk + 1
    m_s, l_s, acc_s = lax.fori_loop(0, n_tiles, sel_step, _flash_init(rows, HEAD_DIM))
    o_s = acc_s * (1.0 / jnp.maximum(l_s, 1e-30))

    start = pl.multiple_of(jnp.maximum(q0 - NSA_WINDOW - win_pos0, 0), 8)
    kpos_w = win_pos0 + start + lax.broadcasted_iota(I32, (1, WIN_SPAN), 1)
    dlt = qpos_r - kpos_w
    s_w = _dot_nt(qs, kwin_ref[0, 0, pl.ds(start, WIN_SPAN), :])
    s_w = jnp.where(dlt >= 0, jnp.where(dlt < NSA_WINDOW, s_w, NEG), NEG)
    m_w = jnp.max(s_w, axis=-1, keepdims=True)
    p_w = jnp.exp(s_w - m_w)
    l_w = jnp.sum(p_w, axis=-1, keepdims=True)
    o_w = _dot(p_w.astype(BF16), vwin_ref[0, 0, pl.ds(start, WIN_SPAN), :]) * (1.0 / jnp.maximum(l_w, 1e-30))

    g = g_ref[0]
    outs = []
    for h in range(hpg):
        r = slice(h * tq, (h + 1) * tq)
        outs.append(g[:, 3 * h:3 * h + 1] * o_c[r] + g[:, 3 * h + 1:3 * h + 2] * o_s[r]
                    + g[:, 3 * h + 2:3 * h + 3] * o_w[r])
    o_ref[0] = jnp.concatenate(outs, axis=-1)


def _nsa_attention(q, gates, cmp, ksel, kwin, *, tq, q_pos0, lk, win_pos0):
    bk, lq, _ = q.shape
    G = NSA_KV_HEADS
    nsp = cmp.shape[3]
    lk_pad = ksel.shape[2]
    lw_pad = kwin.shape[2]
    tk = min(KV_TILE, lk_pad)
    ns = -(-lk // NSA_SEL_BLOCK)
    assert tq <= LANE and lw_pad >= WIN_SPAN and q_pos0 + lq <= lk_pad
    kern = functools.partial(_nsa_kernel, tq=tq, tk=tk, q_pos0=q_pos0, win_pos0=win_pos0,
                             nsp=nsp, n_top=min(NSA_TOPN, ns))
    gw = (NSA_HEADS // G) * HEAD_DIM
    return pl.pallas_call(
        kern,
        grid=(bk, G, lq // tq),
        in_specs=[pl.BlockSpec((1, tq, gw), lambda b, g, i: (b, i, g)),
                  pl.BlockSpec((1, tq, LANE), lambda b, g, i: (b, i, g)),
                  pl.BlockSpec((1, 2, 2, nsp, HEAD_DIM), lambda b, g, i: (b, g, 0, 0, 0)),
                  pl.BlockSpec((1, 1, lk_pad, HEAD_DIM), lambda b, g, i: (b, g, 0, 0)),
                  pl.BlockSpec((1, 1, lk_pad, HEAD_DIM), lambda b, g, i: (b, G + g, 0, 0)),
                  pl.BlockSpec((1, 1, lw_pad, HEAD_DIM), lambda b, g, i: (b, g, 0, 0)),
                  pl.BlockSpec((1, 1, lw_pad, HEAD_DIM), lambda b, g, i: (b, G + g, 0, 0))],
        out_specs=pl.BlockSpec((1, tq, gw), lambda b, g, i: (b, i, g)),
        out_shape=jax.ShapeDtypeStruct((bk, lq, NSA_HEADS * HEAD_DIM), F32),
        compiler_params=_cparams(("parallel", "parallel", "arbitrary")),
        name="nsa_attention",
    )(q, gates, cmp, ksel, ksel, kwin, kwin)


def _shift_rows(x, d, fill):
    rolled = pltpu.roll(x, d, axis=0)
    row = lax.broadcasted_iota(I32, x.shape, 0)
    return jnp.where(row >= d, rolled, fill)


def _lru_kernel(x_ref, hist_ref, h0_ref, cw_ref, cb_ref, wr_ref, br_ref, wi_ref, bi_ref, lam_ref,
                o_ref, hl_ref, tail_ref, h_ref, *, tl, last_row):
    li = pl.program_id(1)

    @pl.when(li == 0)
    def _():
        tail_ref[...] = jnp.concatenate(
            [jnp.zeros((8 - (CONV_WIDTH - 1), x_ref.shape[-1]), F32), hist_ref[0]], axis=0)
        h_ref[...] = h0_ref[0]

    x = x_ref[0]
    xp = jnp.concatenate([tail_ref[...], x], axis=0)
    cw = cw_ref[...]
    conv = sum(xp[8 - (CONV_WIDTH - 1) + j:8 - (CONV_WIDTH - 1) + j + tl] * cw[j:j + 1]
               for j in range(CONV_WIDTH))
    conv = cb_ref[...] + conv
    tail_ref[...] = x[tl - 8:tl]

    cb16 = conv.astype(BF16)
    r = jax.nn.sigmoid(_dot(cb16, wr_ref[...]) + br_ref[...])
    ig = jax.nn.sigmoid(_dot(cb16, wi_ref[...]) + bi_ref[...])
    log_a = -LRU_C * r * jax.nn.softplus(-lam_ref[...])
    a = jnp.exp(log_a)
    th = jnp.tanh(log_a)
    b = jnp.sqrt(-2.0 * th / (1.0 - th)) * (ig * conv)

    d = 1
    while d < tl:
        a_prev = _shift_rows(a, d, 1.0)
        b_prev = _shift_rows(b, d, 0.0)
        b = a * b_prev + b
        a = a * a_prev
        d *= 2
    h = a * h_ref[...] + b
    o_ref[0] = h
    h_ref[...] = h[tl - 1:tl]

    @pl.when(li == pl.num_programs(1) - 1)
    def _():
        hl_ref[0] = h[last_row:last_row + 1]


def _block_diag(w):
    nb, bw, _ = w.shape
    eye = jnp.eye(nb, dtype=w.dtype)
    return (eye[:, None, :, None] * w[:, :, None, :]).reshape(nb * bw, nb * bw)


def _conv_rglru(x_b, hist, h0, conv_w, conv_b, w_r, b_r, w_i, b_i, lam, *, tl, n_valid):
    b, l, w = x_b.shape
    assert tl >= 8 and l % tl == 0 and n_valid > l - tl
    kern = functools.partial(_lru_kernel, tl=tl, last_row=(n_valid - 1) % tl)
    vec = lambda: pl.BlockSpec((1, w), lambda bi, li: (0, 0))
    h, h_last = pl.pallas_call(
        kern,
        grid=(b, l // tl),
        in_specs=[pl.BlockSpec((1, tl, w), lambda bi, li: (bi, li, 0)),
                  pl.BlockSpec((1, CONV_WIDTH - 1, w), lambda bi, li: (bi, 0, 0)),
                  pl.BlockSpec((1, 1, w), lambda bi, li: (bi, 0, 0)),
                  pl.BlockSpec((CONV_WIDTH, w), lambda bi, li: (0, 0)),
                  vec(),
                  pl.BlockSpec((w, w), lambda bi, li: (0, 0)), vec(),
                  pl.BlockSpec((w, w), lambda bi, li: (0, 0)), vec(), vec()],
        out_specs=[pl.BlockSpec((1, tl, w), lambda bi, li: (bi, li, 0)),
                   pl.BlockSpec((1, 1, w), lambda bi, li: (bi, 0, 0))],
        out_shape=[jax.ShapeDtypeStruct((b, l, w), F32), jax.ShapeDtypeStruct((b, 1, w), F32)],
        scratch_shapes=[pltpu.VMEM((8, w), F32), pltpu.VMEM((1, w), F32)],
        compiler_params=_cparams(("parallel", "arbitrary")),
        name="conv_rglru",
    )(x_b, hist, h0.reshape(b, 1, w), conv_w, conv_b.reshape(1, w),
      _block_diag(w_r).astype(BF16), b_r.reshape(1, w), _block_diag(w_i).astype(BF16), b_i.reshape(1, w),
      lam.reshape(1, w))
    return h, h_last.reshape(b, w)


def _out_kernel(oa_ref, za_ref, ob_ref, zb_ref, x_ref, gate_ref, w_ref, fg_ref, o_ref, *, final_norm):
    half = oa_ref.shape[-1]
    ma = (oa_ref[0] * jax.nn.silu(za_ref[0])).astype(BF16)
    mb = (ob_ref[0] * jax.nn.silu(zb_ref[0])).astype(BF16)
    y = _dot(ma, w_ref[0:half, :]) + _dot(mb, w_ref[half:2 * half, :])
    out = x_ref[0] + gate_ref[0] * y
    if final_norm:
        out = out * lax.rsqrt(jnp.mean(out * out, axis=-1, keepdims=True) + NORM_EPS) * fg_ref[...]
    o_ref[0] = out


def _out_project(o_a, z_a, o_b, z_b, x, gate, w_out, final_g, *, tl, final_norm):
    b, l, d = x.shape
    half = o_a.shape[-1]
    ts = gate.shape[1]
    tm = 1 if ts == 1 else tl
    mod_map = (lambda bi, li: (bi, 0, 0)) if ts == 1 else (lambda bi, li: (bi, li, 0))
    act = lambda: pl.BlockSpec((1, tl, half), lambda bi, li: (bi, li, 0))
    return pl.pallas_call(
        functools.partial(_out_kernel, final_norm=final_norm),
        grid=(b, l // tl),
        in_specs=[act(), act(), act(), act(),
                  pl.BlockSpec((1, tl, d), lambda bi, li: (bi, li, 0)),
                  pl.BlockSpec((1, tm, d), mod_map),
                  pl.BlockSpec((2 * half, d), lambda bi, li: (0, 0)),
                  pl.BlockSpec((1, d), lambda bi, li: (0, 0))],
        out_specs=pl.BlockSpec((1, tl, d), lambda bi, li: (bi, li, 0)),
        out_shape=jax.ShapeDtypeStruct((b, l, d), F32),
        compiler_params=_cparams(("parallel", "arbitrary")),
        name="out_project",
    )(o_a, z_a, o_b, z_b, x, gate, w_out, final_g.reshape(1, d))


def _diff_kernel(q_ref, k_ref, v_ref, lamv_ref, subg_ref, o_ref, *, tq, tk, q_pos0):
    hpg = DIFF_HEADS // DIFF_KV_HEADS
    qi = pl.program_id(2)
    q0 = q_pos0 + qi * tq
    rows = 2 * hpg * tq
    qpos = q0 + lax.broadcasted_iota(I32, (tq, 1), 0)
    qpos_r = jnp.concatenate([qpos] * (2 * hpg), axis=0)
    q = q_ref[0] * (DIFF_HALF ** -0.5)
    zero = jnp.zeros((tq, DIFF_HALF), F32)
    parts = []
    for mp in range(2):
        for h in range(hpg):
            qh = q[:, (2 * h + mp) * DIFF_HALF:(2 * h + mp + 1) * DIFF_HALF]
            parts.append(jnp.concatenate([qh, zero] if mp == 0 else [zero, qh], axis=-1))
    qs = jnp.concatenate(parts, axis=0).astype(BF16)

    def step(masked):
        def f(j, carry):
            k0 = pl.multiple_of(j * tk, tk)
            s = _dot_nt(qs, k_ref[0, 0, pl.ds(k0, tk), :])
            if masked:
                kpos = k0 + lax.broadcasted_iota(I32, (1, tk), 1)
                s = jnp.where(kpos <= qpos_r, s, NEG)
            return _flash_update(s, v_ref[0, 0, pl.ds(k0, tk), :], *carry)
        return f

    n_full = (q0 + 1) // tk
    n_tiles = (q0 + tq - 1) // tk + 1
    carry = lax.fori_loop(0, n_full, step(False), _flash_init(rows, 2 * DIFF_HALF))
    m, l, acc = lax.fori_loop(n_full, n_tiles, step(True), carry)
    o = acc * (1.0 / jnp.maximum(l, 1e-30))
    lq = lamv_ref[...]
    lam = (jnp.exp(jnp.sum(lq[0:1] * lq[1:2], axis=-1, keepdims=True))
           - jnp.exp(jnp.sum(lq[2:3] * lq[3:4], axis=-1, keepdims=True)) + DIFF_LAMBDA_INIT)
    half = hpg * tq
    od = o[0:half] - lam * o[half:2 * half]
    od = od * lax.rsqrt(jnp.mean(od * od, axis=-1, keepdims=True) + NORM_EPS)
    od = od * subg_ref[...] * (1.0 - DIFF_LAMBDA_INIT)
    o_ref[0] = jnp.concatenate([od[h * tq:(h + 1) * tq] for h in range(hpg)], axis=-1)


def _diff_attention(q, kv, lamv, subln_g, *, tq, q_pos0):
    bk, lq, _ = q.shape
    G = DIFF_KV_HEADS
    lk_pad = kv.shape[2]
    tk = min(KV_TILE, lk_pad)
    gw = (DIFF_HEADS // G) * 2 * DIFF_HALF
    assert q_pos0 + lq <= lk_pad
    return pl.pallas_call(
        functools.partial(_diff_kernel, tq=tq, tk=tk, q_pos0=q_pos0),
        grid=(bk, G, lq // tq),
        in_specs=[pl.BlockSpec((1, tq, gw), lambda b, g, i: (b, i, g)),
                  pl.BlockSpec((1, 1, lk_pad, 2 * DIFF_HALF), lambda b, g, i: (b, g, 0, 0)),
                  pl.BlockSpec((1, 1, lk_pad, 2 * DIFF_HALF), lambda b, g, i: (b, G + g, 0, 0)),
                  pl.BlockSpec((4, DIFF_HALF), lambda b, g, i: (0, 0)),
                  pl.BlockSpec((1, 2 * DIFF_HALF), lambda b, g, i: (0, 0))],
        out_specs=pl.BlockSpec((1, tq, gw), lambda b, g, i: (b, i, g)),
        out_shape=jax.ShapeDtypeStruct((bk, lq, DIFF_HEADS * 2 * DIFF_HALF), F32),
        compiler_params=_cparams(("parallel", "parallel", "arbitrary")),
        name="diff_attention",
    )(q, kv, kv, lamv, subln_g.reshape(1, 2 * DIFF_HALF))


def _dsa_kernel(q_ref, qi_ref, kw_ref, kidx_ref, kv_ref, o_ref, key_ref, *, tq, tk, q_pos0, n_sel, lk_pad):
    G = DSA_KV_HEADS
    hpg = DSA_HEADS // G
    qblk = pl.program_id(1)
    q0 = q_pos0 + qblk * tq
    qpos = q0 + lax.broadcasted_iota(I32, (tq, 1), 0)
    n_tiles = (q0 + tq - 1) // tk + 1
    lanes = tk // LANE

    qidx = qi_ref[0]
    qidx = [qidx[:, h * IDX_DIM:(h + 1) * IDX_DIM].astype(BF16) for h in range(IDX_HEADS)]
    wi = kw_ref[0][:, IDX_DIM:IDX_DIM + IDX_HEADS] * (IDX_HEADS ** -0.5)

    def score_step(j, _):
        k0 = pl.multiple_of(j * tk, tk)
        kt = kidx_ref[0, 0, pl.ds(k0, tk), :]
        score = jnp.zeros((tq, tk), F32)
        for h in range(IDX_HEADS):
            s = _dot_nt(qidx[h], kt) * (IDX_DIM ** -0.5)
            score = score + wi[:, h:h + 1] * jnp.maximum(s, 0.0)
        kpos = k0 + lax.broadcasted_iota(I32, (1, tk), 1)
        bits = pltpu.bitcast(score, I32)
        key = jnp.where(bits < 0, bits ^ 0x7FFFFFFF, bits)
        key = jnp.where(score == 0.0, 0, key)
        key = jnp.where(kpos <= qpos, jnp.where(score > 0.5 * NEG, key, INT_MIN), INT_MIN)
        key_ref[j] = key
        return 0

    lax.fori_loop(0, n_tiles, score_step, 0)

    def count(pred):
        def f(j, acc):
            key = key_ref[j]
            kpos = j * tk + lax.broadcasted_iota(I32, (1, tk), 1)
            hit = jnp.where(pred(key, kpos), 1, 0)
            for c in range(lanes):
                acc = acc + hit[:, c * LANE:(c + 1) * LANE]
            return acc
        acc = lax.fori_loop(0, n_tiles, f, jnp.zeros((tq, LANE), I32))
        return jnp.sum(acc, axis=-1, keepdims=True)

    def bit_step(i, prefix):
        cand = prefix | lax.shift_left(jnp.int32(1), 31 - i)
        cand_s = cand ^ INT_MIN
        cnt = count(lambda key, kpos: key >= cand_s)
        return jnp.where(cnt >= n_sel, cand, prefix)

    prefix = lax.fori_loop(0, 32, bit_step, jnp.zeros((tq, 1), I32))
    thr = prefix ^ INT_MIN

    n_gt = count(lambda key, kpos: key > thr)
    n_eq = count(lambda key, kpos: key == thr)
    need = n_sel - n_gt
    tie = (n_eq > need) & (thr > INT_MIN)
    any_tie = jnp.max(jnp.where(tie, 1, 0)) > 0

    def tie_bound():
        def f(i, lo_hi):
            lo, hi = lo_hi
            mid = (lo + hi) // 2
            cnt = count(lambda key, kpos: (key == thr) & (kpos <= mid))
            ok = cnt >= need
            return jnp.where(ok, lo, mid + 1), jnp.where(ok, mid, hi)
        n_it = int(math.ceil(math.log2(lk_pad))) + 1
        lo, hi = lax.fori_loop(0, n_it, f, (jnp.zeros((tq, 1), I32), jnp.full((tq, 1), lk_pad - 1, I32)))
        return jnp.where(tie, hi, lk_pad)

    jmax = lax.cond(any_tie, tie_bound, lambda: jnp.full((tq, 1), lk_pad, I32))
    thr_adm = jnp.maximum(thr, INT_MIN + 1)

    q = q_ref[0] * (HEAD_DIM ** -0.5)
    qs = [_stack_heads(q[:, g * hpg * HEAD_DIM:(g + 1) * hpg * HEAD_DIM], hpg, HEAD_DIM).astype(BF16)
          for g in range(G)]
    rows = hpg * tq

    def att_step(j, carry):
        k0 = pl.multiple_of(j * tk, tk)
        key = key_ref[j]
        kpos = k0 + lax.broadcasted_iota(I32, (1, tk), 1)
        picked = jnp.where(key > thr_adm, 1.0, jnp.where((key == thr_adm) & (kpos <= jmax), 1.0, 0.0))
        mask = jnp.concatenate([picked] * hpg, axis=0) > 0.5
        out = []
        for g in range(G):
            s = _dot_nt(qs[g], kv_ref[0, g, pl.ds(k0, tk), :])
            s = jnp.where(mask, s, NEG)
            out.append(_flash_update(s, kv_ref[0, G + g, pl.ds(k0, tk), :], *carry[g]))
        return tuple(out)

    res = lax.fori_loop(0, n_tiles, att_step, tuple(_flash_init(rows, HEAD_DIM) for _ in range(G)))
    outs = []
    for g in range(G):
        m, l, acc = res[g]
        o = acc * (1.0 / jnp.maximum(l, 1e-30))
        outs.extend(o[h * tq:(h + 1) * tq] for h in range(hpg))
    o_ref[0] = jnp.concatenate(outs, axis=-1)


def _dsa_attention(q, qi, kw, kidx, kv, *, tq, q_pos0, lk):
    bk, lq, _ = q.shape
    lk_pad = kv.shape[2]
    tk = min(KV_TILE, lk_pad)
    n_sel = min(DSA_TOPK_MAX, lk // 4)
    assert q_pos0 + lq <= lk_pad and tk >= n_sel
    kern = functools.partial(_dsa_kernel, tq=tq, tk=tk, q_pos0=q_pos0, n_sel=n_sel, lk_pad=lk_pad)
    return pl.pallas_call(
        kern,
        grid=(bk, lq // tq),
        in_specs=[pl.BlockSpec((1, tq, DSA_HEADS * HEAD_DIM), lambda b, i: (b, i, 0)),
                  pl.BlockSpec((1, tq, IDX_HEADS * IDX_DIM), lambda b, i: (b, i, 0)),
                  pl.BlockSpec((1, tq, LANE), lambda b, i: (b, i, 0)),
                  pl.BlockSpec((1, 1, lk_pad, IDX_DIM), lambda b, i: (b, 0, 0, 0)),
                  pl.BlockSpec((1, 4, lk_pad, HEAD_DIM), lambda b, i: (b, 0, 0, 0))],
        out_specs=pl.BlockSpec((1, tq, DSA_HEADS * HEAD_DIM), lambda b, i: (b, i, 0)),
        out_shape=jax.ShapeDtypeStruct((bk, lq, DSA_HEADS * HEAD_DIM), F32),
        scratch_shapes=[pltpu.VMEM((lk_pad // tk, tq, tk), I32)],
        compiler_params=_cparams(("parallel", "arbitrary")),
        name="dsa_attention",
    )(q, qi, kw, kidx, kv)


L0_SIZES = (512, 768, 24, 512, 512, 512)
L1_SIZES = (512, 256, 256, 512, 512, 128, 128, 256, 64, 4, 512)


def _l0_weight(w_in):
    d = w_in.shape[0]
    q, kv6, gl, z_a, x_b, z_b = jnp.split(w_in, np.cumsum(L0_SIZES)[:-1].tolist(), axis=1)
    pad = jnp.zeros((d, LANE - 12), w_in.dtype)
    w = jnp.concatenate([q, kv6, z_a, x_b, z_b, gl[:, :12], pad, gl[:, 12:], pad], axis=1)
    segs = [(0, 512), (512, 1024), (1024, 1280), (1280, 1792), (1792, 2304), (2304, 2816), (2816, 3072)]
    return w.astype(BF16), segs


def _l1_weight(w_in):
    d = w_in.shape[0]
    qc, kc, vc, z_c, qd, kd, vd, qi, ki, wi, z_d = jnp.split(w_in, np.cumsum(L1_SIZES)[:-1].tolist(), axis=1)
    pad = jnp.zeros((d, LANE - IDX_DIM - IDX_HEADS), w_in.dtype)
    w = jnp.concatenate([qc, kc, vc, z_c, qd, kd, vd, qi, z_d, ki, wi, pad], axis=1)
    segs = [(0, 512), (512, 1024), (1024, 1536), (1536, 2048), (2048, 2304), (2304, 2560), (2560, 3072),
            (3072, 3200)]
    return w.astype(BF16), segs


def _pad_rows(x, n):
    return jnp.pad(x, ((0, 0), (0, n - x.shape[1]), (0, 0)))


def _layer0(x, mod, past, w, *, tl, tq):
    (norm_g, w_in, cmp_wk, cmp_wv, conv_w, conv_b, lru_wr, lru_br, lru_wi, lru_bi, lru_lambda, w_out) = w
    shift, scale, gate = mod
    b, l, d = x.shape
    w_p, segs = _l0_weight(w_in)
    flat = shift.shape[1] != 1
    xin = x.reshape(1, b * l, d) if flat else x
    q, kvp, kvw, z_a, x_b, z_b, gates = _project(xin, norm_g, shift, scale, w_p, segs, 6, tl)
    if flat:
        q, kvp, kvw, z_a, x_b, z_b, gates = (t.reshape(b, l, -1) for t in (q, kvp, kvw, z_a, x_b, z_b, gates))
    w4 = jnp.concatenate([jnp.concatenate([cmp_wk, cmp_wk, cmp_wv, cmp_wv], axis=1)] * 2, axis=0)
    lq = _round_up(l, 8)
    if past is None:
        pool, table = _identity_pages(kvp)
        ksel, cmp = _gather_chunks(pool, table, None, HEAD_DIM, lane0=4 * HEAD_DIM, n_chunks=4, cmp_w=w4)
        wpool, wtable = _identity_pages(kvw)
        kwin = _gather_chunks(wpool, wtable, None, HEAD_DIM)
        kv_win = kvw
        q_pos0, lk, win_pos0 = 0, l, 0
        hist = jnp.zeros((b, CONV_WIDTH - 1, x_b.shape[-1]), F32)
        h0 = jnp.zeros((b, x_b.shape[-1]), F32)
    else:
        pool, table, win_buf, hist, h0 = past
        assert l < NSA_CMP_BLOCK
        past_len = table.shape[1] * PAGE_SIZE
        ksel, cmp = _gather_chunks(pool.reshape(pool.shape[0], PAGE_SIZE, -1), table, _pad_rows(kvp, 8),
                                   HEAD_DIM, lane0=4 * HEAD_DIM, n_chunks=4, cmp_w=w4)
        kv_win = jnp.concatenate([win_buf.reshape(b, win_buf.shape[1], -1), kvw], axis=1)
        lw_pad = _round_up(max(kv_win.shape[1], WIN_SPAN), PAGE_SIZE)
        wpool, wtable = _identity_pages(_pad_rows(kv_win, lw_pad))
        kwin = _gather_chunks(wpool, wtable, None, HEAD_DIM)
        q_pos0, lk, win_pos0 = past_len, past_len + l, past_len - win_buf.shape[1]
    nsp = _round_up(cmp.shape[3], LANE)
    cmp = jnp.pad(cmp, ((0, 0), (0, 0), (0, 0), (0, nsp - cmp.shape[3]), (0, 0)))
    cmp = cmp.reshape(b, 2, 2, 2, nsp, HEAD_DIM).transpose(0, 2, 1, 3, 4, 5).reshape(b, 4, 2, nsp, HEAD_DIM)
    o_a = _nsa_attention(_pad_rows(q, lq), _pad_rows(gates, lq), cmp, ksel, kwin,
                         tq=min(tq, lq), q_pos0=q_pos0, lk=lk, win_pos0=win_pos0)[:, :l]
    o_b, h_last = _conv_rglru(_pad_rows(x_b, lq), hist, h0, conv_w, conv_b, lru_wr, lru_br, lru_wi, lru_bi,
                              lru_lambda, tl=min(256, lq), n_valid=l)
    o_b = o_b[:, :l]
    fl = (lambda t: t.reshape(1, b * l, -1)) if flat else (lambda t: t)
    x_new = _out_project(fl(o_a), fl(z_a), fl(o_b), fl(z_b), xin, gate, w_out.astype(BF16),
                         jnp.ones((d,), F32), tl=tl, final_norm=False).reshape(b, l, d)
    win_keep = min(NSA_WINDOW, kv_win.shape[1])
    conv_src = jnp.concatenate([hist, x_b], axis=1) if l < CONV_WIDTH - 1 else x_b
    states = (kvp.reshape(b, l, 4, NSA_KV_HEADS, HEAD_DIM),
              kv_win[:, -win_keep:].reshape(b, win_keep, 2, NSA_KV_HEADS, HEAD_DIM),
              conv_src[:, -(CONV_WIDTH - 1):], h_last)
    return x_new, states


def _layer1(x, mod, past, w, final_g, *, tl, tq):
    (norm_g, w_in, lam_q1, lam_k1, lam_q2, lam_k2, subln_g, w_out) = w
    shift, scale, gate = mod
    b, l, d = x.shape
    w_p, segs = _l1_weight(w_in)
    flat = shift.shape[1] != 1
    xin = x.reshape(1, b * l, d) if flat else x
    qc, kvc, z_c, qd, kvd, qi, z_d, kiw = _project(xin, norm_g, shift, scale, w_p, segs, -1, tl)
    if flat:
        qc, kvc, z_c, qd, kvd, qi, z_d, kiw = (t.reshape(b, l, -1) for t in (qc, kvc, z_c, qd, kvd, qi, z_d, kiw))
    lq = _round_up(l, 8)
    if past is None:
        diff_kv = _gather_chunks(*_identity_pages(kvc), None, 2 * DIFF_HALF)
        dsa_kv = _gather_chunks(*_identity_pages(kvd), None, HEAD_DIM)
        kidx = _gather_chunks(*_identity_pages(kiw), None, IDX_DIM, n_chunks=1)
        q_pos0, lk = 0, l
    else:
        diff_pool, dsa_pool, kidx_pool, table = past
        past_len = table.shape[1] * PAGE_SIZE
        rs = lambda p: p.reshape(p.shape[0], PAGE_SIZE, -1)
        diff_kv = _gather_chunks(rs(diff_pool), table, _pad_rows(kvc, 8), 2 * DIFF_HALF)
        dsa_kv = _gather_chunks(rs(dsa_pool), table, _pad_rows(kvd, 8), HEAD_DIM)
        kidx = _gather_chunks(rs(kidx_pool), table, _pad_rows(kiw[:, :, :IDX_DIM], 8), IDX_DIM)
        q_pos0, lk = past_len, past_len + l
    lamv = jnp.stack([lam_q1, lam_k1, lam_q2, lam_k2], axis=0)
    o_c = _diff_attention(_pad_rows(qc, lq), diff_kv, lamv, subln_g, tq=min(tq, lq), q_pos0=q_pos0)[:, :l]
    o_d = _dsa_attention(_pad_rows(qd, lq), _pad_rows(qi, lq), _pad_rows(kiw, lq), kidx, dsa_kv,
                         tq=min(tq, lq), q_pos0=q_pos0, lk=lk)[:, :l]
    fl = (lambda t: t.reshape(1, b * l, -1)) if flat else (lambda t: t)
    y = _out_project(fl(o_c), fl(z_c), fl(o_d), fl(z_d), xin, gate, w_out.astype(BF16), final_g,
                     tl=tl, final_norm=True).reshape(b, l, d)
    states = (kvc.reshape(b, l, 2, DIFF_KV_HEADS, 2 * DIFF_HALF),
              kvd.reshape(b, l, 2, DSA_KV_HEADS, HEAD_DIM), kiw[:, :, :IDX_DIM])
    return y, states


def kernel(x_prompt, x_sample, cache_l0_nsa_kv, state_l0_win_kv, state_l0_conv, state_l0_lru_h,
           cache_l1_diff_kv, cache_l1_dsa_kv, cache_l1_dsa_kidx, page_table, c_prompt, c_sample,
           l0_norm_g, l0_ada_w, l0_ada_b, l0_w_in, l0_cmp_wk, l0_cmp_wv, l0_conv_w, l0_conv_b,
           l0_lru_wr, l0_lru_br, l0_lru_wi, l0_lru_bi, l0_lru_lambda, l0_w_out,
           l1_norm_g, l1_ada_w, l1_ada_b, l1_w_in, l1_lam_q1, l1_lam_k1, l1_lam_q2, l1_lam_k2,
           l1_subln_g, l1_w_out, final_norm_g):
    bp, lp, d = x_prompt.shape
    bs, ls, _ = x_sample.shape
    c_all = jnp.concatenate([c_prompt, c_sample], axis=0)

    def mods(ada_w, ada_b):
        m = _modulation(c_all, ada_w, ada_b)
        mp = tuple(t[:, None] for t in jnp.split(m[:bp], 3, axis=-1))
        ms = tuple(jnp.repeat(t, ls, axis=0)[None] for t in jnp.split(m[bp:], 3, axis=-1))
        return mp, ms

    tl_p = min(512, lp)
    tl_s = bs * ls
    tq = min(LANE, lp)
    w0 = (l0_norm_g, l0_w_in, l0_cmp_wk, l0_cmp_wv, l0_conv_w, l0_conv_b, l0_lru_wr, l0_lru_br,
          l0_lru_wi, l0_lru_bi, l0_lru_lambda, l0_w_out)
    mp0, ms0 = mods(l0_ada_w, l0_ada_b)
    xp, (nsa_kv_p, win_p, conv_p, h_p) = _layer0(x_prompt, mp0, None, w0, tl=tl_p, tq=tq)
    xs, (nsa_kv_s, win_s, conv_s, h_s) = _layer0(
        x_sample, ms0, (cache_l0_nsa_kv, page_table, state_l0_win_kv, state_l0_conv, state_l0_lru_h), w0,
        tl=tl_s, tq=tq)
    w1 = (l1_norm_g, l1_w_in, l1_lam_q1, l1_lam_k1, l1_lam_q2, l1_lam_k2, l1_subln_g, l1_w_out)
    mp1, ms1 = mods(l1_ada_w, l1_ada_b)
    y_p, (diff_kv_p, dsa_kv_p, kidx_p) = _layer1(xp, mp1, None, w1, final_norm_g, tl=tl_p, tq=tq)
    y_s, (diff_kv_s, dsa_kv_s, kidx_s) = _layer1(
        xs, ms1, (cache_l1_diff_kv, cache_l1_dsa_kv, cache_l1_dsa_kidx, page_table), w1, final_norm_g,
        tl=tl_s, tq=tq)
    return (y_p, y_s, nsa_kv_p, nsa_kv_s, win_p, win_s, conv_p, conv_s, h_p, h_s,
            diff_kv_p, diff_kv_s, dsa_kv_p, dsa_kv_s, kidx_p, kidx_s)
```

```python
import functools
import math

import jax
import jax.numpy as jnp
import numpy as np
from jax import lax
from jax.experimental import pallas as pl
from jax.experimental.pallas import tpu as pltpu

F32 = jnp.float32
BF16 = jnp.bfloat16
I32 = jnp.int32

PAGE_SIZE = 128
HEAD_DIM = 64
NSA_HEADS = 8
NSA_KV_HEADS = 2
NSA_CMP_BLOCK = 32
NSA_SEL_BLOCK = 64
NSA_TOPN = 16
NSA_WINDOW = 512
FORCE_SCORE = 1e4
LRU_BLOCKS = 8
LRU_C = 8.0
CONV_WIDTH = 4
DIFF_HALF = 64
DIFF_HEADS = 4
DIFF_KV_HEADS = 2
DIFF_LAMBDA_INIT = 0.8 - 0.6 * math.exp(-0.3 * 1)
DSA_HEADS = 8
DSA_KV_HEADS = 2
IDX_HEADS = 4
IDX_DIM = 64
DSA_TOPK_MAX = 256
NORM_EPS = 1e-6
NEG = -1e30
REMOVED = -3e38
INT_MIN = -2 ** 31
LOG2E = math.log2(math.e)

LANE = 128
VMEM_LIMIT = 56 * 1024 * 1024
KV_TILE = 512
WIN_SPAN = NSA_WINDOW + LANE
PAGES_PER_STEP = 16


def _cparams(sem):
    return pltpu.CompilerParams(dimension_semantics=sem, vmem_limit_bytes=VMEM_LIMIT)


def _dot(a, b):
    return jnp.dot(a, b, preferred_element_type=F32)


def _dot_nt(a, b):
    return lax.dot_general(a, b, (((1,), (1,)), ((), ())), preferred_element_type=F32)


def _round_up(x, m):
    return (x + m - 1) // m * m


def _mod_kernel(c_ref, w_ref, b_ref, o_ref):
    o_ref[...] = jnp.dot(c_ref[...], w_ref[...], preferred_element_type=F32,
                         precision=lax.Precision.HIGHEST) + b_ref[...]


def _modulation(c, w, b):
    bc, d = c.shape
    n = w.shape[1]
    tn = 512
    return pl.pallas_call(
        _mod_kernel,
        grid=(n // tn,),
        in_specs=[pl.BlockSpec((bc, d), lambda j: (0, 0)),
                  pl.BlockSpec((d, tn), lambda j: (0, j)),
                  pl.BlockSpec((1, tn), lambda j: (0, j))],
        out_specs=pl.BlockSpec((bc, tn), lambda j: (0, j)),
        out_shape=jax.ShapeDtypeStruct((bc, n), F32),
        compiler_params=_cparams(("arbitrary",)),
        name="modulation",
    )(c, w, b.reshape(1, n))


def _proj_kernel(x_ref, g_ref, sh_ref, sc_ref, w_ref, *o_refs, segs, sigmoid_seg):
    x = x_ref[0]
    y = x * lax.rsqrt(jnp.mean(x * x, axis=-1, keepdims=True) + NORM_EPS)
    h = (y * g_ref[...]) * (1.0 + sc_ref[0]) + sh_ref[0]
    hb = h.astype(BF16)
    for i, ((a, b), o_ref) in enumerate(zip(segs, o_refs)):
        r = _dot(hb, w_ref[:, a:b])
        if i == sigmoid_seg:
            r = jax.nn.sigmoid(r)
        o_ref[0] = r


def _project(x, g, shift, scale, w, segs, sigmoid_seg, tl):
    b, l, d = x.shape
    ts = shift.shape[1]
    tm = 1 if ts == 1 else tl
    mod_map = (lambda bi, li: (bi, 0, 0)) if ts == 1 else (lambda bi, li: (bi, li, 0))
    p = w.shape[1]
    kern = functools.partial(_proj_kernel, segs=tuple(segs), sigmoid_seg=sigmoid_seg)
    return pl.pallas_call(
        kern,
        grid=(b, l // tl),
        in_specs=[pl.BlockSpec((1, tl, d), lambda bi, li: (bi, li, 0)),
                  pl.BlockSpec((1, d), lambda bi, li: (0, 0)),
                  pl.BlockSpec((1, tm, d), mod_map),
                  pl.BlockSpec((1, tm, d), mod_map),
                  pl.BlockSpec((d, p), lambda bi, li: (0, 0))],
        out_specs=[pl.BlockSpec((1, tl, e - a), lambda bi, li: (bi, li, 0)) for a, e in segs],
        out_shape=[jax.ShapeDtypeStruct((b, l, e - a), F32) for a, e in segs],
        compiler_params=_cparams(("parallel", "arbitrary")),
        name="norm_mod_project",
    )(x, g.reshape(1, d), shift, scale, w)


def _gather_kernel(pt_ref, *refs, pp, n_page_steps, layout, cw, c_all, n_cmp, n_out, has_new):
    page_refs = refs[:pp]
    pos = pp
    new_ref = None
    if has_new:
        new_ref = refs[pos]
        pos += 1
    w2_ref = None
    if n_cmp:
        w2_ref = refs[pos]
        pos += 1
    out_ref = refs[pos]
    cmp_ref = refs[pos + 1] if n_cmp else None
    j = pl.program_id(1)
    ones_col = jnp.where(lax.broadcasted_iota(I32, (PAGE_SIZE, LANE - cw), 1) == 0, 1.0, 0.0) if cw < LANE else None

    def chunk(i, c):
        if layout == "rows":
            return page_refs[i][0, :, c * cw:(c + 1) * cw]
        if layout == "cols":
            return page_refs[i][0, c].T
        return page_refs[i][0, pl.ds(c, PAGE_SIZE, stride=c_all), :]

    def emit(i, c, x):
        if c < n_cmp:
            nb = PAGE_SIZE // NSA_SEL_BLOCK
            prod = x.reshape(nb, NSA_SEL_BLOCK, cw) * w2_ref[c][None]
            cmp_ref[0, c, 0, i * nb:(i + 1) * nb, :] = jnp.sum(prod[:, :NSA_CMP_BLOCK], axis=1)
            cmp_ref[0, c, 1, i * nb:(i + 1) * nb, :] = jnp.sum(prod[:, NSA_CMP_BLOCK:], axis=1)
        else:
            if ones_col is not None:
                x = jnp.concatenate([x, ones_col], axis=-1)
            out_ref[0, c - n_cmp, i * PAGE_SIZE:(i + 1) * PAGE_SIZE, :] = x.astype(BF16)

    def pages():
        for i in range(pp):
            for c in range(n_cmp + n_out):
                emit(i, c, chunk(i, c))

    if has_new:
        pl.when(j < n_page_steps)(pages)

        @pl.when(j >= n_page_steps)
        def _():
            new = new_ref[0]
            for c in range(n_cmp + n_out):
                xc = new[:, c * cw:(c + 1) * cw]
                emit(0, c, jnp.concatenate([xc, jnp.zeros((PAGE_SIZE - xc.shape[0], cw), F32)], axis=0))
                for i in range(1, pp):
                    emit(i, c, jnp.zeros((PAGE_SIZE, cw), F32))
    else:
        pages()


def _gather_chunks(pool, layout, page_table, new, *, cw, c_all, n_out, n_cmp=0, cmp_w=None):
    bk, n_pages = page_table.shape
    pp = min(PAGES_PER_STEP, n_pages)
    assert n_pages % pp == 0
    n_page_steps = n_pages // pp
    has_new = new is not None
    n_steps = n_page_steps + (1 if has_new else 0)
    rows = pp * PAGE_SIZE
    lk_pad = n_steps * rows
    page_block = (1,) + pool.shape[1:]
    zeros = (0,) * (len(page_block) - 1)

    def page_map(i):
        def f(b, j, pt):
            return (pt[b, jnp.minimum(j * pp + i, n_pages - 1)],) + zeros
        return f

    in_specs = [pl.BlockSpec(page_block, page_map(i)) for i in range(pp)]
    args = [pool] * pp
    if has_new:
        in_specs.append(pl.BlockSpec((1,) + new.shape[1:], lambda b, j, pt: (b, 0, 0)))
        args.append(new)
    if n_cmp:
        in_specs.append(pl.BlockSpec(cmp_w.shape, lambda b, j, pt: (0, 0, 0)))
        args.append(cmp_w)
    out_specs = [pl.BlockSpec((1, n_out, rows, LANE), lambda b, j, pt: (b, 0, j, 0))]
    out_shape = [jax.ShapeDtypeStruct((bk, n_out, lk_pad, LANE), BF16)]
    if n_cmp:
        nb = rows // NSA_SEL_BLOCK
        out_specs.append(pl.BlockSpec((1, n_cmp, 2, nb, HEAD_DIM), lambda b, j, pt: (b, 0, 0, j, 0)))
        out_shape.append(jax.ShapeDtypeStruct((bk, n_cmp, 2, lk_pad // NSA_SEL_BLOCK, HEAD_DIM), F32))
    kern = functools.partial(_gather_kernel, pp=pp, n_page_steps=n_page_steps, layout=layout, cw=cw,
                             c_all=c_all, n_cmp=n_cmp, n_out=n_out, has_new=has_new)
    outs = pl.pallas_call(
        kern,
        grid_spec=pltpu.PrefetchScalarGridSpec(
            num_scalar_prefetch=1, grid=(bk, n_steps), in_specs=in_specs, out_specs=out_specs),
        out_shape=out_shape,
        compiler_params=_cparams(("parallel", "arbitrary")),
        name="gather_pages",
    )(page_table, *args)
    return outs if n_cmp else outs[0]


def _identity_pages(x):
    b, l, w = x.shape
    n_pages = l // PAGE_SIZE
    pool = x.reshape(b * n_pages, PAGE_SIZE, w)
    table = jnp.arange(b * n_pages, dtype=I32).reshape(b, n_pages)
    return pool, table


def _flash_tile(s_all, bias, v_tile, carry, n_heads, tq, l_in_acc):
    m, l, acc = carry
    s = s_all
    if bias is not None:
        s = jnp.concatenate([s_all[h * tq:(h + 1) * tq] + bias for h in range(n_heads)], axis=0)
    m_new = jnp.maximum(m, jnp.max(s, axis=-1, keepdims=True))
    p = jnp.exp2(s - m_new)
    alpha = jnp.exp2(m - m_new)
    acc = alpha * acc + _dot(p.astype(BF16), v_tile)
    if not l_in_acc:
        l = alpha * l + jnp.sum(p, axis=-1, keepdims=True)
    return m_new, l, acc


def _flash_init(rows, dv):
    return (jnp.full((rows, 1), NEG, F32), jnp.zeros((rows, 1), F32), jnp.zeros((rows, dv), F32))


def _flash_out(acc):
    return acc[:, :HEAD_DIM] * (1.0 / jnp.maximum(acc[:, HEAD_DIM:HEAD_DIM + 1], 1e-30))


def _stack_heads(q, n, width):
    return jnp.concatenate([q[:, h * width:(h + 1) * width] for h in range(n)], axis=0)


def _pad_lanes(x):
    return jnp.concatenate([x, jnp.zeros((x.shape[0], LANE - x.shape[1]), x.dtype)], axis=-1)


def _nsa_kernel(q_ref, g_ref, cmp_ref, ksel_ref, kwin_ref, o_ref, *,
                tq, tk, q_pos0, win_pos0, nsp, n_top):
    G = NSA_KV_HEADS
    hpg = NSA_HEADS // G
    gw = hpg * HEAD_DIM
    qi = pl.program_id(1)
    q0 = q_pos0 + qi * tq
    qpos = q0 + lax.broadcasted_iota(I32, (tq, 1), 0)
    qpos_r = jnp.concatenate([qpos] * hpg, axis=0)
    rows = hpg * tq
    q_all = q_ref[0] * (HEAD_DIM ** -0.5 * LOG2E)
    blk = lax.broadcasted_iota(I32, (1, nsp), 1)
    blk_f = blk.astype(F32)
    cur = qpos // NSA_SEL_BLOCK
    vis_e = (blk * NSA_SEL_BLOCK + (NSA_CMP_BLOCK - 1)) <= qpos_r
    vis_o = (blk * NSA_SEL_BLOCK + (NSA_SEL_BLOCK - 1)) <= qpos_r
    start = pl.multiple_of(jnp.maximum(q0 - NSA_WINDOW - win_pos0, 0), 8)
    kpos_w = win_pos0 + start + lax.broadcasted_iota(I32, (1, WIN_SPAN), 1)
    dlt = qpos - kpos_w
    bias_w = jnp.where(dlt >= 0, jnp.where(dlt < NSA_WINDOW, 0.0, NEG), NEG)

    def prologue(g):
        qs64 = _stack_heads(q_all[:, g * gw:(g + 1) * gw], hpg, HEAD_DIM)
        qs = _pad_lanes(qs64).astype(BF16)
        qs64 = qs64.astype(BF16)

        s_w = _dot_nt(qs, kwin_ref[0, g, pl.ds(start, WIN_SPAN), :])
        _, _, acc_w = _flash_tile(s_w, bias_w, kwin_ref[0, G + g, pl.ds(start, WIN_SPAN), :],
                                  _flash_init(rows, LANE), hpg, tq, True)
        o_w = _flash_out(acc_w)

        kce = cmp_ref[0, 2 * g, 0].astype(BF16)
        kco = cmp_ref[0, 2 * g, 1].astype(BF16)
        vce = cmp_ref[0, 2 * g + 1, 0].astype(BF16)
        vco = cmp_ref[0, 2 * g + 1, 1].astype(BF16)
        s_e = jnp.where(vis_e, _dot_nt(qs64, kce), NEG)
        s_o = jnp.where(vis_o, _dot_nt(qs64, kco), NEG)
        m = jnp.maximum(jnp.max(s_e, axis=-1, keepdims=True), jnp.max(s_o, axis=-1, keepdims=True))
        p_e = jnp.where(vis_e, jnp.exp2(s_e - m), 0.0)
        p_o = jnp.where(vis_o, jnp.exp2(s_o - m), 0.0)
        den = jnp.sum(p_e, axis=-1, keepdims=True) + jnp.sum(p_o, axis=-1, keepdims=True)
        inv = 1.0 / jnp.maximum(den, 1e-30)
        p_e = p_e * inv
        p_o = p_o * inv
        o_c = _dot(p_e.astype(BF16), vce) + _dot(p_o.astype(BF16), vco)

        pe_h = sum(p_e[h * tq:(h + 1) * tq] for h in range(hpg))
        po_h = sum(p_o[h * tq:(h + 1) * tq] for h in range(hpg))
        imp = pe_h + po_h
        imp = jnp.where((blk == cur) | (blk == 0), FORCE_SCORE, imp)
        imp = jnp.where(blk <= cur, imp, NEG)
        sel = jnp.zeros((tq, nsp), F32)
        for _ in range(n_top):
            mx = jnp.max(imp, axis=-1, keepdims=True)
            first = jnp.min(jnp.where(imp == mx, blk_f, float(nsp)), axis=-1, keepdims=True)
            pick = blk_f == first
            sel = jnp.where(pick & (mx > 0.5 * NEG), 1.0, sel)
            imp = jnp.where(pick, REMOVED, imp)
        return qs, o_c, o_w, sel.astype(BF16)

    pro = [prologue(g) for g in range(G)]

    blk_col = lax.broadcasted_iota(I32, (nsp, 1), 0)

    def sel_step(j, carry):
        k0 = pl.multiple_of(j * tk, tk)
        kpos = k0 + lax.broadcasted_iota(I32, (1, tk), 1)
        expand = jnp.where(blk_col == kpos // NSA_SEL_BLOCK, 1.0, 0.0).astype(BF16)
        out = []
        for g in range(G):
            picked = _dot(pro[g][3], expand)
            bias = jnp.where(kpos <= qpos, jnp.where(picked > 0.5, 0.0, NEG), NEG)
            s = _dot_nt(pro[g][0], ksel_ref[0, g, pl.ds(k0, tk), :])
            out.append(_flash_tile(s, bias, ksel_ref[0, G + g, pl.ds(k0, tk), :], carry[g], hpg, tq, True))
        return tuple(out)

    n_tiles = (q0 + tq - 1) // tk + 1
    res = lax.fori_loop(0, n_tiles, sel_step, tuple(_flash_init(rows, LANE) for _ in range(G)))

    gates = g_ref[0]
    outs = []
    for g in range(G):
        _, o_c, o_w, _ = pro[g]
        o_s = _flash_out(res[g][2])
        for h in range(hpg):
            r = slice(h * tq, (h + 1) * tq)
            c = g * LANE + 3 * h
            outs.append(gates[:, c:c + 1] * o_c[r] + gates[:, c + 1:c + 2] * o_s[r]
                        + gates[:, c + 2:c + 3] * o_w[r])
    o_ref[0] = jnp.concatenate(outs, axis=-1)


def _nsa_attention(q, gates, cmp, ksel, kwin, *, tq, q_pos0, lk, win_pos0):
    bk, lq, _ = q.shape
    G = NSA_KV_HEADS
    nsp = cmp.shape[3]
    lk_pad = ksel.shape[2]
    lw_pad = kwin.shape[2]
    tk = min(KV_TILE, lk_pad)
    ns = -(-lk // NSA_SEL_BLOCK)
    assert tq <= LANE and lw_pad >= WIN_SPAN and q_pos0 + lq <= lk_pad
    kern = functools.partial(_nsa_kernel, tq=tq, tk=tk, q_pos0=q_pos0, win_pos0=win_pos0,
                             nsp=nsp, n_top=min(NSA_TOPN, ns))
    qw = NSA_HEADS * HEAD_DIM
    return pl.pallas_call(
        kern,
        grid=(bk, lq // tq),
        in_specs=[pl.BlockSpec((1, tq, qw), lambda b, i: (b, i, 0)),
                  pl.BlockSpec((1, tq, G * LANE), lambda b, i: (b, i, 0)),
                  pl.BlockSpec((1, 2 * G, 2, nsp, HEAD_DIM), lambda b, i: (b, 0, 0, 0, 0)),
                  pl.BlockSpec((1, 2 * G, lk_pad, LANE), lambda b, i: (b, 0, 0, 0)),
                  pl.BlockSpec((1, 2 * G, lw_pad, LANE), lambda b, i: (b, 0, 0, 0))],
        out_specs=pl.BlockSpec((1, tq, qw), lambda b, i: (b, i, 0)),
        out_shape=jax.ShapeDtypeStruct((bk, lq, qw), F32),
        compiler_params=_cparams(("parallel", "arbitrary")),
        name="nsa_attention",
    )(q, gates, cmp, ksel, kwin)


def _shift_rows(x, d, fill):
    rolled = pltpu.roll(x, d, axis=0)
    row = lax.broadcasted_iota(I32, x.shape, 0)
    return jnp.where(row >= d, rolled, fill)


def _lru_kernel(x_ref, hist_ref, h0_ref, cw_ref, cb_ref, wr_ref, br_ref, wi_ref, bi_ref, lam_ref,
                o_ref, hl_ref, tail_ref, h_ref, *, tl, last_row):
    li = pl.program_id(1)

    @pl.when(li == 0)
    def _():
        tail_ref[...] = jnp.concatenate(
            [jnp.zeros((8 - (CONV_WIDTH - 1), x_ref.shape[-1]), F32), hist_ref[0]], axis=0)
        h_ref[...] = h0_ref[0]

    x = x_ref[0]
    xp = jnp.concatenate([tail_ref[...], x], axis=0)
    cw = cw_ref[...]
    conv = sum(xp[8 - (CONV_WIDTH - 1) + j:8 - (CONV_WIDTH - 1) + j + tl] * cw[j:j + 1]
               for j in range(CONV_WIDTH))
    conv = cb_ref[...] + conv
    tail_ref[...] = x[tl - 8:tl]

    cb16 = conv.astype(BF16)
    r = jax.nn.sigmoid(_dot(cb16, wr_ref[...]) + br_ref[...])
    ig = jax.nn.sigmoid(_dot(cb16, wi_ref[...]) + bi_ref[...])
    log_a = -LRU_C * r * jax.nn.softplus(-lam_ref[...])
    a = jnp.exp(log_a)
    th = jnp.tanh(log_a)
    b = jnp.sqrt(-2.0 * th / (1.0 - th)) * (ig * conv)

    d = 1
    while d < tl:
        a_prev = _shift_rows(a, d, 1.0)
        b_prev = _shift_rows(b, d, 0.0)
        b = a * b_prev + b
        a = a * a_prev
        d *= 2
    h = a * h_ref[...] + b
    o_ref[0] = h
    h_ref[...] = h[tl - 1:tl]

    @pl.when(li == pl.num_programs(1) - 1)
    def _():
        hl_ref[0] = h[last_row:last_row + 1]


def _block_diag(w):
    nb, bw, _ = w.shape
    eye = jnp.eye(nb, dtype=w.dtype)
    return (eye[:, None, :, None] * w[:, :, None, :]).reshape(nb * bw, nb * bw)


def _conv_rglru(x_b, hist, h0, conv_w, conv_b, w_r, b_r, w_i, b_i, lam, *, tl, n_valid):
    b, l, w = x_b.shape
    assert tl >= 8 and l % tl == 0 and n_valid > l - tl
    kern = functools.partial(_lru_kernel, tl=tl, last_row=(n_valid - 1) % tl)
    vec = lambda: pl.BlockSpec((1, w), lambda bi, li: (0, 0))
    h, h_last = pl.pallas_call(
        kern,
        grid=(b, l // tl),
        in_specs=[pl.BlockSpec((1, tl, w), lambda bi, li: (bi, li, 0)),
                  pl.BlockSpec((1, CONV_WIDTH - 1, w), lambda bi, li: (bi, 0, 0)),
                  pl.BlockSpec((1, 1, w), lambda bi, li: (bi, 0, 0)),
                  pl.BlockSpec((CONV_WIDTH, w), lambda bi, li: (0, 0)),
                  vec(),
                  pl.BlockSpec((w, w), lambda bi, li: (0, 0)), vec(),
                  pl.BlockSpec((w, w), lambda bi, li: (0, 0)), vec(), vec()],
        out_specs=[pl.BlockSpec((1, tl, w), lambda bi, li: (bi, li, 0)),
                   pl.BlockSpec((1, 1, w), lambda bi, li: (bi, 0, 0))],
        out_shape=[jax.ShapeDtypeStruct((b, l, w), F32), jax.ShapeDtypeStruct((b, 1, w), F32)],
        scratch_shapes=[pltpu.VMEM((8, w), F32), pltpu.VMEM((1, w), F32)],
        compiler_params=_cparams(("parallel", "arbitrary")),
        name="conv_rglru",
    )(x_b, hist, h0.reshape(b, 1, w), conv_w, conv_b.reshape(1, w),
      _block_diag(w_r).astype(BF16), b_r.reshape(1, w), _block_diag(w_i).astype(BF16), b_i.reshape(1, w),
      lam.reshape(1, w))
    return h, h_last.reshape(b, w)


def _out_kernel(oa_ref, za_ref, ob_ref, zb_ref, x_ref, gate_ref, w_ref, fg_ref, o_ref, *, final_norm):
    half = oa_ref.shape[-1]
    ma = (oa_ref[0] * jax.nn.silu(za_ref[0])).astype(BF16)
    mb = (ob_ref[0] * jax.nn.silu(zb_ref[0])).astype(BF16)
    y = _dot(ma, w_ref[0:half, :]) + _dot(mb, w_ref[half:2 * half, :])
    out = x_ref[0] + gate_ref[0] * y
    if final_norm:
        out = out * lax.rsqrt(jnp.mean(out * out, axis=-1, keepdims=True) + NORM_EPS) * fg_ref[...]
    o_ref[0] = out


def _out_project(o_a, z_a, o_b, z_b, x, gate, w_out, final_g, *, tl, final_norm):
    b, l, d = x.shape
    half = o_a.shape[-1]
    ts = gate.shape[1]
    tm = 1 if ts == 1 else tl
    mod_map = (lambda bi, li: (bi, 0, 0)) if ts == 1 else (lambda bi, li: (bi, li, 0))
    act = lambda: pl.BlockSpec((1, tl, half), lambda bi, li: (bi, li, 0))
    return pl.pallas_call(
        functools.partial(_out_kernel, final_norm=final_norm),
        grid=(b, l // tl),
        in_specs=[act(), act(), act(), act(),
                  pl.BlockSpec((1, tl, d), lambda bi, li: (bi, li, 0)),
                  pl.BlockSpec((1, tm, d), mod_map),
                  pl.BlockSpec((2 * half, d), lambda bi, li: (0, 0)),
                  pl.BlockSpec((1, d), lambda bi, li: (0, 0))],
        out_specs=pl.BlockSpec((1, tl, d), lambda bi, li: (bi, li, 0)),
        out_shape=jax.ShapeDtypeStruct((b, l, d), F32),
        compiler_params=_cparams(("parallel", "arbitrary")),
        name="out_project",
    )(o_a, z_a, o_b, z_b, x, gate, w_out, final_g.reshape(1, d))


def _diff_kernel(q_ref, k_ref, v_ref, lamv_ref, subg_ref, o_ref, *, tq, tk, q_pos0):
    hpg = DIFF_HEADS // DIFF_KV_HEADS
    qi = pl.program_id(2)
    q0 = q_pos0 + qi * tq
    n_maps = 2 * hpg
    rows = n_maps * tq
    qpos = q0 + lax.broadcasted_iota(I32, (tq, 1), 0)
    q = q_ref[0] * (DIFF_HALF ** -0.5 * LOG2E)
    zero = jnp.zeros((tq, DIFF_HALF), F32)
    parts = []
    for mp in range(2):
        for h in range(hpg):
            qh = q[:, (2 * h + mp) * DIFF_HALF:(2 * h + mp + 1) * DIFF_HALF]
            parts.append(jnp.concatenate([qh, zero] if mp == 0 else [zero, qh], axis=-1))
    qs = jnp.concatenate(parts, axis=0).astype(BF16)

    def step(masked):
        def f(j, carry):
            k0 = pl.multiple_of(j * tk, tk)
            s = _dot_nt(qs, k_ref[0, 0, pl.ds(k0, tk), :])
            bias = None
            if masked:
                kpos = k0 + lax.broadcasted_iota(I32, (1, tk), 1)
                bias = jnp.where(kpos <= qpos, 0.0, NEG)
            return _flash_tile(s, bias, v_ref[0, 0, pl.ds(k0, tk), :], carry, n_maps, tq, False)
        return f

    n_full = (q0 + 1) // tk
    n_tiles = (q0 + tq - 1) // tk + 1
    carry = lax.fori_loop(0, n_full, step(False), _flash_init(rows, 2 * DIFF_HALF))
    m, l, acc = lax.fori_loop(n_full, n_tiles, step(True), carry)
    o = acc * (1.0 / jnp.maximum(l, 1e-30))
    lq = lamv_ref[...]
    lam = (jnp.exp(jnp.sum(lq[0:1] * lq[1:2], axis=-1, keepdims=True))
           - jnp.exp(jnp.sum(lq[2:3] * lq[3:4], axis=-1, keepdims=True)) + DIFF_LAMBDA_INIT)
    half = hpg * tq
    od = o[0:half] - lam * o[half:2 * half]
    od = od * lax.rsqrt(jnp.mean(od * od, axis=-1, keepdims=True) + NORM_EPS)
    od = od * subg_ref[...] * (1.0 - DIFF_LAMBDA_INIT)
    o_ref[0] = jnp.concatenate([od[h * tq:(h + 1) * tq] for h in range(hpg)], axis=-1)


def _diff_attention(q, kv, lamv, subln_g, *, tq, q_pos0):
    bk, lq, _ = q.shape
    G = DIFF_KV_HEADS
    lk_pad = kv.shape[2]
    tk = min(KV_TILE, lk_pad)
    gw = (DIFF_HEADS // G) * 2 * DIFF_HALF
    assert q_pos0 + lq <= lk_pad
    return pl.pallas_call(
        functools.partial(_diff_kernel, tq=tq, tk=tk, q_pos0=q_pos0),
        grid=(bk, G, lq // tq),
        in_specs=[pl.BlockSpec((1, tq, gw), lambda b, g, i: (b, i, g)),
                  pl.BlockSpec((1, 1, lk_pad, 2 * DIFF_HALF), lambda b, g, i: (b, g, 0, 0)),
                  pl.BlockSpec((1, 1, lk_pad, 2 * DIFF_HALF), lambda b, g, i: (b, G + g, 0, 0)),
                  pl.BlockSpec((4, DIFF_HALF), lambda b, g, i: (0, 0)),
                  pl.BlockSpec((1, 2 * DIFF_HALF), lambda b, g, i: (0, 0))],
        out_specs=pl.BlockSpec((1, tq, gw), lambda b, g, i: (b, i, g)),
        out_shape=jax.ShapeDtypeStruct((bk, lq, DIFF_HEADS * 2 * DIFF_HALF), F32),
        compiler_params=_cparams(("parallel", "parallel", "arbitrary")),
        name="diff_attention",
    )(q, kv, kv, lamv, subln_g.reshape(1, 2 * DIFF_HALF))


def _dsa_kernel(q_ref, qi_ref, kw_ref, kidx_ref, kv_ref, o_ref, key_ref, *, tq, tk, q_pos0, n_sel, lk_pad):
    G = DSA_KV_HEADS
    hpg = DSA_HEADS // G
    qblk = pl.program_id(1)
    q0 = q_pos0 + qblk * tq
    qpos = q0 + lax.broadcasted_iota(I32, (tq, 1), 0)
    n_tiles = (q0 + tq - 1) // tk + 1
    lanes = tk // LANE

    qidx = _pad_lanes(_stack_heads(qi_ref[0] * (IDX_DIM ** -0.5), IDX_HEADS, IDX_DIM)).astype(BF16)
    wi = kw_ref[0][:, IDX_DIM:IDX_DIM + IDX_HEADS] * (IDX_HEADS ** -0.5)

    def score_step(j, kmax):
        k0 = pl.multiple_of(j * tk, tk)
        s_all = _dot_nt(qidx, kidx_ref[0, 0, pl.ds(k0, tk), :])
        score = jnp.zeros((tq, tk), F32)
        for h in range(IDX_HEADS):
            score = score + wi[:, h:h + 1] * jnp.maximum(s_all[h * tq:(h + 1) * tq], 0.0)
        kpos = k0 + lax.broadcasted_iota(I32, (1, tk), 1)
        bits = pltpu.bitcast(score, I32)
        key = jnp.where(bits < 0, bits ^ 0x7FFFFFFF, bits)
        key = jnp.where(score == 0.0, 0, key)
        key = jnp.where(kpos <= qpos, jnp.where(score > 0.5 * NEG, key, INT_MIN), INT_MIN)
        key_ref[j] = key
        return jnp.maximum(kmax, jnp.max(key, axis=-1, keepdims=True))

    kmax = lax.fori_loop(0, n_tiles, score_step, jnp.full((tq, 1), INT_MIN, I32))

    def count(pred):
        def f(j, acc):
            key = key_ref[j]
            kpos = j * tk + lax.broadcasted_iota(I32, (1, tk), 1)
            hit = jnp.where(pred(key, kpos), 1, 0)
            for c in range(lanes):
                acc = acc + hit[:, c * LANE:(c + 1) * LANE]
            return acc
        acc = lax.fori_loop(0, n_tiles, f, jnp.zeros((tq, LANE), I32))
        return jnp.sum(acc, axis=-1, keepdims=True)

    def unfinished(lo_hi):
        lo, hi = lo_hi
        return jnp.max(jnp.where(lo < hi, 1, 0)) > 0

    def bisect(lo_hi):
        lo, hi = lo_hi
        mid = (lo >> 1) + (hi >> 1) + ((lo | hi) & 1)
        cnt = count(lambda key, kpos: key >= mid)
        lo = jnp.where(cnt >= n_sel, mid, lo)
        hi = jnp.where(cnt > n_sel, hi, jnp.where(cnt == n_sel, mid, mid - 1))
        return lo, hi

    thr, _ = lax.while_loop(unfinished, bisect, (jnp.full((tq, 1), INT_MIN, I32), kmax))

    n_gt = count(lambda key, kpos: key > thr)
    n_eq = count(lambda key, kpos: key == thr)
    need = n_sel - n_gt
    tie = (n_eq > need) & (thr > INT_MIN)
    any_tie = jnp.max(jnp.where(tie, 1, 0)) > 0

    def tie_bound():
        def f(i, lo_hi):
            lo, hi = lo_hi
            mid = (lo + hi) // 2
            cnt = count(lambda key, kpos: (key == thr) & (kpos <= mid))
            ok = cnt >= need
            return jnp.where(ok, lo, mid + 1), jnp.where(ok, mid, hi)
        n_it = int(math.ceil(math.log2(lk_pad))) + 1
        lo, hi = lax.fori_loop(0, n_it, f, (jnp.zeros((tq, 1), I32), jnp.full((tq, 1), lk_pad - 1, I32)))
        return jnp.where(tie, hi, lk_pad)

    jmax = lax.cond(any_tie, tie_bound, lambda: jnp.full((tq, 1), lk_pad, I32))
    thr_adm = jnp.maximum(thr, INT_MIN + 1)

    q = q_ref[0] * (HEAD_DIM ** -0.5 * LOG2E)
    qs = [_pad_lanes(_stack_heads(q[:, g * hpg * HEAD_DIM:(g + 1) * hpg * HEAD_DIM], hpg, HEAD_DIM)).astype(BF16)
          for g in range(G)]
    rows = hpg * tq

    def att_step(j, carry):
        k0 = pl.multiple_of(j * tk, tk)
        key = key_ref[j]
        kpos = k0 + lax.broadcasted_iota(I32, (1, tk), 1)
        bias = jnp.where(key > thr_adm, 0.0, jnp.where((key == thr_adm) & (kpos <= jmax), 0.0, NEG))
        out = []
        for g in range(G):
            s = _dot_nt(qs[g], kv_ref[0, g, pl.ds(k0, tk), :])
            out.append(_flash_tile(s, bias, kv_ref[0, G + g, pl.ds(k0, tk), :], carry[g], hpg, tq, True))
        return tuple(out)

    res = lax.fori_loop(0, n_tiles, att_step, tuple(_flash_init(rows, LANE) for _ in range(G)))
    outs = []
    for g in range(G):
        o = _flash_out(res[g][2])
        outs.extend(o[h * tq:(h + 1) * tq] for h in range(hpg))
    o_ref[0] = jnp.concatenate(outs, axis=-1)


def _dsa_attention(q, qi, kw, kidx, kv, *, tq, q_pos0, lk):
    bk, lq, _ = q.shape
    lk_pad = kv.shape[2]
    tk = min(KV_TILE, lk_pad)
    n_sel = min(DSA_TOPK_MAX, lk // 4)
    assert q_pos0 + lq <= lk_pad and tk >= n_sel
    kern = functools.partial(_dsa_kernel, tq=tq, tk=tk, q_pos0=q_pos0, n_sel=n_sel, lk_pad=lk_pad)
    return pl.pallas_call(
        kern,
        grid=(bk, lq // tq),
        in_specs=[pl.BlockSpec((1, tq, DSA_HEADS * HEAD_DIM), lambda b, i: (b, i, 0)),
                  pl.BlockSpec((1, tq, IDX_HEADS * IDX_DIM), lambda b, i: (b, i, 0)),
                  pl.BlockSpec((1, tq, LANE), lambda b, i: (b, i, 0)),
                  pl.BlockSpec((1, 1, lk_pad, LANE), lambda b, i: (b, 0, 0, 0)),
                  pl.BlockSpec((1, 4, lk_pad, LANE), lambda b, i: (b, 0, 0, 0))],
        out_specs=pl.BlockSpec((1, tq, DSA_HEADS * HEAD_DIM), lambda b, i: (b, i, 0)),
        out_shape=jax.ShapeDtypeStruct((bk, lq, DSA_HEADS * HEAD_DIM), F32),
        scratch_shapes=[pltpu.VMEM((lk_pad // tk, tq, tk), I32)],
        compiler_params=_cparams(("parallel", "arbitrary")),
        name="dsa_attention",
    )(q, qi, kw, kidx, kv)


L0_SIZES = (512, 768, 24, 512, 512, 512)
L1_SIZES = (512, 256, 256, 512, 512, 128, 128, 256, 64, 4, 512)


def _l0_weight(w_in):
    d = w_in.shape[0]
    q, kv6, gl, z_a, x_b, z_b = jnp.split(w_in, np.cumsum(L0_SIZES)[:-1].tolist(), axis=1)
    pad = jnp.zeros((d, LANE - 12), w_in.dtype)
    w = jnp.concatenate([q, kv6, z_a, x_b, z_b, gl[:, :12], pad, gl[:, 12:], pad], axis=1)
    segs = [(0, 512), (512, 1024), (1024, 1280), (1280, 1792), (1792, 2304), (2304, 2816), (2816, 3072)]
    return w.astype(BF16), segs


def _l1_weight(w_in):
    d = w_in.shape[0]
    qc, kc, vc, z_c, qd, kd, vd, qi, ki, wi, z_d = jnp.split(w_in, np.cumsum(L1_SIZES)[:-1].tolist(), axis=1)
    pad = jnp.zeros((d, LANE - IDX_DIM - IDX_HEADS), w_in.dtype)
    w = jnp.concatenate([qc, kc, vc, z_c, qd, kd, vd, qi, z_d, ki, wi, pad], axis=1)
    segs = [(0, 512), (512, 1024), (1024, 1536), (1536, 2048), (2048, 2304), (2304, 2560), (2560, 3072),
            (3072, 3200)]
    return w.astype(BF16), segs


def _pad_rows(x, n):
    return jnp.pad(x, ((0, 0), (0, n - x.shape[1]), (0, 0)))


def _cols_pool(pool):
    npool, ps = pool.shape[:2]
    cw = pool.shape[-1]
    perm = (0,) + tuple(range(2, pool.ndim)) + (1,)
    return jnp.transpose(pool, perm).reshape(npool, -1, cw, ps)


def _layer0(x, mod, past, w, *, tl, tq):
    (norm_g, w_in, cmp_wk, cmp_wv, conv_w, conv_b, lru_wr, lru_br, lru_wi, lru_bi, lru_lambda, w_out) = w
    shift, scale, gate = mod
    b, l, d = x.shape
    w_p, segs = _l0_weight(w_in)
    flat = shift.shape[1] != 1
    xin = x.reshape(1, b * l, d) if flat else x
    q, kvp, kvw, z_a, x_b, z_b, gates = _project(xin, norm_g, shift, scale, w_p, segs, 6, tl)
    if flat:
        q, kvp, kvw, z_a, x_b, z_b, gates = (t.reshape(b, l, -1) for t in (q, kvp, kvw, z_a, x_b, z_b, gates))
    wk2 = jnp.concatenate([cmp_wk, cmp_wk], axis=0)
    wv2 = jnp.concatenate([cmp_wv, cmp_wv], axis=0)
    w2 = jnp.stack([wk2, wk2, wv2, wv2], axis=0)
    nsa = dict(cw=HEAD_DIM, c_all=8, n_out=4, n_cmp=4, cmp_w=w2)
    lq = _round_up(l, 8)
    if past is None:
        ppool, ptable = _identity_pages(kvp)
        ksel, cmp = _gather_chunks(ppool, "rows", ptable, None, **nsa)
        wpool, wtable = _identity_pages(kvw)
        kwin = _gather_chunks(wpool, "rows", wtable, None, cw=HEAD_DIM, c_all=4, n_out=4)
        kv_win = kvw
        q_pos0, lk, win_pos0 = 0, l, 0
        hist = jnp.zeros((b, CONV_WIDTH - 1, x_b.shape[-1]), F32)
        h0 = jnp.zeros((b, x_b.shape[-1]), F32)
    else:
        pool, table, win_buf, hist, h0 = past
        assert l < NSA_CMP_BLOCK
        past_len = table.shape[1] * PAGE_SIZE
        ksel, cmp = _gather_chunks(_cols_pool(pool), "cols", table, _pad_rows(kvp, 8), **nsa)
        kv_win = jnp.concatenate([win_buf.reshape(b, win_buf.shape[1], -1), kvw], axis=1)
        lw_pad = _round_up(max(kv_win.shape[1], WIN_SPAN), PAGE_SIZE)
        wpool, wtable = _identity_pages(_pad_rows(kv_win, lw_pad))
        kwin = _gather_chunks(wpool, "rows", wtable, None, cw=HEAD_DIM, c_all=4, n_out=4)
        q_pos0, lk, win_pos0 = past_len, past_len + l, past_len - win_buf.shape[1]
    nsp = _round_up(cmp.shape[3], LANE)
    cmp = jnp.pad(cmp, ((0, 0), (0, 0), (0, 0), (0, nsp - cmp.shape[3]), (0, 0)))
    cmp = cmp.reshape(b, 2, 2, 2, nsp, HEAD_DIM).transpose(0, 2, 1, 3, 4, 5).reshape(b, 4, 2, nsp, HEAD_DIM)
    o_a = _nsa_attention(_pad_rows(q, lq), _pad_rows(gates, lq), cmp, ksel, kwin,
                         tq=min(tq, lq), q_pos0=q_pos0, lk=lk, win_pos0=win_pos0)[:, :l]
    o_b, h_last = _conv_rglru(_pad_rows(x_b, lq), hist, h0, conv_w, conv_b, lru_wr, lru_br, lru_wi, lru_bi,
                              lru_lambda, tl=min(256, lq), n_valid=l)
    o_b = o_b[:, :l]
    fl = (lambda t: t.reshape(1, b * l, -1)) if flat else (lambda t: t)
    x_new = _out_project(fl(o_a), fl(z_a), fl(o_b), fl(z_b), xin, gate, w_out.astype(BF16),
                         jnp.ones((d,), F32), tl=tl, final_norm=False).reshape(b, l, d)
    win_keep = min(NSA_WINDOW, kv_win.shape[1])
    conv_src = jnp.concatenate([hist, x_b], axis=1) if l < CONV_WIDTH - 1 else x_b
    states = (kvp.reshape(b, l, 4, NSA_KV_HEADS, HEAD_DIM),
              kv_win[:, -win_keep:].reshape(b, win_keep, 2, NSA_KV_HEADS, HEAD_DIM),
              conv_src[:, -(CONV_WIDTH - 1):], h_last)
    return x_new, states


def _layer1(x, mod, past, w, final_g, *, tl, tq):
    (norm_g, w_in, lam_q1, lam_k1, lam_q2, lam_k2, subln_g, w_out) = w
    shift, scale, gate = mod
    b, l, d = x.shape
    w_p, segs = _l1_weight(w_in)
    flat = shift.shape[1] != 1
    xin = x.reshape(1, b * l, d) if flat else x
    qc, kvc, z_c, qd, kvd, qi, z_d, kiw = _project(xin, norm_g, shift, scale, w_p, segs, -1, tl)
    if flat:
        qc, kvc, z_c, qd, kvd, qi, z_d, kiw = (t.reshape(b, l, -1) for t in (qc, kvc, z_c, qd, kvd, qi, z_d, kiw))
    lq = _round_up(l, 8)
    diff_a = dict(cw=2 * DIFF_HALF, c_all=4, n_out=4)
    dsa_a = dict(cw=HEAD_DIM, c_all=4, n_out=4)
    kidx_a = dict(cw=IDX_DIM, c_all=1, n_out=1)
    if past is None:
        dpool, dtable = _identity_pages(kvc)
        diff_kv = _gather_chunks(dpool, "rows", dtable, None, **diff_a)
        spool, stable = _identity_pages(kvd)
        dsa_kv = _gather_chunks(spool, "rows", stable, None, **dsa_a)
        ipool, itable = _identity_pages(kiw)
        kidx = _gather_chunks(ipool, "rows", itable, None, **kidx_a)
        q_pos0, lk = 0, l
    else:
        diff_pool, dsa_pool, kidx_pool, table = past
        past_len = table.shape[1] * PAGE_SIZE
        diff_rows = diff_pool.reshape(diff_pool.shape[0], PAGE_SIZE * 4, 2 * DIFF_HALF)
        diff_kv = _gather_chunks(diff_rows, "strided", table, _pad_rows(kvc, 8), **diff_a)
        dsa_kv = _gather_chunks(_cols_pool(dsa_pool), "cols", table, _pad_rows(kvd, 8), **dsa_a)
        kidx = _gather_chunks(_cols_pool(kidx_pool), "cols", table, _pad_rows(kiw[:, :, :IDX_DIM], 8), **kidx_a)
        q_pos0, lk = past_len, past_len + l
    lamv = jnp.stack([lam_q1, lam_k1, lam_q2, lam_k2], axis=0)
    o_c = _diff_attention(_pad_rows(qc, lq), diff_kv, lamv, subln_g, tq=min(tq, lq), q_pos0=q_pos0)[:, :l]
    o_d = _dsa_attention(_pad_rows(qd, lq), _pad_rows(qi, lq), _pad_rows(kiw, lq), kidx, dsa_kv,
                         tq=min(tq, lq), q_pos0=q_pos0, lk=lk)[:, :l]
    fl = (lambda t: t.reshape(1, b * l, -1)) if flat else (lambda t: t)
    y = _out_project(fl(o_c), fl(z_c), fl(o_d), fl(z_d), xin, gate, w_out.astype(BF16), final_g,
                     tl=tl, final_norm=True).reshape(b, l, d)
    states = (kvc.reshape(b, l, 2, DIFF_KV_HEADS, 2 * DIFF_HALF),
              kvd.reshape(b, l, 2, DSA_KV_HEADS, HEAD_DIM), kiw[:, :, :IDX_DIM])
    return y, states


def kernel(x_prompt, x_sample, cache_l0_nsa_kv, state_l0_win_kv, state_l0_conv, state_l0_lru_h,
           cache_l1_diff_kv, cache_l1_dsa_kv, cache_l1_dsa_kidx, page_table, c_prompt, c_sample,
           l0_norm_g, l0_ada_w, l0_ada_b, l0_w_in, l0_cmp_wk, l0_cmp_wv, l0_conv_w, l0_conv_b,
           l0_lru_wr, l0_lru_br, l0_lru_wi, l0_lru_bi, l0_lru_lambda, l0_w_out,
           l1_norm_g, l1_ada_w, l1_ada_b, l1_w_in, l1_lam_q1, l1_lam_k1, l1_lam_q2, l1_lam_k2,
           l1_subln_g, l1_w_out, final_norm_g):
    bp, lp, d = x_prompt.shape
    bs, ls, _ = x_sample.shape
    c_all = jnp.concatenate([c_prompt, c_sample], axis=0)

    def mods(ada_w, ada_b):
        m = _modulation(c_all, ada_w, ada_b)
        mp = tuple(t[:, None] for t in jnp.split(m[:bp], 3, axis=-1))
        ms = tuple(jnp.repeat(t, ls, axis=0)[None] for t in jnp.split(m[bp:], 3, axis=-1))
        return mp, ms

    tl_p = min(512, lp)
    tl_s = bs * ls
    tq = min(LANE, lp)
    w0 = (l0_norm_g, l0_w_in, l0_cmp_wk, l0_cmp_wv, l0_conv_w, l0_conv_b, l0_lru_wr, l0_lru_br,
          l0_lru_wi, l0_lru_bi, l0_lru_lambda, l0_w_out)
    mp0, ms0 = mods(l0_ada_w, l0_ada_b)
    xp, (nsa_kv_p, win_p, conv_p, h_p) = _layer0(x_prompt, mp0, None, w0, tl=tl_p, tq=tq)
    xs, (nsa_kv_s, win_s, conv_s, h_s) = _layer0(
        x_sample, ms0, (cache_l0_nsa_kv, page_table, state_l0_win_kv, state_l0_conv, state_l0_lru_h), w0,
        tl=tl_s, tq=tq)
    w1 = (l1_norm_g, l1_w_in, l1_lam_q1, l1_lam_k1, l1_lam_q2, l1_lam_k2, l1_subln_g, l1_w_out)
    mp1, ms1 = mods(l1_ada_w, l1_ada_b)
    y_p, (diff_kv_p, dsa_kv_p, kidx_p) = _layer1(xp, mp1, None, w1, final_norm_g, tl=tl_p, tq=tq)
    y_s, (diff_kv_s, dsa_kv_s, kidx_s) = _layer1(
        xs, ms1, (cache_l1_diff_kv, cache_l1_dsa_kv, cache_l1_dsa_kidx, page_table), w1, final_norm_g,
        tl=tl_s, tq=tq)
    return (y_p, y_s, nsa_kv_p, nsa_kv_s, win_p, win_s, conv_p, conv_s, h_p, h_s,
            diff_kv_p, diff_kv_s, dsa_kv_p, dsa_kv_s, kidx_p, kidx_s)
```

```python
import functools
import math

import jax
import jax.numpy as jnp
import numpy as np
from jax import lax
from jax.experimental import pallas as pl
from jax.experimental.pallas import tpu as pltpu

F32 = jnp.float32
BF16 = jnp.bfloat16
I32 = jnp.int32

PAGE_SIZE = 128
HEAD_DIM = 64
NSA_HEADS = 8
NSA_KV_HEADS = 2
NSA_CMP_BLOCK = 32
NSA_SEL_BLOCK = 64
NSA_TOPN = 16
NSA_WINDOW = 512
FORCE_SCORE = 1e4
LRU_BLOCKS = 8
LRU_C = 8.0
CONV_WIDTH = 4
DIFF_HALF = 64
DIFF_HEADS = 4
DIFF_KV_HEADS = 2
DIFF_LAMBDA_INIT = 0.8 - 0.6 * math.exp(-0.3 * 1)
DSA_HEADS = 8
DSA_KV_HEADS = 2
IDX_HEADS = 4
IDX_DIM = 64
DSA_TOPK_MAX = 256
NORM_EPS = 1e-6
NEG = -1e30
REMOVED = -3e38
INT_MIN = -2 ** 31
LOG2E = math.log2(math.e)

LANE = 128
VMEM_LIMIT = 56 * 1024 * 1024
KV_TILE = 512
Q_TILE = 256
MAX_KV_TILE = 2048
PROBE_BINADES = 3


def _kv_tile(tq, lk_pad):
    t = min(KV_TILE * max(1, Q_TILE // tq), MAX_KV_TILE, lk_pad)
    while lk_pad % t:
        t -= KV_TILE
    return t


def _win_span(tq):
    return _round_up(NSA_WINDOW + tq, LANE)
PAGES_PER_STEP = 16


def _cparams(sem):
    return pltpu.CompilerParams(dimension_semantics=sem, vmem_limit_bytes=VMEM_LIMIT)


def _dot(a, b):
    return jnp.dot(a, b, preferred_element_type=F32)


def _dot_nt(a, b):
    return lax.dot_general(a, b, (((1,), (1,)), ((), ())), preferred_element_type=F32)


def _round_up(x, m):
    return (x + m - 1) // m * m


def _mod_kernel(c_ref, w_ref, b_ref, o_ref):
    o_ref[...] = jnp.dot(c_ref[...], w_ref[...], preferred_element_type=F32,
                         precision=lax.Precision.HIGHEST) + b_ref[...]


def _modulation(c, w, b):
    bc, d = c.shape
    n = w.shape[1]
    tn = 512
    return pl.pallas_call(
        _mod_kernel,
        grid=(n // tn,),
        in_specs=[pl.BlockSpec((bc, d), lambda j: (0, 0)),
                  pl.BlockSpec((d, tn), lambda j: (0, j)),
                  pl.BlockSpec((1, tn), lambda j: (0, j))],
        out_specs=pl.BlockSpec((bc, tn), lambda j: (0, j)),
        out_shape=jax.ShapeDtypeStruct((bc, n), F32),
        compiler_params=_cparams(("arbitrary",)),
        name="modulation",
    )(c, w, b.reshape(1, n))


def _proj_kernel(x_ref, g_ref, sh_ref, sc_ref, w_ref, *o_refs, segs, sigmoid_seg):
    x = x_ref[0]
    y = x * lax.rsqrt(jnp.mean(x * x, axis=-1, keepdims=True) + NORM_EPS)
    h = (y * g_ref[...]) * (1.0 + sc_ref[0]) + sh_ref[0]
    hb = h.astype(BF16)
    for i, ((a, b), o_ref) in enumerate(zip(segs, o_refs)):
        r = _dot(hb, w_ref[:, a:b])
        if i == sigmoid_seg:
            r = jax.nn.sigmoid(r)
        o_ref[0] = r


def _project(x, g, shift, scale, w, segs, sigmoid_seg, tl):
    b, l, d = x.shape
    ts = shift.shape[1]
    tm = 1 if ts == 1 else tl
    mod_map = (lambda bi, li: (bi, 0, 0)) if ts == 1 else (lambda bi, li: (bi, li, 0))
    p = w.shape[1]
    kern = functools.partial(_proj_kernel, segs=tuple(segs), sigmoid_seg=sigmoid_seg)
    return pl.pallas_call(
        kern,
        grid=(b, l // tl),
        in_specs=[pl.BlockSpec((1, tl, d), lambda bi, li: (bi, li, 0)),
                  pl.BlockSpec((1, d), lambda bi, li: (0, 0)),
                  pl.BlockSpec((1, tm, d), mod_map),
                  pl.BlockSpec((1, tm, d), mod_map),
                  pl.BlockSpec((d, p), lambda bi, li: (0, 0))],
        out_specs=[pl.BlockSpec((1, tl, e - a), lambda bi, li: (bi, li, 0)) for a, e in segs],
        out_shape=[jax.ShapeDtypeStruct((b, l, e - a), F32) for a, e in segs],
        compiler_params=_cparams(("parallel", "arbitrary")),
        name="norm_mod_project",
    )(x, g.reshape(1, d), shift, scale, w)


def _gather_kernel(pt_ref, *refs, pp, n_page_steps, layout, cw, c_all, n_cmp, n_out, has_new):
    page_refs = refs[:pp]
    pos = pp
    new_ref = None
    if has_new:
        new_ref = refs[pos]
        pos += 1
    w2_ref = None
    if n_cmp:
        w2_ref = refs[pos]
        pos += 1
    out_ref = refs[pos]
    cmp_ref = refs[pos + 1] if n_cmp else None
    j = pl.program_id(1)
    ones_col = jnp.where(lax.broadcasted_iota(I32, (PAGE_SIZE, LANE - cw), 1) == 0, 1.0, 0.0) if cw < LANE else None

    def chunk(i, c):
        if layout == "rows":
            return page_refs[i][0, :, c * cw:(c + 1) * cw]
        if layout == "cols":
            return page_refs[i][0, c].T
        return page_refs[i][0, pl.ds(c, PAGE_SIZE, stride=c_all), :]

    def emit(i, c, x):
        if c < n_cmp:
            nb = PAGE_SIZE // NSA_SEL_BLOCK
            prod = x.reshape(nb, NSA_SEL_BLOCK, cw) * w2_ref[c][None]
            cmp_ref[0, c, 0, i * nb:(i + 1) * nb, :] = jnp.sum(prod[:, :NSA_CMP_BLOCK], axis=1)
            cmp_ref[0, c, 1, i * nb:(i + 1) * nb, :] = jnp.sum(prod[:, NSA_CMP_BLOCK:], axis=1)
        else:
            if ones_col is not None:
                x = jnp.concatenate([x, ones_col], axis=-1)
            out_ref[0, c - n_cmp, i * PAGE_SIZE:(i + 1) * PAGE_SIZE, :] = x.astype(BF16)

    def pages():
        for i in range(pp):
            for c in range(n_cmp + n_out):
                emit(i, c, chunk(i, c))

    if has_new:
        pl.when(j < n_page_steps)(pages)

        @pl.when(j >= n_page_steps)
        def _():
            new = new_ref[0]
            for c in range(n_cmp + n_out):
                xc = new[:, c * cw:(c + 1) * cw]
                emit(0, c, jnp.concatenate([xc, jnp.zeros((PAGE_SIZE - xc.shape[0], cw), F32)], axis=0))
                for i in range(1, pp):
                    emit(i, c, jnp.zeros((PAGE_SIZE, cw), F32))
    else:
        pages()


def _gather_chunks(pool, layout, page_table, new, *, cw, c_all, n_out, n_cmp=0, cmp_w=None):
    bk, n_pages = page_table.shape
    pp = min(PAGES_PER_STEP, n_pages)
    assert n_pages % pp == 0
    n_page_steps = n_pages // pp
    has_new = new is not None
    n_steps = n_page_steps + (1 if has_new else 0)
    rows = pp * PAGE_SIZE
    lk_pad = n_steps * rows
    page_block = (1,) + pool.shape[1:]
    zeros = (0,) * (len(page_block) - 1)

    def page_map(i):
        def f(b, j, pt):
            return (pt[b, jnp.minimum(j * pp + i, n_pages - 1)],) + zeros
        return f

    in_specs = [pl.BlockSpec(page_block, page_map(i)) for i in range(pp)]
    args = [pool] * pp
    if has_new:
        in_specs.append(pl.BlockSpec((1,) + new.shape[1:], lambda b, j, pt: (b, 0, 0)))
        args.append(new)
    if n_cmp:
        in_specs.append(pl.BlockSpec(cmp_w.shape, lambda b, j, pt: (0, 0, 0)))
        args.append(cmp_w)
    out_specs = [pl.BlockSpec((1, n_out, rows, LANE), lambda b, j, pt: (b, 0, j, 0))]
    out_shape = [jax.ShapeDtypeStruct((bk, n_out, lk_pad, LANE), BF16)]
    if n_cmp:
        nb = rows // NSA_SEL_BLOCK
        out_specs.append(pl.BlockSpec((1, n_cmp, 2, nb, HEAD_DIM), lambda b, j, pt: (b, 0, 0, j, 0)))
        out_shape.append(jax.ShapeDtypeStruct((bk, n_cmp, 2, lk_pad // NSA_SEL_BLOCK, HEAD_DIM), F32))
    kern = functools.partial(_gather_kernel, pp=pp, n_page_steps=n_page_steps, layout=layout, cw=cw,
                             c_all=c_all, n_cmp=n_cmp, n_out=n_out, has_new=has_new)
    outs = pl.pallas_call(
        kern,
        grid_spec=pltpu.PrefetchScalarGridSpec(
            num_scalar_prefetch=1, grid=(bk, n_steps), in_specs=in_specs, out_specs=out_specs),
        out_shape=out_shape,
        compiler_params=_cparams(("parallel", "arbitrary")),
        name="gather_pages",
    )(page_table, *args)
    return outs if n_cmp else outs[0]


def _identity_pages(x):
    b, l, w = x.shape
    n_pages = l // PAGE_SIZE
    pool = x.reshape(b * n_pages, PAGE_SIZE, w)
    table = jnp.arange(b * n_pages, dtype=I32).reshape(b, n_pages)
    return pool, table


def _flash_tile(s_all, bias, v_tile, carry, n_heads, tq, l_in_acc):
    m, l, acc = carry
    s = s_all
    if bias is not None:
        s = jnp.concatenate([s_all[h * tq:(h + 1) * tq] + bias for h in range(n_heads)], axis=0)
    m_new = jnp.maximum(m, jnp.max(s, axis=-1, keepdims=True))
    p = jnp.exp2(s - m_new)
    alpha = jnp.exp2(m - m_new)
    acc = alpha * acc + _dot(p.astype(BF16), v_tile)
    if not l_in_acc:
        l = alpha * l + jnp.sum(p, axis=-1, keepdims=True)
    return m_new, l, acc


def _flash_init(rows, dv):
    return (jnp.full((rows, 1), NEG, F32), jnp.zeros((rows, 1), F32), jnp.zeros((rows, dv), F32))


def _flash_out(acc):
    return acc[:, :HEAD_DIM] * (1.0 / jnp.maximum(acc[:, HEAD_DIM:HEAD_DIM + 1], 1e-30))


def _stack_heads(q, n, width):
    return jnp.concatenate([q[:, h * width:(h + 1) * width] for h in range(n)], axis=0)


def _pad_lanes(x):
    return jnp.concatenate([x, jnp.zeros((x.shape[0], LANE - x.shape[1]), x.dtype)], axis=-1)


def _nsa_kernel(q_ref, g_ref, cmp_ref, ksel_ref, kwin_ref, o_ref, *,
                tq, tk, q_pos0, win_pos0, nsp, n_top):
    G = NSA_KV_HEADS
    hpg = NSA_HEADS // G
    gw = hpg * HEAD_DIM
    qi = pl.program_id(1)
    q0 = q_pos0 + qi * tq
    qpos = q0 + lax.broadcasted_iota(I32, (tq, 1), 0)
    qpos_r = jnp.concatenate([qpos] * hpg, axis=0)
    rows = hpg * tq
    q_all = q_ref[0] * (HEAD_DIM ** -0.5 * LOG2E)
    blk = lax.broadcasted_iota(I32, (1, nsp), 1)
    blk_f = blk.astype(F32)
    cur = qpos // NSA_SEL_BLOCK
    vis_e = (blk * NSA_SEL_BLOCK + (NSA_CMP_BLOCK - 1)) <= qpos_r
    vis_o = (blk * NSA_SEL_BLOCK + (NSA_SEL_BLOCK - 1)) <= qpos_r
    start = pl.multiple_of(jnp.maximum(q0 - NSA_WINDOW - win_pos0, 0), 8)
    span = _win_span(tq)
    kpos_w = win_pos0 + start + lax.broadcasted_iota(I32, (1, span), 1)
    dlt = qpos - kpos_w
    bias_w = jnp.where(dlt >= 0, jnp.where(dlt < NSA_WINDOW, 0.0, NEG), NEG)

    def prologue(g):
        qs64 = _stack_heads(q_all[:, g * gw:(g + 1) * gw], hpg, HEAD_DIM)
        qs = _pad_lanes(qs64).astype(BF16)
        qs64 = qs64.astype(BF16)

        s_w = _dot_nt(qs, kwin_ref[0, g, pl.ds(start, span), :])
        _, _, acc_w = _flash_tile(s_w, bias_w, kwin_ref[0, G + g, pl.ds(start, span), :],
                                  _flash_init(rows, LANE), hpg, tq, True)
        o_w = _flash_out(acc_w)

        kce = cmp_ref[0, 2 * g, 0].astype(BF16)
        kco = cmp_ref[0, 2 * g, 1].astype(BF16)
        vce = cmp_ref[0, 2 * g + 1, 0].astype(BF16)
        vco = cmp_ref[0, 2 * g + 1, 1].astype(BF16)
        s_e = jnp.where(vis_e, _dot_nt(qs64, kce), NEG)
        s_o = jnp.where(vis_o, _dot_nt(qs64, kco), NEG)
        m = jnp.maximum(jnp.max(s_e, axis=-1, keepdims=True), jnp.max(s_o, axis=-1, keepdims=True))
        p_e = jnp.where(vis_e, jnp.exp2(s_e - m), 0.0)
        p_o = jnp.where(vis_o, jnp.exp2(s_o - m), 0.0)
        den = jnp.sum(p_e, axis=-1, keepdims=True) + jnp.sum(p_o, axis=-1, keepdims=True)
        inv = 1.0 / jnp.maximum(den, 1e-30)
        p_e = p_e * inv
        p_o = p_o * inv
        o_c = _dot(p_e.astype(BF16), vce) + _dot(p_o.astype(BF16), vco)

        pe_h = sum(p_e[h * tq:(h + 1) * tq] for h in range(hpg))
        po_h = sum(p_o[h * tq:(h + 1) * tq] for h in range(hpg))
        imp = pe_h + po_h
        imp = jnp.where((blk == cur) | (blk == 0), FORCE_SCORE, imp)
        imp = jnp.where(blk <= cur, imp, NEG)
        sel = jnp.zeros((tq, nsp), F32)
        for _ in range(n_top):
            mx = jnp.max(imp, axis=-1, keepdims=True)
            first = jnp.min(jnp.where(imp == mx, blk_f, float(nsp)), axis=-1, keepdims=True)
            pick = blk_f == first
            sel = jnp.where(pick & (mx > 0.5 * NEG), 1.0, sel)
            imp = jnp.where(pick, REMOVED, imp)
        return qs, o_c, o_w, sel.astype(BF16)

    pro = [prologue(g) for g in range(G)]

    blk_col = lax.broadcasted_iota(I32, (nsp, 1), 0)

    def sel_step(j, carry):
        k0 = pl.multiple_of(j * tk, tk)
        kpos = k0 + lax.broadcasted_iota(I32, (1, tk), 1)
        expand = jnp.where(blk_col == kpos // NSA_SEL_BLOCK, 1.0, 0.0).astype(BF16)
        out = []
        for g in range(G):
            picked = _dot(pro[g][3], expand)
            bias = jnp.where(kpos <= qpos, jnp.where(picked > 0.5, 0.0, NEG), NEG)
            s = _dot_nt(pro[g][0], ksel_ref[0, g, pl.ds(k0, tk), :])
            out.append(_flash_tile(s, bias, ksel_ref[0, G + g, pl.ds(k0, tk), :], carry[g], hpg, tq, True))
        return tuple(out)

    n_tiles = (q0 + tq - 1) // tk + 1
    res = lax.fori_loop(0, n_tiles, sel_step, tuple(_flash_init(rows, LANE) for _ in range(G)))

    gates = g_ref[0]
    outs = []
    for g in range(G):
        _, o_c, o_w, _ = pro[g]
        o_s = _flash_out(res[g][2])
        for h in range(hpg):
            r = slice(h * tq, (h + 1) * tq)
            c = g * LANE + 3 * h
            outs.append(gates[:, c:c + 1] * o_c[r] + gates[:, c + 1:c + 2] * o_s[r]
                        + gates[:, c + 2:c + 3] * o_w[r])
    o_ref[0] = jnp.concatenate(outs, axis=-1)


def _nsa_attention(q, gates, cmp, ksel, kwin, *, tq, q_pos0, lk, win_pos0):
    bk, lq, _ = q.shape
    G = NSA_KV_HEADS
    nsp = cmp.shape[3]
    lk_pad = ksel.shape[2]
    lw_pad = kwin.shape[2]
    tk = _kv_tile(tq, lk_pad)
    ns = -(-lk // NSA_SEL_BLOCK)
    assert lw_pad >= _win_span(tq) and q_pos0 + lq <= lk_pad
    kern = functools.partial(_nsa_kernel, tq=tq, tk=tk, q_pos0=q_pos0, win_pos0=win_pos0,
                             nsp=nsp, n_top=min(NSA_TOPN, ns))
    qw = NSA_HEADS * HEAD_DIM
    return pl.pallas_call(
        kern,
        grid=(bk, lq // tq),
        in_specs=[pl.BlockSpec((1, tq, qw), lambda b, i: (b, i, 0)),
                  pl.BlockSpec((1, tq, G * LANE), lambda b, i: (b, i, 0)),
                  pl.BlockSpec((1, 2 * G, 2, nsp, HEAD_DIM), lambda b, i: (b, 0, 0, 0, 0)),
                  pl.BlockSpec((1, 2 * G, lk_pad, LANE), lambda b, i: (b, 0, 0, 0)),
                  pl.BlockSpec((1, 2 * G, lw_pad, LANE), lambda b, i: (b, 0, 0, 0))],
        out_specs=pl.BlockSpec((1, tq, qw), lambda b, i: (b, i, 0)),
        out_shape=jax.ShapeDtypeStruct((bk, lq, qw), F32),
        compiler_params=_cparams(("parallel", "arbitrary")),
        name="nsa_attention",
    )(q, gates, cmp, ksel, kwin)


def _shift_rows(x, d, fill):
    rolled = pltpu.roll(x, d, axis=0)
    row = lax.broadcasted_iota(I32, x.shape, 0)
    return jnp.where(row >= d, rolled, fill)


def _lru_kernel(x_ref, hist_ref, h0_ref, cw_ref, cb_ref, wr_ref, br_ref, wi_ref, bi_ref, lam_ref,
                o_ref, hl_ref, tail_ref, h_ref, *, tl, last_row):
    li = pl.program_id(1)

    @pl.when(li == 0)
    def _():
        tail_ref[...] = jnp.concatenate(
            [jnp.zeros((8 - (CONV_WIDTH - 1), x_ref.shape[-1]), F32), hist_ref[0]], axis=0)
        h_ref[...] = h0_ref[0]

    x = x_ref[0]
    xp = jnp.concatenate([tail_ref[...], x], axis=0)
    cw = cw_ref[...]
    conv = sum(xp[8 - (CONV_WIDTH - 1) + j:8 - (CONV_WIDTH - 1) + j + tl] * cw[j:j + 1]
               for j in range(CONV_WIDTH))
    conv = cb_ref[...] + conv
    tail_ref[...] = x[tl - 8:tl]

    cb16 = conv.astype(BF16)
    r = jax.nn.sigmoid(_dot(cb16, wr_ref[...]) + br_ref[...])
    ig = jax.nn.sigmoid(_dot(cb16, wi_ref[...]) + bi_ref[...])
    log_a = -LRU_C * r * jax.nn.softplus(-lam_ref[...])
    a = jnp.exp(log_a)
    th = jnp.tanh(log_a)
    b = jnp.sqrt(-2.0 * th / (1.0 - th)) * (ig * conv)

    d = 1
    while d < tl:
        a_prev = _shift_rows(a, d, 1.0)
        b_prev = _shift_rows(b, d, 0.0)
        b = a * b_prev + b
        a = a * a_prev
        d *= 2
    h = a * h_ref[...] + b
    o_ref[0] = h
    h_ref[...] = h[tl - 1:tl]

    @pl.when(li == pl.num_programs(1) - 1)
    def _():
        hl_ref[0] = h[last_row:last_row + 1]


def _block_diag(w):
    nb, bw, _ = w.shape
    eye = jnp.eye(nb, dtype=w.dtype)
    return (eye[:, None, :, None] * w[:, :, None, :]).reshape(nb * bw, nb * bw)


def _conv_rglru(x_b, hist, h0, conv_w, conv_b, w_r, b_r, w_i, b_i, lam, *, tl, n_valid):
    b, l, w = x_b.shape
    assert tl >= 8 and l % tl == 0 and n_valid > l - tl
    kern = functools.partial(_lru_kernel, tl=tl, last_row=(n_valid - 1) % tl)
    vec = lambda: pl.BlockSpec((1, w), lambda bi, li: (0, 0))
    h, h_last = pl.pallas_call(
        kern,
        grid=(b, l // tl),
        in_specs=[pl.BlockSpec((1, tl, w), lambda bi, li: (bi, li, 0)),
                  pl.BlockSpec((1, CONV_WIDTH - 1, w), lambda bi, li: (bi, 0, 0)),
                  pl.BlockSpec((1, 1, w), lambda bi, li: (bi, 0, 0)),
                  pl.BlockSpec((CONV_WIDTH, w), lambda bi, li: (0, 0)),
                  vec(),
                  pl.BlockSpec((w, w), lambda bi, li: (0, 0)), vec(),
                  pl.BlockSpec((w, w), lambda bi, li: (0, 0)), vec(), vec()],
        out_specs=[pl.BlockSpec((1, tl, w), lambda bi, li: (bi, li, 0)),
                   pl.BlockSpec((1, 1, w), lambda bi, li: (bi, 0, 0))],
        out_shape=[jax.ShapeDtypeStruct((b, l, w), F32), jax.ShapeDtypeStruct((b, 1, w), F32)],
        scratch_shapes=[pltpu.VMEM((8, w), F32), pltpu.VMEM((1, w), F32)],
        compiler_params=_cparams(("parallel", "arbitrary")),
        name="conv_rglru",
    )(x_b, hist, h0.reshape(b, 1, w), conv_w, conv_b.reshape(1, w),
      _block_diag(w_r).astype(BF16), b_r.reshape(1, w), _block_diag(w_i).astype(BF16), b_i.reshape(1, w),
      lam.reshape(1, w))
    return h, h_last.reshape(b, w)


def _out_kernel(oa_ref, za_ref, ob_ref, zb_ref, x_ref, gate_ref, w_ref, fg_ref, o_ref, *, final_norm):
    half = oa_ref.shape[-1]
    ma = (oa_ref[0] * jax.nn.silu(za_ref[0])).astype(BF16)
    mb = (ob_ref[0] * jax.nn.silu(zb_ref[0])).astype(BF16)
    y = _dot(ma, w_ref[0:half, :]) + _dot(mb, w_ref[half:2 * half, :])
    out = x_ref[0] + gate_ref[0] * y
    if final_norm:
        out = out * lax.rsqrt(jnp.mean(out * out, axis=-1, keepdims=True) + NORM_EPS) * fg_ref[...]
    o_ref[0] = out


def _out_project(o_a, z_a, o_b, z_b, x, gate, w_out, final_g, *, tl, final_norm):
    b, l, d = x.shape
    half = o_a.shape[-1]
    ts = gate.shape[1]
    tm = 1 if ts == 1 else tl
    mod_map = (lambda bi, li: (bi, 0, 0)) if ts == 1 else (lambda bi, li: (bi, li, 0))
    act = lambda: pl.BlockSpec((1, tl, half), lambda bi, li: (bi, li, 0))
    return pl.pallas_call(
        functools.partial(_out_kernel, final_norm=final_norm),
        grid=(b, l // tl),
        in_specs=[act(), act(), act(), act(),
                  pl.BlockSpec((1, tl, d), lambda bi, li: (bi, li, 0)),
                  pl.BlockSpec((1, tm, d), mod_map),
                  pl.BlockSpec((2 * half, d), lambda bi, li: (0, 0)),
                  pl.BlockSpec((1, d), lambda bi, li: (0, 0))],
        out_specs=pl.BlockSpec((1, tl, d), lambda bi, li: (bi, li, 0)),
        out_shape=jax.ShapeDtypeStruct((b, l, d), F32),
        compiler_params=_cparams(("parallel", "arbitrary")),
        name="out_project",
    )(o_a, z_a, o_b, z_b, x, gate, w_out, final_g.reshape(1, d))


def _diff_kernel(q_ref, k_ref, v_ref, lamv_ref, subg_ref, o_ref, *, tq, tk, q_pos0):
    hpg = DIFF_HEADS // DIFF_KV_HEADS
    qi = pl.program_id(2)
    q0 = q_pos0 + qi * tq
    n_maps = 2 * hpg
    rows = n_maps * tq
    qpos = q0 + lax.broadcasted_iota(I32, (tq, 1), 0)
    q = q_ref[0] * (DIFF_HALF ** -0.5 * LOG2E)
    zero = jnp.zeros((tq, DIFF_HALF), F32)
    parts = []
    for mp in range(2):
        for h in range(hpg):
            qh = q[:, (2 * h + mp) * DIFF_HALF:(2 * h + mp + 1) * DIFF_HALF]
            parts.append(jnp.concatenate([qh, zero] if mp == 0 else [zero, qh], axis=-1))
    qs = jnp.concatenate(parts, axis=0).astype(BF16)

    def step(masked):
        def f(j, carry):
            k0 = pl.multiple_of(j * tk, tk)
            s = _dot_nt(qs, k_ref[0, 0, pl.ds(k0, tk), :])
            bias = None
            if masked:
                kpos = k0 + lax.broadcasted_iota(I32, (1, tk), 1)
                bias = jnp.where(kpos <= qpos, 0.0, NEG)
            return _flash_tile(s, bias, v_ref[0, 0, pl.ds(k0, tk), :], carry, n_maps, tq, False)
        return f

    n_full = (q0 + 1) // tk
    n_tiles = (q0 + tq - 1) // tk + 1
    carry = lax.fori_loop(0, n_full, step(False), _flash_init(rows, 2 * DIFF_HALF))
    m, l, acc = lax.fori_loop(n_full, n_tiles, step(True), carry)
    o = acc * (1.0 / jnp.maximum(l, 1e-30))
    lq = lamv_ref[...]
    lam = (jnp.exp(jnp.sum(lq[0:1] * lq[1:2], axis=-1, keepdims=True))
           - jnp.exp(jnp.sum(lq[2:3] * lq[3:4], axis=-1, keepdims=True)) + DIFF_LAMBDA_INIT)
    half = hpg * tq
    od = o[0:half] - lam * o[half:2 * half]
    od = od * lax.rsqrt(jnp.mean(od * od, axis=-1, keepdims=True) + NORM_EPS)
    od = od * subg_ref[...] * (1.0 - DIFF_LAMBDA_INIT)
    o_ref[0] = jnp.concatenate([od[h * tq:(h + 1) * tq] for h in range(hpg)], axis=-1)


def _diff_attention(q, kv, lamv, subln_g, *, tq, q_pos0):
    bk, lq, _ = q.shape
    G = DIFF_KV_HEADS
    lk_pad = kv.shape[2]
    tk = _kv_tile(tq, lk_pad)
    gw = (DIFF_HEADS // G) * 2 * DIFF_HALF
    assert q_pos0 + lq <= lk_pad
    return pl.pallas_call(
        functools.partial(_diff_kernel, tq=tq, tk=tk, q_pos0=q_pos0),
        grid=(bk, G, lq // tq),
        in_specs=[pl.BlockSpec((1, tq, gw), lambda b, g, i: (b, i, g)),
                  pl.BlockSpec((1, 1, lk_pad, 2 * DIFF_HALF), lambda b, g, i: (b, g, 0, 0)),
                  pl.BlockSpec((1, 1, lk_pad, 2 * DIFF_HALF), lambda b, g, i: (b, G + g, 0, 0)),
                  pl.BlockSpec((4, DIFF_HALF), lambda b, g, i: (0, 0)),
                  pl.BlockSpec((1, 2 * DIFF_HALF), lambda b, g, i: (0, 0))],
        out_specs=pl.BlockSpec((1, tq, gw), lambda b, g, i: (b, i, g)),
        out_shape=jax.ShapeDtypeStruct((bk, lq, DIFF_HEADS * 2 * DIFF_HALF), F32),
        compiler_params=_cparams(("parallel", "parallel", "arbitrary")),
        name="diff_attention",
    )(q, kv, kv, lamv, subln_g.reshape(1, 2 * DIFF_HALF))


def _dsa_kernel(q_ref, qi_ref, kw_ref, kidx_ref, kv_ref, tri_ref, o_ref, key_ref, *,
                tq, tk, q_pos0, n_sel, one_block):
    G = DSA_KV_HEADS
    hpg = DSA_HEADS // G
    q0 = q_pos0 + (0 if one_block else pl.program_id(1) * tq)
    qpos = q0 + lax.broadcasted_iota(I32, (tq, 1), 0)
    n_tiles = (q0 + tq - 1) // tk + 1
    lanes = tk // LANE

    qidx = _pad_lanes(_stack_heads(qi_ref[0] * (IDX_DIM ** -0.5), IDX_HEADS, IDX_DIM)).astype(BF16)
    wi = kw_ref[0][:, IDX_DIM:IDX_DIM + IDX_HEADS] * (IDX_HEADS ** -0.5)

    def score_step(j, kmax):
        k0 = pl.multiple_of(j * tk, tk)
        s_all = _dot_nt(qidx, kidx_ref[0, 0, pl.ds(k0, tk), :])
        score = jnp.zeros((tq, tk), F32)
        for h in range(IDX_HEADS):
            score = score + wi[:, h:h + 1] * jnp.maximum(s_all[h * tq:(h + 1) * tq], 0.0)
        kpos = k0 + lax.broadcasted_iota(I32, (1, tk), 1)
        bits = pltpu.bitcast(score, I32)
        key = jnp.where(bits < 0, bits ^ 0x7FFFFFFF, bits)
        key = jnp.where(score == 0.0, 0, key)
        key = jnp.where(kpos <= qpos, jnp.where(score > 0.5 * NEG, key, INT_MIN), INT_MIN)
        key_ref[j] = key
        return jnp.maximum(kmax, jnp.max(key, axis=-1, keepdims=True))

    kmax = lax.fori_loop(0, n_tiles, score_step, jnp.full((tq, 1), INT_MIN, I32))

    def count(*bounds):
        def f(j, accs):
            key = key_ref[j]
            out = []
            for bound, acc in zip(bounds, accs):
                hit = jnp.where(key >= bound, 1.0, 0.0)
                for c in range(lanes):
                    acc = acc + hit[:, c * LANE:(c + 1) * LANE]
                out.append(acc)
            return tuple(out)
        accs = lax.fori_loop(0, n_tiles, f, tuple(jnp.zeros((tq, LANE), F32) for _ in bounds),
                             unroll=one_block)
        return [jnp.sum(acc, axis=-1, keepdims=True) for acc in accs]

    k_f = float(n_sel)
    probe = jnp.maximum(kmax - (PROBE_BINADES << 23), 1)
    c_pos, c_nn, c_probe = count(1, 0, probe)
    lo0 = jnp.where(c_pos >= k_f, jnp.where(c_probe >= k_f, probe, 1), jnp.where(c_nn >= k_f, 0, INT_MIN))
    hi0 = jnp.where(c_pos >= k_f, jnp.where(c_probe >= k_f, kmax, probe - 1), jnp.where(c_nn >= k_f, 0, -1))

    def unfinished(lo_hi):
        lo, hi = lo_hi
        return jnp.max(jnp.where(lo < hi, 1.0, 0.0)) > 0.0

    def bisect(lo_hi):
        lo, hi = lo_hi
        mid = (lo >> 1) + (hi >> 1) + ((lo | hi) & 1)
        cnt, = count(mid)
        lo = jnp.where(cnt >= k_f, mid, lo)
        hi = jnp.where(cnt > k_f, hi, jnp.where(cnt == k_f, mid, mid - 1))
        return lo, hi

    thr, _ = lax.while_loop(unfinished, bisect, (lo0, hi0))
    thr = jnp.maximum(thr, INT_MIN + 1)
    n_gt, = count(thr + 1)
    need = k_f - n_gt

    q = q_ref[0] * (HEAD_DIM ** -0.5 * LOG2E)
    qs = [_pad_lanes(_stack_heads(q[:, g * hpg * HEAD_DIM:(g + 1) * hpg * HEAD_DIM], hpg, HEAD_DIM)).astype(BF16)
          for g in range(G)]
    rows = hpg * tq

    def att_step(j, carry):
        seen, flash = carry
        k0 = pl.multiple_of(j * tk, tk)
        key = key_ref[j]
        tied = key == thr
        tied_b = jnp.where(tied, 1.0, 0.0).astype(BF16)
        ranks = []
        for c in range(tk // KV_TILE):
            r = _dot(tied_b[:, c * KV_TILE:(c + 1) * KV_TILE], tri_ref[...]) + seen
            ranks.append(r)
            seen = r[:, KV_TILE - 1:KV_TILE]
        rank = jnp.concatenate(ranks, axis=-1)
        bias = jnp.where(key > thr, 0.0, jnp.where(tied, jnp.where(rank <= need, 0.0, NEG), NEG))
        out = []
        for g in range(G):
            s = _dot_nt(qs[g], kv_ref[0, g, pl.ds(k0, tk), :])
            out.append(_flash_tile(s, bias, kv_ref[0, G + g, pl.ds(k0, tk), :], flash[g], hpg, tq, True))
        return seen, tuple(out)

    _, res = lax.fori_loop(0, n_tiles, att_step,
                           (jnp.zeros((tq, 1), F32), tuple(_flash_init(rows, LANE) for _ in range(G))))
    outs = []
    for g in range(G):
        o = _flash_out(res[g][2])
        outs.extend(o[h * tq:(h + 1) * tq] for h in range(hpg))
    o_ref[0] = jnp.concatenate(outs, axis=-1)


def _dsa_attention(q, qi, kw, kidx, kv, *, tq, q_pos0, lk):
    bk, lq, _ = q.shape
    lk_pad = kv.shape[2]
    tk = _kv_tile(tq, lk_pad)
    n_sel = min(DSA_TOPK_MAX, lk // 4)
    assert q_pos0 + lq <= lk_pad and tk >= n_sel
    kern = functools.partial(_dsa_kernel, tq=tq, tk=tk, q_pos0=q_pos0, n_sel=n_sel, one_block=lq == tq)
    assert tk % KV_TILE == 0
    tri = jnp.triu(jnp.ones((KV_TILE, KV_TILE), BF16))
    return pl.pallas_call(
        kern,
        grid=(bk, lq // tq),
        in_specs=[pl.BlockSpec((1, tq, DSA_HEADS * HEAD_DIM), lambda b, i: (b, i, 0)),
                  pl.BlockSpec((1, tq, IDX_HEADS * IDX_DIM), lambda b, i: (b, i, 0)),
                  pl.BlockSpec((1, tq, LANE), lambda b, i: (b, i, 0)),
                  pl.BlockSpec((1, 1, lk_pad, LANE), lambda b, i: (b, 0, 0, 0)),
                  pl.BlockSpec((1, 4, lk_pad, LANE), lambda b, i: (b, 0, 0, 0)),
                  pl.BlockSpec((KV_TILE, KV_TILE), lambda b, i: (0, 0))],
        out_specs=pl.BlockSpec((1, tq, DSA_HEADS * HEAD_DIM), lambda b, i: (b, i, 0)),
        out_shape=jax.ShapeDtypeStruct((bk, lq, DSA_HEADS * HEAD_DIM), F32),
        scratch_shapes=[pltpu.VMEM((lk_pad // tk, tq, tk), I32)],
        compiler_params=_cparams(("parallel", "arbitrary")),
        name="dsa_attention",
    )(q, qi, kw, kidx, kv, tri)


L0_SIZES = (512, 768, 24, 512, 512, 512)
L1_SIZES = (512, 256, 256, 512, 512, 128, 128, 256, 64, 4, 512)


def _l0_weight(w_in):
    d = w_in.shape[0]
    q, kv6, gl, z_a, x_b, z_b = jnp.split(w_in, np.cumsum(L0_SIZES)[:-1].tolist(), axis=1)
    pad = jnp.zeros((d, LANE - 12), w_in.dtype)
    w = jnp.concatenate([q, kv6, z_a, x_b, z_b, gl[:, :12], pad, gl[:, 12:], pad], axis=1)
    segs = [(0, 512), (512, 1024), (1024, 1280), (1280, 1792), (1792, 2304), (2304, 2816), (2816, 3072)]
    return w.astype(BF16), segs


def _l1_weight(w_in):
    d = w_in.shape[0]
    qc, kc, vc, z_c, qd, kd, vd, qi, ki, wi, z_d = jnp.split(w_in, np.cumsum(L1_SIZES)[:-1].tolist(), axis=1)
    pad = jnp.zeros((d, LANE - IDX_DIM - IDX_HEADS), w_in.dtype)
    w = jnp.concatenate([qc, kc, vc, z_c, qd, kd, vd, qi, z_d, ki, wi, pad], axis=1)
    segs = [(0, 512), (512, 1024), (1024, 1536), (1536, 2048), (2048, 2304), (2304, 2560), (2560, 3072),
            (3072, 3200)]
    return w.astype(BF16), segs


def _pad_rows(x, n):
    return jnp.pad(x, ((0, 0), (0, n - x.shape[1]), (0, 0)))


def _cols_pool(pool):
    npool, ps = pool.shape[:2]
    cw = pool.shape[-1]
    perm = (0,) + tuple(range(2, pool.ndim)) + (1,)
    return jnp.transpose(pool, perm).reshape(npool, -1, cw, ps)


def _layer0(x, mod, past, w, *, tl, tq):
    (norm_g, w_in, cmp_wk, cmp_wv, conv_w, conv_b, lru_wr, lru_br, lru_wi, lru_bi, lru_lambda, w_out) = w
    shift, scale, gate = mod
    b, l, d = x.shape
    w_p, segs = _l0_weight(w_in)
    flat = shift.shape[1] != 1
    xin = x.reshape(1, b * l, d) if flat else x
    q, kvp, kvw, z_a, x_b, z_b, gates = _project(xin, norm_g, shift, scale, w_p, segs, 6, tl)
    if flat:
        q, kvp, kvw, z_a, x_b, z_b, gates = (t.reshape(b, l, -1) for t in (q, kvp, kvw, z_a, x_b, z_b, gates))
    wk2 = jnp.concatenate([cmp_wk, cmp_wk], axis=0)
    wv2 = jnp.concatenate([cmp_wv, cmp_wv], axis=0)
    w2 = jnp.stack([wk2, wk2, wv2, wv2], axis=0)
    nsa = dict(cw=HEAD_DIM, c_all=8, n_out=4, n_cmp=4, cmp_w=w2)
    lq = _round_up(l, 8)
    if past is None:
        ppool, ptable = _identity_pages(kvp)
        ksel, cmp = _gather_chunks(ppool, "rows", ptable, None, **nsa)
        wpool, wtable = _identity_pages(kvw)
        kwin = _gather_chunks(wpool, "rows", wtable, None, cw=HEAD_DIM, c_all=4, n_out=4)
        kv_win = kvw
        q_pos0, lk, win_pos0 = 0, l, 0
        hist = jnp.zeros((b, CONV_WIDTH - 1, x_b.shape[-1]), F32)
        h0 = jnp.zeros((b, x_b.shape[-1]), F32)
    else:
        pool, table, win_buf, hist, h0 = past
        assert l < NSA_CMP_BLOCK
        past_len = table.shape[1] * PAGE_SIZE
        ksel, cmp = _gather_chunks(_cols_pool(pool), "cols", table, _pad_rows(kvp, 8), **nsa)
        kv_win = jnp.concatenate([win_buf.reshape(b, win_buf.shape[1], -1), kvw], axis=1)
        lw_pad = _round_up(max(kv_win.shape[1], _win_span(min(tq, lq))), PAGE_SIZE)
        wpool, wtable = _identity_pages(_pad_rows(kv_win, lw_pad))
        kwin = _gather_chunks(wpool, "rows", wtable, None, cw=HEAD_DIM, c_all=4, n_out=4)
        q_pos0, lk, win_pos0 = past_len, past_len + l, past_len - win_buf.shape[1]
    nsp = _round_up(cmp.shape[3], LANE)
    cmp = jnp.pad(cmp, ((0, 0), (0, 0), (0, 0), (0, nsp - cmp.shape[3]), (0, 0)))
    cmp = cmp.reshape(b, 2, 2, 2, nsp, HEAD_DIM).transpose(0, 2, 1, 3, 4, 5).reshape(b, 4, 2, nsp, HEAD_DIM)
    o_a = _nsa_attention(_pad_rows(q, lq), _pad_rows(gates, lq), cmp, ksel, kwin,
                         tq=min(tq, lq), q_pos0=q_pos0, lk=lk, win_pos0=win_pos0)[:, :l]
    o_b, h_last = _conv_rglru(_pad_rows(x_b, lq), hist, h0, conv_w, conv_b, lru_wr, lru_br, lru_wi, lru_bi,
                              lru_lambda, tl=min(256, lq), n_valid=l)
    o_b = o_b[:, :l]
    fl = (lambda t: t.reshape(1, b * l, -1)) if flat else (lambda t: t)
    x_new = _out_project(fl(o_a), fl(z_a), fl(o_b), fl(z_b), xin, gate, w_out.astype(BF16),
                         jnp.ones((d,), F32), tl=tl, final_norm=False).reshape(b, l, d)
    win_keep = min(NSA_WINDOW, kv_win.shape[1])
    conv_src = jnp.concatenate([hist, x_b], axis=1) if l < CONV_WIDTH - 1 else x_b
    states = (kvp.reshape(b, l, 4, NSA_KV_HEADS, HEAD_DIM),
              kv_win[:, -win_keep:].reshape(b, win_keep, 2, NSA_KV_HEADS, HEAD_DIM),
              conv_src[:, -(CONV_WIDTH - 1):], h_last)
    return x_new, states


def _layer1(x, mod, past, w, final_g, *, tl, tq):
    (norm_g, w_in, lam_q1, lam_k1, lam_q2, lam_k2, subln_g, w_out) = w
    shift, scale, gate = mod
    b, l, d = x.shape
    w_p, segs = _l1_weight(w_in)
    flat = shift.shape[1] != 1
    xin = x.reshape(1, b * l, d) if flat else x
    qc, kvc, z_c, qd, kvd, qi, z_d, kiw = _project(xin, norm_g, shift, scale, w_p, segs, -1, tl)
    if flat:
        qc, kvc, z_c, qd, kvd, qi, z_d, kiw = (t.reshape(b, l, -1) for t in (qc, kvc, z_c, qd, kvd, qi, z_d, kiw))
    lq = _round_up(l, 8)
    diff_a = dict(cw=2 * DIFF_HALF, c_all=4, n_out=4)
    dsa_a = dict(cw=HEAD_DIM, c_all=4, n_out=4)
    kidx_a = dict(cw=IDX_DIM, c_all=1, n_out=1)
    if past is None:
        dpool, dtable = _identity_pages(kvc)
        diff_kv = _gather_chunks(dpool, "rows", dtable, None, **diff_a)
        spool, stable = _identity_pages(kvd)
        dsa_kv = _gather_chunks(spool, "rows", stable, None, **dsa_a)
        ipool, itable = _identity_pages(kiw)
        kidx = _gather_chunks(ipool, "rows", itable, None, **kidx_a)
        q_pos0, lk = 0, l
    else:
        diff_pool, dsa_pool, kidx_pool, table = past
        past_len = table.shape[1] * PAGE_SIZE
        diff_rows = diff_pool.reshape(diff_pool.shape[0], PAGE_SIZE * 4, 2 * DIFF_HALF)
        diff_kv = _gather_chunks(diff_rows, "strided", table, _pad_rows(kvc, 8), **diff_a)
        dsa_kv = _gather_chunks(_cols_pool(dsa_pool), "cols", table, _pad_rows(kvd, 8), **dsa_a)
        kidx = _gather_chunks(_cols_pool(kidx_pool), "cols", table, _pad_rows(kiw[:, :, :IDX_DIM], 8), **kidx_a)
        q_pos0, lk = past_len, past_len + l
    lamv = jnp.stack([lam_q1, lam_k1, lam_q2, lam_k2], axis=0)
    o_c = _diff_attention(_pad_rows(qc, lq), diff_kv, lamv, subln_g, tq=min(tq, lq), q_pos0=q_pos0)[:, :l]
    o_d = _dsa_attention(_pad_rows(qd, lq), _pad_rows(qi, lq), _pad_rows(kiw, lq), kidx, dsa_kv,
                         tq=min(tq, lq), q_pos0=q_pos0, lk=lk)[:, :l]
    fl = (lambda t: t.reshape(1, b * l, -1)) if flat else (lambda t: t)
    y = _out_project(fl(o_c), fl(z_c), fl(o_d), fl(z_d), xin, gate, w_out.astype(BF16), final_g,
                     tl=tl, final_norm=True).reshape(b, l, d)
    states = (kvc.reshape(b, l, 2, DIFF_KV_HEADS, 2 * DIFF_HALF),
              kvd.reshape(b, l, 2, DSA_KV_HEADS, HEAD_DIM), kiw[:, :, :IDX_DIM])
    return y, states


def kernel(x_prompt, x_sample, cache_l0_nsa_kv, state_l0_win_kv, state_l0_conv, state_l0_lru_h,
           cache_l1_diff_kv, cache_l1_dsa_kv, cache_l1_dsa_kidx, page_table, c_prompt, c_sample,
           l0_norm_g, l0_ada_w, l0_ada_b, l0_w_in, l0_cmp_wk, l0_cmp_wv, l0_conv_w, l0_conv_b,
           l0_lru_wr, l0_lru_br, l0_lru_wi, l0_lru_bi, l0_lru_lambda, l0_w_out,
           l1_norm_g, l1_ada_w, l1_ada_b, l1_w_in, l1_lam_q1, l1_lam_k1, l1_lam_q2, l1_lam_k2,
           l1_subln_g, l1_w_out, final_norm_g):
    bp, lp, d = x_prompt.shape
    bs, ls, _ = x_sample.shape
    c_all = jnp.concatenate([c_prompt, c_sample], axis=0)

    def mods(ada_w, ada_b):
        m = _modulation(c_all, ada_w, ada_b)
        mp = tuple(t[:, None] for t in jnp.split(m[:bp], 3, axis=-1))
        ms = tuple(jnp.repeat(t, ls, axis=0)[None] for t in jnp.split(m[bp:], 3, axis=-1))
        return mp, ms

    tl_p = min(512, lp)
    tl_s = bs * ls
    tq = min(Q_TILE, lp)
    w0 = (l0_norm_g, l0_w_in, l0_cmp_wk, l0_cmp_wv, l0_conv_w, l0_conv_b, l0_lru_wr, l0_lru_br,
          l0_lru_wi, l0_lru_bi, l0_lru_lambda, l0_w_out)
    mp0, ms0 = mods(l0_ada_w, l0_ada_b)
    xp, (nsa_kv_p, win_p, conv_p, h_p) = _layer0(x_prompt, mp0, None, w0, tl=tl_p, tq=tq)
    xs, (nsa_kv_s, win_s, conv_s, h_s) = _layer0(
        x_sample, ms0, (cache_l0_nsa_kv, page_table, state_l0_win_kv, state_l0_conv, state_l0_lru_h), w0,
        tl=tl_s, tq=tq)
    w1 = (l1_norm_g, l1_w_in, l1_lam_q1, l1_lam_k1, l1_lam_q2, l1_lam_k2, l1_subln_g, l1_w_out)
    mp1, ms1 = mods(l1_ada_w, l1_ada_b)
    y_p, (diff_kv_p, dsa_kv_p, kidx_p) = _layer1(xp, mp1, None, w1, final_norm_g, tl=tl_p, tq=tq)
    y_s, (diff_kv_s, dsa_kv_s, kidx_s) = _layer1(
        xs, ms1, (cache_l1_diff_kv, cache_l1_dsa_kv, cache_l1_dsa_kidx, page_table), w1, final_norm_g,
        tl=tl_s, tq=tq)
    return (y_p, y_s, nsa_kv_p, nsa_kv_s, win_p, win_s, conv_p, conv_s, h_p, h_s,
            diff_kv_p, diff_kv_s, dsa_kv_p, dsa_kv_s, kidx_p, kidx_s)
```

```python
import functools
import math

import jax
import jax.numpy as jnp
import numpy as np
from jax import lax
from jax.experimental import pallas as pl
from jax.experimental.pallas import tpu as pltpu

F32 = jnp.float32
BF16 = jnp.bfloat16
I32 = jnp.int32

PAGE_SIZE = 128
HEAD_DIM = 64
NSA_HEADS = 8
NSA_KV_HEADS = 2
NSA_CMP_BLOCK = 32
NSA_SEL_BLOCK = 64
NSA_TOPN = 16
NSA_WINDOW = 512
FORCE_SCORE = 1e4
LRU_BLOCKS = 8
LRU_C = 8.0
CONV_WIDTH = 4
DIFF_HALF = 64
DIFF_HEADS = 4
DIFF_KV_HEADS = 2
DIFF_LAMBDA_INIT = 0.8 - 0.6 * math.exp(-0.3 * 1)
DSA_HEADS = 8
DSA_KV_HEADS = 2
IDX_HEADS = 4
IDX_DIM = 64
DSA_TOPK_MAX = 256
NORM_EPS = 1e-6
NEG = -1e30
REMOVED = -3e38
INT_MIN = -2 ** 31
LOG2E = math.log2(math.e)

LANE = 128
VMEM_LIMIT = 56 * 1024 * 1024
KV_TILE = 512
Q_TILE = 256
MAX_KV_TILE = 2048
PROBE_BINADES = 3
COUNT_ROWS = 64


def _kv_tile(tq, lk_pad):
    t = min(KV_TILE * max(1, Q_TILE // tq), MAX_KV_TILE, lk_pad)
    while lk_pad % t:
        t -= KV_TILE
    return t


def _win_span(tq):
    return _round_up(NSA_WINDOW + tq, LANE)
PAGES_PER_STEP = 16


def _cparams(sem):
    return pltpu.CompilerParams(dimension_semantics=sem, vmem_limit_bytes=VMEM_LIMIT)


def _dot(a, b):
    return jnp.dot(a, b, preferred_element_type=F32)


def _dot_nt(a, b):
    return lax.dot_general(a, b, (((1,), (1,)), ((), ())), preferred_element_type=F32)


def _round_up(x, m):
    return (x + m - 1) // m * m


def _mod_kernel(c_ref, w_ref, b_ref, o_ref):
    o_ref[...] = jnp.dot(c_ref[...], w_ref[...], preferred_element_type=F32,
                         precision=lax.Precision.HIGHEST) + b_ref[...]


def _modulation(c, w, b):
    bc, d = c.shape
    n = w.shape[1]
    tn = 512
    return pl.pallas_call(
        _mod_kernel,
        grid=(n // tn,),
        in_specs=[pl.BlockSpec((bc, d), lambda j: (0, 0)),
                  pl.BlockSpec((d, tn), lambda j: (0, j)),
                  pl.BlockSpec((1, tn), lambda j: (0, j))],
        out_specs=pl.BlockSpec((bc, tn), lambda j: (0, j)),
        out_shape=jax.ShapeDtypeStruct((bc, n), F32),
        compiler_params=_cparams(("arbitrary",)),
        name="modulation",
    )(c, w, b.reshape(1, n))


def _proj_kernel(x_ref, g_ref, sh_ref, sc_ref, w_ref, *o_refs, segs, sigmoid_seg):
    x = x_ref[0]
    y = x * lax.rsqrt(jnp.mean(x * x, axis=-1, keepdims=True) + NORM_EPS)
    h = (y * g_ref[...]) * (1.0 + sc_ref[0]) + sh_ref[0]
    hb = h.astype(BF16)
    for i, ((a, b), o_ref) in enumerate(zip(segs, o_refs)):
        r = _dot(hb, w_ref[:, a:b])
        if i == sigmoid_seg:
            r = jax.nn.sigmoid(r)
        o_ref[0] = r


def _project(x, g, shift, scale, w, segs, sigmoid_seg, tl):
    b, l, d = x.shape
    ts = shift.shape[1]
    tm = 1 if ts == 1 else tl
    mod_map = (lambda bi, li: (bi, 0, 0)) if ts == 1 else (lambda bi, li: (bi, li, 0))
    p = w.shape[1]
    kern = functools.partial(_proj_kernel, segs=tuple(segs), sigmoid_seg=sigmoid_seg)
    return pl.pallas_call(
        kern,
        grid=(b, l // tl),
        in_specs=[pl.BlockSpec((1, tl, d), lambda bi, li: (bi, li, 0)),
                  pl.BlockSpec((1, d), lambda bi, li: (0, 0)),
                  pl.BlockSpec((1, tm, d), mod_map),
                  pl.BlockSpec((1, tm, d), mod_map),
                  pl.BlockSpec((d, p), lambda bi, li: (0, 0))],
        out_specs=[pl.BlockSpec((1, tl, e - a), lambda bi, li: (bi, li, 0)) for a, e in segs],
        out_shape=[jax.ShapeDtypeStruct((b, l, e - a), F32) for a, e in segs],
        compiler_params=_cparams(("parallel", "arbitrary")),
        name="norm_mod_project",
    )(x, g.reshape(1, d), shift, scale, w)


def _gather_kernel(pt_ref, *refs, pp, n_page_steps, layout, cw, c_all, n_cmp, n_out, has_new):
    page_refs = refs[:pp]
    pos = pp
    new_ref = None
    if has_new:
        new_ref = refs[pos]
        pos += 1
    w2_ref = None
    if n_cmp:
        w2_ref = refs[pos]
        pos += 1
    out_ref = refs[pos]
    cmp_ref = refs[pos + 1] if n_cmp else None
    j = pl.program_id(1)
    ones_col = jnp.where(lax.broadcasted_iota(I32, (PAGE_SIZE, LANE - cw), 1) == 0, 1.0, 0.0) if cw < LANE else None

    def chunk(i, c):
        if layout == "rows":
            return page_refs[i][0, :, c * cw:(c + 1) * cw]
        if layout == "cols":
            return page_refs[i][0, c].T
        return page_refs[i][0, pl.ds(c, PAGE_SIZE, stride=c_all), :]

    def emit(i, c, x):
        if c < n_cmp:
            nb = PAGE_SIZE // NSA_SEL_BLOCK
            prod = x.reshape(nb, NSA_SEL_BLOCK, cw) * w2_ref[c][None]
            cmp_ref[0, c, 0, i * nb:(i + 1) * nb, :] = jnp.sum(prod[:, :NSA_CMP_BLOCK], axis=1)
            cmp_ref[0, c, 1, i * nb:(i + 1) * nb, :] = jnp.sum(prod[:, NSA_CMP_BLOCK:], axis=1)
        else:
            if ones_col is not None:
                x = jnp.concatenate([x, ones_col], axis=-1)
            out_ref[0, c - n_cmp, i * PAGE_SIZE:(i + 1) * PAGE_SIZE, :] = x.astype(BF16)

    def pages():
        for i in range(pp):
            for c in range(n_cmp + n_out):
                emit(i, c, chunk(i, c))

    if has_new:
        pl.when(j < n_page_steps)(pages)

        @pl.when(j >= n_page_steps)
        def _():
            new = new_ref[0]
            for c in range(n_cmp + n_out):
                xc = new[:, c * cw:(c + 1) * cw]
                emit(0, c, jnp.concatenate([xc, jnp.zeros((PAGE_SIZE - xc.shape[0], cw), F32)], axis=0))
                for i in range(1, pp):
                    emit(i, c, jnp.zeros((PAGE_SIZE, cw), F32))
    else:
        pages()


def _gather_chunks(pool, layout, page_table, new, *, cw, c_all, n_out, n_cmp=0, cmp_w=None):
    bk, n_pages = page_table.shape
    pp = min(PAGES_PER_STEP, n_pages)
    assert n_pages % pp == 0
    n_page_steps = n_pages // pp
    has_new = new is not None
    n_steps = n_page_steps + (1 if has_new else 0)
    rows = pp * PAGE_SIZE
    lk_pad = n_steps * rows
    page_block = (1,) + pool.shape[1:]
    zeros = (0,) * (len(page_block) - 1)

    def page_map(i):
        def f(b, j, pt):
            return (pt[b, jnp.minimum(j * pp + i, n_pages - 1)],) + zeros
        return f

    in_specs = [pl.BlockSpec(page_block, page_map(i)) for i in range(pp)]
    args = [pool] * pp
    if has_new:
        in_specs.append(pl.BlockSpec((1,) + new.shape[1:], lambda b, j, pt: (b, 0, 0)))
        args.append(new)
    if n_cmp:
        in_specs.append(pl.BlockSpec(cmp_w.shape, lambda b, j, pt: (0, 0, 0)))
        args.append(cmp_w)
    out_specs = [pl.BlockSpec((1, n_out, rows, LANE), lambda b, j, pt: (b, 0, j, 0))]
    out_shape = [jax.ShapeDtypeStruct((bk, n_out, lk_pad, LANE), BF16)]
    if n_cmp:
        nb = rows // NSA_SEL_BLOCK
        out_specs.append(pl.BlockSpec((1, n_cmp, 2, nb, HEAD_DIM), lambda b, j, pt: (b, 0, 0, j, 0)))
        out_shape.append(jax.ShapeDtypeStruct((bk, n_cmp, 2, lk_pad // NSA_SEL_BLOCK, HEAD_DIM), F32))
    kern = functools.partial(_gather_kernel, pp=pp, n_page_steps=n_page_steps, layout=layout, cw=cw,
                             c_all=c_all, n_cmp=n_cmp, n_out=n_out, has_new=has_new)
    outs = pl.pallas_call(
        kern,
        grid_spec=pltpu.PrefetchScalarGridSpec(
            num_scalar_prefetch=1, grid=(bk, n_steps), in_specs=in_specs, out_specs=out_specs),
        out_shape=out_shape,
        compiler_params=_cparams(("parallel", "arbitrary")),
        name="gather_pages",
    )(page_table, *args)
    return outs if n_cmp else outs[0]


def _identity_pages(x):
    b, l, w = x.shape
    n_pages = l // PAGE_SIZE
    pool = x.reshape(b * n_pages, PAGE_SIZE, w)
    table = jnp.arange(b * n_pages, dtype=I32).reshape(b, n_pages)
    return pool, table


def _flash_tile(s_all, bias, v_tile, carry, n_heads, tq, l_in_acc):
    m, l, acc = carry
    s = s_all
    if bias is not None:
        s = jnp.concatenate([s_all[h * tq:(h + 1) * tq] + bias for h in range(n_heads)], axis=0)
    m_new = jnp.maximum(m, jnp.max(s, axis=-1, keepdims=True))
    p = jnp.exp2(s - m_new)
    alpha = jnp.exp2(m - m_new)
    acc = alpha * acc + _dot(p.astype(BF16), v_tile)
    if not l_in_acc:
        l = alpha * l + jnp.sum(p, axis=-1, keepdims=True)
    return m_new, l, acc


def _flash_init(rows, dv):
    return (jnp.full((rows, 1), NEG, F32), jnp.zeros((rows, 1), F32), jnp.zeros((rows, dv), F32))


def _flash_out(acc):
    return acc[:, :HEAD_DIM] * (1.0 / jnp.maximum(acc[:, HEAD_DIM:HEAD_DIM + 1], 1e-30))


def _stack_heads(q, n, width):
    return jnp.concatenate([q[:, h * width:(h + 1) * width] for h in range(n)], axis=0)


def _pad_lanes(x):
    return jnp.concatenate([x, jnp.zeros((x.shape[0], LANE - x.shape[1]), x.dtype)], axis=-1)


def _nsa_kernel(q_ref, g_ref, cmp_ref, ksel_ref, kwin_ref, o_ref, *,
                tq, tk, q_pos0, win_pos0, nsp, n_top):
    G = NSA_KV_HEADS
    hpg = NSA_HEADS // G
    gw = hpg * HEAD_DIM
    qi = pl.program_id(1)
    q0 = q_pos0 + qi * tq
    qpos = q0 + lax.broadcasted_iota(I32, (tq, 1), 0)
    qpos_r = jnp.concatenate([qpos] * hpg, axis=0)
    rows = hpg * tq
    q_all = q_ref[0] * (HEAD_DIM ** -0.5 * LOG2E)
    blk = lax.broadcasted_iota(I32, (1, nsp), 1)
    tqp = max(tq, LANE)
    blk_r = lax.broadcasted_iota(I32, (nsp, 1), 0)
    blk_rf = blk_r.astype(F32)
    cur_l = (q0 + lax.broadcasted_iota(I32, (1, tqp), 1)) // NSA_SEL_BLOCK
    vis_e = (blk * NSA_SEL_BLOCK + (NSA_CMP_BLOCK - 1)) <= qpos_r
    vis_o = (blk * NSA_SEL_BLOCK + (NSA_SEL_BLOCK - 1)) <= qpos_r
    start = pl.multiple_of(jnp.maximum(q0 - NSA_WINDOW - win_pos0, 0), 8)
    span = _win_span(tq)
    kpos_w = win_pos0 + start + lax.broadcasted_iota(I32, (1, span), 1)
    dlt = qpos - kpos_w
    bias_w = jnp.where(dlt >= 0, jnp.where(dlt < NSA_WINDOW, 0.0, NEG), NEG)

    def prologue(g):
        qs64 = _stack_heads(q_all[:, g * gw:(g + 1) * gw], hpg, HEAD_DIM)
        qs = _pad_lanes(qs64).astype(BF16)
        qs64 = qs64.astype(BF16)

        s_w = _dot_nt(qs, kwin_ref[0, g, pl.ds(start, span), :])
        _, _, acc_w = _flash_tile(s_w, bias_w, kwin_ref[0, G + g, pl.ds(start, span), :],
                                  _flash_init(rows, LANE), hpg, tq, True)
        o_w = _flash_out(acc_w)

        kce = cmp_ref[0, 2 * g, 0].astype(BF16)
        kco = cmp_ref[0, 2 * g, 1].astype(BF16)
        vce = cmp_ref[0, 2 * g + 1, 0].astype(BF16)
        vco = cmp_ref[0, 2 * g + 1, 1].astype(BF16)
        s_e = jnp.where(vis_e, _dot_nt(qs64, kce), NEG)
        s_o = jnp.where(vis_o, _dot_nt(qs64, kco), NEG)
        m = jnp.maximum(jnp.max(s_e, axis=-1, keepdims=True), jnp.max(s_o, axis=-1, keepdims=True))
        p_e = jnp.where(vis_e, jnp.exp2(s_e - m), 0.0)
        p_o = jnp.where(vis_o, jnp.exp2(s_o - m), 0.0)
        den = jnp.sum(p_e, axis=-1, keepdims=True) + jnp.sum(p_o, axis=-1, keepdims=True)
        inv = 1.0 / jnp.maximum(den, 1e-30)
        p_e = p_e * inv
        p_o = p_o * inv
        o_c = _dot(p_e.astype(BF16), vce) + _dot(p_o.astype(BF16), vco)

        pe_h = sum(p_e[h * tq:(h + 1) * tq] for h in range(hpg))
        po_h = sum(p_o[h * tq:(h + 1) * tq] for h in range(hpg))
        imp = pe_h + po_h
        if tqp > tq:
            imp = jnp.concatenate([imp, jnp.zeros((tqp - tq, nsp), F32)], axis=0)
        imp = imp.T
        imp = jnp.where((blk_r == cur_l) | (blk_r == 0), FORCE_SCORE, imp)
        imp = jnp.where(blk_r <= cur_l, imp, NEG)
        sel = jnp.zeros((nsp, tqp), F32)
        for _ in range(n_top):
            mx = jnp.max(imp, axis=0, keepdims=True)
            first = jnp.min(jnp.where(imp == mx, blk_rf, float(nsp)), axis=0, keepdims=True)
            pick = blk_rf == first
            sel = jnp.where(pick & (mx > 0.5 * NEG), 1.0, sel)
            imp = jnp.where(pick, REMOVED, imp)
        return qs, o_c, o_w, sel.T[:tq].astype(BF16)

    pro = [prologue(g) for g in range(G)]

    blk_col = lax.broadcasted_iota(I32, (nsp, 1), 0)

    def sel_step(j, carry):
        k0 = pl.multiple_of(j * tk, tk)
        kpos = k0 + lax.broadcasted_iota(I32, (1, tk), 1)
        expand = jnp.where(blk_col == kpos // NSA_SEL_BLOCK, 1.0, 0.0).astype(BF16)
        out = []
        for g in range(G):
            picked = _dot(pro[g][3], expand)
            bias = jnp.where(kpos <= qpos, jnp.where(picked > 0.5, 0.0, NEG), NEG)
            s = _dot_nt(pro[g][0], ksel_ref[0, g, pl.ds(k0, tk), :])
            out.append(_flash_tile(s, bias, ksel_ref[0, G + g, pl.ds(k0, tk), :], carry[g], hpg, tq, True))
        return tuple(out)

    n_tiles = (q0 + tq - 1) // tk + 1
    res = lax.fori_loop(0, n_tiles, sel_step, tuple(_flash_init(rows, LANE) for _ in range(G)))

    gates = g_ref[0]
    outs = []
    for g in range(G):
        _, o_c, o_w, _ = pro[g]
        o_s = _flash_out(res[g][2])
        for h in range(hpg):
            r = slice(h * tq, (h + 1) * tq)
            c = g * LANE + 3 * h
            outs.append(gates[:, c:c + 1] * o_c[r] + gates[:, c + 1:c + 2] * o_s[r]
                        + gates[:, c + 2:c + 3] * o_w[r])
    o_ref[0] = jnp.concatenate(outs, axis=-1)


def _nsa_attention(q, gates, cmp, ksel, kwin, *, tq, q_pos0, lk, win_pos0):
    bk, lq, _ = q.shape
    G = NSA_KV_HEADS
    nsp = cmp.shape[3]
    lk_pad = ksel.shape[2]
    lw_pad = kwin.shape[2]
    tk = _kv_tile(tq, lk_pad)
    ns = -(-lk // NSA_SEL_BLOCK)
    assert lw_pad >= _win_span(tq) and q_pos0 + lq <= lk_pad
    kern = functools.partial(_nsa_kernel, tq=tq, tk=tk, q_pos0=q_pos0, win_pos0=win_pos0,
                             nsp=nsp, n_top=min(NSA_TOPN, ns))
    qw = NSA_HEADS * HEAD_DIM
    return pl.pallas_call(
        kern,
        grid=(bk, lq // tq),
        in_specs=[pl.BlockSpec((1, tq, qw), lambda b, i: (b, i, 0)),
                  pl.BlockSpec((1, tq, G * LANE), lambda b, i: (b, i, 0)),
                  pl.BlockSpec((1, 2 * G, 2, nsp, HEAD_DIM), lambda b, i: (b, 0, 0, 0, 0)),
                  pl.BlockSpec((1, 2 * G, lk_pad, LANE), lambda b, i: (b, 0, 0, 0)),
                  pl.BlockSpec((1, 2 * G, lw_pad, LANE), lambda b, i: (b, 0, 0, 0))],
        out_specs=pl.BlockSpec((1, tq, qw), lambda b, i: (b, i, 0)),
        out_shape=jax.ShapeDtypeStruct((bk, lq, qw), F32),
        compiler_params=_cparams(("parallel", "arbitrary")),
        name="nsa_attention",
    )(q, gates, cmp, ksel, kwin)


def _shift_rows(x, d, fill):
    rolled = pltpu.roll(x, d, axis=0)
    row = lax.broadcasted_iota(I32, x.shape, 0)
    return jnp.where(row >= d, rolled, fill)


def _lru_kernel(x_ref, hist_ref, h0_ref, cw_ref, cb_ref, wr_ref, br_ref, wi_ref, bi_ref, lam_ref,
                o_ref, hl_ref, tail_ref, h_ref, *, tl, last_row):
    li = pl.program_id(1)

    @pl.when(li == 0)
    def _():
        tail_ref[...] = jnp.concatenate(
            [jnp.zeros((8 - (CONV_WIDTH - 1), x_ref.shape[-1]), F32), hist_ref[0]], axis=0)
        h_ref[...] = h0_ref[0]

    x = x_ref[0]
    xp = jnp.concatenate([tail_ref[...], x], axis=0)
    cw = cw_ref[...]
    conv = sum(xp[8 - (CONV_WIDTH - 1) + j:8 - (CONV_WIDTH - 1) + j + tl] * cw[j:j + 1]
               for j in range(CONV_WIDTH))
    conv = cb_ref[...] + conv
    tail_ref[...] = x[tl - 8:tl]

    cb16 = conv.astype(BF16)
    r = jax.nn.sigmoid(_dot(cb16, wr_ref[...]) + br_ref[...])
    ig = jax.nn.sigmoid(_dot(cb16, wi_ref[...]) + bi_ref[...])
    log_a = -LRU_C * r * jax.nn.softplus(-lam_ref[...])
    a = jnp.exp(log_a)
    th = jnp.tanh(log_a)
    b = jnp.sqrt(-2.0 * th / (1.0 - th)) * (ig * conv)

    d = 1
    while d < tl:
        a_prev = _shift_rows(a, d, 1.0)
        b_prev = _shift_rows(b, d, 0.0)
        b = a * b_prev + b
        a = a * a_prev
        d *= 2
    h = a * h_ref[...] + b
    o_ref[0] = h
    h_ref[...] = h[tl - 1:tl]

    @pl.when(li == pl.num_programs(1) - 1)
    def _():
        hl_ref[0] = h[last_row:last_row + 1]


def _block_diag(w):
    nb, bw, _ = w.shape
    eye = jnp.eye(nb, dtype=w.dtype)
    return (eye[:, None, :, None] * w[:, :, None, :]).reshape(nb * bw, nb * bw)


def _conv_rglru(x_b, hist, h0, conv_w, conv_b, w_r, b_r, w_i, b_i, lam, *, tl, n_valid):
    b, l, w = x_b.shape
    assert tl >= 8 and l % tl == 0 and n_valid > l - tl
    kern = functools.partial(_lru_kernel, tl=tl, last_row=(n_valid - 1) % tl)
    vec = lambda: pl.BlockSpec((1, w), lambda bi, li: (0, 0))
    h, h_last = pl.pallas_call(
        kern,
        grid=(b, l // tl),
        in_specs=[pl.BlockSpec((1, tl, w), lambda bi, li: (bi, li, 0)),
                  pl.BlockSpec((1, CONV_WIDTH - 1, w), lambda bi, li: (bi, 0, 0)),
                  pl.BlockSpec((1, 1, w), lambda bi, li: (bi, 0, 0)),
                  pl.BlockSpec((CONV_WIDTH, w), lambda bi, li: (0, 0)),
                  vec(),
                  pl.BlockSpec((w, w), lambda bi, li: (0, 0)), vec(),
                  pl.BlockSpec((w, w), lambda bi, li: (0, 0)), vec(), vec()],
        out_specs=[pl.BlockSpec((1, tl, w), lambda bi, li: (bi, li, 0)),
                   pl.BlockSpec((1, 1, w), lambda bi, li: (bi, 0, 0))],
        out_shape=[jax.ShapeDtypeStruct((b, l, w), F32), jax.ShapeDtypeStruct((b, 1, w), F32)],
        scratch_shapes=[pltpu.VMEM((8, w), F32), pltpu.VMEM((1, w), F32)],
        compiler_params=_cparams(("parallel", "arbitrary")),
        name="conv_rglru",
    )(x_b, hist, h0.reshape(b, 1, w), conv_w, conv_b.reshape(1, w),
      _block_diag(w_r).astype(BF16), b_r.reshape(1, w), _block_diag(w_i).astype(BF16), b_i.reshape(1, w),
      lam.reshape(1, w))
    return h, h_last.reshape(b, w)


def _out_kernel(oa_ref, za_ref, ob_ref, zb_ref, x_ref, gate_ref, w_ref, fg_ref, o_ref, *, final_norm):
    half = oa_ref.shape[-1]
    ma = (oa_ref[0] * jax.nn.silu(za_ref[0])).astype(BF16)
    mb = (ob_ref[0] * jax.nn.silu(zb_ref[0])).astype(BF16)
    y = _dot(ma, w_ref[0:half, :]) + _dot(mb, w_ref[half:2 * half, :])
    out = x_ref[0] + gate_ref[0] * y
    if final_norm:
        out = out * lax.rsqrt(jnp.mean(out * out, axis=-1, keepdims=True) + NORM_EPS) * fg_ref[...]
    o_ref[0] = out


def _out_project(o_a, z_a, o_b, z_b, x, gate, w_out, final_g, *, tl, final_norm):
    b, l, d = x.shape
    half = o_a.shape[-1]
    ts = gate.shape[1]
    tm = 1 if ts == 1 else tl
    mod_map = (lambda bi, li: (bi, 0, 0)) if ts == 1 else (lambda bi, li: (bi, li, 0))
    act = lambda: pl.BlockSpec((1, tl, half), lambda bi, li: (bi, li, 0))
    return pl.pallas_call(
        functools.partial(_out_kernel, final_norm=final_norm),
        grid=(b, l // tl),
        in_specs=[act(), act(), act(), act(),
                  pl.BlockSpec((1, tl, d), lambda bi, li: (bi, li, 0)),
                  pl.BlockSpec((1, tm, d), mod_map),
                  pl.BlockSpec((2 * half, d), lambda bi, li: (0, 0)),
                  pl.BlockSpec((1, d), lambda bi, li: (0, 0))],
        out_specs=pl.BlockSpec((1, tl, d), lambda bi, li: (bi, li, 0)),
        out_shape=jax.ShapeDtypeStruct((b, l, d), F32),
        compiler_params=_cparams(("parallel", "arbitrary")),
        name="out_project",
    )(o_a, z_a, o_b, z_b, x, gate, w_out, final_g.reshape(1, d))


def _diff_kernel(q_ref, k_ref, v_ref, lamv_ref, subg_ref, o_ref, *, tq, tk, q_pos0):
    hpg = DIFF_HEADS // DIFF_KV_HEADS
    qi = pl.program_id(2)
    q0 = q_pos0 + qi * tq
    n_maps = 2 * hpg
    rows = n_maps * tq
    qpos = q0 + lax.broadcasted_iota(I32, (tq, 1), 0)
    q = q_ref[0] * (DIFF_HALF ** -0.5 * LOG2E)
    zero = jnp.zeros((tq, DIFF_HALF), F32)
    parts = []
    for mp in range(2):
        for h in range(hpg):
            qh = q[:, (2 * h + mp) * DIFF_HALF:(2 * h + mp + 1) * DIFF_HALF]
            parts.append(jnp.concatenate([qh, zero] if mp == 0 else [zero, qh], axis=-1))
    qs = jnp.concatenate(parts, axis=0).astype(BF16)

    def step(masked):
        def f(j, carry):
            k0 = pl.multiple_of(j * tk, tk)
            s = _dot_nt(qs, k_ref[0, 0, pl.ds(k0, tk), :])
            bias = None
            if masked:
                kpos = k0 + lax.broadcasted_iota(I32, (1, tk), 1)
                bias = jnp.where(kpos <= qpos, 0.0, NEG)
            return _flash_tile(s, bias, v_ref[0, 0, pl.ds(k0, tk), :], carry, n_maps, tq, False)
        return f

    n_full = (q0 + 1) // tk
    n_tiles = (q0 + tq - 1) // tk + 1
    carry = lax.fori_loop(0, n_full, step(False), _flash_init(rows, 2 * DIFF_HALF))
    m, l, acc = lax.fori_loop(n_full, n_tiles, step(True), carry)
    o = acc * (1.0 / jnp.maximum(l, 1e-30))
    lq = lamv_ref[...]
    lam = (jnp.exp(jnp.sum(lq[0:1] * lq[1:2], axis=-1, keepdims=True))
           - jnp.exp(jnp.sum(lq[2:3] * lq[3:4], axis=-1, keepdims=True)) + DIFF_LAMBDA_INIT)
    half = hpg * tq
    od = o[0:half] - lam * o[half:2 * half]
    od = od * lax.rsqrt(jnp.mean(od * od, axis=-1, keepdims=True) + NORM_EPS)
    od = od * subg_ref[...] * (1.0 - DIFF_LAMBDA_INIT)
    o_ref[0] = jnp.concatenate([od[h * tq:(h + 1) * tq] for h in range(hpg)], axis=-1)


def _diff_attention(q, kv, lamv, subln_g, *, tq, q_pos0):
    bk, lq, _ = q.shape
    G = DIFF_KV_HEADS
    lk_pad = kv.shape[2]
    tk = _kv_tile(tq, lk_pad)
    gw = (DIFF_HEADS // G) * 2 * DIFF_HALF
    assert q_pos0 + lq <= lk_pad
    return pl.pallas_call(
        functools.partial(_diff_kernel, tq=tq, tk=tk, q_pos0=q_pos0),
        grid=(bk, G, lq // tq),
        in_specs=[pl.BlockSpec((1, tq, gw), lambda b, g, i: (b, i, g)),
                  pl.BlockSpec((1, 1, lk_pad, 2 * DIFF_HALF), lambda b, g, i: (b, g, 0, 0)),
                  pl.BlockSpec((1, 1, lk_pad, 2 * DIFF_HALF), lambda b, g, i: (b, G + g, 0, 0)),
                  pl.BlockSpec((4, DIFF_HALF), lambda b, g, i: (0, 0)),
                  pl.BlockSpec((1, 2 * DIFF_HALF), lambda b, g, i: (0, 0))],
        out_specs=pl.BlockSpec((1, tq, gw), lambda b, g, i: (b, i, g)),
        out_shape=jax.ShapeDtypeStruct((bk, lq, DIFF_HEADS * 2 * DIFF_HALF), F32),
        compiler_params=_cparams(("parallel", "parallel", "arbitrary")),
        name="diff_attention",
    )(q, kv, kv, lamv, subln_g.reshape(1, 2 * DIFF_HALF))


def _dsa_kernel(q_ref, qi_ref, kw_ref, kidx_ref, kv_ref, tri_ref, o_ref, key_ref, keyt_ref, *,
                tq, tk, q_pos0, n_sel, one_block):
    G = DSA_KV_HEADS
    hpg = DSA_HEADS // G
    transposed = tq % LANE == 0
    q0 = q_pos0 + (0 if one_block else pl.program_id(1) * tq)
    qpos = q0 + lax.broadcasted_iota(I32, (tq, 1), 0)
    n_tiles = (q0 + tq - 1) // tk + 1
    lanes = tk // LANE

    qidx = _pad_lanes(_stack_heads(qi_ref[0] * (IDX_DIM ** -0.5), IDX_HEADS, IDX_DIM)).astype(BF16)
    wi = kw_ref[0][:, IDX_DIM:IDX_DIM + IDX_HEADS] * (IDX_HEADS ** -0.5)

    def score_step(j, kmax):
        k0 = pl.multiple_of(j * tk, tk)
        s_all = _dot_nt(qidx, kidx_ref[0, 0, pl.ds(k0, tk), :])
        score = jnp.zeros((tq, tk), F32)
        for h in range(IDX_HEADS):
            score = score + wi[:, h:h + 1] * jnp.maximum(s_all[h * tq:(h + 1) * tq], 0.0)
        kpos = k0 + lax.broadcasted_iota(I32, (1, tk), 1)
        bits = pltpu.bitcast(score, I32)
        key = jnp.where(bits < 0, bits ^ 0x7FFFFFFF, bits)
        key = jnp.where(score == 0.0, 0, key)
        key = jnp.where(kpos <= qpos, jnp.where(score > 0.5 * NEG, key, INT_MIN), INT_MIN)
        key_ref[j] = key
        if not transposed:
            return jnp.maximum(kmax, jnp.max(key, axis=-1, keepdims=True))
        key_t = key.T
        keyt_ref[j] = key_t
        return jnp.maximum(kmax, jnp.max(key_t, axis=0, keepdims=True))

    kmax = lax.fori_loop(0, n_tiles, score_step, jnp.full((1, tq) if transposed else (tq, 1), INT_MIN, I32))

    def count(*bounds):
        def f(j, accs):
            keys = keyt_ref[j] if transposed else key_ref[j]
            out = []
            for bound, acc in zip(bounds, accs):
                hit = jnp.where(keys >= bound, 1.0, 0.0)
                if transposed:
                    acc = acc + jnp.sum(hit.reshape(tk // COUNT_ROWS, COUNT_ROWS, tq), axis=0)
                else:
                    for c in range(lanes):
                        acc = acc + hit[:, c * LANE:(c + 1) * LANE]
                out.append(acc)
            return tuple(out)
        acc0 = jnp.zeros((COUNT_ROWS, tq) if transposed else (tq, LANE), F32)
        accs = lax.fori_loop(0, n_tiles, f, tuple(acc0 for _ in bounds),
                             unroll=one_block)
        return [jnp.sum(acc, axis=0 if transposed else -1, keepdims=True) for acc in accs]

    k_f = float(n_sel)
    probe = jnp.maximum(kmax - (PROBE_BINADES << 23), 1)
    c_adm, c_nn, c_pos, c_probe = count(INT_MIN + 1, 0, 1, probe)
    few, pos, zero, high = c_adm < k_f, c_pos >= k_f, c_nn >= k_f, c_probe >= k_f
    lo0 = jnp.where(pos, jnp.where(high, probe, 1), jnp.where(zero, 0, INT_MIN))
    hi0 = jnp.where(pos, jnp.where(high, kmax, probe - 1), jnp.where(zero, 0, jnp.where(few, INT_MIN, -1)))

    def unfinished(lo_hi):
        lo, hi = lo_hi
        return jnp.max(jnp.where(lo < hi, 1.0, 0.0)) > 0.0

    def bisect(lo_hi):
        lo, hi = lo_hi
        mid = (lo >> 1) + (hi >> 1) + ((lo | hi) & 1)
        cnt, = count(mid)
        lo = jnp.where(cnt >= k_f, mid, lo)
        hi = jnp.where(cnt > k_f, hi, jnp.where(cnt == k_f, mid, mid - 1))
        return lo, hi

    thr, _ = lax.while_loop(unfinished, lambda s: bisect(bisect(s)), (lo0, hi0))
    thr = jnp.maximum(thr, INT_MIN + 1)
    n_gt, = count(thr + 1)
    need = k_f - n_gt

    def along_rows(v):
        rep = jnp.broadcast_to(v, (LANE, tq)).T
        return jnp.concatenate([rep] * lanes, axis=-1)

    if transposed:
        thr = along_rows(thr)
        need = along_rows(need)

    q = q_ref[0] * (HEAD_DIM ** -0.5 * LOG2E)
    qs = [_pad_lanes(_stack_heads(q[:, g * hpg * HEAD_DIM:(g + 1) * hpg * HEAD_DIM], hpg, HEAD_DIM)).astype(BF16)
          for g in range(G)]
    rows = hpg * tq

    def att_step(j, carry):
        seen, flash = carry
        k0 = pl.multiple_of(j * tk, tk)
        key = key_ref[j]
        tied = key == thr
        tied_b = jnp.where(tied, 1.0, 0.0).astype(BF16)
        ranks = []
        for c in range(tk // KV_TILE):
            r = _dot(tied_b[:, c * KV_TILE:(c + 1) * KV_TILE], tri_ref[...]) + seen
            ranks.append(r)
            seen = r[:, KV_TILE - 1:KV_TILE]
        rank = jnp.concatenate(ranks, axis=-1)
        bias = jnp.where(key > thr, 0.0, jnp.where(tied, jnp.where(rank <= need, 0.0, NEG), NEG))
        out = []
        for g in range(G):
            s = _dot_nt(qs[g], kv_ref[0, g, pl.ds(k0, tk), :])
            out.append(_flash_tile(s, bias, kv_ref[0, G + g, pl.ds(k0, tk), :], flash[g], hpg, tq, True))
        return seen, tuple(out)

    _, res = lax.fori_loop(0, n_tiles, att_step,
                           (jnp.zeros((tq, 1), F32), tuple(_flash_init(rows, LANE) for _ in range(G))))
    outs = []
    for g in range(G):
        o = _flash_out(res[g][2])
        outs.extend(o[h * tq:(h + 1) * tq] for h in range(hpg))
    o_ref[0] = jnp.concatenate(outs, axis=-1)


def _dsa_attention(q, qi, kw, kidx, kv, *, tq, q_pos0, lk):
    bk, lq, _ = q.shape
    lk_pad = kv.shape[2]
    tk = _kv_tile(tq, lk_pad)
    n_sel = min(DSA_TOPK_MAX, lk // 4)
    assert q_pos0 + lq <= lk_pad and tk >= n_sel
    kern = functools.partial(_dsa_kernel, tq=tq, tk=tk, q_pos0=q_pos0, n_sel=n_sel, one_block=lq == tq)
    assert tk % KV_TILE == 0
    tri = jnp.triu(jnp.ones((KV_TILE, KV_TILE), BF16))
    return pl.pallas_call(
        kern,
        grid=(bk, lq // tq),
        in_specs=[pl.BlockSpec((1, tq, DSA_HEADS * HEAD_DIM), lambda b, i: (b, i, 0)),
                  pl.BlockSpec((1, tq, IDX_HEADS * IDX_DIM), lambda b, i: (b, i, 0)),
                  pl.BlockSpec((1, tq, LANE), lambda b, i: (b, i, 0)),
                  pl.BlockSpec((1, 1, lk_pad, LANE), lambda b, i: (b, 0, 0, 0)),
                  pl.BlockSpec((1, 4, lk_pad, LANE), lambda b, i: (b, 0, 0, 0)),
                  pl.BlockSpec((KV_TILE, KV_TILE), lambda b, i: (0, 0))],
        out_specs=pl.BlockSpec((1, tq, DSA_HEADS * HEAD_DIM), lambda b, i: (b, i, 0)),
        out_shape=jax.ShapeDtypeStruct((bk, lq, DSA_HEADS * HEAD_DIM), F32),
        scratch_shapes=[pltpu.VMEM((lk_pad // tk, tq, tk), I32),
                        pltpu.VMEM((lk_pad // tk, tk, tq) if tq % LANE == 0 else (8, LANE), I32)],
        compiler_params=_cparams(("parallel", "arbitrary")),
        name="dsa_attention",
    )(q, qi, kw, kidx, kv, tri)


L0_SIZES = (512, 768, 24, 512, 512, 512)
L1_SIZES = (512, 256, 256, 512, 512, 128, 128, 256, 64, 4, 512)


def _l0_weight(w_in):
    d = w_in.shape[0]
    q, kv6, gl, z_a, x_b, z_b = jnp.split(w_in, np.cumsum(L0_SIZES)[:-1].tolist(), axis=1)
    pad = jnp.zeros((d, LANE - 12), w_in.dtype)
    w = jnp.concatenate([q, kv6, z_a, x_b, z_b, gl[:, :12], pad, gl[:, 12:], pad], axis=1)
    segs = [(0, 512), (512, 1024), (1024, 1280), (1280, 1792), (1792, 2304), (2304, 2816), (2816, 3072)]
    return w.astype(BF16), segs


def _l1_weight(w_in):
    d = w_in.shape[0]
    qc, kc, vc, z_c, qd, kd, vd, qi, ki, wi, z_d = jnp.split(w_in, np.cumsum(L1_SIZES)[:-1].tolist(), axis=1)
    pad = jnp.zeros((d, LANE - IDX_DIM - IDX_HEADS), w_in.dtype)
    w = jnp.concatenate([qc, kc, vc, z_c, qd, kd, vd, qi, z_d, ki, wi, pad], axis=1)
    segs = [(0, 512), (512, 1024), (1024, 1536), (1536, 2048), (2048, 2304), (2304, 2560), (2560, 3072),
            (3072, 3200)]
    return w.astype(BF16), segs


def _pad_rows(x, n):
    return jnp.pad(x, ((0, 0), (0, n - x.shape[1]), (0, 0)))


def _cols_pool(pool):
    npool, ps = pool.shape[:2]
    cw = pool.shape[-1]
    perm = (0,) + tuple(range(2, pool.ndim)) + (1,)
    return jnp.transpose(pool, perm).reshape(npool, -1, cw, ps)


def _layer0(x, mod, past, w, *, tl, tq):
    (norm_g, w_in, cmp_wk, cmp_wv, conv_w, conv_b, lru_wr, lru_br, lru_wi, lru_bi, lru_lambda, w_out) = w
    shift, scale, gate = mod
    b, l, d = x.shape
    w_p, segs = _l0_weight(w_in)
    flat = shift.shape[1] != 1
    xin = x.reshape(1, b * l, d) if flat else x
    q, kvp, kvw, z_a, x_b, z_b, gates = _project(xin, norm_g, shift, scale, w_p, segs, 6, tl)
    if flat:
        q, kvp, kvw, z_a, x_b, z_b, gates = (t.reshape(b, l, -1) for t in (q, kvp, kvw, z_a, x_b, z_b, gates))
    wk2 = jnp.concatenate([cmp_wk, cmp_wk], axis=0)
    wv2 = jnp.concatenate([cmp_wv, cmp_wv], axis=0)
    w2 = jnp.stack([wk2, wk2, wv2, wv2], axis=0)
    nsa = dict(cw=HEAD_DIM, c_all=8, n_out=4, n_cmp=4, cmp_w=w2)
    lq = _round_up(l, 8)
    if past is None:
        ppool, ptable = _identity_pages(kvp)
        ksel, cmp = _gather_chunks(ppool, "rows", ptable, None, **nsa)
        wpool, wtable = _identity_pages(kvw)
        kwin = _gather_chunks(wpool, "rows", wtable, None, cw=HEAD_DIM, c_all=4, n_out=4)
        kv_win = kvw
        q_pos0, lk, win_pos0 = 0, l, 0
        hist = jnp.zeros((b, CONV_WIDTH - 1, x_b.shape[-1]), F32)
        h0 = jnp.zeros((b, x_b.shape[-1]), F32)
    else:
        pool, table, win_buf, hist, h0 = past
        assert l < NSA_CMP_BLOCK
        past_len = table.shape[1] * PAGE_SIZE
        ksel, cmp = _gather_chunks(_cols_pool(pool), "cols", table, _pad_rows(kvp, 8), **nsa)
        kv_win = jnp.concatenate([win_buf.reshape(b, win_buf.shape[1], -1), kvw], axis=1)
        lw_pad = _round_up(max(kv_win.shape[1], _win_span(min(tq, lq))), PAGE_SIZE)
        wpool, wtable = _identity_pages(_pad_rows(kv_win, lw_pad))
        kwin = _gather_chunks(wpool, "rows", wtable, None, cw=HEAD_DIM, c_all=4, n_out=4)
        q_pos0, lk, win_pos0 = past_len, past_len + l, past_len - win_buf.shape[1]
    nsp = _round_up(cmp.shape[3], LANE)
    cmp = jnp.pad(cmp, ((0, 0), (0, 0), (0, 0), (0, nsp - cmp.shape[3]), (0, 0)))
    cmp = cmp.reshape(b, 2, 2, 2, nsp, HEAD_DIM).transpose(0, 2, 1, 3, 4, 5).reshape(b, 4, 2, nsp, HEAD_DIM)
    o_a = _nsa_attention(_pad_rows(q, lq), _pad_rows(gates, lq), cmp, ksel, kwin,
                         tq=min(tq, lq), q_pos0=q_pos0, lk=lk, win_pos0=win_pos0)[:, :l]
    o_b, h_last = _conv_rglru(_pad_rows(x_b, lq), hist, h0, conv_w, conv_b, lru_wr, lru_br, lru_wi, lru_bi,
                              lru_lambda, tl=min(256, lq), n_valid=l)
    o_b = o_b[:, :l]
    fl = (lambda t: t.reshape(1, b * l, -1)) if flat else (lambda t: t)
    x_new = _out_project(fl(o_a), fl(z_a), fl(o_b), fl(z_b), xin, gate, w_out.astype(BF16),
                         jnp.ones((d,), F32), tl=tl, final_norm=False).reshape(b, l, d)
    win_keep = min(NSA_WINDOW, kv_win.shape[1])
    conv_src = jnp.concatenate([hist, x_b], axis=1) if l < CONV_WIDTH - 1 else x_b
    states = (kvp.reshape(b, l, 4, NSA_KV_HEADS, HEAD_DIM),
              kv_win[:, -win_keep:].reshape(b, win_keep, 2, NSA_KV_HEADS, HEAD_DIM),
              conv_src[:, -(CONV_WIDTH - 1):], h_last)
    return x_new, states


def _layer1(x, mod, past, w, final_g, *, tl, tq):
    (norm_g, w_in, lam_q1, lam_k1, lam_q2, lam_k2, subln_g, w_out) = w
    shift, scale, gate = mod
    b, l, d = x.shape
    w_p, segs = _l1_weight(w_in)
    flat = shift.shape[1] != 1
    xin = x.reshape(1, b * l, d) if flat else x
    qc, kvc, z_c, qd, kvd, qi, z_d, kiw = _project(xin, norm_g, shift, scale, w_p, segs, -1, tl)
    if flat:
        qc, kvc, z_c, qd, kvd, qi, z_d, kiw = (t.reshape(b, l, -1) for t in (qc, kvc, z_c, qd, kvd, qi, z_d, kiw))
    lq = _round_up(l, 8)
    diff_a = dict(cw=2 * DIFF_HALF, c_all=4, n_out=4)
    dsa_a = dict(cw=HEAD_DIM, c_all=4, n_out=4)
    kidx_a = dict(cw=IDX_DIM, c_all=1, n_out=1)
    if past is None:
        dpool, dtable = _identity_pages(kvc)
        diff_kv = _gather_chunks(dpool, "rows", dtable, None, **diff_a)
        spool, stable = _identity_pages(kvd)
        dsa_kv = _gather_chunks(spool, "rows", stable, None, **dsa_a)
        ipool, itable = _identity_pages(kiw)
        kidx = _gather_chunks(ipool, "rows", itable, None, **kidx_a)
        q_pos0, lk = 0, l
    else:
        diff_pool, dsa_pool, kidx_pool, table = past
        past_len = table.shape[1] * PAGE_SIZE
        diff_rows = diff_pool.reshape(diff_pool.shape[0], PAGE_SIZE * 4, 2 * DIFF_HALF)
        diff_kv = _gather_chunks(diff_rows, "strided", table, _pad_rows(kvc, 8), **diff_a)
        dsa_kv = _gather_chunks(_cols_pool(dsa_pool), "cols", table, _pad_rows(kvd, 8), **dsa_a)
        kidx = _gather_chunks(_cols_pool(kidx_pool), "cols", table, _pad_rows(kiw[:, :, :IDX_DIM], 8), **kidx_a)
        q_pos0, lk = past_len, past_len + l
    lamv = jnp.stack([lam_q1, lam_k1, lam_q2, lam_k2], axis=0)
    o_c = _diff_attention(_pad_rows(qc, lq), diff_kv, lamv, subln_g, tq=min(tq, lq), q_pos0=q_pos0)[:, :l]
    o_d = _dsa_attention(_pad_rows(qd, lq), _pad_rows(qi, lq), _pad_rows(kiw, lq), kidx, dsa_kv,
                         tq=min(tq, lq), q_pos0=q_pos0, lk=lk)[:, :l]
    fl = (lambda t: t.reshape(1, b * l, -1)) if flat else (lambda t: t)
    y = _out_project(fl(o_c), fl(z_c), fl(o_d), fl(z_d), xin, gate, w_out.astype(BF16), final_g,
                     tl=tl, final_norm=True).reshape(b, l, d)
    states = (kvc.reshape(b, l, 2, DIFF_KV_HEADS, 2 * DIFF_HALF),
              kvd.reshape(b, l, 2, DSA_KV_HEADS, HEAD_DIM), kiw[:, :, :IDX_DIM])
    return y, states


def kernel(x_prompt, x_sample, cache_l0_nsa_kv, state_l0_win_kv, state_l0_conv, state_l0_lru_h,
           cache_l1_diff_kv, cache_l1_dsa_kv, cache_l1_dsa_kidx, page_table, c_prompt, c_sample,
           l0_norm_g, l0_ada_w, l0_ada_b, l0_w_in, l0_cmp_wk, l0_cmp_wv, l0_conv_w, l0_conv_b,
           l0_lru_wr, l0_lru_br, l0_lru_wi, l0_lru_bi, l0_lru_lambda, l0_w_out,
           l1_norm_g, l1_ada_w, l1_ada_b, l1_w_in, l1_lam_q1, l1_lam_k1, l1_lam_q2, l1_lam_k2,
           l1_subln_g, l1_w_out, final_norm_g):
    bp, lp, d = x_prompt.shape
    bs, ls, _ = x_sample.shape
    c_all = jnp.concatenate([c_prompt, c_sample], axis=0)

    def mods(ada_w, ada_b):
        m = _modulation(c_all, ada_w, ada_b)
        mp = tuple(t[:, None] for t in jnp.split(m[:bp], 3, axis=-1))
        ms = tuple(jnp.repeat(t, ls, axis=0)[None] for t in jnp.split(m[bp:], 3, axis=-1))
        return mp, ms

    tl_p = min(512, lp)
    tl_s = bs * ls
    tq = min(Q_TILE, lp)
    w0 = (l0_norm_g, l0_w_in, l0_cmp_wk, l0_cmp_wv, l0_conv_w, l0_conv_b, l0_lru_wr, l0_lru_br,
          l0_lru_wi, l0_lru_bi, l0_lru_lambda, l0_w_out)
    mp0, ms0 = mods(l0_ada_w, l0_ada_b)
    xp, (nsa_kv_p, win_p, conv_p, h_p) = _layer0(x_prompt, mp0, None, w0, tl=tl_p, tq=tq)
    xs, (nsa_kv_s, win_s, conv_s, h_s) = _layer0(
        x_sample, ms0, (cache_l0_nsa_kv, page_table, state_l0_win_kv, state_l0_conv, state_l0_lru_h), w0,
        tl=tl_s, tq=tq)
    w1 = (l1_norm_g, l1_w_in, l1_lam_q1, l1_lam_k1, l1_lam_q2, l1_lam_k2, l1_subln_g, l1_w_out)
    mp1, ms1 = mods(l1_ada_w, l1_ada_b)
    y_p, (diff_kv_p, dsa_kv_p, kidx_p) = _layer1(xp, mp1, None, w1, final_norm_g, tl=tl_p, tq=tq)
    y_s, (diff_kv_s, dsa_kv_s, kidx_s) = _layer1(
        xs, ms1, (cache_l1_diff_kv, cache_l1_dsa_kv, cache_l1_dsa_kidx, page_table), w1, final_norm_g,
        tl=tl_s, tq=tq)
    return (y_p, y_s, nsa_kv_p, nsa_kv_s, win_p, win_s, conv_p, conv_s, h_p, h_s,
            diff_kv_p, diff_kv_s, dsa_kv_p, dsa_kv_s, kidx_p, kidx_s)
```

```python
import functools
import math

import jax
import jax.numpy as jnp
import numpy as np
from jax import lax
from jax.experimental import pallas as pl
from jax.experimental.pallas import tpu as pltpu

F32 = jnp.float32
BF16 = jnp.bfloat16
I32 = jnp.int32

PAGE_SIZE = 128
HEAD_DIM = 64
NSA_HEADS = 8
NSA_KV_HEADS = 2
NSA_CMP_BLOCK = 32
NSA_SEL_BLOCK = 64
NSA_TOPN = 16
NSA_WINDOW = 512
FORCE_SCORE = 1e4
LRU_BLOCKS = 8
LRU_C = 8.0
CONV_WIDTH = 4
DIFF_HALF = 64
DIFF_HEADS = 4
DIFF_KV_HEADS = 2
DIFF_LAMBDA_INIT = 0.8 - 0.6 * math.exp(-0.3 * 1)
DSA_HEADS = 8
DSA_KV_HEADS = 2
IDX_HEADS = 4
IDX_DIM = 64
DSA_TOPK_MAX = 256
NORM_EPS = 1e-6
NEG = -1e30
REMOVED = -3e38
INT_MIN = -2 ** 31
LOG2E = math.log2(math.e)

LANE = 128
VMEM_LIMIT = 56 * 1024 * 1024
KV_TILE = 512
Q_TILE = 256
MAX_KV_TILE = 2048
PROBE_BINADES = 3
COUNT_ROWS = 64


def _kv_tile(tq, lk_pad):
    t = min(KV_TILE * max(1, Q_TILE // tq), MAX_KV_TILE, lk_pad)
    while lk_pad % t:
        t -= KV_TILE
    return t


def _win_span(tq):
    return _round_up(NSA_WINDOW + tq, LANE)
PAGES_PER_STEP = 16


def _cparams(sem):
    return pltpu.CompilerParams(dimension_semantics=sem, vmem_limit_bytes=VMEM_LIMIT)


def _dot(a, b):
    return jnp.dot(a, b, preferred_element_type=F32)


def _dot_nt(a, b):
    return lax.dot_general(a, b, (((1,), (1,)), ((), ())), preferred_element_type=F32)


def _round_up(x, m):
    return (x + m - 1) // m * m


def _mod_kernel(c_ref, w_ref, b_ref, o_ref):
    o_ref[...] = jnp.dot(c_ref[...], w_ref[...], preferred_element_type=F32,
                         precision=lax.Precision.HIGHEST) + b_ref[...]


def _modulation(c, w, b):
    bc, d = c.shape
    n = w.shape[1]
    tn = 512
    return pl.pallas_call(
        _mod_kernel,
        grid=(n // tn,),
        in_specs=[pl.BlockSpec((bc, d), lambda j: (0, 0)),
                  pl.BlockSpec((d, tn), lambda j: (0, j)),
                  pl.BlockSpec((1, tn), lambda j: (0, j))],
        out_specs=pl.BlockSpec((bc, tn), lambda j: (0, j)),
        out_shape=jax.ShapeDtypeStruct((bc, n), F32),
        compiler_params=_cparams(("arbitrary",)),
        name="modulation",
    )(c, w, b.reshape(1, n))


def _proj_kernel(x_ref, g_ref, sh_ref, sc_ref, w_ref, *o_refs, segs, sigmoid_seg):
    x = x_ref[0]
    y = x * lax.rsqrt(jnp.mean(x * x, axis=-1, keepdims=True) + NORM_EPS)
    h = (y * g_ref[...]) * (1.0 + sc_ref[0]) + sh_ref[0]
    hb = h.astype(BF16)
    for i, ((a, b), o_ref) in enumerate(zip(segs, o_refs)):
        r = _dot(hb, w_ref[:, a:b])
        if i == sigmoid_seg:
            r = jax.nn.sigmoid(r)
        o_ref[0] = r


def _project(x, g, shift, scale, w, segs, sigmoid_seg, tl):
    b, l, d = x.shape
    ts = shift.shape[1]
    tm = 1 if ts == 1 else tl
    mod_map = (lambda bi, li: (bi, 0, 0)) if ts == 1 else (lambda bi, li: (bi, li, 0))
    p = w.shape[1]
    kern = functools.partial(_proj_kernel, segs=tuple(segs), sigmoid_seg=sigmoid_seg)
    return pl.pallas_call(
        kern,
        grid=(b, l // tl),
        in_specs=[pl.BlockSpec((1, tl, d), lambda bi, li: (bi, li, 0)),
                  pl.BlockSpec((1, d), lambda bi, li: (0, 0)),
                  pl.BlockSpec((1, tm, d), mod_map),
                  pl.BlockSpec((1, tm, d), mod_map),
                  pl.BlockSpec((d, p), lambda bi, li: (0, 0))],
        out_specs=[pl.BlockSpec((1, tl, e - a), lambda bi, li: (bi, li, 0)) for a, e in segs],
        out_shape=[jax.ShapeDtypeStruct((b, l, e - a), F32) for a, e in segs],
        compiler_params=_cparams(("parallel", "arbitrary")),
        name="norm_mod_project",
    )(x, g.reshape(1, d), shift, scale, w)


def _gather_kernel(pt_ref, *refs, pp, n_page_steps, layout, cw, c_all, n_cmp, n_out, has_new):
    page_refs = refs[:pp]
    pos = pp
    new_ref = None
    if has_new:
        new_ref = refs[pos]
        pos += 1
    w2_ref = None
    if n_cmp:
        w2_ref = refs[pos]
        pos += 1
    out_ref = refs[pos]
    cmp_ref = refs[pos + 1] if n_cmp else None
    j = pl.program_id(1)
    ones_col = jnp.where(lax.broadcasted_iota(I32, (PAGE_SIZE, LANE - cw), 1) == 0, 1.0, 0.0) if cw < LANE else None

    def chunk(i, c):
        if layout == "rows":
            return page_refs[i][0, :, c * cw:(c + 1) * cw]
        if layout == "cols":
            return page_refs[i][0, c].T
        return page_refs[i][0, pl.ds(c, PAGE_SIZE, stride=c_all), :]

    def emit(i, c, x):
        if c < n_cmp:
            nb = PAGE_SIZE // NSA_SEL_BLOCK
            prod = x.reshape(nb, NSA_SEL_BLOCK, cw) * w2_ref[c][None]
            cmp_ref[0, c, 0, i * nb:(i + 1) * nb, :] = jnp.sum(prod[:, :NSA_CMP_BLOCK], axis=1)
            cmp_ref[0, c, 1, i * nb:(i + 1) * nb, :] = jnp.sum(prod[:, NSA_CMP_BLOCK:], axis=1)
        else:
            if ones_col is not None:
                x = jnp.concatenate([x, ones_col], axis=-1)
            out_ref[0, c - n_cmp, i * PAGE_SIZE:(i + 1) * PAGE_SIZE, :] = x.astype(BF16)

    def pages():
        for i in range(pp):
            for c in range(n_cmp + n_out):
                emit(i, c, chunk(i, c))

    if has_new:
        pl.when(j < n_page_steps)(pages)

        @pl.when(j >= n_page_steps)
        def _():
            new = new_ref[0]
            for c in range(n_cmp + n_out):
                xc = new[:, c * cw:(c + 1) * cw]
                emit(0, c, jnp.concatenate([xc, jnp.zeros((PAGE_SIZE - xc.shape[0], cw), F32)], axis=0))
                for i in range(1, pp):
                    emit(i, c, jnp.zeros((PAGE_SIZE, cw), F32))
    else:
        pages()


def _gather_chunks(pool, layout, page_table, new, *, cw, c_all, n_out, n_cmp=0, cmp_w=None):
    bk, n_pages = page_table.shape
    pp = min(PAGES_PER_STEP, n_pages)
    assert n_pages % pp == 0
    n_page_steps = n_pages // pp
    has_new = new is not None
    n_steps = n_page_steps + (1 if has_new else 0)
    rows = pp * PAGE_SIZE
    lk_pad = n_steps * rows
    page_block = (1,) + pool.shape[1:]
    zeros = (0,) * (len(page_block) - 1)

    def page_map(i):
        def f(b, j, pt):
            return (pt[b, jnp.minimum(j * pp + i, n_pages - 1)],) + zeros
        return f

    in_specs = [pl.BlockSpec(page_block, page_map(i)) for i in range(pp)]
    args = [pool] * pp
    if has_new:
        in_specs.append(pl.BlockSpec((1,) + new.shape[1:], lambda b, j, pt: (b, 0, 0)))
        args.append(new)
    if n_cmp:
        in_specs.append(pl.BlockSpec(cmp_w.shape, lambda b, j, pt: (0, 0, 0)))
        args.append(cmp_w)
    out_specs = [pl.BlockSpec((1, n_out, rows, LANE), lambda b, j, pt: (b, 0, j, 0))]
    out_shape = [jax.ShapeDtypeStruct((bk, n_out, lk_pad, LANE), BF16)]
    if n_cmp:
        nb = rows // NSA_SEL_BLOCK
        out_specs.append(pl.BlockSpec((1, n_cmp, 2, nb, HEAD_DIM), lambda b, j, pt: (b, 0, 0, j, 0)))
        out_shape.append(jax.ShapeDtypeStruct((bk, n_cmp, 2, lk_pad // NSA_SEL_BLOCK, HEAD_DIM), F32))
    kern = functools.partial(_gather_kernel, pp=pp, n_page_steps=n_page_steps, layout=layout, cw=cw,
                             c_all=c_all, n_cmp=n_cmp, n_out=n_out, has_new=has_new)
    outs = pl.pallas_call(
        kern,
        grid_spec=pltpu.PrefetchScalarGridSpec(
            num_scalar_prefetch=1, grid=(bk, n_steps), in_specs=in_specs, out_specs=out_specs),
        out_shape=out_shape,
        compiler_params=_cparams(("parallel", "arbitrary")),
        name="gather_pages",
    )(page_table, *args)
    return outs if n_cmp else outs[0]


def _identity_pages(x):
    b, l, w = x.shape
    n_pages = l // PAGE_SIZE
    pool = x.reshape(b * n_pages, PAGE_SIZE, w)
    table = jnp.arange(b * n_pages, dtype=I32).reshape(b, n_pages)
    return pool, table


def _flash_tile(s_all, bias, v_tile, carry, n_heads, tq, l_in_acc, v_t=False):
    m, l, acc = carry
    s = s_all
    if bias is not None:
        s = jnp.concatenate([s_all[h * tq:(h + 1) * tq] + bias for h in range(n_heads)], axis=0)
    m_new = jnp.maximum(m, jnp.max(s, axis=-1, keepdims=True))
    p = jnp.exp2(s - m_new)
    alpha = jnp.exp2(m - m_new)
    acc = alpha * acc + (_dot_nt if v_t else _dot)(p.astype(BF16), v_tile)
    if not l_in_acc:
        l = alpha * l + jnp.sum(p, axis=-1, keepdims=True)
    return m_new, l, acc


def _flash_init(rows, dv):
    return (jnp.full((rows, 1), NEG, F32), jnp.zeros((rows, 1), F32), jnp.zeros((rows, dv), F32))


def _flash_out(acc):
    return acc[:, :HEAD_DIM] * (1.0 / jnp.maximum(acc[:, HEAD_DIM:HEAD_DIM + 1], 1e-30))


def _stack_heads(q, n, width):
    return jnp.concatenate([q[:, h * width:(h + 1) * width] for h in range(n)], axis=0)


def _pad_lanes(x):
    return jnp.concatenate([x, jnp.zeros((x.shape[0], LANE - x.shape[1]), x.dtype)], axis=-1)


def _nsa_prologue(qg, kw, vw, cmp4, *, q0, tq, nsp, n_top, win_pos0, start):
    hpg = NSA_HEADS // NSA_KV_HEADS
    rows = hpg * tq
    tqp = max(tq, LANE)
    qpos = q0 + lax.broadcasted_iota(I32, (tq, 1), 0)
    qpos_r = jnp.concatenate([qpos] * hpg, axis=0)
    blk = lax.broadcasted_iota(I32, (1, nsp), 1)
    blk_r = lax.broadcasted_iota(I32, (nsp, 1), 0)
    blk_rf = blk_r.astype(F32)
    cur_l = (q0 + lax.broadcasted_iota(I32, (1, tqp), 1)) // NSA_SEL_BLOCK
    vis_e = (blk * NSA_SEL_BLOCK + (NSA_CMP_BLOCK - 1)) <= qpos_r
    vis_o = (blk * NSA_SEL_BLOCK + (NSA_SEL_BLOCK - 1)) <= qpos_r
    kpos_w = win_pos0 + start + lax.broadcasted_iota(I32, (1, kw.shape[0]), 1)
    dlt = qpos - kpos_w
    bias_w = jnp.where(dlt >= 0, jnp.where(dlt < NSA_WINDOW, 0.0, NEG), NEG)

    qs64 = _stack_heads(qg, hpg, HEAD_DIM)
    qs = _pad_lanes(qs64).astype(BF16)
    qs64 = qs64.astype(BF16)

    _, _, acc_w = _flash_tile(_dot_nt(qs, kw), bias_w, vw, _flash_init(rows, LANE), hpg, tq, True)
    o_w = _flash_out(acc_w)

    kce, kco, vce, vco = (x.astype(BF16) for x in cmp4)
    s_e = jnp.where(vis_e, _dot_nt(qs64, kce), NEG)
    s_o = jnp.where(vis_o, _dot_nt(qs64, kco), NEG)
    m = jnp.maximum(jnp.max(s_e, axis=-1, keepdims=True), jnp.max(s_o, axis=-1, keepdims=True))
    p_e = jnp.where(vis_e, jnp.exp2(s_e - m), 0.0)
    p_o = jnp.where(vis_o, jnp.exp2(s_o - m), 0.0)
    den = jnp.sum(p_e, axis=-1, keepdims=True) + jnp.sum(p_o, axis=-1, keepdims=True)
    inv = 1.0 / jnp.maximum(den, 1e-30)
    p_e = p_e * inv
    p_o = p_o * inv
    o_c = _dot(p_e.astype(BF16), vce) + _dot(p_o.astype(BF16), vco)

    pe_h = sum(p_e[h * tq:(h + 1) * tq] for h in range(hpg))
    po_h = sum(p_o[h * tq:(h + 1) * tq] for h in range(hpg))
    imp = pe_h + po_h
    if tqp > tq:
        imp = jnp.concatenate([imp, jnp.zeros((tqp - tq, nsp), F32)], axis=0)
    imp = imp.T
    imp = jnp.where((blk_r == cur_l) | (blk_r == 0), FORCE_SCORE, imp)
    imp = jnp.where(blk_r <= cur_l, imp, NEG)
    sel = jnp.zeros((nsp, tqp), F32)
    for _ in range(n_top):
        mx = jnp.max(imp, axis=0, keepdims=True)
        first = jnp.min(jnp.where(imp == mx, blk_rf, float(nsp)), axis=0, keepdims=True)
        pick = blk_rf == first
        sel = jnp.where(pick & (mx > 0.5 * NEG), 1.0, sel)
        imp = jnp.where(pick, REMOVED, imp)
    return qs, qs64, o_c, o_w, sel.T[:tq].astype(BF16)


def _nsa_combine(gates, branches, tq):
    hpg = NSA_HEADS // NSA_KV_HEADS
    outs = []
    for g, (o_c, o_s, o_w) in enumerate(branches):
        for h in range(hpg):
            r = slice(h * tq, (h + 1) * tq)
            c = g * LANE + 3 * h
            outs.append(gates[:, c:c + 1] * o_c[r] + gates[:, c + 1:c + 2] * o_s[r]
                        + gates[:, c + 2:c + 3] * o_w[r])
    return jnp.concatenate(outs, axis=-1)


def _nsa_kernel(q_ref, g_ref, cmp_ref, ksel_ref, kwin_ref, o_ref, *,
                tq, tk, q_pos0, win_pos0, nsp, n_top):
    G = NSA_KV_HEADS
    hpg = NSA_HEADS // G
    gw = hpg * HEAD_DIM
    qi = pl.program_id(1)
    q0 = q_pos0 + qi * tq
    qpos = q0 + lax.broadcasted_iota(I32, (tq, 1), 0)
    rows = hpg * tq
    q_all = q_ref[0] * (HEAD_DIM ** -0.5 * LOG2E)
    start = pl.multiple_of(jnp.maximum(q0 - NSA_WINDOW - win_pos0, 0), 8)
    span = _win_span(tq)
    pro = [_nsa_prologue(q_all[:, g * gw:(g + 1) * gw],
                         kwin_ref[0, g, pl.ds(start, span), :], kwin_ref[0, G + g, pl.ds(start, span), :],
                         tuple(cmp_ref[0, 2 * g + kv, eo] for kv in range(2) for eo in range(2)),
                         q0=q0, tq=tq, nsp=nsp, n_top=n_top, win_pos0=win_pos0, start=start)
           for g in range(G)]

    blk_col = lax.broadcasted_iota(I32, (nsp, 1), 0)

    def sel_step(j, carry):
        k0 = pl.multiple_of(j * tk, tk)
        kpos = k0 + lax.broadcasted_iota(I32, (1, tk), 1)
        expand = jnp.where(blk_col == kpos // NSA_SEL_BLOCK, 1.0, 0.0).astype(BF16)
        out = []
        for g in range(G):
            picked = _dot(pro[g][4], expand)
            bias = jnp.where(kpos <= qpos, jnp.where(picked > 0.5, 0.0, NEG), NEG)
            s = _dot_nt(pro[g][0], ksel_ref[0, g, pl.ds(k0, tk), :])
            out.append(_flash_tile(s, bias, ksel_ref[0, G + g, pl.ds(k0, tk), :], carry[g], hpg, tq, True))
        return tuple(out)

    n_tiles = (q0 + tq - 1) // tk + 1
    res = lax.fori_loop(0, n_tiles, sel_step, tuple(_flash_init(rows, LANE) for _ in range(G)))
    o_ref[0] = _nsa_combine(g_ref[0], [(pro[g][2], _flash_out(res[g][2]), pro[g][3]) for g in range(G)], tq)


def _nsa_attention(q, gates, cmp, ksel, kwin, *, tq, q_pos0, lk, win_pos0):
    bk, lq, _ = q.shape
    G = NSA_KV_HEADS
    nsp = cmp.shape[3]
    lk_pad = ksel.shape[2]
    lw_pad = kwin.shape[2]
    tk = _kv_tile(tq, lk_pad)
    ns = -(-lk // NSA_SEL_BLOCK)
    assert lw_pad >= _win_span(tq) and q_pos0 + lq <= lk_pad
    kern = functools.partial(_nsa_kernel, tq=tq, tk=tk, q_pos0=q_pos0, win_pos0=win_pos0,
                             nsp=nsp, n_top=min(NSA_TOPN, ns))
    qw = NSA_HEADS * HEAD_DIM
    return pl.pallas_call(
        kern,
        grid=(bk, lq // tq),
        in_specs=[pl.BlockSpec((1, tq, qw), lambda b, i: (b, i, 0)),
                  pl.BlockSpec((1, tq, G * LANE), lambda b, i: (b, i, 0)),
                  pl.BlockSpec((1, 2 * G, 2, nsp, HEAD_DIM), lambda b, i: (b, 0, 0, 0, 0)),
                  pl.BlockSpec((1, 2 * G, lk_pad, LANE), lambda b, i: (b, 0, 0, 0)),
                  pl.BlockSpec((1, 2 * G, lw_pad, LANE), lambda b, i: (b, 0, 0, 0))],
        out_specs=pl.BlockSpec((1, tq, qw), lambda b, i: (b, i, 0)),
        out_shape=jax.ShapeDtypeStruct((bk, lq, qw), F32),
        compiler_params=_cparams(("parallel", "arbitrary")),
        name="nsa_attention",
    )(q, gates, cmp, ksel, kwin)


def _nsa_decode_kernel(pt_ref, q_ref, g_ref, new_ref, w2_ref, kwin_ref, *refs,
                       pp, n_steps, tq, n_new, past_len, win_pos0, nsp, n_top):
    page_refs = refs[:pp]
    o_ref, cmp_ref, oc_ref, ow_ref, sel_ref, m_ref, l_ref, acc_ref = refs[pp:]
    G = NSA_KV_HEADS
    hpg = NSA_HEADS // G
    gw = hpg * HEAD_DIM
    rows = hpg * tq
    j = pl.program_id(1)
    keys = pp * PAGE_SIZE
    nb = keys // NSA_SEL_BLOCK
    q_all = q_ref[0] * (HEAD_DIM ** -0.5 * LOG2E)

    def queries(g):
        qs64 = _stack_heads(q_all[:, g * gw:(g + 1) * gw], hpg, HEAD_DIM)
        return _pad_lanes(qs64).astype(BF16), qs64.astype(BF16)

    @pl.when(j == 0)
    def _():
        cmp_ref[...] = jnp.zeros(cmp_ref.shape, F32)

    @pl.when(j < n_steps)
    def _():
        r0 = pl.multiple_of(j * nb, nb)
        for c in range(2 * G):
            x = jnp.concatenate([r[0, c].T for r in page_refs], axis=0)
            prod = x.reshape(nb, NSA_SEL_BLOCK, HEAD_DIM) * w2_ref[c][None]
            cmp_ref[c, 0, pl.ds(r0, nb), :] = jnp.sum(prod[:, :NSA_CMP_BLOCK], axis=1)
            cmp_ref[c, 1, pl.ds(r0, nb), :] = jnp.sum(prod[:, NSA_CMP_BLOCK:], axis=1)

    @pl.when(j == n_steps - 1)
    def _():
        for g in range(G):
            _, _, o_c, o_w, sel = _nsa_prologue(
                q_all[:, g * gw:(g + 1) * gw], kwin_ref[0, g], kwin_ref[0, G + g],
                tuple(cmp_ref[g + G * kv, eo] for kv in range(2) for eo in range(2)),
                q0=past_len, tq=tq, nsp=nsp, n_top=n_top, win_pos0=win_pos0, start=0)
            oc_ref[g] = o_c
            ow_ref[g] = o_w
            sel_ref[g] = sel
        m_ref[...] = jnp.full(m_ref.shape, NEG, F32)
        l_ref[...] = jnp.zeros(l_ref.shape, F32)
        acc_ref[...] = jnp.zeros(acc_ref.shape, F32)

    def update(g, s, bias, v, v_t):
        carry = (m_ref[g], l_ref[g], acc_ref[g])
        m, l, acc = _flash_tile(s, bias, v, carry, hpg, tq, False, v_t=v_t)
        m_ref[g] = m
        l_ref[g] = l
        acc_ref[g] = acc

    @pl.when(j >= n_steps)
    def _():
        kpos = (j - n_steps) * keys + lax.broadcasted_iota(I32, (1, keys), 1)
        blk_col = lax.broadcasted_iota(I32, (nsp, 1), 0)
        expand = jnp.where(blk_col == kpos // NSA_SEL_BLOCK, 1.0, 0.0).astype(BF16)
        for g in range(G):
            _, qs64 = queries(g)
            k_t = jnp.concatenate([r[0, g] for r in page_refs], axis=-1).astype(BF16)
            v_t = jnp.concatenate([r[0, G + g] for r in page_refs], axis=-1).astype(BF16)
            bias = jnp.where(_dot(sel_ref[g], expand) > 0.5, 0.0, NEG)
            update(g, _dot(qs64, k_t), bias, v_t, True)

    @pl.when(j == 2 * n_steps - 1)
    def _():
        new = new_ref[0]
        pad = jnp.zeros((LANE - tq, HEAD_DIM), F32)
        row = lax.broadcasted_iota(I32, (tq, 1), 0)
        col = lax.broadcasted_iota(I32, (1, LANE), 1)
        own = past_len // NSA_SEL_BLOCK
        branches = []
        for g in range(G):
            _, qs64 = queries(g)
            k = jnp.concatenate([new[:, (2 * G + g) * HEAD_DIM:(2 * G + g + 1) * HEAD_DIM], pad], axis=0)
            v = jnp.concatenate([new[:, (3 * G + g) * HEAD_DIM:(3 * G + g + 1) * HEAD_DIM], pad], axis=0)
            picked = sel_ref[g][:, own:own + 1].astype(F32) > 0.5
            bias = jnp.where((col <= row) & (col < n_new) & picked, 0.0, NEG)
            update(g, _dot_nt(qs64, k.astype(BF16)), bias, v.astype(BF16), False)
            o_s = acc_ref[g] * (1.0 / jnp.maximum(l_ref[g], 1e-30))
            branches.append((oc_ref[g], o_s, ow_ref[g]))
        o_ref[0] = _nsa_combine(g_ref[0], branches, tq)


def _nsa_decode(q, gates, pool, page_table, new, w2, kwin, *, n_new, win_pos0):
    bk, tq, qw = q.shape
    n_pages = page_table.shape[1]
    pp = min(PAGES_PER_STEP, n_pages)
    past_len = n_pages * PAGE_SIZE
    assert n_pages % pp == 0 and past_len % NSA_SEL_BLOCK == 0 and n_new <= NSA_CMP_BLOCK
    assert kwin.shape[2] == _win_span(tq)
    n_steps = n_pages // pp
    G = NSA_KV_HEADS
    rows = (NSA_HEADS // G) * tq
    lk = past_len + n_new
    ns = -(-lk // NSA_SEL_BLOCK)
    nsp = _round_up(ns, LANE)

    def page_map(i):
        return lambda b, j, pt: (pt[b, (j % n_steps) * pp + i], j // n_steps, 0, 0)

    const3 = lambda b, j, pt: (b, 0, 0)
    in_specs = [pl.BlockSpec((1, tq, qw), const3),
                pl.BlockSpec((1, tq, G * LANE), const3),
                pl.BlockSpec((1, tq, new.shape[-1]), const3),
                pl.BlockSpec(w2.shape, lambda b, j, pt: (0, 0, 0)),
                pl.BlockSpec((1,) + kwin.shape[1:], lambda b, j, pt: (b, 0, 0, 0))]
    in_specs += [pl.BlockSpec((1, 2 * G, HEAD_DIM, PAGE_SIZE), page_map(i)) for i in range(pp)]
    kern = functools.partial(_nsa_decode_kernel, pp=pp, n_steps=n_steps, tq=tq, n_new=n_new, past_len=past_len,
                             win_pos0=win_pos0, nsp=nsp, n_top=min(NSA_TOPN, ns))
    return pl.pallas_call(
        kern,
        grid_spec=pltpu.PrefetchScalarGridSpec(
            num_scalar_prefetch=1, grid=(bk, 2 * n_steps), in_specs=in_specs,
            out_specs=pl.BlockSpec((1, tq, qw), const3),
            scratch_shapes=[pltpu.VMEM((2 * G, 2, nsp, HEAD_DIM), F32),
                            pltpu.VMEM((G, rows, HEAD_DIM), F32), pltpu.VMEM((G, rows, HEAD_DIM), F32),
                            pltpu.VMEM((G, tq, nsp), BF16),
                            pltpu.VMEM((G, rows, 1), F32), pltpu.VMEM((G, rows, 1), F32),
                            pltpu.VMEM((G, rows, HEAD_DIM), F32)]),
        out_shape=jax.ShapeDtypeStruct((bk, tq, qw), F32),
        compiler_params=_cparams(("parallel", "arbitrary")),
        name="nsa_decode",
    )(page_table, q, gates, new, w2, kwin, *([pool] * pp))


def _shift_rows(x, d, fill):
    rolled = pltpu.roll(x, d, axis=0)
    row = lax.broadcasted_iota(I32, x.shape, 0)
    return jnp.where(row >= d, rolled, fill)


def _lru_kernel(x_ref, hist_ref, h0_ref, cw_ref, cb_ref, wr_ref, br_ref, wi_ref, bi_ref, lam_ref,
                o_ref, hl_ref, tail_ref, h_ref, *, tl, last_row):
    li = pl.program_id(1)

    @pl.when(li == 0)
    def _():
        tail_ref[...] = jnp.concatenate(
            [jnp.zeros((8 - (CONV_WIDTH - 1), x_ref.shape[-1]), F32), hist_ref[0]], axis=0)
        h_ref[...] = h0_ref[0]

    x = x_ref[0]
    xp = jnp.concatenate([tail_ref[...], x], axis=0)
    cw = cw_ref[...]
    conv = sum(xp[8 - (CONV_WIDTH - 1) + j:8 - (CONV_WIDTH - 1) + j + tl] * cw[j:j + 1]
               for j in range(CONV_WIDTH))
    conv = cb_ref[...] + conv
    tail_ref[...] = x[tl - 8:tl]

    cb16 = conv.astype(BF16)
    r = jax.nn.sigmoid(_dot(cb16, wr_ref[...]) + br_ref[...])
    ig = jax.nn.sigmoid(_dot(cb16, wi_ref[...]) + bi_ref[...])
    log_a = -LRU_C * r * jax.nn.softplus(-lam_ref[...])
    a = jnp.exp(log_a)
    th = jnp.tanh(log_a)
    b = jnp.sqrt(-2.0 * th / (1.0 - th)) * (ig * conv)

    d = 1
    while d < tl:
        a_prev = _shift_rows(a, d, 1.0)
        b_prev = _shift_rows(b, d, 0.0)
        b = a * b_prev + b
        a = a * a_prev
        d *= 2
    h = a * h_ref[...] + b
    o_ref[0] = h
    h_ref[...] = h[tl - 1:tl]

    @pl.when(li == pl.num_programs(1) - 1)
    def _():
        hl_ref[0] = h[last_row:last_row + 1]


def _block_diag(w):
    nb, bw, _ = w.shape
    eye = jnp.eye(nb, dtype=w.dtype)
    return (eye[:, None, :, None] * w[:, :, None, :]).reshape(nb * bw, nb * bw)


def _conv_rglru(x_b, hist, h0, conv_w, conv_b, w_r, b_r, w_i, b_i, lam, *, tl, n_valid):
    b, l, w = x_b.shape
    assert tl >= 8 and l % tl == 0 and n_valid > l - tl
    kern = functools.partial(_lru_kernel, tl=tl, last_row=(n_valid - 1) % tl)
    vec = lambda: pl.BlockSpec((1, w), lambda bi, li: (0, 0))
    h, h_last = pl.pallas_call(
        kern,
        grid=(b, l // tl),
        in_specs=[pl.BlockSpec((1, tl, w), lambda bi, li: (bi, li, 0)),
                  pl.BlockSpec((1, CONV_WIDTH - 1, w), lambda bi, li: (bi, 0, 0)),
                  pl.BlockSpec((1, 1, w), lambda bi, li: (bi, 0, 0)),
                  pl.BlockSpec((CONV_WIDTH, w), lambda bi, li: (0, 0)),
                  vec(),
                  pl.BlockSpec((w, w), lambda bi, li: (0, 0)), vec(),
                  pl.BlockSpec((w, w), lambda bi, li: (0, 0)), vec(), vec()],
        out_specs=[pl.BlockSpec((1, tl, w), lambda bi, li: (bi, li, 0)),
                   pl.BlockSpec((1, 1, w), lambda bi, li: (bi, 0, 0))],
        out_shape=[jax.ShapeDtypeStruct((b, l, w), F32), jax.ShapeDtypeStruct((b, 1, w), F32)],
        scratch_shapes=[pltpu.VMEM((8, w), F32), pltpu.VMEM((1, w), F32)],
        compiler_params=_cparams(("parallel", "arbitrary")),
        name="conv_rglru",
    )(x_b, hist, h0.reshape(b, 1, w), conv_w, conv_b.reshape(1, w),
      _block_diag(w_r).astype(BF16), b_r.reshape(1, w), _block_diag(w_i).astype(BF16), b_i.reshape(1, w),
      lam.reshape(1, w))
    return h, h_last.reshape(b, w)


def _out_kernel(oa_ref, za_ref, ob_ref, zb_ref, x_ref, gate_ref, w_ref, fg_ref, o_ref, *, final_norm):
    half = oa_ref.shape[-1]
    ma = (oa_ref[0] * jax.nn.silu(za_ref[0])).astype(BF16)
    mb = (ob_ref[0] * jax.nn.silu(zb_ref[0])).astype(BF16)
    y = _dot(ma, w_ref[0:half, :]) + _dot(mb, w_ref[half:2 * half, :])
    out = x_ref[0] + gate_ref[0] * y
    if final_norm:
        out = out * lax.rsqrt(jnp.mean(out * out, axis=-1, keepdims=True) + NORM_EPS) * fg_ref[...]
    o_ref[0] = out


def _out_project(o_a, z_a, o_b, z_b, x, gate, w_out, final_g, *, tl, final_norm):
    b, l, d = x.shape
    half = o_a.shape[-1]
    ts = gate.shape[1]
    tm = 1 if ts == 1 else tl
    mod_map = (lambda bi, li: (bi, 0, 0)) if ts == 1 else (lambda bi, li: (bi, li, 0))
    act = lambda: pl.BlockSpec((1, tl, half), lambda bi, li: (bi, li, 0))
    return pl.pallas_call(
        functools.partial(_out_kernel, final_norm=final_norm),
        grid=(b, l // tl),
        in_specs=[act(), act(), act(), act(),
                  pl.BlockSpec((1, tl, d), lambda bi, li: (bi, li, 0)),
                  pl.BlockSpec((1, tm, d), mod_map),
                  pl.BlockSpec((2 * half, d), lambda bi, li: (0, 0)),
                  pl.BlockSpec((1, d), lambda bi, li: (0, 0))],
        out_specs=pl.BlockSpec((1, tl, d), lambda bi, li: (bi, li, 0)),
        out_shape=jax.ShapeDtypeStruct((b, l, d), F32),
        compiler_params=_cparams(("parallel", "arbitrary")),
        name="out_project",
    )(o_a, z_a, o_b, z_b, x, gate, w_out, final_g.reshape(1, d))


def _diff_queries(q, tq):
    hpg = DIFF_HEADS // DIFF_KV_HEADS
    q = q * (DIFF_HALF ** -0.5 * LOG2E)
    zero = jnp.zeros((tq, DIFF_HALF), F32)
    parts = []
    for mp in range(2):
        for h in range(hpg):
            qh = q[:, (2 * h + mp) * DIFF_HALF:(2 * h + mp + 1) * DIFF_HALF]
            parts.append(jnp.concatenate([qh, zero] if mp == 0 else [zero, qh], axis=-1))
    return jnp.concatenate(parts, axis=0).astype(BF16)


def _diff_finish(l, acc, lamv, subg, tq):
    hpg = DIFF_HEADS // DIFF_KV_HEADS
    o = acc * (1.0 / jnp.maximum(l, 1e-30))
    lam = (jnp.exp(jnp.sum(lamv[0:1] * lamv[1:2], axis=-1, keepdims=True))
           - jnp.exp(jnp.sum(lamv[2:3] * lamv[3:4], axis=-1, keepdims=True)) + DIFF_LAMBDA_INIT)
    half = hpg * tq
    od = o[0:half] - lam * o[half:2 * half]
    od = od * lax.rsqrt(jnp.mean(od * od, axis=-1, keepdims=True) + NORM_EPS)
    od = od * subg * (1.0 - DIFF_LAMBDA_INIT)
    return jnp.concatenate([od[h * tq:(h + 1) * tq] for h in range(hpg)], axis=-1)


def _diff_kernel(q_ref, k_ref, v_ref, lamv_ref, subg_ref, o_ref, *, tq, tk, q_pos0):
    hpg = DIFF_HEADS // DIFF_KV_HEADS
    qi = pl.program_id(2)
    q0 = q_pos0 + qi * tq
    n_maps = 2 * hpg
    rows = n_maps * tq
    qpos = q0 + lax.broadcasted_iota(I32, (tq, 1), 0)
    qs = _diff_queries(q_ref[0], tq)

    def step(masked):
        def f(j, carry):
            k0 = pl.multiple_of(j * tk, tk)
            s = _dot_nt(qs, k_ref[0, 0, pl.ds(k0, tk), :])
            bias = None
            if masked:
                kpos = k0 + lax.broadcasted_iota(I32, (1, tk), 1)
                bias = jnp.where(kpos <= qpos, 0.0, NEG)
            return _flash_tile(s, bias, v_ref[0, 0, pl.ds(k0, tk), :], carry, n_maps, tq, False)
        return f

    n_full = (q0 + 1) // tk
    n_tiles = (q0 + tq - 1) // tk + 1
    carry = lax.fori_loop(0, n_full, step(False), _flash_init(rows, 2 * DIFF_HALF))
    m, l, acc = lax.fori_loop(n_full, n_tiles, step(True), carry)
    o_ref[0] = _diff_finish(l, acc, lamv_ref[...], subg_ref[...], tq)


def _diff_decode_kernel(pt_ref, q_ref, new_ref, lamv_ref, subg_ref, *refs, pp, tq, n_new):
    page_refs = refs[:pp]
    o_ref, m_ref, l_ref, acc_ref = refs[pp:]
    G = DIFF_KV_HEADS
    n_maps = 2 * (DIFF_HEADS // G)
    gw = n_maps * DIFF_HALF
    j = pl.program_id(1)

    @pl.when(j == 0)
    def _():
        m_ref[...] = jnp.full(m_ref.shape, NEG, F32)
        l_ref[...] = jnp.zeros(l_ref.shape, F32)
        acc_ref[...] = jnp.zeros(acc_ref.shape, F32)

    def update(g, qs, k, v, bias):
        carry = (m_ref[g], l_ref[g], acc_ref[g])
        m, l, acc = _flash_tile(_dot_nt(qs, k), bias, v, carry, n_maps, tq, False)
        m_ref[g] = m
        l_ref[g] = l
        acc_ref[g] = acc

    qs = [_diff_queries(q_ref[0][:, g * gw:(g + 1) * gw], tq) for g in range(G)]
    for g in range(G):
        k = jnp.concatenate([r[0, pl.ds(g, PAGE_SIZE, stride=2 * G), :] for r in page_refs], axis=0)
        v = jnp.concatenate([r[0, pl.ds(G + g, PAGE_SIZE, stride=2 * G), :] for r in page_refs], axis=0)
        update(g, qs[g], k.astype(BF16), v.astype(BF16), None)

    @pl.when(j == pl.num_programs(1) - 1)
    def _():
        new = new_ref[0]
        pad = jnp.zeros((LANE - tq, 2 * DIFF_HALF), F32)
        row = lax.broadcasted_iota(I32, (tq, 1), 0)
        col = lax.broadcasted_iota(I32, (1, LANE), 1)
        bias = jnp.where((col <= row) & (col < n_new), 0.0, NEG)
        outs = []
        for g in range(G):
            k = jnp.concatenate([new[:, g * 2 * DIFF_HALF:(g + 1) * 2 * DIFF_HALF], pad], axis=0)
            v = jnp.concatenate([new[:, (G + g) * 2 * DIFF_HALF:(G + g + 1) * 2 * DIFF_HALF], pad], axis=0)
            update(g, qs[g], k.astype(BF16), v.astype(BF16), bias)
            outs.append(_diff_finish(l_ref[g], acc_ref[g], lamv_ref[...], subg_ref[...], tq))
        o_ref[0] = jnp.concatenate(outs, axis=-1)


def _diff_decode(q, pool, page_table, new, lamv, subln_g, *, n_new):
    bk, tq, qw = q.shape
    n_pages = page_table.shape[1]
    pp = min(PAGES_PER_STEP, n_pages)
    assert n_pages % pp == 0
    G = DIFF_KV_HEADS
    rows = 2 * (DIFF_HEADS // G) * tq

    def page_map(i):
        return lambda b, j, pt: (pt[b, j * pp + i], 0, 0)

    in_specs = [pl.BlockSpec((1, tq, qw), lambda b, j, pt: (b, 0, 0)),
                pl.BlockSpec((1, tq, new.shape[-1]), lambda b, j, pt: (b, 0, 0)),
                pl.BlockSpec((4, DIFF_HALF), lambda b, j, pt: (0, 0)),
                pl.BlockSpec((1, 2 * DIFF_HALF), lambda b, j, pt: (0, 0))]
    in_specs += [pl.BlockSpec((1,) + pool.shape[1:], page_map(i)) for i in range(pp)]
    return pl.pallas_call(
        functools.partial(_diff_decode_kernel, pp=pp, tq=tq, n_new=n_new),
        grid_spec=pltpu.PrefetchScalarGridSpec(
            num_scalar_prefetch=1, grid=(bk, n_pages // pp), in_specs=in_specs,
            out_specs=pl.BlockSpec((1, tq, qw), lambda b, j, pt: (b, 0, 0)),
            scratch_shapes=[pltpu.VMEM((G, rows, 1), F32), pltpu.VMEM((G, rows, 1), F32),
                            pltpu.VMEM((G, rows, 2 * DIFF_HALF), F32)]),
        out_shape=jax.ShapeDtypeStruct((bk, tq, qw), F32),
        compiler_params=_cparams(("parallel", "arbitrary")),
        name="diff_decode",
    )(page_table, q, new, lamv, subln_g.reshape(1, 2 * DIFF_HALF), *([pool] * pp))


def _diff_attention(q, kv, lamv, subln_g, *, tq, q_pos0):
    bk, lq, _ = q.shape
    G = DIFF_KV_HEADS
    lk_pad = kv.shape[2]
    tk = _kv_tile(tq, lk_pad)
    gw = (DIFF_HEADS // G) * 2 * DIFF_HALF
    assert q_pos0 + lq <= lk_pad
    return pl.pallas_call(
        functools.partial(_diff_kernel, tq=tq, tk=tk, q_pos0=q_pos0),
        grid=(bk, G, lq // tq),
        in_specs=[pl.BlockSpec((1, tq, gw), lambda b, g, i: (b, i, g)),
                  pl.BlockSpec((1, 1, lk_pad, 2 * DIFF_HALF), lambda b, g, i: (b, g, 0, 0)),
                  pl.BlockSpec((1, 1, lk_pad, 2 * DIFF_HALF), lambda b, g, i: (b, G + g, 0, 0)),
                  pl.BlockSpec((4, DIFF_HALF), lambda b, g, i: (0, 0)),
                  pl.BlockSpec((1, 2 * DIFF_HALF), lambda b, g, i: (0, 0))],
        out_specs=pl.BlockSpec((1, tq, gw), lambda b, g, i: (b, i, g)),
        out_shape=jax.ShapeDtypeStruct((bk, lq, DIFF_HEADS * 2 * DIFF_HALF), F32),
        compiler_params=_cparams(("parallel", "parallel", "arbitrary")),
        name="diff_attention",
    )(q, kv, kv, lamv, subln_g.reshape(1, 2 * DIFF_HALF))


def _dsa_kernel(q_ref, qi_ref, kw_ref, kidx_ref, kv_ref, tri_ref, o_ref, key_ref, keyt_ref, *,
                tq, tk, q_pos0, n_sel, one_block):
    G = DSA_KV_HEADS
    hpg = DSA_HEADS // G
    transposed = tq % LANE == 0
    q0 = q_pos0 + (0 if one_block else pl.program_id(1) * tq)
    qpos = q0 + lax.broadcasted_iota(I32, (tq, 1), 0)
    n_tiles = (q0 + tq - 1) // tk + 1
    lanes = tk // LANE

    qidx = _pad_lanes(_stack_heads(qi_ref[0] * (IDX_DIM ** -0.5), IDX_HEADS, IDX_DIM)).astype(BF16)
    wi = kw_ref[0][:, IDX_DIM:IDX_DIM + IDX_HEADS] * (IDX_HEADS ** -0.5)

    def score_step(j, kmax):
        k0 = pl.multiple_of(j * tk, tk)
        s_all = _dot_nt(qidx, kidx_ref[0, 0, pl.ds(k0, tk), :])
        score = jnp.zeros((tq, tk), F32)
        for h in range(IDX_HEADS):
            score = score + wi[:, h:h + 1] * jnp.maximum(s_all[h * tq:(h + 1) * tq], 0.0)
        kpos = k0 + lax.broadcasted_iota(I32, (1, tk), 1)
        bits = pltpu.bitcast(score, I32)
        key = jnp.where(bits < 0, bits ^ 0x7FFFFFFF, bits)
        key = jnp.where(score == 0.0, 0, key)
        key = jnp.where(kpos <= qpos, jnp.where(score > 0.5 * NEG, key, INT_MIN), INT_MIN)
        key_ref[j] = key
        if not transposed:
            return jnp.maximum(kmax, jnp.max(key, axis=-1, keepdims=True))
        key_t = key.T
        keyt_ref[j] = key_t
        return jnp.maximum(kmax, jnp.max(key_t, axis=0, keepdims=True))

    kmax = lax.fori_loop(0, n_tiles, score_step, jnp.full((1, tq) if transposed else (tq, 1), INT_MIN, I32))

    def count(*bounds):
        def f(j, accs):
            keys = keyt_ref[j] if transposed else key_ref[j]
            out = []
            for bound, acc in zip(bounds, accs):
                hit = jnp.where(keys >= bound, 1.0, 0.0)
                if transposed:
                    acc = acc + jnp.sum(hit.reshape(tk // COUNT_ROWS, COUNT_ROWS, tq), axis=0)
                else:
                    for c in range(lanes):
                        acc = acc + hit[:, c * LANE:(c + 1) * LANE]
                out.append(acc)
            return tuple(out)
        acc0 = jnp.zeros((COUNT_ROWS, tq) if transposed else (tq, LANE), F32)
        accs = lax.fori_loop(0, n_tiles, f, tuple(acc0 for _ in bounds),
                             unroll=one_block)
        return [jnp.sum(acc, axis=0 if transposed else -1, keepdims=True) for acc in accs]

    k_f = float(n_sel)
    probe = jnp.maximum(kmax - (PROBE_BINADES << 23), 1)
    c_adm, c_nn, c_pos, c_probe = count(INT_MIN + 1, 0, 1, probe)
    few, pos, zero, high = c_adm < k_f, c_pos >= k_f, c_nn >= k_f, c_probe >= k_f
    lo0 = jnp.where(pos, jnp.where(high, probe, 1), jnp.where(zero, 0, INT_MIN))
    hi0 = jnp.where(pos, jnp.where(high, kmax, probe - 1), jnp.where(zero, 0, jnp.where(few, INT_MIN, -1)))

    def unfinished(lo_hi):
        lo, hi = lo_hi
        return jnp.max(jnp.where(lo < hi, 1.0, 0.0)) > 0.0

    def bisect(lo_hi):
        lo, hi = lo_hi
        mid = (lo >> 1) + (hi >> 1) + ((lo | hi) & 1)
        cnt, = count(mid)
        lo = jnp.where(cnt >= k_f, mid, lo)
        hi = jnp.where(cnt > k_f, hi, jnp.where(cnt == k_f, mid, mid - 1))
        return lo, hi

    thr, _ = lax.while_loop(unfinished, lambda s: bisect(bisect(s)), (lo0, hi0))
    thr = jnp.maximum(thr, INT_MIN + 1)
    n_gt, = count(thr + 1)
    need = k_f - n_gt

    def along_rows(v):
        rep = jnp.broadcast_to(v, (LANE, tq)).T
        return jnp.concatenate([rep] * lanes, axis=-1)

    if transposed:
        thr = along_rows(thr)
        need = along_rows(need)

    q = q_ref[0] * (HEAD_DIM ** -0.5 * LOG2E)
    qs = [_pad_lanes(_stack_heads(q[:, g * hpg * HEAD_DIM:(g + 1) * hpg * HEAD_DIM], hpg, HEAD_DIM)).astype(BF16)
          for g in range(G)]
    rows = hpg * tq

    def att_step(j, carry):
        seen, flash = carry
        k0 = pl.multiple_of(j * tk, tk)
        key = key_ref[j]
        tied = key == thr
        tied_b = jnp.where(tied, 1.0, 0.0).astype(BF16)
        ranks = []
        for c in range(tk // KV_TILE):
            r = _dot(tied_b[:, c * KV_TILE:(c + 1) * KV_TILE], tri_ref[...]) + seen
            ranks.append(r)
            seen = r[:, KV_TILE - 1:KV_TILE]
        rank = jnp.concatenate(ranks, axis=-1)
        bias = jnp.where(key > thr, 0.0, jnp.where(tied, jnp.where(rank <= need, 0.0, NEG), NEG))
        out = []
        for g in range(G):
            s = _dot_nt(qs[g], kv_ref[0, g, pl.ds(k0, tk), :])
            out.append(_flash_tile(s, bias, kv_ref[0, G + g, pl.ds(k0, tk), :], flash[g], hpg, tq, True))
        return seen, tuple(out)

    _, res = lax.fori_loop(0, n_tiles, att_step,
                           (jnp.zeros((tq, 1), F32), tuple(_flash_init(rows, LANE) for _ in range(G))))
    outs = []
    for g in range(G):
        o = _flash_out(res[g][2])
        outs.extend(o[h * tq:(h + 1) * tq] for h in range(hpg))
    o_ref[0] = jnp.concatenate(outs, axis=-1)


def _dsa_attention(q, qi, kw, kidx, kv, *, tq, q_pos0, lk):
    bk, lq, _ = q.shape
    lk_pad = kv.shape[2]
    tk = _kv_tile(tq, lk_pad)
    n_sel = min(DSA_TOPK_MAX, lk // 4)
    assert q_pos0 + lq <= lk_pad and tk >= n_sel
    kern = functools.partial(_dsa_kernel, tq=tq, tk=tk, q_pos0=q_pos0, n_sel=n_sel, one_block=lq == tq)
    assert tk % KV_TILE == 0
    tri = jnp.triu(jnp.ones((KV_TILE, KV_TILE), BF16))
    return pl.pallas_call(
        kern,
        grid=(bk, lq // tq),
        in_specs=[pl.BlockSpec((1, tq, DSA_HEADS * HEAD_DIM), lambda b, i: (b, i, 0)),
                  pl.BlockSpec((1, tq, IDX_HEADS * IDX_DIM), lambda b, i: (b, i, 0)),
                  pl.BlockSpec((1, tq, LANE), lambda b, i: (b, i, 0)),
                  pl.BlockSpec((1, 1, lk_pad, LANE), lambda b, i: (b, 0, 0, 0)),
                  pl.BlockSpec((1, 4, lk_pad, LANE), lambda b, i: (b, 0, 0, 0)),
                  pl.BlockSpec((KV_TILE, KV_TILE), lambda b, i: (0, 0))],
        out_specs=pl.BlockSpec((1, tq, DSA_HEADS * HEAD_DIM), lambda b, i: (b, i, 0)),
        out_shape=jax.ShapeDtypeStruct((bk, lq, DSA_HEADS * HEAD_DIM), F32),
        scratch_shapes=[pltpu.VMEM((lk_pad // tk, tq, tk), I32),
                        pltpu.VMEM((lk_pad // tk, tk, tq) if tq % LANE == 0 else (8, LANE), I32)],
        compiler_params=_cparams(("parallel", "arbitrary")),
        name="dsa_attention",
    )(q, qi, kw, kidx, kv, tri)


L0_SIZES = (512, 768, 24, 512, 512, 512)
L1_SIZES = (512, 256, 256, 512, 512, 128, 128, 256, 64, 4, 512)


def _l0_weight(w_in):
    d = w_in.shape[0]
    q, kv6, gl, z_a, x_b, z_b = jnp.split(w_in, np.cumsum(L0_SIZES)[:-1].tolist(), axis=1)
    pad = jnp.zeros((d, LANE - 12), w_in.dtype)
    w = jnp.concatenate([q, kv6, z_a, x_b, z_b, gl[:, :12], pad, gl[:, 12:], pad], axis=1)
    segs = [(0, 512), (512, 1024), (1024, 1280), (1280, 1792), (1792, 2304), (2304, 2816), (2816, 3072)]
    return w.astype(BF16), segs


def _l1_weight(w_in):
    d = w_in.shape[0]
    qc, kc, vc, z_c, qd, kd, vd, qi, ki, wi, z_d = jnp.split(w_in, np.cumsum(L1_SIZES)[:-1].tolist(), axis=1)
    pad = jnp.zeros((d, LANE - IDX_DIM - IDX_HEADS), w_in.dtype)
    w = jnp.concatenate([qc, kc, vc, z_c, qd, kd, vd, qi, z_d, ki, wi, pad], axis=1)
    segs = [(0, 512), (512, 1024), (1024, 1536), (1536, 2048), (2048, 2304), (2304, 2560), (2560, 3072),
            (3072, 3200)]
    return w.astype(BF16), segs


def _pad_rows(x, n):
    return jnp.pad(x, ((0, 0), (0, n - x.shape[1]), (0, 0)))


def _cols_pool(pool):
    npool, ps = pool.shape[:2]
    cw = pool.shape[-1]
    perm = (0,) + tuple(range(2, pool.ndim)) + (1,)
    return jnp.transpose(pool, perm).reshape(npool, -1, cw, ps)


def _layer0(x, mod, past, w, *, tl, tq):
    (norm_g, w_in, cmp_wk, cmp_wv, conv_w, conv_b, lru_wr, lru_br, lru_wi, lru_bi, lru_lambda, w_out) = w
    shift, scale, gate = mod
    b, l, d = x.shape
    w_p, segs = _l0_weight(w_in)
    flat = shift.shape[1] != 1
    xin = x.reshape(1, b * l, d) if flat else x
    q, kvp, kvw, z_a, x_b, z_b, gates = _project(xin, norm_g, shift, scale, w_p, segs, 6, tl)
    if flat:
        q, kvp, kvw, z_a, x_b, z_b, gates = (t.reshape(b, l, -1) for t in (q, kvp, kvw, z_a, x_b, z_b, gates))
    wk2 = jnp.concatenate([cmp_wk, cmp_wk], axis=0)
    wv2 = jnp.concatenate([cmp_wv, cmp_wv], axis=0)
    w2 = jnp.stack([wk2, wk2, wv2, wv2], axis=0)
    nsa = dict(cw=HEAD_DIM, c_all=8, n_out=4, n_cmp=4, cmp_w=w2)
    lq = _round_up(l, 8)
    if past is None:
        ppool, ptable = _identity_pages(kvp)
        ksel, cmp = _gather_chunks(ppool, "rows", ptable, None, **nsa)
        wpool, wtable = _identity_pages(kvw)
        kwin = _gather_chunks(wpool, "rows", wtable, None, cw=HEAD_DIM, c_all=4, n_out=4)
        kv_win = kvw
        hist = jnp.zeros((b, CONV_WIDTH - 1, x_b.shape[-1]), F32)
        h0 = jnp.zeros((b, x_b.shape[-1]), F32)
        nsp = _round_up(cmp.shape[3], LANE)
        cmp = jnp.pad(cmp, ((0, 0), (0, 0), (0, 0), (0, nsp - cmp.shape[3]), (0, 0)))
        cmp = cmp.reshape(b, 2, 2, 2, nsp, HEAD_DIM).transpose(0, 2, 1, 3, 4, 5).reshape(b, 4, 2, nsp, HEAD_DIM)
        o_a = _nsa_attention(_pad_rows(q, lq), _pad_rows(gates, lq), cmp, ksel, kwin,
                             tq=min(tq, lq), q_pos0=0, lk=l, win_pos0=0)[:, :l]
    else:
        pool, table, win_buf, hist, h0 = past
        assert lq <= tq
        kv_win = jnp.concatenate([win_buf.reshape(b, win_buf.shape[1], -1), kvw], axis=1)
        lw_pad = _win_span(lq)
        assert kv_win.shape[1] <= lw_pad and lw_pad % PAGE_SIZE == 0
        wpool, wtable = _identity_pages(_pad_rows(kv_win, lw_pad))
        kwin = _gather_chunks(wpool, "rows", wtable, None, cw=HEAD_DIM, c_all=4, n_out=4)
        o_a = _nsa_decode(_pad_rows(q, lq), _pad_rows(gates, lq), _cols_pool(pool), table, _pad_rows(kvp, lq),
                          w2, kwin, n_new=l, win_pos0=table.shape[1] * PAGE_SIZE - win_buf.shape[1])[:, :l]
    o_b, h_last = _conv_rglru(_pad_rows(x_b, lq), hist, h0, conv_w, conv_b, lru_wr, lru_br, lru_wi, lru_bi,
                              lru_lambda, tl=min(256, lq), n_valid=l)
    o_b = o_b[:, :l]
    fl = (lambda t: t.reshape(1, b * l, -1)) if flat else (lambda t: t)
    x_new = _out_project(fl(o_a), fl(z_a), fl(o_b), fl(z_b), xin, gate, w_out.astype(BF16),
                         jnp.ones((d,), F32), tl=tl, final_norm=False).reshape(b, l, d)
    win_keep = min(NSA_WINDOW, kv_win.shape[1])
    conv_src = jnp.concatenate([hist, x_b], axis=1) if l < CONV_WIDTH - 1 else x_b
    states = (kvp.reshape(b, l, 4, NSA_KV_HEADS, HEAD_DIM),
              kv_win[:, -win_keep:].reshape(b, win_keep, 2, NSA_KV_HEADS, HEAD_DIM),
              conv_src[:, -(CONV_WIDTH - 1):], h_last)
    return x_new, states


def _layer1(x, mod, past, w, final_g, *, tl, tq):
    (norm_g, w_in, lam_q1, lam_k1, lam_q2, lam_k2, subln_g, w_out) = w
    shift, scale, gate = mod
    b, l, d = x.shape
    w_p, segs = _l1_weight(w_in)
    flat = shift.shape[1] != 1
    xin = x.reshape(1, b * l, d) if flat else x
    qc, kvc, z_c, qd, kvd, qi, z_d, kiw = _project(xin, norm_g, shift, scale, w_p, segs, -1, tl)
    if flat:
        qc, kvc, z_c, qd, kvd, qi, z_d, kiw = (t.reshape(b, l, -1) for t in (qc, kvc, z_c, qd, kvd, qi, z_d, kiw))
    lq = _round_up(l, 8)
    diff_a = dict(cw=2 * DIFF_HALF, c_all=4, n_out=4)
    dsa_a = dict(cw=HEAD_DIM, c_all=4, n_out=4)
    kidx_a = dict(cw=IDX_DIM, c_all=1, n_out=1)
    if past is None:
        dpool, dtable = _identity_pages(kvc)
        diff_kv = _gather_chunks(dpool, "rows", dtable, None, **diff_a)
        spool, stable = _identity_pages(kvd)
        dsa_kv = _gather_chunks(spool, "rows", stable, None, **dsa_a)
        ipool, itable = _identity_pages(kiw)
        kidx = _gather_chunks(ipool, "rows", itable, None, **kidx_a)
        q_pos0, lk = 0, l
    else:
        diff_pool, dsa_pool, kidx_pool, table = past
        past_len = table.shape[1] * PAGE_SIZE
        diff_rows = diff_pool.reshape(diff_pool.shape[0], PAGE_SIZE * 4, 2 * DIFF_HALF)
        diff_kv = None
        dsa_kv = _gather_chunks(_cols_pool(dsa_pool), "cols", table, _pad_rows(kvd, 8), **dsa_a)
        kidx = _gather_chunks(_cols_pool(kidx_pool), "cols", table, _pad_rows(kiw[:, :, :IDX_DIM], 8), **kidx_a)
        q_pos0, lk = past_len, past_len + l
    lamv = jnp.stack([lam_q1, lam_k1, lam_q2, lam_k2], axis=0)
    if past is None:
        o_c = _diff_attention(_pad_rows(qc, lq), diff_kv, lamv, subln_g, tq=min(tq, lq), q_pos0=q_pos0)[:, :l]
    else:
        assert lq <= tq
        o_c = _diff_decode(_pad_rows(qc, lq), diff_rows, table, _pad_rows(kvc, lq), lamv, subln_g, n_new=l)[:, :l]
    o_d = _dsa_attention(_pad_rows(qd, lq), _pad_rows(qi, lq), _pad_rows(kiw, lq), kidx, dsa_kv,
                         tq=min(tq, lq), q_pos0=q_pos0, lk=lk)[:, :l]
    fl = (lambda t: t.reshape(1, b * l, -1)) if flat else (lambda t: t)
    y = _out_project(fl(o_c), fl(z_c), fl(o_d), fl(z_d), xin, gate, w_out.astype(BF16), final_g,
                     tl=tl, final_norm=True).reshape(b, l, d)
    states = (kvc.reshape(b, l, 2, DIFF_KV_HEADS, 2 * DIFF_HALF),
              kvd.reshape(b, l, 2, DSA_KV_HEADS, HEAD_DIM), kiw[:, :, :IDX_DIM])
    return y, states


def kernel(x_prompt, x_sample, cache_l0_nsa_kv, state_l0_win_kv, state_l0_conv, state_l0_lru_h,
           cache_l1_diff_kv, cache_l1_dsa_kv, cache_l1_dsa_kidx, page_table, c_prompt, c_sample,
           l0_norm_g, l0_ada_w, l0_ada_b, l0_w_in, l0_cmp_wk, l0_cmp_wv, l0_conv_w, l0_conv_b,
           l0_lru_wr, l0_lru_br, l0_lru_wi, l0_lru_bi, l0_lru_lambda, l0_w_out,
           l1_norm_g, l1_ada_w, l1_ada_b, l1_w_in, l1_lam_q1, l1_lam_k1, l1_lam_q2, l1_lam_k2,
           l1_subln_g, l1_w_out, final_norm_g):
    bp, lp, d = x_prompt.shape
    bs, ls, _ = x_sample.shape
    c_all = jnp.concatenate([c_prompt, c_sample], axis=0)

    def mods(ada_w, ada_b):
        m = _modulation(c_all, ada_w, ada_b)
        mp = tuple(t[:, None] for t in jnp.split(m[:bp], 3, axis=-1))
        ms = tuple(jnp.repeat(t, ls, axis=0)[None] for t in jnp.split(m[bp:], 3, axis=-1))
        return mp, ms

    tl_p = min(512, lp)
    tl_s = bs * ls
    tq = min(Q_TILE, lp)
    w0 = (l0_norm_g, l0_w_in, l0_cmp_wk, l0_cmp_wv, l0_conv_w, l0_conv_b, l0_lru_wr, l0_lru_br,
          l0_lru_wi, l0_lru_bi, l0_lru_lambda, l0_w_out)
    mp0, ms0 = mods(l0_ada_w, l0_ada_b)
    xp, (nsa_kv_p, win_p, conv_p, h_p) = _layer0(x_prompt, mp0, None, w0, tl=tl_p, tq=tq)
    xs, (nsa_kv_s, win_s, conv_s, h_s) = _layer0(
        x_sample, ms0, (cache_l0_nsa_kv, page_table, state_l0_win_kv, state_l0_conv, state_l0_lru_h), w0,
        tl=tl_s, tq=tq)
    w1 = (l1_norm_g, l1_w_in, l1_lam_q1, l1_lam_k1, l1_lam_q2, l1_lam_k2, l1_subln_g, l1_w_out)
    mp1, ms1 = mods(l1_ada_w, l1_ada_b)
    y_p, (diff_kv_p, dsa_kv_p, kidx_p) = _layer1(xp, mp1, None, w1, final_norm_g, tl=tl_p, tq=tq)
    y_s, (diff_kv_s, dsa_kv_s, kidx_s) = _layer1(
        xs, ms1, (cache_l1_diff_kv, cache_l1_dsa_kv, cache_l1_dsa_kidx, page_table), w1, final_norm_g,
        tl=tl_s, tq=tq)
    return (y_p, y_s, nsa_kv_p, nsa_kv_s, win_p, win_s, conv_p, conv_s, h_p, h_s,
            diff_kv_p, diff_kv_s, dsa_kv_p, dsa_kv_s, kidx_p, kidx_s)
```

```python
import functools
import math

import jax
import jax.numpy as jnp
import numpy as np
from jax import lax
from jax.experimental import pallas as pl
from jax.experimental.pallas import tpu as pltpu

F32 = jnp.float32
BF16 = jnp.bfloat16
I32 = jnp.int32

PAGE_SIZE = 128
HEAD_DIM = 64
NSA_HEADS = 8
NSA_KV_HEADS = 2
NSA_CMP_BLOCK = 32
NSA_SEL_BLOCK = 64
NSA_TOPN = 16
NSA_WINDOW = 512
FORCE_SCORE = 1e4
LRU_BLOCKS = 8
LRU_C = 8.0
CONV_WIDTH = 4
DIFF_HALF = 64
DIFF_HEADS = 4
DIFF_KV_HEADS = 2
DIFF_LAMBDA_INIT = 0.8 - 0.6 * math.exp(-0.3 * 1)
DSA_HEADS = 8
DSA_KV_HEADS = 2
IDX_HEADS = 4
IDX_DIM = 64
DSA_TOPK_MAX = 256
NORM_EPS = 1e-6
NEG = -1e30
REMOVED = -3e38
INT_MIN = -2 ** 31
LOG2E = math.log2(math.e)

LANE = 128
VMEM_LIMIT = 56 * 1024 * 1024
KV_TILE = 1024
TRI_TILE = 512
Q_TILE = 256
MAX_KV_TILE = 2048
PROBE_BINADES = 3
COUNT_ROWS = 64


def _kv_tile(tq, lk_pad):
    t = min(KV_TILE * max(1, Q_TILE // tq), MAX_KV_TILE, lk_pad)
    while lk_pad % t:
        t -= TRI_TILE
    return t


def _win_span(tq):
    return _round_up(NSA_WINDOW + tq, LANE)
PAGES_PER_STEP = 16


def _cparams(sem):
    return pltpu.CompilerParams(dimension_semantics=sem, vmem_limit_bytes=VMEM_LIMIT)


def _dot(a, b):
    return jnp.dot(a, b, preferred_element_type=F32)


def _dot_nt(a, b):
    return lax.dot_general(a, b, (((1,), (1,)), ((), ())), preferred_element_type=F32)


def _round_up(x, m):
    return (x + m - 1) // m * m


def _mod_kernel(c_ref, w_ref, b_ref, o_ref):
    o_ref[...] = jnp.dot(c_ref[...], w_ref[...], preferred_element_type=F32,
                         precision=lax.Precision.HIGHEST) + b_ref[...]


def _modulation(c, w, b):
    bc, d = c.shape
    n = w.shape[1]
    tn = 512
    return pl.pallas_call(
        _mod_kernel,
        grid=(n // tn,),
        in_specs=[pl.BlockSpec((bc, d), lambda j: (0, 0)),
                  pl.BlockSpec((d, tn), lambda j: (0, j)),
                  pl.BlockSpec((1, tn), lambda j: (0, j))],
        out_specs=pl.BlockSpec((bc, tn), lambda j: (0, j)),
        out_shape=jax.ShapeDtypeStruct((bc, n), F32),
        compiler_params=_cparams(("arbitrary",)),
        name="modulation",
    )(c, w, b.reshape(1, n))


def _proj_kernel(x_ref, g_ref, sh_ref, sc_ref, w_ref, *o_refs, segs, sigmoid_seg):
    x = x_ref[0]
    y = x * lax.rsqrt(jnp.mean(x * x, axis=-1, keepdims=True) + NORM_EPS)
    h = (y * g_ref[...]) * (1.0 + sc_ref[0]) + sh_ref[0]
    hb = h.astype(BF16)
    for i, ((a, b), o_ref) in enumerate(zip(segs, o_refs)):
        r = _dot(hb, w_ref[:, a:b])
        if i == sigmoid_seg:
            r = jax.nn.sigmoid(r)
        o_ref[0] = r


def _project(x, g, shift, scale, w, segs, sigmoid_seg, tl):
    b, l, d = x.shape
    ts = shift.shape[1]
    tm = 1 if ts == 1 else tl
    mod_map = (lambda bi, li: (bi, 0, 0)) if ts == 1 else (lambda bi, li: (bi, li, 0))
    p = w.shape[1]
    kern = functools.partial(_proj_kernel, segs=tuple(segs), sigmoid_seg=sigmoid_seg)
    return pl.pallas_call(
        kern,
        grid=(b, l // tl),
        in_specs=[pl.BlockSpec((1, tl, d), lambda bi, li: (bi, li, 0)),
                  pl.BlockSpec((1, d), lambda bi, li: (0, 0)),
                  pl.BlockSpec((1, tm, d), mod_map),
                  pl.BlockSpec((1, tm, d), mod_map),
                  pl.BlockSpec((d, p), lambda bi, li: (0, 0))],
        out_specs=[pl.BlockSpec((1, tl, e - a), lambda bi, li: (bi, li, 0)) for a, e in segs],
        out_shape=[jax.ShapeDtypeStruct((b, l, e - a), F32) for a, e in segs],
        compiler_params=_cparams(("parallel", "arbitrary")),
        name="norm_mod_project",
    )(x, g.reshape(1, d), shift, scale, w)


def _gather_kernel(pt_ref, *refs, pp, n_page_steps, layout, cw, c_all, n_cmp, n_out, has_new):
    page_refs = refs[:pp]
    pos = pp
    new_ref = None
    if has_new:
        new_ref = refs[pos]
        pos += 1
    w2_ref = None
    if n_cmp:
        w2_ref = refs[pos]
        pos += 1
    out_ref = refs[pos]
    cmp_ref = refs[pos + 1] if n_cmp else None
    j = pl.program_id(1)
    ones_col = jnp.where(lax.broadcasted_iota(I32, (PAGE_SIZE, LANE - cw), 1) == 0, 1.0, 0.0) if cw < LANE else None

    def chunk(i, c):
        if layout == "rows":
            return page_refs[i][0, :, c * cw:(c + 1) * cw]
        if layout == "cols":
            return page_refs[i][0, c].T
        return page_refs[i][0, pl.ds(c, PAGE_SIZE, stride=c_all), :]

    def emit(i, c, x):
        if c < n_cmp:
            nb = PAGE_SIZE // NSA_SEL_BLOCK
            prod = x.reshape(nb, NSA_SEL_BLOCK, cw) * w2_ref[c][None]
            cmp_ref[0, c, 0, i * nb:(i + 1) * nb, :] = jnp.sum(prod[:, :NSA_CMP_BLOCK], axis=1)
            cmp_ref[0, c, 1, i * nb:(i + 1) * nb, :] = jnp.sum(prod[:, NSA_CMP_BLOCK:], axis=1)
        else:
            if ones_col is not None:
                x = jnp.concatenate([x, ones_col], axis=-1)
            out_ref[0, c - n_cmp, i * PAGE_SIZE:(i + 1) * PAGE_SIZE, :] = x.astype(BF16)

    def pages():
        for i in range(pp):
            for c in range(n_cmp + n_out):
                emit(i, c, chunk(i, c))

    if has_new:
        pl.when(j < n_page_steps)(pages)

        @pl.when(j >= n_page_steps)
        def _():
            new = new_ref[0]
            for c in range(n_cmp + n_out):
                xc = new[:, c * cw:(c + 1) * cw]
                emit(0, c, jnp.concatenate([xc, jnp.zeros((PAGE_SIZE - xc.shape[0], cw), F32)], axis=0))
                for i in range(1, pp):
                    emit(i, c, jnp.zeros((PAGE_SIZE, cw), F32))
    else:
        pages()


def _gather_chunks(pool, layout, page_table, new, *, cw, c_all, n_out, n_cmp=0, cmp_w=None):
    bk, n_pages = page_table.shape
    pp = min(PAGES_PER_STEP, n_pages)
    assert n_pages % pp == 0
    n_page_steps = n_pages // pp
    has_new = new is not None
    n_steps = n_page_steps + (1 if has_new else 0)
    rows = pp * PAGE_SIZE
    lk_pad = n_steps * rows
    page_block = (1,) + pool.shape[1:]
    zeros = (0,) * (len(page_block) - 1)

    def page_map(i):
        def f(b, j, pt):
            return (pt[b, jnp.minimum(j * pp + i, n_pages - 1)],) + zeros
        return f

    in_specs = [pl.BlockSpec(page_block, page_map(i)) for i in range(pp)]
    args = [pool] * pp
    if has_new:
        in_specs.append(pl.BlockSpec((1,) + new.shape[1:], lambda b, j, pt: (b, 0, 0)))
        args.append(new)
    if n_cmp:
        in_specs.append(pl.BlockSpec(cmp_w.shape, lambda b, j, pt: (0, 0, 0)))
        args.append(cmp_w)
    out_specs = [pl.BlockSpec((1, n_out, rows, LANE), lambda b, j, pt: (b, 0, j, 0))]
    out_shape = [jax.ShapeDtypeStruct((bk, n_out, lk_pad, LANE), BF16)]
    if n_cmp:
        nb = rows // NSA_SEL_BLOCK
        out_specs.append(pl.BlockSpec((1, n_cmp, 2, nb, HEAD_DIM), lambda b, j, pt: (b, 0, 0, j, 0)))
        out_shape.append(jax.ShapeDtypeStruct((bk, n_cmp, 2, lk_pad // NSA_SEL_BLOCK, HEAD_DIM), F32))
    kern = functools.partial(_gather_kernel, pp=pp, n_page_steps=n_page_steps, layout=layout, cw=cw,
                             c_all=c_all, n_cmp=n_cmp, n_out=n_out, has_new=has_new)
    outs = pl.pallas_call(
        kern,
        grid_spec=pltpu.PrefetchScalarGridSpec(
            num_scalar_prefetch=1, grid=(bk, n_steps), in_specs=in_specs, out_specs=out_specs),
        out_shape=out_shape,
        compiler_params=_cparams(("parallel", "arbitrary")),
        name="gather_pages",
    )(page_table, *args)
    return outs if n_cmp else outs[0]


def _identity_pages(x):
    b, l, w = x.shape
    n_pages = l // PAGE_SIZE
    pool = x.reshape(b * n_pages, PAGE_SIZE, w)
    table = jnp.arange(b * n_pages, dtype=I32).reshape(b, n_pages)
    return pool, table


def _flash_tile(s_all, bias, v_tile, carry, n_heads, tq, l_in_acc, v_t=False):
    m, l, acc = carry
    s = s_all
    if bias is not None:
        s = jnp.concatenate([s_all[h * tq:(h + 1) * tq] + bias for h in range(n_heads)], axis=0)
    m_new = jnp.maximum(m, jnp.max(s, axis=-1, keepdims=True))
    p = jnp.exp2(s - m_new)
    alpha = jnp.exp2(m - m_new)
    acc = alpha * acc + (_dot_nt if v_t else _dot)(p.astype(BF16), v_tile)
    if not l_in_acc:
        l = alpha * l + jnp.sum(p, axis=-1, keepdims=True)
    return m_new, l, acc


def _flash_init(rows, dv):
    return (jnp.full((rows, 1), NEG, F32), jnp.zeros((rows, 1), F32), jnp.zeros((rows, dv), F32))


def _flash_out(acc):
    return acc[:, :HEAD_DIM] * (1.0 / jnp.maximum(acc[:, HEAD_DIM:HEAD_DIM + 1], 1e-30))


def _stack_heads(q, n, width):
    return jnp.concatenate([q[:, h * width:(h + 1) * width] for h in range(n)], axis=0)


def _pad_lanes(x):
    return jnp.concatenate([x, jnp.zeros((x.shape[0], LANE - x.shape[1]), x.dtype)], axis=-1)


def _nsa_prologue(qg, kw, vw, cmp4, *, q0, tq, nsp, n_top, win_pos0, start):
    hpg = NSA_HEADS // NSA_KV_HEADS
    rows = hpg * tq
    tqp = max(tq, LANE)
    qpos = q0 + lax.broadcasted_iota(I32, (tq, 1), 0)
    qpos_r = jnp.concatenate([qpos] * hpg, axis=0)
    blk = lax.broadcasted_iota(I32, (1, nsp), 1)
    blk_r = lax.broadcasted_iota(I32, (nsp, 1), 0)
    blk_rf = blk_r.astype(F32)
    cur_l = (q0 + lax.broadcasted_iota(I32, (1, tqp), 1)) // NSA_SEL_BLOCK
    vis_e = (blk * NSA_SEL_BLOCK + (NSA_CMP_BLOCK - 1)) <= qpos_r
    vis_o = (blk * NSA_SEL_BLOCK + (NSA_SEL_BLOCK - 1)) <= qpos_r
    kpos_w = win_pos0 + start + lax.broadcasted_iota(I32, (1, kw.shape[0]), 1)
    dlt = qpos - kpos_w
    bias_w = jnp.where(dlt >= 0, jnp.where(dlt < NSA_WINDOW, 0.0, NEG), NEG)

    qs64 = _stack_heads(qg, hpg, HEAD_DIM)
    qs = _pad_lanes(qs64).astype(BF16)
    qs64 = qs64.astype(BF16)

    _, _, acc_w = _flash_tile(_dot_nt(qs, kw), bias_w, vw, _flash_init(rows, LANE), hpg, tq, True)
    o_w = _flash_out(acc_w)

    kce, kco, vce, vco = (x.astype(BF16) for x in cmp4)
    s_e = jnp.where(vis_e, _dot_nt(qs64, kce), NEG)
    s_o = jnp.where(vis_o, _dot_nt(qs64, kco), NEG)
    m = jnp.maximum(jnp.max(s_e, axis=-1, keepdims=True), jnp.max(s_o, axis=-1, keepdims=True))
    p_e = jnp.where(vis_e, jnp.exp2(s_e - m), 0.0)
    p_o = jnp.where(vis_o, jnp.exp2(s_o - m), 0.0)
    den = jnp.sum(p_e, axis=-1, keepdims=True) + jnp.sum(p_o, axis=-1, keepdims=True)
    inv = 1.0 / jnp.maximum(den, 1e-30)
    p_e = p_e * inv
    p_o = p_o * inv
    o_c = _dot(p_e.astype(BF16), vce) + _dot(p_o.astype(BF16), vco)

    pe_h = sum(p_e[h * tq:(h + 1) * tq] for h in range(hpg))
    po_h = sum(p_o[h * tq:(h + 1) * tq] for h in range(hpg))
    imp = pe_h + po_h
    if tqp > tq:
        imp = jnp.concatenate([imp, jnp.zeros((tqp - tq, nsp), F32)], axis=0)
    imp = imp.T
    imp = jnp.where((blk_r == cur_l) | (blk_r == 0), FORCE_SCORE, imp)
    imp = jnp.where(blk_r <= cur_l, imp, NEG)
    sel = jnp.zeros((nsp, tqp), F32)
    for _ in range(n_top):
        mx = jnp.max(imp, axis=0, keepdims=True)
        first = jnp.min(jnp.where(imp == mx, blk_rf, float(nsp)), axis=0, keepdims=True)
        pick = blk_rf == first
        sel = jnp.where(pick & (mx > 0.5 * NEG), 1.0, sel)
        imp = jnp.where(pick, REMOVED, imp)
    return qs, qs64, o_c, o_w, sel.T[:tq].astype(BF16)


def _nsa_combine(gates, branches, tq):
    hpg = NSA_HEADS // NSA_KV_HEADS
    outs = []
    for g, (o_c, o_s, o_w) in enumerate(branches):
        for h in range(hpg):
            r = slice(h * tq, (h + 1) * tq)
            c = g * LANE + 3 * h
            outs.append(gates[:, c:c + 1] * o_c[r] + gates[:, c + 1:c + 2] * o_s[r]
                        + gates[:, c + 2:c + 3] * o_w[r])
    return jnp.concatenate(outs, axis=-1)


def _nsa_kernel(q_ref, g_ref, cmp_ref, ksel_ref, kwin_ref, o_ref, *,
                tq, tk, q_pos0, win_pos0, nsp, n_top):
    G = NSA_KV_HEADS
    hpg = NSA_HEADS // G
    gw = hpg * HEAD_DIM
    qi = pl.program_id(1)
    q0 = q_pos0 + qi * tq
    qpos = q0 + lax.broadcasted_iota(I32, (tq, 1), 0)
    rows = hpg * tq
    q_all = q_ref[0] * (HEAD_DIM ** -0.5 * LOG2E)
    start = pl.multiple_of(jnp.maximum(q0 - NSA_WINDOW - win_pos0, 0), 8)
    span = _win_span(tq)
    pro = [_nsa_prologue(q_all[:, g * gw:(g + 1) * gw],
                         kwin_ref[0, g, pl.ds(start, span), :], kwin_ref[0, G + g, pl.ds(start, span), :],
                         tuple(cmp_ref[0, 2 * g + kv, eo] for kv in range(2) for eo in range(2)),
                         q0=q0, tq=tq, nsp=nsp, n_top=n_top, win_pos0=win_pos0, start=start)
           for g in range(G)]

    blk_col = lax.broadcasted_iota(I32, (nsp, 1), 0)

    def sel_step(j, carry):
        k0 = pl.multiple_of(j * tk, tk)
        kpos = k0 + lax.broadcasted_iota(I32, (1, tk), 1)
        expand = jnp.where(blk_col == kpos // NSA_SEL_BLOCK, 1.0, 0.0).astype(BF16)
        out = []
        for g in range(G):
            picked = _dot(pro[g][4], expand)
            bias = jnp.where(kpos <= qpos, jnp.where(picked > 0.5, 0.0, NEG), NEG)
            s = _dot_nt(pro[g][0], ksel_ref[0, g, pl.ds(k0, tk), :])
            out.append(_flash_tile(s, bias, ksel_ref[0, G + g, pl.ds(k0, tk), :], carry[g], hpg, tq, True))
        return tuple(out)

    n_tiles = (q0 + tq - 1) // tk + 1
    res = lax.fori_loop(0, n_tiles, sel_step, tuple(_flash_init(rows, LANE) for _ in range(G)))
    o_ref[0] = _nsa_combine(g_ref[0], [(pro[g][2], _flash_out(res[g][2]), pro[g][3]) for g in range(G)], tq)


def _nsa_attention(q, gates, cmp, ksel, kwin, *, tq, q_pos0, lk, win_pos0):
    bk, lq, _ = q.shape
    G = NSA_KV_HEADS
    nsp = cmp.shape[3]
    lk_pad = ksel.shape[2]
    lw_pad = kwin.shape[2]
    tk = _kv_tile(tq, lk_pad)
    ns = -(-lk // NSA_SEL_BLOCK)
    assert lw_pad >= _win_span(tq) and q_pos0 + lq <= lk_pad
    kern = functools.partial(_nsa_kernel, tq=tq, tk=tk, q_pos0=q_pos0, win_pos0=win_pos0,
                             nsp=nsp, n_top=min(NSA_TOPN, ns))
    qw = NSA_HEADS * HEAD_DIM
    return pl.pallas_call(
        kern,
        grid=(bk, lq // tq),
        in_specs=[pl.BlockSpec((1, tq, qw), lambda b, i: (b, i, 0)),
                  pl.BlockSpec((1, tq, G * LANE), lambda b, i: (b, i, 0)),
                  pl.BlockSpec((1, 2 * G, 2, nsp, HEAD_DIM), lambda b, i: (b, 0, 0, 0, 0)),
                  pl.BlockSpec((1, 2 * G, lk_pad, LANE), lambda b, i: (b, 0, 0, 0)),
                  pl.BlockSpec((1, 2 * G, lw_pad, LANE), lambda b, i: (b, 0, 0, 0))],
        out_specs=pl.BlockSpec((1, tq, qw), lambda b, i: (b, i, 0)),
        out_shape=jax.ShapeDtypeStruct((bk, lq, qw), F32),
        compiler_params=_cparams(("parallel", "arbitrary")),
        name="nsa_attention",
    )(q, gates, cmp, ksel, kwin)


def _nsa_decode_kernel(pt_ref, q_ref, g_ref, new_ref, w2_ref, kwin_ref, *refs,
                       pp, n_steps, tq, n_new, past_len, win_pos0, nsp, n_top):
    page_refs = refs[:pp]
    o_ref, cmp_ref, oc_ref, ow_ref, sel_ref, m_ref, l_ref, acc_ref = refs[pp:]
    G = NSA_KV_HEADS
    hpg = NSA_HEADS // G
    gw = hpg * HEAD_DIM
    rows = hpg * tq
    j = pl.program_id(1)
    keys = pp * PAGE_SIZE
    nb = keys // NSA_SEL_BLOCK
    q_all = q_ref[0] * (HEAD_DIM ** -0.5 * LOG2E)

    def queries(g):
        qs64 = _stack_heads(q_all[:, g * gw:(g + 1) * gw], hpg, HEAD_DIM)
        return _pad_lanes(qs64).astype(BF16), qs64.astype(BF16)

    @pl.when(j == 0)
    def _():
        cmp_ref[...] = jnp.zeros(cmp_ref.shape, F32)

    @pl.when(j < n_steps)
    def _():
        r0 = pl.multiple_of(j * nb, nb)
        for c in range(2 * G):
            x = jnp.concatenate([r[0, c].T for r in page_refs], axis=0)
            prod = x.reshape(nb, NSA_SEL_BLOCK, HEAD_DIM) * w2_ref[c][None]
            cmp_ref[c, 0, pl.ds(r0, nb), :] = jnp.sum(prod[:, :NSA_CMP_BLOCK], axis=1)
            cmp_ref[c, 1, pl.ds(r0, nb), :] = jnp.sum(prod[:, NSA_CMP_BLOCK:], axis=1)

    @pl.when(j == n_steps - 1)
    def _():
        for g in range(G):
            _, _, o_c, o_w, sel = _nsa_prologue(
                q_all[:, g * gw:(g + 1) * gw], kwin_ref[0, g], kwin_ref[0, G + g],
                tuple(cmp_ref[g + G * kv, eo] for kv in range(2) for eo in range(2)),
                q0=past_len, tq=tq, nsp=nsp, n_top=n_top, win_pos0=win_pos0, start=0)
            oc_ref[g] = o_c
            ow_ref[g] = o_w
            sel_ref[g] = sel
        m_ref[...] = jnp.full(m_ref.shape, NEG, F32)
        l_ref[...] = jnp.zeros(l_ref.shape, F32)
        acc_ref[...] = jnp.zeros(acc_ref.shape, F32)

    def update(g, s, bias, v, v_t):
        carry = (m_ref[g], l_ref[g], acc_ref[g])
        m, l, acc = _flash_tile(s, bias, v, carry, hpg, tq, False, v_t=v_t)
        m_ref[g] = m
        l_ref[g] = l
        acc_ref[g] = acc

    @pl.when(j >= n_steps)
    def _():
        kpos = (j - n_steps) * keys + lax.broadcasted_iota(I32, (1, keys), 1)
        blk_col = lax.broadcasted_iota(I32, (nsp, 1), 0)
        expand = jnp.where(blk_col == kpos // NSA_SEL_BLOCK, 1.0, 0.0).astype(BF16)
        for g in range(G):
            _, qs64 = queries(g)
            k_t = jnp.concatenate([r[0, g] for r in page_refs], axis=-1).astype(BF16)
            v_t = jnp.concatenate([r[0, G + g] for r in page_refs], axis=-1).astype(BF16)
            bias = jnp.where(_dot(sel_ref[g], expand) > 0.5, 0.0, NEG)
            update(g, _dot(qs64, k_t), bias, v_t, True)

    @pl.when(j == 2 * n_steps - 1)
    def _():
        new = new_ref[0]
        pad = jnp.zeros((LANE - tq, HEAD_DIM), F32)
        row = lax.broadcasted_iota(I32, (tq, 1), 0)
        col = lax.broadcasted_iota(I32, (1, LANE), 1)
        own = past_len // NSA_SEL_BLOCK
        branches = []
        for g in range(G):
            _, qs64 = queries(g)
            k = jnp.concatenate([new[:, (2 * G + g) * HEAD_DIM:(2 * G + g + 1) * HEAD_DIM], pad], axis=0)
            v = jnp.concatenate([new[:, (3 * G + g) * HEAD_DIM:(3 * G + g + 1) * HEAD_DIM], pad], axis=0)
            picked = sel_ref[g][:, own:own + 1].astype(F32) > 0.5
            bias = jnp.where((col <= row) & (col < n_new) & picked, 0.0, NEG)
            update(g, _dot_nt(qs64, k.astype(BF16)), bias, v.astype(BF16), False)
            o_s = acc_ref[g] * (1.0 / jnp.maximum(l_ref[g], 1e-30))
            branches.append((oc_ref[g], o_s, ow_ref[g]))
        o_ref[0] = _nsa_combine(g_ref[0], branches, tq)


def _nsa_decode(q, gates, pool, page_table, new, w2, kwin, *, n_new, win_pos0):
    bk, tq, qw = q.shape
    n_pages = page_table.shape[1]
    pp = min(PAGES_PER_STEP, n_pages)
    past_len = n_pages * PAGE_SIZE
    assert n_pages % pp == 0 and past_len % NSA_SEL_BLOCK == 0 and n_new <= NSA_CMP_BLOCK
    assert kwin.shape[2] == _win_span(tq)
    n_steps = n_pages // pp
    G = NSA_KV_HEADS
    rows = (NSA_HEADS // G) * tq
    lk = past_len + n_new
    ns = -(-lk // NSA_SEL_BLOCK)
    nsp = _round_up(ns, LANE)

    def page_map(i):
        return lambda b, j, pt: (pt[b, (j % n_steps) * pp + i], j // n_steps, 0, 0)

    const3 = lambda b, j, pt: (b, 0, 0)
    in_specs = [pl.BlockSpec((1, tq, qw), const3),
                pl.BlockSpec((1, tq, G * LANE), const3),
                pl.BlockSpec((1, tq, new.shape[-1]), const3),
                pl.BlockSpec(w2.shape, lambda b, j, pt: (0, 0, 0)),
                pl.BlockSpec((1,) + kwin.shape[1:], lambda b, j, pt: (b, 0, 0, 0))]
    in_specs += [pl.BlockSpec((1, 2 * G, HEAD_DIM, PAGE_SIZE), page_map(i)) for i in range(pp)]
    kern = functools.partial(_nsa_decode_kernel, pp=pp, n_steps=n_steps, tq=tq, n_new=n_new, past_len=past_len,
                             win_pos0=win_pos0, nsp=nsp, n_top=min(NSA_TOPN, ns))
    return pl.pallas_call(
        kern,
        grid_spec=pltpu.PrefetchScalarGridSpec(
            num_scalar_prefetch=1, grid=(bk, 2 * n_steps), in_specs=in_specs,
            out_specs=pl.BlockSpec((1, tq, qw), const3),
            scratch_shapes=[pltpu.VMEM((2 * G, 2, nsp, HEAD_DIM), F32),
                            pltpu.VMEM((G, rows, HEAD_DIM), F32), pltpu.VMEM((G, rows, HEAD_DIM), F32),
                            pltpu.VMEM((G, tq, nsp), BF16),
                            pltpu.VMEM((G, rows, 1), F32), pltpu.VMEM((G, rows, 1), F32),
                            pltpu.VMEM((G, rows, HEAD_DIM), F32)]),
        out_shape=jax.ShapeDtypeStruct((bk, tq, qw), F32),
        compiler_params=_cparams(("parallel", "arbitrary")),
        name="nsa_decode",
    )(page_table, q, gates, new, w2, kwin, *([pool] * pp))


def _shift_rows(x, d, fill):
    rolled = pltpu.roll(x, d, axis=0)
    row = lax.broadcasted_iota(I32, x.shape, 0)
    return jnp.where(row >= d, rolled, fill)


def _lru_kernel(x_ref, hist_ref, h0_ref, cw_ref, cb_ref, wr_ref, br_ref, wi_ref, bi_ref, lam_ref,
                o_ref, hl_ref, tail_ref, h_ref, *, tl, last_row):
    li = pl.program_id(1)

    @pl.when(li == 0)
    def _():
        tail_ref[...] = jnp.concatenate(
            [jnp.zeros((8 - (CONV_WIDTH - 1), x_ref.shape[-1]), F32), hist_ref[0]], axis=0)
        h_ref[...] = h0_ref[0]

    x = x_ref[0]
    xp = jnp.concatenate([tail_ref[...], x], axis=0)
    cw = cw_ref[...]
    conv = sum(xp[8 - (CONV_WIDTH - 1) + j:8 - (CONV_WIDTH - 1) + j + tl] * cw[j:j + 1]
               for j in range(CONV_WIDTH))
    conv = cb_ref[...] + conv
    tail_ref[...] = x[tl - 8:tl]

    cb16 = conv.astype(BF16)
    r = jax.nn.sigmoid(_dot(cb16, wr_ref[...]) + br_ref[...])
    ig = jax.nn.sigmoid(_dot(cb16, wi_ref[...]) + bi_ref[...])
    log_a = -LRU_C * r * jax.nn.softplus(-lam_ref[...])
    a = jnp.exp(log_a)
    th = jnp.tanh(log_a)
    b = jnp.sqrt(-2.0 * th / (1.0 - th)) * (ig * conv)

    d = 1
    while d < tl:
        a_prev = _shift_rows(a, d, 1.0)
        b_prev = _shift_rows(b, d, 0.0)
        b = a * b_prev + b
        a = a * a_prev
        d *= 2
    h = a * h_ref[...] + b
    o_ref[0] = h
    h_ref[...] = h[tl - 1:tl]

    @pl.when(li == pl.num_programs(1) - 1)
    def _():
        hl_ref[0] = h[last_row:last_row + 1]


def _block_diag(w):
    nb, bw, _ = w.shape
    eye = jnp.eye(nb, dtype=w.dtype)
    return (eye[:, None, :, None] * w[:, :, None, :]).reshape(nb * bw, nb * bw)


def _conv_rglru(x_b, hist, h0, conv_w, conv_b, w_r, b_r, w_i, b_i, lam, *, tl, n_valid):
    b, l, w = x_b.shape
    assert tl >= 8 and l % tl == 0 and n_valid > l - tl
    kern = functools.partial(_lru_kernel, tl=tl, last_row=(n_valid - 1) % tl)
    vec = lambda: pl.BlockSpec((1, w), lambda bi, li: (0, 0))
    h, h_last = pl.pallas_call(
        kern,
        grid=(b, l // tl),
        in_specs=[pl.BlockSpec((1, tl, w), lambda bi, li: (bi, li, 0)),
                  pl.BlockSpec((1, CONV_WIDTH - 1, w), lambda bi, li: (bi, 0, 0)),
                  pl.BlockSpec((1, 1, w), lambda bi, li: (bi, 0, 0)),
                  pl.BlockSpec((CONV_WIDTH, w), lambda bi, li: (0, 0)),
                  vec(),
                  pl.BlockSpec((w, w), lambda bi, li: (0, 0)), vec(),
                  pl.BlockSpec((w, w), lambda bi, li: (0, 0)), vec(), vec()],
        out_specs=[pl.BlockSpec((1, tl, w), lambda bi, li: (bi, li, 0)),
                   pl.BlockSpec((1, 1, w), lambda bi, li: (bi, 0, 0))],
        out_shape=[jax.ShapeDtypeStruct((b, l, w), F32), jax.ShapeDtypeStruct((b, 1, w), F32)],
        scratch_shapes=[pltpu.VMEM((8, w), F32), pltpu.VMEM((1, w), F32)],
        compiler_params=_cparams(("parallel", "arbitrary")),
        name="conv_rglru",
    )(x_b, hist, h0.reshape(b, 1, w), conv_w, conv_b.reshape(1, w),
      _block_diag(w_r).astype(BF16), b_r.reshape(1, w), _block_diag(w_i).astype(BF16), b_i.reshape(1, w),
      lam.reshape(1, w))
    return h, h_last.reshape(b, w)


def _out_kernel(oa_ref, za_ref, ob_ref, zb_ref, x_ref, gate_ref, w_ref, fg_ref, o_ref, *, final_norm):
    half = oa_ref.shape[-1]
    ma = (oa_ref[0] * jax.nn.silu(za_ref[0])).astype(BF16)
    mb = (ob_ref[0] * jax.nn.silu(zb_ref[0])).astype(BF16)
    y = _dot(ma, w_ref[0:half, :]) + _dot(mb, w_ref[half:2 * half, :])
    out = x_ref[0] + gate_ref[0] * y
    if final_norm:
        out = out * lax.rsqrt(jnp.mean(out * out, axis=-1, keepdims=True) + NORM_EPS) * fg_ref[...]
    o_ref[0] = out


def _out_project(o_a, z_a, o_b, z_b, x, gate, w_out, final_g, *, tl, final_norm):
    b, l, d = x.shape
    half = o_a.shape[-1]
    ts = gate.shape[1]
    tm = 1 if ts == 1 else tl
    mod_map = (lambda bi, li: (bi, 0, 0)) if ts == 1 else (lambda bi, li: (bi, li, 0))
    act = lambda: pl.BlockSpec((1, tl, half), lambda bi, li: (bi, li, 0))
    return pl.pallas_call(
        functools.partial(_out_kernel, final_norm=final_norm),
        grid=(b, l // tl),
        in_specs=[act(), act(), act(), act(),
                  pl.BlockSpec((1, tl, d), lambda bi, li: (bi, li, 0)),
                  pl.BlockSpec((1, tm, d), mod_map),
                  pl.BlockSpec((2 * half, d), lambda bi, li: (0, 0)),
                  pl.BlockSpec((1, d), lambda bi, li: (0, 0))],
        out_specs=pl.BlockSpec((1, tl, d), lambda bi, li: (bi, li, 0)),
        out_shape=jax.ShapeDtypeStruct((b, l, d), F32),
        compiler_params=_cparams(("parallel", "arbitrary")),
        name="out_project",
    )(o_a, z_a, o_b, z_b, x, gate, w_out, final_g.reshape(1, d))


def _diff_queries(q, tq):
    hpg = DIFF_HEADS // DIFF_KV_HEADS
    q = q * (DIFF_HALF ** -0.5 * LOG2E)
    zero = jnp.zeros((tq, DIFF_HALF), F32)
    parts = []
    for mp in range(2):
        for h in range(hpg):
            qh = q[:, (2 * h + mp) * DIFF_HALF:(2 * h + mp + 1) * DIFF_HALF]
            parts.append(jnp.concatenate([qh, zero] if mp == 0 else [zero, qh], axis=-1))
    return jnp.concatenate(parts, axis=0).astype(BF16)


def _diff_finish(l, acc, lamv, subg, tq):
    hpg = DIFF_HEADS // DIFF_KV_HEADS
    o = acc * (1.0 / jnp.maximum(l, 1e-30))
    lam = (jnp.exp(jnp.sum(lamv[0:1] * lamv[1:2], axis=-1, keepdims=True))
           - jnp.exp(jnp.sum(lamv[2:3] * lamv[3:4], axis=-1, keepdims=True)) + DIFF_LAMBDA_INIT)
    half = hpg * tq
    od = o[0:half] - lam * o[half:2 * half]
    od = od * lax.rsqrt(jnp.mean(od * od, axis=-1, keepdims=True) + NORM_EPS)
    od = od * subg * (1.0 - DIFF_LAMBDA_INIT)
    return jnp.concatenate([od[h * tq:(h + 1) * tq] for h in range(hpg)], axis=-1)


def _diff_kernel(q_ref, k_ref, v_ref, lamv_ref, subg_ref, o_ref, *, tq, tk, q_pos0):
    hpg = DIFF_HEADS // DIFF_KV_HEADS
    qi = pl.program_id(2)
    q0 = q_pos0 + qi * tq
    n_maps = 2 * hpg
    rows = n_maps * tq
    qpos = q0 + lax.broadcasted_iota(I32, (tq, 1), 0)
    qs = _diff_queries(q_ref[0], tq)

    def step(masked):
        def f(j, carry):
            k0 = pl.multiple_of(j * tk, tk)
            s = _dot_nt(qs, k_ref[0, 0, pl.ds(k0, tk), :])
            bias = None
            if masked:
                kpos = k0 + lax.broadcasted_iota(I32, (1, tk), 1)
                bias = jnp.where(kpos <= qpos, 0.0, NEG)
            return _flash_tile(s, bias, v_ref[0, 0, pl.ds(k0, tk), :], carry, n_maps, tq, False)
        return f

    n_full = (q0 + 1) // tk
    n_tiles = (q0 + tq - 1) // tk + 1
    carry = lax.fori_loop(0, n_full, step(False), _flash_init(rows, 2 * DIFF_HALF))
    m, l, acc = lax.fori_loop(n_full, n_tiles, step(True), carry)
    o_ref[0] = _diff_finish(l, acc, lamv_ref[...], subg_ref[...], tq)


def _diff_decode_kernel(pt_ref, q_ref, new_ref, lamv_ref, subg_ref, *refs, pp, tq, n_new):
    page_refs = refs[:pp]
    o_ref, m_ref, l_ref, acc_ref = refs[pp:]
    G = DIFF_KV_HEADS
    n_maps = 2 * (DIFF_HEADS // G)
    gw = n_maps * DIFF_HALF
    j = pl.program_id(1)

    @pl.when(j == 0)
    def _():
        m_ref[...] = jnp.full(m_ref.shape, NEG, F32)
        l_ref[...] = jnp.zeros(l_ref.shape, F32)
        acc_ref[...] = jnp.zeros(acc_ref.shape, F32)

    def update(g, qs, k, v, bias):
        carry = (m_ref[g], l_ref[g], acc_ref[g])
        m, l, acc = _flash_tile(_dot_nt(qs, k), bias, v, carry, n_maps, tq, False)
        m_ref[g] = m
        l_ref[g] = l
        acc_ref[g] = acc

    qs = [_diff_queries(q_ref[0][:, g * gw:(g + 1) * gw], tq) for g in range(G)]
    for g in range(G):
        k = jnp.concatenate([r[0, pl.ds(g, PAGE_SIZE, stride=2 * G), :] for r in page_refs], axis=0)
        v = jnp.concatenate([r[0, pl.ds(G + g, PAGE_SIZE, stride=2 * G), :] for r in page_refs], axis=0)
        update(g, qs[g], k.astype(BF16), v.astype(BF16), None)

    @pl.when(j == pl.num_programs(1) - 1)
    def _():
        new = new_ref[0]
        pad = jnp.zeros((LANE - tq, 2 * DIFF_HALF), F32)
        row = lax.broadcasted_iota(I32, (tq, 1), 0)
        col = lax.broadcasted_iota(I32, (1, LANE), 1)
        bias = jnp.where((col <= row) & (col < n_new), 0.0, NEG)
        outs = []
        for g in range(G):
            k = jnp.concatenate([new[:, g * 2 * DIFF_HALF:(g + 1) * 2 * DIFF_HALF], pad], axis=0)
            v = jnp.concatenate([new[:, (G + g) * 2 * DIFF_HALF:(G + g + 1) * 2 * DIFF_HALF], pad], axis=0)
            update(g, qs[g], k.astype(BF16), v.astype(BF16), bias)
            outs.append(_diff_finish(l_ref[g], acc_ref[g], lamv_ref[...], subg_ref[...], tq))
        o_ref[0] = jnp.concatenate(outs, axis=-1)


def _diff_decode(q, pool, page_table, new, lamv, subln_g, *, n_new):
    bk, tq, qw = q.shape
    n_pages = page_table.shape[1]
    pp = min(PAGES_PER_STEP, n_pages)
    assert n_pages % pp == 0
    G = DIFF_KV_HEADS
    rows = 2 * (DIFF_HEADS // G) * tq

    def page_map(i):
        return lambda b, j, pt: (pt[b, j * pp + i], 0, 0)

    in_specs = [pl.BlockSpec((1, tq, qw), lambda b, j, pt: (b, 0, 0)),
                pl.BlockSpec((1, tq, new.shape[-1]), lambda b, j, pt: (b, 0, 0)),
                pl.BlockSpec((4, DIFF_HALF), lambda b, j, pt: (0, 0)),
                pl.BlockSpec((1, 2 * DIFF_HALF), lambda b, j, pt: (0, 0))]
    in_specs += [pl.BlockSpec((1,) + pool.shape[1:], page_map(i)) for i in range(pp)]
    return pl.pallas_call(
        functools.partial(_diff_decode_kernel, pp=pp, tq=tq, n_new=n_new),
        grid_spec=pltpu.PrefetchScalarGridSpec(
            num_scalar_prefetch=1, grid=(bk, n_pages // pp), in_specs=in_specs,
            out_specs=pl.BlockSpec((1, tq, qw), lambda b, j, pt: (b, 0, 0)),
            scratch_shapes=[pltpu.VMEM((G, rows, 1), F32), pltpu.VMEM((G, rows, 1), F32),
                            pltpu.VMEM((G, rows, 2 * DIFF_HALF), F32)]),
        out_shape=jax.ShapeDtypeStruct((bk, tq, qw), F32),
        compiler_params=_cparams(("parallel", "arbitrary")),
        name="diff_decode",
    )(page_table, q, new, lamv, subln_g.reshape(1, 2 * DIFF_HALF), *([pool] * pp))


def _diff_attention(q, kv, lamv, subln_g, *, tq, q_pos0):
    bk, lq, _ = q.shape
    G = DIFF_KV_HEADS
    lk_pad = kv.shape[2]
    tk = _kv_tile(tq, lk_pad)
    gw = (DIFF_HEADS // G) * 2 * DIFF_HALF
    assert q_pos0 + lq <= lk_pad
    return pl.pallas_call(
        functools.partial(_diff_kernel, tq=tq, tk=tk, q_pos0=q_pos0),
        grid=(bk, G, lq // tq),
        in_specs=[pl.BlockSpec((1, tq, gw), lambda b, g, i: (b, i, g)),
                  pl.BlockSpec((1, 1, lk_pad, 2 * DIFF_HALF), lambda b, g, i: (b, g, 0, 0)),
                  pl.BlockSpec((1, 1, lk_pad, 2 * DIFF_HALF), lambda b, g, i: (b, G + g, 0, 0)),
                  pl.BlockSpec((4, DIFF_HALF), lambda b, g, i: (0, 0)),
                  pl.BlockSpec((1, 2 * DIFF_HALF), lambda b, g, i: (0, 0))],
        out_specs=pl.BlockSpec((1, tq, gw), lambda b, g, i: (b, i, g)),
        out_shape=jax.ShapeDtypeStruct((bk, lq, DIFF_HEADS * 2 * DIFF_HALF), F32),
        compiler_params=_cparams(("parallel", "parallel", "arbitrary")),
        name="diff_attention",
    )(q, kv, kv, lamv, subln_g.reshape(1, 2 * DIFF_HALF))


def _dsa_keys(s_all, wi, causal, tq):
    score = jnp.zeros((tq, s_all.shape[1]), F32)
    for h in range(IDX_HEADS):
        score = score + wi[:, h:h + 1] * jnp.maximum(s_all[h * tq:(h + 1) * tq], 0.0)
    bits = pltpu.bitcast(score, I32)
    key = jnp.where(bits < 0, bits ^ 0x7FFFFFFF, bits)
    key = jnp.where(score == 0.0, 0, key)
    return jnp.where(causal, jnp.where(score > 0.5 * NEG, key, INT_MIN), INT_MIN)


def _dsa_threshold(read, n_tiles, kmax, *, tq, tk, n_sel, transposed, unroll):
    lanes = tk // LANE

    def count(*bounds):
        def f(j, accs):
            keys = read(j)
            out = []
            for bound, acc in zip(bounds, accs):
                hit = jnp.where(keys >= bound, 1.0, 0.0)
                if transposed:
                    acc = acc + jnp.sum(hit.reshape(tk // COUNT_ROWS, COUNT_ROWS, tq), axis=0)
                else:
                    for c in range(lanes):
                        acc = acc + hit[:, c * LANE:(c + 1) * LANE]
                out.append(acc)
            return tuple(out)
        acc0 = jnp.zeros((COUNT_ROWS, tq) if transposed else (tq, LANE), F32)
        accs = lax.fori_loop(0, n_tiles, f, tuple(acc0 for _ in bounds), unroll=unroll)
        return [jnp.sum(acc, axis=0 if transposed else -1, keepdims=True) for acc in accs]

    k_f = float(n_sel)
    probe = jnp.maximum(kmax - (PROBE_BINADES << 23), 1)
    c_adm, c_nn, c_pos, c_probe = count(INT_MIN + 1, 0, 1, probe)
    few, pos, zero, high = c_adm < k_f, c_pos >= k_f, c_nn >= k_f, c_probe >= k_f
    lo0 = jnp.where(pos, jnp.where(high, probe, 1), jnp.where(zero, 0, INT_MIN))
    hi0 = jnp.where(pos, jnp.where(high, kmax, probe - 1), jnp.where(zero, 0, jnp.where(few, INT_MIN, -1)))

    def unfinished(lo_hi):
        lo, hi = lo_hi
        return jnp.max(jnp.where(lo < hi, 1.0, 0.0)) > 0.0

    def bisect(lo_hi):
        lo, hi = lo_hi
        mid = (lo >> 1) + (hi >> 1) + ((lo | hi) & 1)
        cnt, = count(mid)
        lo = jnp.where(cnt >= k_f, mid, lo)
        hi = jnp.where(cnt > k_f, hi, jnp.where(cnt == k_f, mid, mid - 1))
        return lo, hi

    thr, _ = lax.while_loop(unfinished, lambda s: bisect(bisect(s)), (lo0, hi0))
    thr = jnp.maximum(thr, INT_MIN + 1)
    n_gt, = count(thr + 1)
    return thr, k_f - n_gt


def _dsa_bias(key, thr, need, seen, tri):
    tied = key == thr
    tied_b = jnp.where(tied, 1.0, 0.0).astype(BF16)
    t = tri.shape[0]
    ranks = []
    for c in range(key.shape[1] // t):
        r = _dot(tied_b[:, c * t:(c + 1) * t], tri) + seen
        ranks.append(r)
        seen = r[:, t - 1:t]
    rank = jnp.concatenate(ranks, axis=-1)
    return jnp.where(key > thr, 0.0, jnp.where(tied, jnp.where(rank <= need, 0.0, NEG), NEG)), seen


def _dsa_kernel(q_ref, qi_ref, kw_ref, kidx_ref, kv_ref, tri_ref, o_ref, key_ref, keyt_ref, *,
                tq, tk, q_pos0, n_sel, one_block):
    G = DSA_KV_HEADS
    hpg = DSA_HEADS // G
    transposed = tq % LANE == 0
    q0 = q_pos0 + (0 if one_block else pl.program_id(1) * tq)
    qpos = q0 + lax.broadcasted_iota(I32, (tq, 1), 0)
    n_tiles = (q0 + tq - 1) // tk + 1
    lanes = tk // LANE

    qidx = _pad_lanes(_stack_heads(qi_ref[0] * (IDX_DIM ** -0.5), IDX_HEADS, IDX_DIM)).astype(BF16)
    wi = kw_ref[0][:, IDX_DIM:IDX_DIM + IDX_HEADS] * (IDX_HEADS ** -0.5)

    def score_step(j, kmax):
        k0 = pl.multiple_of(j * tk, tk)
        s_all = _dot_nt(qidx, kidx_ref[0, 0, pl.ds(k0, tk), :])
        kpos = k0 + lax.broadcasted_iota(I32, (1, tk), 1)
        key = _dsa_keys(s_all, wi, kpos <= qpos, tq)
        key_ref[j] = key
        if not transposed:
            return jnp.maximum(kmax, jnp.max(key, axis=-1, keepdims=True))
        key_t = key.T
        keyt_ref[j] = key_t
        return jnp.maximum(kmax, jnp.max(key_t, axis=0, keepdims=True))

    kmax = lax.fori_loop(0, n_tiles, score_step, jnp.full((1, tq) if transposed else (tq, 1), INT_MIN, I32))
    thr, need = _dsa_threshold((lambda j: keyt_ref[j]) if transposed else (lambda j: key_ref[j]), n_tiles, kmax,
                               tq=tq, tk=tk, n_sel=n_sel, transposed=transposed, unroll=one_block)

    def along_rows(v):
        rep = jnp.broadcast_to(v, (LANE, tq)).T
        return jnp.concatenate([rep] * lanes, axis=-1)

    if transposed:
        thr = along_rows(thr)
        need = along_rows(need)

    q = q_ref[0] * (HEAD_DIM ** -0.5 * LOG2E)
    qs = [_pad_lanes(_stack_heads(q[:, g * hpg * HEAD_DIM:(g + 1) * hpg * HEAD_DIM], hpg, HEAD_DIM)).astype(BF16)
          for g in range(G)]
    rows = hpg * tq

    def att_step(j, carry):
        seen, flash = carry
        k0 = pl.multiple_of(j * tk, tk)
        bias, seen = _dsa_bias(key_ref[j], thr, need, seen, tri_ref[...])
        out = []
        for g in range(G):
            s = _dot_nt(qs[g], kv_ref[0, g, pl.ds(k0, tk), :])
            out.append(_flash_tile(s, bias, kv_ref[0, G + g, pl.ds(k0, tk), :], flash[g], hpg, tq, True))
        return seen, tuple(out)

    _, res = lax.fori_loop(0, n_tiles, att_step,
                           (jnp.zeros((tq, 1), F32), tuple(_flash_init(rows, LANE) for _ in range(G))))
    outs = []
    for g in range(G):
        o = _flash_out(res[g][2])
        outs.extend(o[h * tq:(h + 1) * tq] for h in range(hpg))
    o_ref[0] = jnp.concatenate(outs, axis=-1)


def _dsa_attention(q, qi, kw, kidx, kv, *, tq, q_pos0, lk):
    bk, lq, _ = q.shape
    lk_pad = kv.shape[2]
    tk = _kv_tile(tq, lk_pad)
    n_sel = min(DSA_TOPK_MAX, lk // 4)
    assert q_pos0 + lq <= lk_pad and tk >= n_sel
    kern = functools.partial(_dsa_kernel, tq=tq, tk=tk, q_pos0=q_pos0, n_sel=n_sel, one_block=lq == tq)
    assert tk % TRI_TILE == 0
    tri = jnp.triu(jnp.ones((TRI_TILE, TRI_TILE), BF16))
    return pl.pallas_call(
        kern,
        grid=(bk, lq // tq),
        in_specs=[pl.BlockSpec((1, tq, DSA_HEADS * HEAD_DIM), lambda b, i: (b, i, 0)),
                  pl.BlockSpec((1, tq, IDX_HEADS * IDX_DIM), lambda b, i: (b, i, 0)),
                  pl.BlockSpec((1, tq, LANE), lambda b, i: (b, i, 0)),
                  pl.BlockSpec((1, 1, lk_pad, LANE), lambda b, i: (b, 0, 0, 0)),
                  pl.BlockSpec((1, 4, lk_pad, LANE), lambda b, i: (b, 0, 0, 0)),
                  pl.BlockSpec((TRI_TILE, TRI_TILE), lambda b, i: (0, 0))],
        out_specs=pl.BlockSpec((1, tq, DSA_HEADS * HEAD_DIM), lambda b, i: (b, i, 0)),
        out_shape=jax.ShapeDtypeStruct((bk, lq, DSA_HEADS * HEAD_DIM), F32),
        scratch_shapes=[pltpu.VMEM((lk_pad // tk, tq, tk), I32),
                        pltpu.VMEM((lk_pad // tk, tk, tq) if tq % LANE == 0 else (8, LANE), I32)],
        compiler_params=_cparams(("parallel", "arbitrary")),
        name="dsa_attention",
    )(q, qi, kw, kidx, kv, tri)


def _dsa_decode_kernel(pt_ref, q_ref, qi_ref, kw_ref, newi_ref, newkv_ref, tri_ref, *refs,
                       pp, n_steps, tq, n_new, n_sel):
    ipage_refs = refs[:pp]
    kvpage_refs = refs[pp:2 * pp]
    o_ref, key_ref, thr_ref, need_ref, seen_ref, m_ref, l_ref, acc_ref = refs[2 * pp:]
    G = DSA_KV_HEADS
    hpg = DSA_HEADS // G
    gw = hpg * HEAD_DIM
    j = pl.program_id(1)
    keys = pp * PAGE_SIZE
    qidx = _stack_heads(qi_ref[0] * (IDX_DIM ** -0.5), IDX_HEADS, IDX_DIM).astype(BF16)
    wi = kw_ref[0][:, IDX_DIM:IDX_DIM + IDX_HEADS] * (IDX_HEADS ** -0.5)
    q_all = q_ref[0] * (HEAD_DIM ** -0.5 * LOG2E)
    row = lax.broadcasted_iota(I32, (tq, 1), 0)
    col = lax.broadcasted_iota(I32, (1, LANE), 1)
    own = (col <= row) & (col < n_new)

    def queries(g):
        return _stack_heads(q_all[:, g * gw:(g + 1) * gw], hpg, HEAD_DIM).astype(BF16)

    @pl.when(j < n_steps)
    def _():
        k_t = jnp.concatenate([r[0, 0] for r in ipage_refs], axis=-1).astype(BF16)
        key_ref[j] = _dsa_keys(_dot(qidx, k_t), wi, jnp.full((tq, keys), True), tq)

    @pl.when(j == n_steps - 1)
    def _():
        pad = jnp.zeros((LANE - tq, IDX_DIM), F32)
        k_new = jnp.concatenate([newi_ref[0], pad], axis=0).astype(BF16)
        key_new = _dsa_keys(_dot_nt(qidx, k_new), wi, own, tq)
        key_ref[n_steps] = jnp.concatenate([key_new, jnp.full((tq, keys - LANE), INT_MIN, I32)], axis=-1)
        kmax = jnp.full((tq, 1), INT_MIN, I32)
        for t in range(n_steps + 1):
            kmax = jnp.maximum(kmax, jnp.max(key_ref[t], axis=-1, keepdims=True))
        thr, need = _dsa_threshold(lambda t: key_ref[t], n_steps + 1, kmax, tq=tq, tk=keys, n_sel=n_sel,
                                   transposed=False, unroll=True)
        thr_ref[...] = thr
        need_ref[...] = need
        seen_ref[...] = jnp.zeros(seen_ref.shape, F32)
        m_ref[...] = jnp.full(m_ref.shape, NEG, F32)
        l_ref[...] = jnp.zeros(l_ref.shape, F32)
        acc_ref[...] = jnp.zeros(acc_ref.shape, F32)

    def update(g, s, bias, v, v_t):
        carry = (m_ref[g], l_ref[g], acc_ref[g])
        m, l, acc = _flash_tile(s, bias, v, carry, hpg, tq, False, v_t=v_t)
        m_ref[g] = m
        l_ref[g] = l
        acc_ref[g] = acc

    @pl.when(j >= n_steps)
    def _():
        bias, seen = _dsa_bias(key_ref[j - n_steps], thr_ref[...], need_ref[...], seen_ref[...], tri_ref[...])
        seen_ref[...] = seen
        for g in range(G):
            k_t = jnp.concatenate([r[0, g] for r in kvpage_refs], axis=-1).astype(BF16)
            v_t = jnp.concatenate([r[0, G + g] for r in kvpage_refs], axis=-1).astype(BF16)
            update(g, _dot(queries(g), k_t), bias, v_t, True)

    @pl.when(j == 2 * n_steps - 1)
    def _():
        new = newkv_ref[0]
        pad = jnp.zeros((LANE - tq, HEAD_DIM), F32)
        bias, _ = _dsa_bias(key_ref[n_steps][:, :LANE], thr_ref[...], need_ref[...], seen_ref[...],
                            tri_ref[0:LANE, 0:LANE])
        outs = []
        for g in range(G):
            k = jnp.concatenate([new[:, g * HEAD_DIM:(g + 1) * HEAD_DIM], pad], axis=0).astype(BF16)
            v = jnp.concatenate([new[:, (G + g) * HEAD_DIM:(G + g + 1) * HEAD_DIM], pad], axis=0).astype(BF16)
            update(g, _dot_nt(queries(g), k), bias, v, False)
            o = acc_ref[g] * (1.0 / jnp.maximum(l_ref[g], 1e-30))
            outs.extend(o[h * tq:(h + 1) * tq] for h in range(hpg))
        o_ref[0] = jnp.concatenate(outs, axis=-1)


def _dsa_decode(q, qi, kw, kidx_pool, kv_pool, page_table, new_kidx, new_kv, *, n_new):
    bk, tq, qw = q.shape
    n_pages = page_table.shape[1]
    pp = min(PAGES_PER_STEP, n_pages)
    assert n_pages % pp == 0 and n_new <= tq
    n_steps = n_pages // pp
    keys = pp * PAGE_SIZE
    lk = n_pages * PAGE_SIZE + n_new
    n_sel = min(DSA_TOPK_MAX, lk // 4)
    G = DSA_KV_HEADS
    rows = (DSA_HEADS // G) * tq
    tri = jnp.triu(jnp.ones((TRI_TILE, TRI_TILE), BF16))
    assert keys % TRI_TILE == 0 and keys >= n_sel

    def ipage_map(i):
        return lambda b, j, pt: (pt[b, jnp.minimum(j, n_steps - 1) * pp + i], 0, 0, 0)

    def kvpage_map(i):
        return lambda b, j, pt: (pt[b, jnp.maximum(j - n_steps, 0) * pp + i], 0, 0, 0)

    const3 = lambda b, j, pt: (b, 0, 0)
    in_specs = [pl.BlockSpec((1, tq, qw), const3),
                pl.BlockSpec((1, tq, qi.shape[-1]), const3),
                pl.BlockSpec((1, tq, LANE), const3),
                pl.BlockSpec((1, tq, IDX_DIM), const3),
                pl.BlockSpec((1, tq, new_kv.shape[-1]), const3),
                pl.BlockSpec((TRI_TILE, TRI_TILE), lambda b, j, pt: (0, 0))]
    in_specs += [pl.BlockSpec((1,) + kidx_pool.shape[1:], ipage_map(i)) for i in range(pp)]
    in_specs += [pl.BlockSpec((1,) + kv_pool.shape[1:], kvpage_map(i)) for i in range(pp)]
    kern = functools.partial(_dsa_decode_kernel, pp=pp, n_steps=n_steps, tq=tq, n_new=n_new, n_sel=n_sel)
    return pl.pallas_call(
        kern,
        grid_spec=pltpu.PrefetchScalarGridSpec(
            num_scalar_prefetch=1, grid=(bk, 2 * n_steps), in_specs=in_specs,
            out_specs=pl.BlockSpec((1, tq, qw), const3),
            scratch_shapes=[pltpu.VMEM((n_steps + 1, tq, keys), I32),
                            pltpu.VMEM((tq, 1), I32), pltpu.VMEM((tq, 1), F32), pltpu.VMEM((tq, 1), F32),
                            pltpu.VMEM((G, rows, 1), F32), pltpu.VMEM((G, rows, 1), F32),
                            pltpu.VMEM((G, rows, HEAD_DIM), F32)]),
        out_shape=jax.ShapeDtypeStruct((bk, tq, qw), F32),
        compiler_params=_cparams(("parallel", "arbitrary")),
        name="dsa_decode",
    )(page_table, q, qi, kw, new_kidx, new_kv, tri, *([kidx_pool] * pp), *([kv_pool] * pp))


L0_SIZES = (512, 768, 24, 512, 512, 512)
L1_SIZES = (512, 256, 256, 512, 512, 128, 128, 256, 64, 4, 512)


def _l0_weight(w_in):
    d = w_in.shape[0]
    q, kv6, gl, z_a, x_b, z_b = jnp.split(w_in, np.cumsum(L0_SIZES)[:-1].tolist(), axis=1)
    pad = jnp.zeros((d, LANE - 12), w_in.dtype)
    w = jnp.concatenate([q, kv6, z_a, x_b, z_b, gl[:, :12], pad, gl[:, 12:], pad], axis=1)
    segs = [(0, 512), (512, 1024), (1024, 1280), (1280, 1792), (1792, 2304), (2304, 2816), (2816, 3072)]
    return w.astype(BF16), segs


def _l1_weight(w_in):
    d = w_in.shape[0]
    qc, kc, vc, z_c, qd, kd, vd, qi, ki, wi, z_d = jnp.split(w_in, np.cumsum(L1_SIZES)[:-1].tolist(), axis=1)
    pad = jnp.zeros((d, LANE - IDX_DIM - IDX_HEADS), w_in.dtype)
    w = jnp.concatenate([qc, kc, vc, z_c, qd, kd, vd, qi, z_d, ki, wi, pad], axis=1)
    segs = [(0, 512), (512, 1024), (1024, 1536), (1536, 2048), (2048, 2304), (2304, 2560), (2560, 3072),
            (3072, 3200)]
    return w.astype(BF16), segs


def _pad_rows(x, n):
    return jnp.pad(x, ((0, 0), (0, n - x.shape[1]), (0, 0)))


def _cols_pool(pool):
    npool, ps = pool.shape[:2]
    cw = pool.shape[-1]
    perm = (0,) + tuple(range(2, pool.ndim)) + (1,)
    return jnp.transpose(pool, perm).reshape(npool, -1, cw, ps)


def _layer0(x, mod, past, w, *, tl, tq):
    (norm_g, w_in, cmp_wk, cmp_wv, conv_w, conv_b, lru_wr, lru_br, lru_wi, lru_bi, lru_lambda, w_out) = w
    shift, scale, gate = mod
    b, l, d = x.shape
    w_p, segs = _l0_weight(w_in)
    flat = shift.shape[1] != 1
    xin = x.reshape(1, b * l, d) if flat else x
    q, kvp, kvw, z_a, x_b, z_b, gates = _project(xin, norm_g, shift, scale, w_p, segs, 6, tl)
    if flat:
        q, kvp, kvw, z_a, x_b, z_b, gates = (t.reshape(b, l, -1) for t in (q, kvp, kvw, z_a, x_b, z_b, gates))
    wk2 = jnp.concatenate([cmp_wk, cmp_wk], axis=0)
    wv2 = jnp.concatenate([cmp_wv, cmp_wv], axis=0)
    w2 = jnp.stack([wk2, wk2, wv2, wv2], axis=0)
    nsa = dict(cw=HEAD_DIM, c_all=8, n_out=4, n_cmp=4, cmp_w=w2)
    lq = _round_up(l, 8)
    if past is None:
        ppool, ptable = _identity_pages(kvp)
        ksel, cmp = _gather_chunks(ppool, "rows", ptable, None, **nsa)
        wpool, wtable = _identity_pages(kvw)
        kwin = _gather_chunks(wpool, "rows", wtable, None, cw=HEAD_DIM, c_all=4, n_out=4)
        kv_win = kvw
        hist = jnp.zeros((b, CONV_WIDTH - 1, x_b.shape[-1]), F32)
        h0 = jnp.zeros((b, x_b.shape[-1]), F32)
        nsp = _round_up(cmp.shape[3], LANE)
        cmp = jnp.pad(cmp, ((0, 0), (0, 0), (0, 0), (0, nsp - cmp.shape[3]), (0, 0)))
        cmp = cmp.reshape(b, 2, 2, 2, nsp, HEAD_DIM).transpose(0, 2, 1, 3, 4, 5).reshape(b, 4, 2, nsp, HEAD_DIM)
        o_a = _nsa_attention(_pad_rows(q, lq), _pad_rows(gates, lq), cmp, ksel, kwin,
                             tq=min(tq, lq), q_pos0=0, lk=l, win_pos0=0)[:, :l]
    else:
        pool, table, win_buf, hist, h0 = past
        assert lq <= tq
        kv_win = jnp.concatenate([win_buf.reshape(b, win_buf.shape[1], -1), kvw], axis=1)
        lw_pad = _win_span(lq)
        assert kv_win.shape[1] <= lw_pad and lw_pad % PAGE_SIZE == 0
        wpool, wtable = _identity_pages(_pad_rows(kv_win, lw_pad))
        kwin = _gather_chunks(wpool, "rows", wtable, None, cw=HEAD_DIM, c_all=4, n_out=4)
        o_a = _nsa_decode(_pad_rows(q, lq), _pad_rows(gates, lq), _cols_pool(pool), table, _pad_rows(kvp, lq),
                          w2, kwin, n_new=l, win_pos0=table.shape[1] * PAGE_SIZE - win_buf.shape[1])[:, :l]
    o_b, h_last = _conv_rglru(_pad_rows(x_b, lq), hist, h0, conv_w, conv_b, lru_wr, lru_br, lru_wi, lru_bi,
                              lru_lambda, tl=min(256, lq), n_valid=l)
    o_b = o_b[:, :l]
    fl = (lambda t: t.reshape(1, b * l, -1)) if flat else (lambda t: t)
    x_new = _out_project(fl(o_a), fl(z_a), fl(o_b), fl(z_b), xin, gate, w_out.astype(BF16),
                         jnp.ones((d,), F32), tl=tl, final_norm=False).reshape(b, l, d)
    win_keep = min(NSA_WINDOW, kv_win.shape[1])
    conv_src = jnp.concatenate([hist, x_b], axis=1) if l < CONV_WIDTH - 1 else x_b
    states = (kvp.reshape(b, l, 4, NSA_KV_HEADS, HEAD_DIM),
              kv_win[:, -win_keep:].reshape(b, win_keep, 2, NSA_KV_HEADS, HEAD_DIM),
              conv_src[:, -(CONV_WIDTH - 1):], h_last)
    return x_new, states


def _layer1(x, mod, past, w, final_g, *, tl, tq):
    (norm_g, w_in, lam_q1, lam_k1, lam_q2, lam_k2, subln_g, w_out) = w
    shift, scale, gate = mod
    b, l, d = x.shape
    w_p, segs = _l1_weight(w_in)
    flat = shift.shape[1] != 1
    xin = x.reshape(1, b * l, d) if flat else x
    qc, kvc, z_c, qd, kvd, qi, z_d, kiw = _project(xin, norm_g, shift, scale, w_p, segs, -1, tl)
    if flat:
        qc, kvc, z_c, qd, kvd, qi, z_d, kiw = (t.reshape(b, l, -1) for t in (qc, kvc, z_c, qd, kvd, qi, z_d, kiw))
    lq = _round_up(l, 8)
    diff_a = dict(cw=2 * DIFF_HALF, c_all=4, n_out=4)
    dsa_a = dict(cw=HEAD_DIM, c_all=4, n_out=4)
    kidx_a = dict(cw=IDX_DIM, c_all=1, n_out=1)
    if past is None:
        dpool, dtable = _identity_pages(kvc)
        diff_kv = _gather_chunks(dpool, "rows", dtable, None, **diff_a)
        spool, stable = _identity_pages(kvd)
        dsa_kv = _gather_chunks(spool, "rows", stable, None, **dsa_a)
        ipool, itable = _identity_pages(kiw)
        kidx = _gather_chunks(ipool, "rows", itable, None, **kidx_a)
        q_pos0, lk = 0, l
    else:
        diff_pool, dsa_pool, kidx_pool, table = past
        past_len = table.shape[1] * PAGE_SIZE
        diff_rows = diff_pool.reshape(diff_pool.shape[0], PAGE_SIZE * 4, 2 * DIFF_HALF)
    lamv = jnp.stack([lam_q1, lam_k1, lam_q2, lam_k2], axis=0)
    if past is None:
        o_c = _diff_attention(_pad_rows(qc, lq), diff_kv, lamv, subln_g, tq=min(tq, lq), q_pos0=0)[:, :l]
        o_d = _dsa_attention(_pad_rows(qd, lq), _pad_rows(qi, lq), _pad_rows(kiw, lq), kidx, dsa_kv,
                             tq=min(tq, lq), q_pos0=0, lk=l)[:, :l]
    else:
        assert lq <= tq
        o_c = _diff_decode(_pad_rows(qc, lq), diff_rows, table, _pad_rows(kvc, lq), lamv, subln_g, n_new=l)[:, :l]
        o_d = _dsa_decode(_pad_rows(qd, lq), _pad_rows(qi, lq), _pad_rows(kiw, lq), _cols_pool(kidx_pool),
                          _cols_pool(dsa_pool), table, _pad_rows(kiw[:, :, :IDX_DIM], lq), _pad_rows(kvd, lq),
                          n_new=l)[:, :l]
    fl = (lambda t: t.reshape(1, b * l, -1)) if flat else (lambda t: t)
    y = _out_project(fl(o_c), fl(z_c), fl(o_d), fl(z_d), xin, gate, w_out.astype(BF16), final_g,
                     tl=tl, final_norm=True).reshape(b, l, d)
    states = (kvc.reshape(b, l, 2, DIFF_KV_HEADS, 2 * DIFF_HALF),
              kvd.reshape(b, l, 2, DSA_KV_HEADS, HEAD_DIM), kiw[:, :, :IDX_DIM])
    return y, states


def kernel(x_prompt, x_sample, cache_l0_nsa_kv, state_l0_win_kv, state_l0_conv, state_l0_lru_h,
           cache_l1_diff_kv, cache_l1_dsa_kv, cache_l1_dsa_kidx, page_table, c_prompt, c_sample,
           l0_norm_g, l0_ada_w, l0_ada_b, l0_w_in, l0_cmp_wk, l0_cmp_wv, l0_conv_w, l0_conv_b,
           l0_lru_wr, l0_lru_br, l0_lru_wi, l0_lru_bi, l0_lru_lambda, l0_w_out,
           l1_norm_g, l1_ada_w, l1_ada_b, l1_w_in, l1_lam_q1, l1_lam_k1, l1_lam_q2, l1_lam_k2,
           l1_subln_g, l1_w_out, final_norm_g):
    bp, lp, d = x_prompt.shape
    bs, ls, _ = x_sample.shape
    c_all = jnp.concatenate([c_prompt, c_sample], axis=0)

    def mods(ada_w, ada_b):
        m = _modulation(c_all, ada_w, ada_b)
        mp = tuple(t[:, None] for t in jnp.split(m[:bp], 3, axis=-1))
        ms = tuple(jnp.repeat(t, ls, axis=0)[None] for t in jnp.split(m[bp:], 3, axis=-1))
        return mp, ms

    tl_p = min(512, lp)
    tl_s = bs * ls
    tq = min(Q_TILE, lp)
    w0 = (l0_norm_g, l0_w_in, l0_cmp_wk, l0_cmp_wv, l0_conv_w, l0_conv_b, l0_lru_wr, l0_lru_br,
          l0_lru_wi, l0_lru_bi, l0_lru_lambda, l0_w_out)
    mp0, ms0 = mods(l0_ada_w, l0_ada_b)
    xp, (nsa_kv_p, win_p, conv_p, h_p) = _layer0(x_prompt, mp0, None, w0, tl=tl_p, tq=tq)
    xs, (nsa_kv_s, win_s, conv_s, h_s) = _layer0(
        x_sample, ms0, (cache_l0_nsa_kv, page_table, state_l0_win_kv, state_l0_conv, state_l0_lru_h), w0,
        tl=tl_s, tq=tq)
    w1 = (l1_norm_g, l1_w_in, l1_lam_q1, l1_lam_k1, l1_lam_q2, l1_lam_k2, l1_subln_g, l1_w_out)
    mp1, ms1 = mods(l1_ada_w, l1_ada_b)
    y_p, (diff_kv_p, dsa_kv_p, kidx_p) = _layer1(xp, mp1, None, w1, final_norm_g, tl=tl_p, tq=tq)
    y_s, (diff_kv_s, dsa_kv_s, kidx_s) = _layer1(
        xs, ms1, (cache_l1_diff_kv, cache_l1_dsa_kv, cache_l1_dsa_kidx, page_table), w1, final_norm_g,
        tl=tl_s, tq=tq)
    return (y_p, y_s, nsa_kv_p, nsa_kv_s, win_p, win_s, conv_p, conv_s, h_p, h_s,
            diff_kv_p, diff_kv_s, dsa_kv_p, dsa_kv_s, kidx_p, kidx_s)
```

```python
import functools
import math

import jax
import jax.numpy as jnp
import numpy as np
from jax import lax
from jax.experimental import pallas as pl
from jax.experimental.pallas import tpu as pltpu

F32 = jnp.float32
BF16 = jnp.bfloat16
I32 = jnp.int32

PAGE_SIZE = 128
HEAD_DIM = 64
NSA_HEADS = 8
NSA_KV_HEADS = 2
NSA_CMP_BLOCK = 32
NSA_SEL_BLOCK = 64
NSA_TOPN = 16
NSA_WINDOW = 512
FORCE_SCORE = 1e4
LRU_BLOCKS = 8
LRU_C = 8.0
CONV_WIDTH = 4
DIFF_HALF = 64
DIFF_HEADS = 4
DIFF_KV_HEADS = 2
DIFF_LAMBDA_INIT = 0.8 - 0.6 * math.exp(-0.3 * 1)
DSA_HEADS = 8
DSA_KV_HEADS = 2
IDX_HEADS = 4
IDX_DIM = 64
DSA_TOPK_MAX = 256
NORM_EPS = 1e-6
NEG = -1e30
REMOVED = -3e38
INT_MIN = -2 ** 31
LOG2E = math.log2(math.e)

LANE = 128
VMEM_LIMIT = 56 * 1024 * 1024
KV_TILE = 1024
TRI_TILE = 256
Q_TILE = 256
MAX_KV_TILE = 2048
PROBE_BINADES = 3
COUNT_ROWS = 64


def _kv_tile(tq, lk_pad):
    t = min(KV_TILE * max(1, Q_TILE // tq), MAX_KV_TILE, lk_pad)
    while lk_pad % t:
        t -= TRI_TILE
    return t


def _win_span(tq):
    return _round_up(NSA_WINDOW + tq, LANE)
PAGES_PER_STEP = 32


def _cparams(sem):
    return pltpu.CompilerParams(dimension_semantics=sem, vmem_limit_bytes=VMEM_LIMIT)


def _dot(a, b):
    return jnp.dot(a, b, preferred_element_type=F32)


def _dot_nt(a, b):
    return lax.dot_general(a, b, (((1,), (1,)), ((), ())), preferred_element_type=F32)


def _round_up(x, m):
    return (x + m - 1) // m * m


def _mod_kernel(c_ref, w_ref, b_ref, o_ref):
    o_ref[...] = jnp.dot(c_ref[...], w_ref[...], preferred_element_type=F32,
                         precision=lax.Precision.HIGHEST) + b_ref[...]


def _modulation(c, w, b):
    bc, d = c.shape
    n = w.shape[1]
    tn = 512
    return pl.pallas_call(
        _mod_kernel,
        grid=(n // tn,),
        in_specs=[pl.BlockSpec((bc, d), lambda j: (0, 0)),
                  pl.BlockSpec((d, tn), lambda j: (0, j)),
                  pl.BlockSpec((1, tn), lambda j: (0, j))],
        out_specs=pl.BlockSpec((bc, tn), lambda j: (0, j)),
        out_shape=jax.ShapeDtypeStruct((bc, n), F32),
        compiler_params=_cparams(("arbitrary",)),
        name="modulation",
    )(c, w, b.reshape(1, n))


def _proj_kernel(x_ref, g_ref, sh_ref, sc_ref, w_ref, *o_refs, segs, sigmoid_seg):
    x = x_ref[0]
    y = x * lax.rsqrt(jnp.mean(x * x, axis=-1, keepdims=True) + NORM_EPS)
    h = (y * g_ref[...]) * (1.0 + sc_ref[0]) + sh_ref[0]
    hb = h.astype(BF16)
    for i, ((a, b), o_ref) in enumerate(zip(segs, o_refs)):
        r = _dot(hb, w_ref[:, a:b])
        if i == sigmoid_seg:
            r = jax.nn.sigmoid(r)
        o_ref[0] = r


def _project(x, g, shift, scale, w, segs, sigmoid_seg, tl):
    b, l, d = x.shape
    ts = shift.shape[1]
    tm = 1 if ts == 1 else tl
    mod_map = (lambda bi, li: (bi, 0, 0)) if ts == 1 else (lambda bi, li: (bi, li, 0))
    p = w.shape[1]
    kern = functools.partial(_proj_kernel, segs=tuple(segs), sigmoid_seg=sigmoid_seg)
    return pl.pallas_call(
        kern,
        grid=(b, l // tl),
        in_specs=[pl.BlockSpec((1, tl, d), lambda bi, li: (bi, li, 0)),
                  pl.BlockSpec((1, d), lambda bi, li: (0, 0)),
                  pl.BlockSpec((1, tm, d), mod_map),
                  pl.BlockSpec((1, tm, d), mod_map),
                  pl.BlockSpec((d, p), lambda bi, li: (0, 0))],
        out_specs=[pl.BlockSpec((1, tl, e - a), lambda bi, li: (bi, li, 0)) for a, e in segs],
        out_shape=[jax.ShapeDtypeStruct((b, l, e - a), F32) for a, e in segs],
        compiler_params=_cparams(("parallel", "arbitrary")),
        name="norm_mod_project",
    )(x, g.reshape(1, d), shift, scale, w)


def _gather_kernel(pt_ref, *refs, pp, n_page_steps, layout, cw, c_all, n_cmp, n_out, has_new):
    page_refs = refs[:pp]
    pos = pp
    new_ref = None
    if has_new:
        new_ref = refs[pos]
        pos += 1
    w2_ref = None
    if n_cmp:
        w2_ref = refs[pos]
        pos += 1
    out_ref = refs[pos]
    cmp_ref = refs[pos + 1] if n_cmp else None
    j = pl.program_id(1)
    ones_col = jnp.where(lax.broadcasted_iota(I32, (PAGE_SIZE, LANE - cw), 1) == 0, 1.0, 0.0) if cw < LANE else None

    def chunk(i, c):
        if layout == "rows":
            return page_refs[i][0, :, c * cw:(c + 1) * cw]
        if layout == "cols":
            return page_refs[i][0, c].T
        return page_refs[i][0, pl.ds(c, PAGE_SIZE, stride=c_all), :]

    def emit(i, c, x):
        if c < n_cmp:
            nb = PAGE_SIZE // NSA_SEL_BLOCK
            prod = x.reshape(nb, NSA_SEL_BLOCK, cw) * w2_ref[c][None]
            cmp_ref[0, c, 0, i * nb:(i + 1) * nb, :] = jnp.sum(prod[:, :NSA_CMP_BLOCK], axis=1)
            cmp_ref[0, c, 1, i * nb:(i + 1) * nb, :] = jnp.sum(prod[:, NSA_CMP_BLOCK:], axis=1)
        else:
            if ones_col is not None:
                x = jnp.concatenate([x, ones_col], axis=-1)
            out_ref[0, c - n_cmp, i * PAGE_SIZE:(i + 1) * PAGE_SIZE, :] = x.astype(BF16)

    def pages():
        for i in range(pp):
            for c in range(n_cmp + n_out):
                emit(i, c, chunk(i, c))

    if has_new:
        pl.when(j < n_page_steps)(pages)

        @pl.when(j >= n_page_steps)
        def _():
            new = new_ref[0]
            for c in range(n_cmp + n_out):
                xc = new[:, c * cw:(c + 1) * cw]
                emit(0, c, jnp.concatenate([xc, jnp.zeros((PAGE_SIZE - xc.shape[0], cw), F32)], axis=0))
                for i in range(1, pp):
                    emit(i, c, jnp.zeros((PAGE_SIZE, cw), F32))
    else:
        pages()


def _gather_chunks(pool, layout, page_table, new, *, cw, c_all, n_out, n_cmp=0, cmp_w=None):
    bk, n_pages = page_table.shape
    pp = min(PAGES_PER_STEP, n_pages)
    assert n_pages % pp == 0
    n_page_steps = n_pages // pp
    has_new = new is not None
    n_steps = n_page_steps + (1 if has_new else 0)
    rows = pp * PAGE_SIZE
    lk_pad = n_steps * rows
    page_block = (1,) + pool.shape[1:]
    zeros = (0,) * (len(page_block) - 1)

    def page_map(i):
        def f(b, j, pt):
            return (pt[b, jnp.minimum(j * pp + i, n_pages - 1)],) + zeros
        return f

    in_specs = [pl.BlockSpec(page_block, page_map(i)) for i in range(pp)]
    args = [pool] * pp
    if has_new:
        in_specs.append(pl.BlockSpec((1,) + new.shape[1:], lambda b, j, pt: (b, 0, 0)))
        args.append(new)
    if n_cmp:
        in_specs.append(pl.BlockSpec(cmp_w.shape, lambda b, j, pt: (0, 0, 0)))
        args.append(cmp_w)
    out_specs = [pl.BlockSpec((1, n_out, rows, LANE), lambda b, j, pt: (b, 0, j, 0))]
    out_shape = [jax.ShapeDtypeStruct((bk, n_out, lk_pad, LANE), BF16)]
    if n_cmp:
        nb = rows // NSA_SEL_BLOCK
        out_specs.append(pl.BlockSpec((1, n_cmp, 2, nb, HEAD_DIM), lambda b, j, pt: (b, 0, 0, j, 0)))
        out_shape.append(jax.ShapeDtypeStruct((bk, n_cmp, 2, lk_pad // NSA_SEL_BLOCK, HEAD_DIM), F32))
    kern = functools.partial(_gather_kernel, pp=pp, n_page_steps=n_page_steps, layout=layout, cw=cw,
                             c_all=c_all, n_cmp=n_cmp, n_out=n_out, has_new=has_new)
    outs = pl.pallas_call(
        kern,
        grid_spec=pltpu.PrefetchScalarGridSpec(
            num_scalar_prefetch=1, grid=(bk, n_steps), in_specs=in_specs, out_specs=out_specs),
        out_shape=out_shape,
        compiler_params=_cparams(("parallel", "arbitrary")),
        name="gather_pages",
    )(page_table, *args)
    return outs if n_cmp else outs[0]


def _identity_pages(x):
    b, l, w = x.shape
    n_pages = l // PAGE_SIZE
    pool = x.reshape(b * n_pages, PAGE_SIZE, w)
    table = jnp.arange(b * n_pages, dtype=I32).reshape(b, n_pages)
    return pool, table


def _flash_tile(s_all, bias, v_tile, carry, n_heads, tq, l_in_acc, v_t=False):
    m, l, acc = carry
    s = s_all
    if bias is not None:
        s = jnp.concatenate([s_all[h * tq:(h + 1) * tq] + bias for h in range(n_heads)], axis=0)
    m_new = jnp.maximum(m, jnp.max(s, axis=-1, keepdims=True))
    p = jnp.exp2(s - m_new)
    alpha = jnp.exp2(m - m_new)
    acc = alpha * acc + (_dot_nt if v_t else _dot)(p.astype(BF16), v_tile)
    if not l_in_acc:
        l = alpha * l + jnp.sum(p, axis=-1, keepdims=True)
    return m_new, l, acc


def _flash_init(rows, dv):
    return (jnp.full((rows, 1), NEG, F32), jnp.zeros((rows, 1), F32), jnp.zeros((rows, dv), F32))


def _flash_out(acc):
    return acc[:, :HEAD_DIM] * (1.0 / jnp.maximum(acc[:, HEAD_DIM:HEAD_DIM + 1], 1e-30))


def _stack_heads(q, n, width):
    return jnp.concatenate([q[:, h * width:(h + 1) * width] for h in range(n)], axis=0)


def _pad_lanes(x):
    return jnp.concatenate([x, jnp.zeros((x.shape[0], LANE - x.shape[1]), x.dtype)], axis=-1)


def _nsa_prologue(qg, kw, vw, cmp4, *, q0, tq, nsp, n_top, win_pos0, start):
    hpg = NSA_HEADS // NSA_KV_HEADS
    rows = hpg * tq
    tqp = max(tq, LANE)
    qpos = q0 + lax.broadcasted_iota(I32, (tq, 1), 0)
    qpos_r = jnp.concatenate([qpos] * hpg, axis=0)
    blk = lax.broadcasted_iota(I32, (1, nsp), 1)
    blk_r = lax.broadcasted_iota(I32, (nsp, 1), 0)
    blk_rf = blk_r.astype(F32)
    cur_l = (q0 + lax.broadcasted_iota(I32, (1, tqp), 1)) // NSA_SEL_BLOCK
    vis_e = (blk * NSA_SEL_BLOCK + (NSA_CMP_BLOCK - 1)) <= qpos_r
    vis_o = (blk * NSA_SEL_BLOCK + (NSA_SEL_BLOCK - 1)) <= qpos_r
    kpos_w = win_pos0 + start + lax.broadcasted_iota(I32, (1, kw.shape[0]), 1)
    dlt = qpos - kpos_w
    bias_w = jnp.where(dlt >= 0, jnp.where(dlt < NSA_WINDOW, 0.0, NEG), NEG)

    qs64 = _stack_heads(qg, hpg, HEAD_DIM)
    qs = _pad_lanes(qs64).astype(BF16)
    qs64 = qs64.astype(BF16)

    _, _, acc_w = _flash_tile(_dot_nt(qs, kw), bias_w, vw, _flash_init(rows, LANE), hpg, tq, True)
    o_w = _flash_out(acc_w)

    kce, kco, vce, vco = (x.astype(BF16) for x in cmp4)
    s_e = jnp.where(vis_e, _dot_nt(qs64, kce), NEG)
    s_o = jnp.where(vis_o, _dot_nt(qs64, kco), NEG)
    m = jnp.maximum(jnp.max(s_e, axis=-1, keepdims=True), jnp.max(s_o, axis=-1, keepdims=True))
    p_e = jnp.where(vis_e, jnp.exp2(s_e - m), 0.0)
    p_o = jnp.where(vis_o, jnp.exp2(s_o - m), 0.0)
    den = jnp.sum(p_e, axis=-1, keepdims=True) + jnp.sum(p_o, axis=-1, keepdims=True)
    inv = 1.0 / jnp.maximum(den, 1e-30)
    p_e = p_e * inv
    p_o = p_o * inv
    o_c = _dot(p_e.astype(BF16), vce) + _dot(p_o.astype(BF16), vco)

    pe_h = sum(p_e[h * tq:(h + 1) * tq] for h in range(hpg))
    po_h = sum(p_o[h * tq:(h + 1) * tq] for h in range(hpg))
    imp = pe_h + po_h
    if tqp > tq:
        imp = jnp.concatenate([imp, jnp.zeros((tqp - tq, nsp), F32)], axis=0)
    imp = imp.T
    imp = jnp.where((blk_r == cur_l) | (blk_r == 0), FORCE_SCORE, imp)
    imp = jnp.where(blk_r <= cur_l, imp, NEG)
    sel = jnp.zeros((nsp, tqp), F32)
    for _ in range(n_top):
        mx = jnp.max(imp, axis=0, keepdims=True)
        first = jnp.min(jnp.where(imp == mx, blk_rf, float(nsp)), axis=0, keepdims=True)
        pick = blk_rf == first
        sel = jnp.where(pick & (mx > 0.5 * NEG), 1.0, sel)
        imp = jnp.where(pick, REMOVED, imp)
    return qs, qs64, o_c, o_w, sel.T[:tq].astype(BF16)


def _nsa_combine(gates, branches, tq):
    hpg = NSA_HEADS // NSA_KV_HEADS
    outs = []
    for g, (o_c, o_s, o_w) in enumerate(branches):
        for h in range(hpg):
            r = slice(h * tq, (h + 1) * tq)
            c = g * LANE + 3 * h
            outs.append(gates[:, c:c + 1] * o_c[r] + gates[:, c + 1:c + 2] * o_s[r]
                        + gates[:, c + 2:c + 3] * o_w[r])
    return jnp.concatenate(outs, axis=-1)


def _nsa_kernel(q_ref, g_ref, cmp_ref, ksel_ref, kwin_ref, o_ref, *,
                tq, tk, q_pos0, win_pos0, nsp, n_top):
    G = NSA_KV_HEADS
    hpg = NSA_HEADS // G
    gw = hpg * HEAD_DIM
    qi = pl.program_id(1)
    q0 = q_pos0 + qi * tq
    qpos = q0 + lax.broadcasted_iota(I32, (tq, 1), 0)
    rows = hpg * tq
    q_all = q_ref[0] * (HEAD_DIM ** -0.5 * LOG2E)
    start = pl.multiple_of(jnp.maximum(q0 - NSA_WINDOW - win_pos0, 0), 8)
    span = _win_span(tq)
    pro = [_nsa_prologue(q_all[:, g * gw:(g + 1) * gw],
                         kwin_ref[0, g, pl.ds(start, span), :], kwin_ref[0, G + g, pl.ds(start, span), :],
                         tuple(cmp_ref[0, 2 * g + kv, eo] for kv in range(2) for eo in range(2)),
                         q0=q0, tq=tq, nsp=nsp, n_top=n_top, win_pos0=win_pos0, start=start)
           for g in range(G)]

    blk_col = lax.broadcasted_iota(I32, (nsp, 1), 0)

    def sel_step(j, carry):
        k0 = pl.multiple_of(j * tk, tk)
        kpos = k0 + lax.broadcasted_iota(I32, (1, tk), 1)
        expand = jnp.where(blk_col == kpos // NSA_SEL_BLOCK, 1.0, 0.0).astype(BF16)
        out = []
        for g in range(G):
            picked = _dot(pro[g][4], expand)
            bias = jnp.where(kpos <= qpos, jnp.where(picked > 0.5, 0.0, NEG), NEG)
            s = _dot_nt(pro[g][0], ksel_ref[0, g, pl.ds(k0, tk), :])
            out.append(_flash_tile(s, bias, ksel_ref[0, G + g, pl.ds(k0, tk), :], carry[g], hpg, tq, True))
        return tuple(out)

    n_tiles = (q0 + tq - 1) // tk + 1
    res = lax.fori_loop(0, n_tiles, sel_step, tuple(_flash_init(rows, LANE) for _ in range(G)))
    o_ref[0] = _nsa_combine(g_ref[0], [(pro[g][2], _flash_out(res[g][2]), pro[g][3]) for g in range(G)], tq)


def _nsa_attention(q, gates, cmp, ksel, kwin, *, tq, q_pos0, lk, win_pos0):
    bk, lq, _ = q.shape
    G = NSA_KV_HEADS
    nsp = cmp.shape[3]
    lk_pad = ksel.shape[2]
    lw_pad = kwin.shape[2]
    tk = _kv_tile(tq, lk_pad)
    ns = -(-lk // NSA_SEL_BLOCK)
    assert lw_pad >= _win_span(tq) and q_pos0 + lq <= lk_pad
    kern = functools.partial(_nsa_kernel, tq=tq, tk=tk, q_pos0=q_pos0, win_pos0=win_pos0,
                             nsp=nsp, n_top=min(NSA_TOPN, ns))
    qw = NSA_HEADS * HEAD_DIM
    return pl.pallas_call(
        kern,
        grid=(bk, lq // tq),
        in_specs=[pl.BlockSpec((1, tq, qw), lambda b, i: (b, i, 0)),
                  pl.BlockSpec((1, tq, G * LANE), lambda b, i: (b, i, 0)),
                  pl.BlockSpec((1, 2 * G, 2, nsp, HEAD_DIM), lambda b, i: (b, 0, 0, 0, 0)),
                  pl.BlockSpec((1, 2 * G, lk_pad, LANE), lambda b, i: (b, 0, 0, 0)),
                  pl.BlockSpec((1, 2 * G, lw_pad, LANE), lambda b, i: (b, 0, 0, 0))],
        out_specs=pl.BlockSpec((1, tq, qw), lambda b, i: (b, i, 0)),
        out_shape=jax.ShapeDtypeStruct((bk, lq, qw), F32),
        compiler_params=_cparams(("parallel", "arbitrary")),
        name="nsa_attention",
    )(q, gates, cmp, ksel, kwin)


def _nsa_decode_kernel(pt_ref, q_ref, g_ref, new_ref, w2_ref, kwin_ref, *refs,
                       pp, n_steps, tq, n_new, past_len, win_pos0, nsp, n_top):
    page_refs = refs[:pp]
    o_ref, cmp_ref, oc_ref, ow_ref, sel_ref, m_ref, l_ref, acc_ref = refs[pp:]
    G = NSA_KV_HEADS
    hpg = NSA_HEADS // G
    gw = hpg * HEAD_DIM
    rows = hpg * tq
    j = pl.program_id(1)
    keys = pp * PAGE_SIZE
    nb = keys // NSA_SEL_BLOCK
    q_all = q_ref[0] * (HEAD_DIM ** -0.5 * LOG2E)

    def queries(g):
        qs64 = _stack_heads(q_all[:, g * gw:(g + 1) * gw], hpg, HEAD_DIM)
        return _pad_lanes(qs64).astype(BF16), qs64.astype(BF16)

    @pl.when(j == 0)
    def _():
        cmp_ref[...] = jnp.zeros(cmp_ref.shape, F32)

    @pl.when(j < n_steps)
    def _():
        r0 = pl.multiple_of(j * nb, nb)
        for c in range(2 * G):
            x = jnp.concatenate([r[0, c].T for r in page_refs], axis=0)
            prod = x.reshape(nb, NSA_SEL_BLOCK, HEAD_DIM) * w2_ref[c][None]
            cmp_ref[c, 0, pl.ds(r0, nb), :] = jnp.sum(prod[:, :NSA_CMP_BLOCK], axis=1)
            cmp_ref[c, 1, pl.ds(r0, nb), :] = jnp.sum(prod[:, NSA_CMP_BLOCK:], axis=1)

    @pl.when(j == n_steps - 1)
    def _():
        for g in range(G):
            _, _, o_c, o_w, sel = _nsa_prologue(
                q_all[:, g * gw:(g + 1) * gw], kwin_ref[0, g], kwin_ref[0, G + g],
                tuple(cmp_ref[g + G * kv, eo] for kv in range(2) for eo in range(2)),
                q0=past_len, tq=tq, nsp=nsp, n_top=n_top, win_pos0=win_pos0, start=0)
            oc_ref[g] = o_c
            ow_ref[g] = o_w
            sel_ref[g] = sel
        m_ref[...] = jnp.full(m_ref.shape, NEG, F32)
        l_ref[...] = jnp.zeros(l_ref.shape, F32)
        acc_ref[...] = jnp.zeros(acc_ref.shape, F32)

    def update(g, s, bias, v, v_t):
        carry = (m_ref[g], l_ref[g], acc_ref[g])
        m, l, acc = _flash_tile(s, bias, v, carry, hpg, tq, False, v_t=v_t)
        m_ref[g] = m
        l_ref[g] = l
        acc_ref[g] = acc

    @pl.when(j >= n_steps)
    def _():
        kpos = (j - n_steps) * keys + lax.broadcasted_iota(I32, (1, keys), 1)
        blk_col = lax.broadcasted_iota(I32, (nsp, 1), 0)
        expand = jnp.where(blk_col == kpos // NSA_SEL_BLOCK, 1.0, 0.0).astype(BF16)
        for g in range(G):
            _, qs64 = queries(g)
            k_t = jnp.concatenate([r[0, g] for r in page_refs], axis=-1).astype(BF16)
            v_t = jnp.concatenate([r[0, G + g] for r in page_refs], axis=-1).astype(BF16)
            bias = jnp.where(_dot(sel_ref[g], expand) > 0.5, 0.0, NEG)
            update(g, _dot(qs64, k_t), bias, v_t, True)

    @pl.when(j == 2 * n_steps - 1)
    def _():
        new = new_ref[0]
        pad = jnp.zeros((LANE - tq, HEAD_DIM), F32)
        row = lax.broadcasted_iota(I32, (tq, 1), 0)
        col = lax.broadcasted_iota(I32, (1, LANE), 1)
        own = past_len // NSA_SEL_BLOCK
        branches = []
        for g in range(G):
            _, qs64 = queries(g)
            k = jnp.concatenate([new[:, (2 * G + g) * HEAD_DIM:(2 * G + g + 1) * HEAD_DIM], pad], axis=0)
            v = jnp.concatenate([new[:, (3 * G + g) * HEAD_DIM:(3 * G + g + 1) * HEAD_DIM], pad], axis=0)
            picked = sel_ref[g][:, own:own + 1].astype(F32) > 0.5
            bias = jnp.where((col <= row) & (col < n_new) & picked, 0.0, NEG)
            update(g, _dot_nt(qs64, k.astype(BF16)), bias, v.astype(BF16), False)
            o_s = acc_ref[g] * (1.0 / jnp.maximum(l_ref[g], 1e-30))
            branches.append((oc_ref[g], o_s, ow_ref[g]))
        o_ref[0] = _nsa_combine(g_ref[0], branches, tq)


def _nsa_decode(q, gates, pool, page_table, new, w2, kwin, *, n_new, win_pos0):
    bk, tq, qw = q.shape
    n_pages = page_table.shape[1]
    pp = min(PAGES_PER_STEP, n_pages)
    past_len = n_pages * PAGE_SIZE
    assert n_pages % pp == 0 and past_len % NSA_SEL_BLOCK == 0 and n_new <= NSA_CMP_BLOCK
    assert kwin.shape[2] == _win_span(tq)
    n_steps = n_pages // pp
    G = NSA_KV_HEADS
    rows = (NSA_HEADS // G) * tq
    lk = past_len + n_new
    ns = -(-lk // NSA_SEL_BLOCK)
    nsp = _round_up(ns, LANE)

    def page_map(i):
        return lambda b, j, pt: (pt[b, (j % n_steps) * pp + i], j // n_steps, 0, 0)

    const3 = lambda b, j, pt: (b, 0, 0)
    in_specs = [pl.BlockSpec((1, tq, qw), const3),
                pl.BlockSpec((1, tq, G * LANE), const3),
                pl.BlockSpec((1, tq, new.shape[-1]), const3),
                pl.BlockSpec(w2.shape, lambda b, j, pt: (0, 0, 0)),
                pl.BlockSpec((1,) + kwin.shape[1:], lambda b, j, pt: (b, 0, 0, 0))]
    in_specs += [pl.BlockSpec((1, 2 * G, HEAD_DIM, PAGE_SIZE), page_map(i)) for i in range(pp)]
    kern = functools.partial(_nsa_decode_kernel, pp=pp, n_steps=n_steps, tq=tq, n_new=n_new, past_len=past_len,
                             win_pos0=win_pos0, nsp=nsp, n_top=min(NSA_TOPN, ns))
    return pl.pallas_call(
        kern,
        grid_spec=pltpu.PrefetchScalarGridSpec(
            num_scalar_prefetch=1, grid=(bk, 2 * n_steps), in_specs=in_specs,
            out_specs=pl.BlockSpec((1, tq, qw), const3),
            scratch_shapes=[pltpu.VMEM((2 * G, 2, nsp, HEAD_DIM), F32),
                            pltpu.VMEM((G, rows, HEAD_DIM), F32), pltpu.VMEM((G, rows, HEAD_DIM), F32),
                            pltpu.VMEM((G, tq, nsp), BF16),
                            pltpu.VMEM((G, rows, 1), F32), pltpu.VMEM((G, rows, 1), F32),
                            pltpu.VMEM((G, rows, HEAD_DIM), F32)]),
        out_shape=jax.ShapeDtypeStruct((bk, tq, qw), F32),
        compiler_params=_cparams(("parallel", "arbitrary")),
        name="nsa_decode",
    )(page_table, q, gates, new, w2, kwin, *([pool] * pp))


def _shift_rows(x, d, fill):
    rolled = pltpu.roll(x, d, axis=0)
    row = lax.broadcasted_iota(I32, x.shape, 0)
    return jnp.where(row >= d, rolled, fill)


def _lru_kernel(x_ref, hist_ref, h0_ref, cw_ref, cb_ref, wr_ref, br_ref, wi_ref, bi_ref, lam_ref,
                o_ref, hl_ref, tail_ref, h_ref, *, tl, last_row):
    li = pl.program_id(1)

    @pl.when(li == 0)
    def _():
        tail_ref[...] = jnp.concatenate(
            [jnp.zeros((8 - (CONV_WIDTH - 1), x_ref.shape[-1]), F32), hist_ref[0]], axis=0)
        h_ref[...] = h0_ref[0]

    x = x_ref[0]
    xp = jnp.concatenate([tail_ref[...], x], axis=0)
    cw = cw_ref[...]
    conv = sum(xp[8 - (CONV_WIDTH - 1) + j:8 - (CONV_WIDTH - 1) + j + tl] * cw[j:j + 1]
               for j in range(CONV_WIDTH))
    conv = cb_ref[...] + conv
    tail_ref[...] = x[tl - 8:tl]

    cb16 = conv.astype(BF16)
    r = jax.nn.sigmoid(_dot(cb16, wr_ref[...]) + br_ref[...])
    ig = jax.nn.sigmoid(_dot(cb16, wi_ref[...]) + bi_ref[...])
    log_a = -LRU_C * r * jax.nn.softplus(-lam_ref[...])
    a = jnp.exp(log_a)
    th = jnp.tanh(log_a)
    b = jnp.sqrt(-2.0 * th / (1.0 - th)) * (ig * conv)

    d = 1
    while d < tl:
        a_prev = _shift_rows(a, d, 1.0)
        b_prev = _shift_rows(b, d, 0.0)
        b = a * b_prev + b
        a = a * a_prev
        d *= 2
    h = a * h_ref[...] + b
    o_ref[0] = h
    h_ref[...] = h[tl - 1:tl]

    @pl.when(li == pl.num_programs(1) - 1)
    def _():
        hl_ref[0] = h[last_row:last_row + 1]


def _block_diag(w):
    nb, bw, _ = w.shape
    eye = jnp.eye(nb, dtype=w.dtype)
    return (eye[:, None, :, None] * w[:, :, None, :]).reshape(nb * bw, nb * bw)


def _conv_rglru(x_b, hist, h0, conv_w, conv_b, w_r, b_r, w_i, b_i, lam, *, tl, n_valid):
    b, l, w = x_b.shape
    assert tl >= 8 and l % tl == 0 and n_valid > l - tl
    kern = functools.partial(_lru_kernel, tl=tl, last_row=(n_valid - 1) % tl)
    vec = lambda: pl.BlockSpec((1, w), lambda bi, li: (0, 0))
    h, h_last = pl.pallas_call(
        kern,
        grid=(b, l // tl),
        in_specs=[pl.BlockSpec((1, tl, w), lambda bi, li: (bi, li, 0)),
                  pl.BlockSpec((1, CONV_WIDTH - 1, w), lambda bi, li: (bi, 0, 0)),
                  pl.BlockSpec((1, 1, w), lambda bi, li: (bi, 0, 0)),
                  pl.BlockSpec((CONV_WIDTH, w), lambda bi, li: (0, 0)),
                  vec(),
                  pl.BlockSpec((w, w), lambda bi, li: (0, 0)), vec(),
                  pl.BlockSpec((w, w), lambda bi, li: (0, 0)), vec(), vec()],
        out_specs=[pl.BlockSpec((1, tl, w), lambda bi, li: (bi, li, 0)),
                   pl.BlockSpec((1, 1, w), lambda bi, li: (bi, 0, 0))],
        out_shape=[jax.ShapeDtypeStruct((b, l, w), F32), jax.ShapeDtypeStruct((b, 1, w), F32)],
        scratch_shapes=[pltpu.VMEM((8, w), F32), pltpu.VMEM((1, w), F32)],
        compiler_params=_cparams(("parallel", "arbitrary")),
        name="conv_rglru",
    )(x_b, hist, h0.reshape(b, 1, w), conv_w, conv_b.reshape(1, w),
      _block_diag(w_r).astype(BF16), b_r.reshape(1, w), _block_diag(w_i).astype(BF16), b_i.reshape(1, w),
      lam.reshape(1, w))
    return h, h_last.reshape(b, w)


def _out_kernel(oa_ref, za_ref, ob_ref, zb_ref, x_ref, gate_ref, w_ref, fg_ref, o_ref, *, final_norm):
    half = oa_ref.shape[-1]
    ma = (oa_ref[0] * jax.nn.silu(za_ref[0])).astype(BF16)
    mb = (ob_ref[0] * jax.nn.silu(zb_ref[0])).astype(BF16)
    y = _dot(ma, w_ref[0:half, :]) + _dot(mb, w_ref[half:2 * half, :])
    out = x_ref[0] + gate_ref[0] * y
    if final_norm:
        out = out * lax.rsqrt(jnp.mean(out * out, axis=-1, keepdims=True) + NORM_EPS) * fg_ref[...]
    o_ref[0] = out


def _out_project(o_a, z_a, o_b, z_b, x, gate, w_out, final_g, *, tl, final_norm):
    b, l, d = x.shape
    half = o_a.shape[-1]
    ts = gate.shape[1]
    tm = 1 if ts == 1 else tl
    mod_map = (lambda bi, li: (bi, 0, 0)) if ts == 1 else (lambda bi, li: (bi, li, 0))
    act = lambda: pl.BlockSpec((1, tl, half), lambda bi, li: (bi, li, 0))
    return pl.pallas_call(
        functools.partial(_out_kernel, final_norm=final_norm),
        grid=(b, l // tl),
        in_specs=[act(), act(), act(), act(),
                  pl.BlockSpec((1, tl, d), lambda bi, li: (bi, li, 0)),
                  pl.BlockSpec((1, tm, d), mod_map),
                  pl.BlockSpec((2 * half, d), lambda bi, li: (0, 0)),
                  pl.BlockSpec((1, d), lambda bi, li: (0, 0))],
        out_specs=pl.BlockSpec((1, tl, d), lambda bi, li: (bi, li, 0)),
        out_shape=jax.ShapeDtypeStruct((b, l, d), F32),
        compiler_params=_cparams(("parallel", "arbitrary")),
        name="out_project",
    )(o_a, z_a, o_b, z_b, x, gate, w_out, final_g.reshape(1, d))


def _diff_queries(q, tq):
    hpg = DIFF_HEADS // DIFF_KV_HEADS
    q = q * (DIFF_HALF ** -0.5 * LOG2E)
    zero = jnp.zeros((tq, DIFF_HALF), F32)
    parts = []
    for mp in range(2):
        for h in range(hpg):
            qh = q[:, (2 * h + mp) * DIFF_HALF:(2 * h + mp + 1) * DIFF_HALF]
            parts.append(jnp.concatenate([qh, zero] if mp == 0 else [zero, qh], axis=-1))
    return jnp.concatenate(parts, axis=0).astype(BF16)


def _diff_finish(l, acc, lamv, subg, tq):
    hpg = DIFF_HEADS // DIFF_KV_HEADS
    o = acc * (1.0 / jnp.maximum(l, 1e-30))
    lam = (jnp.exp(jnp.sum(lamv[0:1] * lamv[1:2], axis=-1, keepdims=True))
           - jnp.exp(jnp.sum(lamv[2:3] * lamv[3:4], axis=-1, keepdims=True)) + DIFF_LAMBDA_INIT)
    half = hpg * tq
    od = o[0:half] - lam * o[half:2 * half]
    od = od * lax.rsqrt(jnp.mean(od * od, axis=-1, keepdims=True) + NORM_EPS)
    od = od * subg * (1.0 - DIFF_LAMBDA_INIT)
    return jnp.concatenate([od[h * tq:(h + 1) * tq] for h in range(hpg)], axis=-1)


def _diff_kernel(q_ref, k_ref, v_ref, lamv_ref, subg_ref, o_ref, *, tq, tk, q_pos0):
    hpg = DIFF_HEADS // DIFF_KV_HEADS
    qi = pl.program_id(2)
    q0 = q_pos0 + qi * tq
    n_maps = 2 * hpg
    rows = n_maps * tq
    qpos = q0 + lax.broadcasted_iota(I32, (tq, 1), 0)
    qs = _diff_queries(q_ref[0], tq)

    def step(masked):
        def f(j, carry):
            k0 = pl.multiple_of(j * tk, tk)
            s = _dot_nt(qs, k_ref[0, 0, pl.ds(k0, tk), :])
            bias = None
            if masked:
                kpos = k0 + lax.broadcasted_iota(I32, (1, tk), 1)
                bias = jnp.where(kpos <= qpos, 0.0, NEG)
            return _flash_tile(s, bias, v_ref[0, 0, pl.ds(k0, tk), :], carry, n_maps, tq, False)
        return f

    n_full = (q0 + 1) // tk
    n_tiles = (q0 + tq - 1) // tk + 1
    carry = lax.fori_loop(0, n_full, step(False), _flash_init(rows, 2 * DIFF_HALF))
    m, l, acc = lax.fori_loop(n_full, n_tiles, step(True), carry)
    o_ref[0] = _diff_finish(l, acc, lamv_ref[...], subg_ref[...], tq)


def _diff_decode_kernel(pt_ref, q_ref, new_ref, lamv_ref, subg_ref, *refs, pp, tq, n_new):
    page_refs = refs[:pp]
    o_ref, m_ref, l_ref, acc_ref = refs[pp:]
    G = DIFF_KV_HEADS
    n_maps = 2 * (DIFF_HEADS // G)
    gw = n_maps * DIFF_HALF
    j = pl.program_id(1)

    @pl.when(j == 0)
    def _():
        m_ref[...] = jnp.full(m_ref.shape, NEG, F32)
        l_ref[...] = jnp.zeros(l_ref.shape, F32)
        acc_ref[...] = jnp.zeros(acc_ref.shape, F32)

    def update(g, qs, k, v, bias):
        carry = (m_ref[g], l_ref[g], acc_ref[g])
        m, l, acc = _flash_tile(_dot_nt(qs, k), bias, v, carry, n_maps, tq, False)
        m_ref[g] = m
        l_ref[g] = l
        acc_ref[g] = acc

    qs = [_diff_queries(q_ref[0][:, g * gw:(g + 1) * gw], tq) for g in range(G)]
    for g in range(G):
        k = jnp.concatenate([r[0, pl.ds(g, PAGE_SIZE, stride=2 * G), :] for r in page_refs], axis=0)
        v = jnp.concatenate([r[0, pl.ds(G + g, PAGE_SIZE, stride=2 * G), :] for r in page_refs], axis=0)
        update(g, qs[g], k.astype(BF16), v.astype(BF16), None)

    @pl.when(j == pl.num_programs(1) - 1)
    def _():
        new = new_ref[0]
        pad = jnp.zeros((LANE - tq, 2 * DIFF_HALF), F32)
        row = lax.broadcasted_iota(I32, (tq, 1), 0)
        col = lax.broadcasted_iota(I32, (1, LANE), 1)
        bias = jnp.where((col <= row) & (col < n_new), 0.0, NEG)
        outs = []
        for g in range(G):
            k = jnp.concatenate([new[:, g * 2 * DIFF_HALF:(g + 1) * 2 * DIFF_HALF], pad], axis=0)
            v = jnp.concatenate([new[:, (G + g) * 2 * DIFF_HALF:(G + g + 1) * 2 * DIFF_HALF], pad], axis=0)
            update(g, qs[g], k.astype(BF16), v.astype(BF16), bias)
            outs.append(_diff_finish(l_ref[g], acc_ref[g], lamv_ref[...], subg_ref[...], tq))
        o_ref[0] = jnp.concatenate(outs, axis=-1)


def _diff_decode(q, pool, page_table, new, lamv, subln_g, *, n_new):
    bk, tq, qw = q.shape
    n_pages = page_table.shape[1]
    pp = min(PAGES_PER_STEP, n_pages)
    assert n_pages % pp == 0
    G = DIFF_KV_HEADS
    rows = 2 * (DIFF_HEADS // G) * tq

    def page_map(i):
        return lambda b, j, pt: (pt[b, j * pp + i], 0, 0)

    in_specs = [pl.BlockSpec((1, tq, qw), lambda b, j, pt: (b, 0, 0)),
                pl.BlockSpec((1, tq, new.shape[-1]), lambda b, j, pt: (b, 0, 0)),
                pl.BlockSpec((4, DIFF_HALF), lambda b, j, pt: (0, 0)),
                pl.BlockSpec((1, 2 * DIFF_HALF), lambda b, j, pt: (0, 0))]
    in_specs += [pl.BlockSpec((1,) + pool.shape[1:], page_map(i)) for i in range(pp)]
    return pl.pallas_call(
        functools.partial(_diff_decode_kernel, pp=pp, tq=tq, n_new=n_new),
        grid_spec=pltpu.PrefetchScalarGridSpec(
            num_scalar_prefetch=1, grid=(bk, n_pages // pp), in_specs=in_specs,
            out_specs=pl.BlockSpec((1, tq, qw), lambda b, j, pt: (b, 0, 0)),
            scratch_shapes=[pltpu.VMEM((G, rows, 1), F32), pltpu.VMEM((G, rows, 1), F32),
                            pltpu.VMEM((G, rows, 2 * DIFF_HALF), F32)]),
        out_shape=jax.ShapeDtypeStruct((bk, tq, qw), F32),
        compiler_params=_cparams(("parallel", "arbitrary")),
        name="diff_decode",
    )(page_table, q, new, lamv, subln_g.reshape(1, 2 * DIFF_HALF), *([pool] * pp))


def _diff_attention(q, kv, lamv, subln_g, *, tq, q_pos0):
    bk, lq, _ = q.shape
    G = DIFF_KV_HEADS
    lk_pad = kv.shape[2]
    tk = _kv_tile(tq, lk_pad)
    gw = (DIFF_HEADS // G) * 2 * DIFF_HALF
    assert q_pos0 + lq <= lk_pad
    return pl.pallas_call(
        functools.partial(_diff_kernel, tq=tq, tk=tk, q_pos0=q_pos0),
        grid=(bk, G, lq // tq),
        in_specs=[pl.BlockSpec((1, tq, gw), lambda b, g, i: (b, i, g)),
                  pl.BlockSpec((1, 1, lk_pad, 2 * DIFF_HALF), lambda b, g, i: (b, g, 0, 0)),
                  pl.BlockSpec((1, 1, lk_pad, 2 * DIFF_HALF), lambda b, g, i: (b, G + g, 0, 0)),
                  pl.BlockSpec((4, DIFF_HALF), lambda b, g, i: (0, 0)),
                  pl.BlockSpec((1, 2 * DIFF_HALF), lambda b, g, i: (0, 0))],
        out_specs=pl.BlockSpec((1, tq, gw), lambda b, g, i: (b, i, g)),
        out_shape=jax.ShapeDtypeStruct((bk, lq, DIFF_HEADS * 2 * DIFF_HALF), F32),
        compiler_params=_cparams(("parallel", "parallel", "arbitrary")),
        name="diff_attention",
    )(q, kv, kv, lamv, subln_g.reshape(1, 2 * DIFF_HALF))


def _dsa_keys(s_all, wi, causal, tq):
    score = jnp.zeros((tq, s_all.shape[1]), F32)
    for h in range(IDX_HEADS):
        score = score + wi[:, h:h + 1] * jnp.maximum(s_all[h * tq:(h + 1) * tq], 0.0)
    bits = pltpu.bitcast(score, I32)
    key = jnp.where(bits < 0, bits ^ 0x7FFFFFFF, bits)
    key = jnp.where(score == 0.0, 0, key)
    return jnp.where(causal, jnp.where(score > 0.5 * NEG, key, INT_MIN), INT_MIN)


def _dsa_threshold(read, n_tiles, kmax, *, tq, tk, n_sel, transposed, unroll):
    lanes = tk // LANE

    def count(*bounds):
        def f(j, accs):
            keys = read(j)
            out = []
            for bound, acc in zip(bounds, accs):
                hit = jnp.where(keys >= bound, 1.0, 0.0)
                if transposed:
                    acc = acc + jnp.sum(hit.reshape(tk // COUNT_ROWS, COUNT_ROWS, tq), axis=0)
                else:
                    for c in range(lanes):
                        acc = acc + hit[:, c * LANE:(c + 1) * LANE]
                out.append(acc)
            return tuple(out)
        acc0 = jnp.zeros((COUNT_ROWS, tq) if transposed else (tq, LANE), F32)
        accs = lax.fori_loop(0, n_tiles, f, tuple(acc0 for _ in bounds), unroll=unroll)
        return [jnp.sum(acc, axis=0 if transposed else -1, keepdims=True) for acc in accs]

    k_f = float(n_sel)
    probe = jnp.maximum(kmax - (PROBE_BINADES << 23), 1)
    c_adm, c_nn, c_pos, c_probe = count(INT_MIN + 1, 0, 1, probe)
    few, pos, zero, high = c_adm < k_f, c_pos >= k_f, c_nn >= k_f, c_probe >= k_f
    lo0 = jnp.where(pos, jnp.where(high, probe, 1), jnp.where(zero, 0, INT_MIN))
    hi0 = jnp.where(pos, jnp.where(high, kmax, probe - 1), jnp.where(zero, 0, jnp.where(few, INT_MIN, -1)))

    def unfinished(lo_hi):
        lo, hi = lo_hi
        return jnp.max(jnp.where(lo < hi, 1.0, 0.0)) > 0.0

    def bisect(lo_hi):
        lo, hi = lo_hi
        mid = (lo >> 1) + (hi >> 1) + ((lo | hi) & 1)
        cnt, = count(mid)
        lo = jnp.where(cnt >= k_f, mid, lo)
        hi = jnp.where(cnt > k_f, hi, jnp.where(cnt == k_f, mid, mid - 1))
        return lo, hi

    thr, _ = lax.while_loop(unfinished, lambda s: bisect(bisect(s)), (lo0, hi0))
    thr = jnp.maximum(thr, INT_MIN + 1)
    n_gt, = count(thr + 1)
    return thr, k_f - n_gt


def _dsa_bias(key, thr, need, seen, tri):
    tied = key == thr
    tied_b = jnp.where(tied, 1.0, 0.0).astype(BF16)
    t = tri.shape[0]
    ranks = []
    for c in range(key.shape[1] // t):
        r = _dot(tied_b[:, c * t:(c + 1) * t], tri) + seen
        ranks.append(r)
        seen = r[:, t - 1:t]
    rank = jnp.concatenate(ranks, axis=-1)
    return jnp.where(key > thr, 0.0, jnp.where(tied, jnp.where(rank <= need, 0.0, NEG), NEG)), seen


def _dsa_kernel(q_ref, qi_ref, kw_ref, kidx_ref, kv_ref, tri_ref, o_ref, key_ref, keyt_ref, *,
                tq, tk, q_pos0, n_sel, one_block):
    G = DSA_KV_HEADS
    hpg = DSA_HEADS // G
    transposed = tq % LANE == 0
    q0 = q_pos0 + (0 if one_block else pl.program_id(1) * tq)
    qpos = q0 + lax.broadcasted_iota(I32, (tq, 1), 0)
    n_tiles = (q0 + tq - 1) // tk + 1
    lanes = tk // LANE

    qidx = _pad_lanes(_stack_heads(qi_ref[0] * (IDX_DIM ** -0.5), IDX_HEADS, IDX_DIM)).astype(BF16)
    wi = kw_ref[0][:, IDX_DIM:IDX_DIM + IDX_HEADS] * (IDX_HEADS ** -0.5)

    def score_step(j, kmax):
        k0 = pl.multiple_of(j * tk, tk)
        s_all = _dot_nt(qidx, kidx_ref[0, 0, pl.ds(k0, tk), :])
        kpos = k0 + lax.broadcasted_iota(I32, (1, tk), 1)
        key = _dsa_keys(s_all, wi, kpos <= qpos, tq)
        key_ref[j] = key
        if not transposed:
            return jnp.maximum(kmax, jnp.max(key, axis=-1, keepdims=True))
        key_t = key.T
        keyt_ref[j] = key_t
        return jnp.maximum(kmax, jnp.max(key_t, axis=0, keepdims=True))

    kmax = lax.fori_loop(0, n_tiles, score_step, jnp.full((1, tq) if transposed else (tq, 1), INT_MIN, I32))
    thr, need = _dsa_threshold((lambda j: keyt_ref[j]) if transposed else (lambda j: key_ref[j]), n_tiles, kmax,
                               tq=tq, tk=tk, n_sel=n_sel, transposed=transposed, unroll=one_block)

    def along_rows(v):
        rep = jnp.broadcast_to(v, (LANE, tq)).T
        return jnp.concatenate([rep] * lanes, axis=-1)

    if transposed:
        thr = along_rows(thr)
        need = along_rows(need)

    q = q_ref[0] * (HEAD_DIM ** -0.5 * LOG2E)
    qs = [_pad_lanes(_stack_heads(q[:, g * hpg * HEAD_DIM:(g + 1) * hpg * HEAD_DIM], hpg, HEAD_DIM)).astype(BF16)
          for g in range(G)]
    rows = hpg * tq

    def att_step(j, carry):
        seen, flash = carry
        k0 = pl.multiple_of(j * tk, tk)
        bias, seen = _dsa_bias(key_ref[j], thr, need, seen, tri_ref[...])
        out = []
        for g in range(G):
            s = _dot_nt(qs[g], kv_ref[0, g, pl.ds(k0, tk), :])
            out.append(_flash_tile(s, bias, kv_ref[0, G + g, pl.ds(k0, tk), :], flash[g], hpg, tq, True))
        return seen, tuple(out)

    _, res = lax.fori_loop(0, n_tiles, att_step,
                           (jnp.zeros((tq, 1), F32), tuple(_flash_init(rows, LANE) for _ in range(G))))
    outs = []
    for g in range(G):
        o = _flash_out(res[g][2])
        outs.extend(o[h * tq:(h + 1) * tq] for h in range(hpg))
    o_ref[0] = jnp.concatenate(outs, axis=-1)


def _dsa_attention(q, qi, kw, kidx, kv, *, tq, q_pos0, lk):
    bk, lq, _ = q.shape
    lk_pad = kv.shape[2]
    tk = _kv_tile(tq, lk_pad)
    n_sel = min(DSA_TOPK_MAX, lk // 4)
    assert q_pos0 + lq <= lk_pad and tk >= n_sel
    kern = functools.partial(_dsa_kernel, tq=tq, tk=tk, q_pos0=q_pos0, n_sel=n_sel, one_block=lq == tq)
    assert tk % TRI_TILE == 0
    tri = jnp.triu(jnp.ones((TRI_TILE, TRI_TILE), BF16))
    return pl.pallas_call(
        kern,
        grid=(bk, lq // tq),
        in_specs=[pl.BlockSpec((1, tq, DSA_HEADS * HEAD_DIM), lambda b, i: (b, i, 0)),
                  pl.BlockSpec((1, tq, IDX_HEADS * IDX_DIM), lambda b, i: (b, i, 0)),
                  pl.BlockSpec((1, tq, LANE), lambda b, i: (b, i, 0)),
                  pl.BlockSpec((1, 1, lk_pad, LANE), lambda b, i: (b, 0, 0, 0)),
                  pl.BlockSpec((1, 4, lk_pad, LANE), lambda b, i: (b, 0, 0, 0)),
                  pl.BlockSpec((TRI_TILE, TRI_TILE), lambda b, i: (0, 0))],
        out_specs=pl.BlockSpec((1, tq, DSA_HEADS * HEAD_DIM), lambda b, i: (b, i, 0)),
        out_shape=jax.ShapeDtypeStruct((bk, lq, DSA_HEADS * HEAD_DIM), F32),
        scratch_shapes=[pltpu.VMEM((lk_pad // tk, tq, tk), I32),
                        pltpu.VMEM((lk_pad // tk, tk, tq) if tq % LANE == 0 else (8, LANE), I32)],
        compiler_params=_cparams(("parallel", "arbitrary")),
        name="dsa_attention",
    )(q, qi, kw, kidx, kv, tri)


def _dsa_decode_kernel(pt_ref, q_ref, qi_ref, kw_ref, newi_ref, newkv_ref, tri_ref, *refs,
                       pp, n_steps, tq, n_new, n_sel):
    ipage_refs = refs[:pp]
    kvpage_refs = refs[pp:2 * pp]
    o_ref, key_ref, thr_ref, need_ref, seen_ref, m_ref, l_ref, acc_ref = refs[2 * pp:]
    G = DSA_KV_HEADS
    hpg = DSA_HEADS // G
    gw = hpg * HEAD_DIM
    j = pl.program_id(1)
    keys = pp * PAGE_SIZE
    qidx = _stack_heads(qi_ref[0] * (IDX_DIM ** -0.5), IDX_HEADS, IDX_DIM).astype(BF16)
    wi = kw_ref[0][:, IDX_DIM:IDX_DIM + IDX_HEADS] * (IDX_HEADS ** -0.5)
    q_all = q_ref[0] * (HEAD_DIM ** -0.5 * LOG2E)
    row = lax.broadcasted_iota(I32, (tq, 1), 0)
    col = lax.broadcasted_iota(I32, (1, LANE), 1)
    own = (col <= row) & (col < n_new)

    def queries(g):
        return _stack_heads(q_all[:, g * gw:(g + 1) * gw], hpg, HEAD_DIM).astype(BF16)

    @pl.when(j < n_steps)
    def _():
        k_t = jnp.concatenate([r[0, 0] for r in ipage_refs], axis=-1).astype(BF16)
        key_ref[j] = _dsa_keys(_dot(qidx, k_t), wi, jnp.full((tq, keys), True), tq)

    @pl.when(j == n_steps - 1)
    def _():
        pad = jnp.zeros((LANE - tq, IDX_DIM), F32)
        k_new = jnp.concatenate([newi_ref[0], pad], axis=0).astype(BF16)
        key_new = _dsa_keys(_dot_nt(qidx, k_new), wi, own, tq)
        key_ref[n_steps] = jnp.concatenate([key_new, jnp.full((tq, keys - LANE), INT_MIN, I32)], axis=-1)
        kmax = jnp.full((tq, 1), INT_MIN, I32)
        for t in range(n_steps + 1):
            kmax = jnp.maximum(kmax, jnp.max(key_ref[t], axis=-1, keepdims=True))
        thr, need = _dsa_threshold(lambda t: key_ref[t], n_steps + 1, kmax, tq=tq, tk=keys, n_sel=n_sel,
                                   transposed=False, unroll=True)
        thr_ref[...] = thr
        need_ref[...] = need
        seen_ref[...] = jnp.zeros(seen_ref.shape, F32)
        m_ref[...] = jnp.full(m_ref.shape, NEG, F32)
        l_ref[...] = jnp.zeros(l_ref.shape, F32)
        acc_ref[...] = jnp.zeros(acc_ref.shape, F32)

    def update(g, s, bias, v, v_t):
        carry = (m_ref[g], l_ref[g], acc_ref[g])
        m, l, acc = _flash_tile(s, bias, v, carry, hpg, tq, False, v_t=v_t)
        m_ref[g] = m
        l_ref[g] = l
        acc_ref[g] = acc

    @pl.when(j >= n_steps)
    def _():
        bias, seen = _dsa_bias(key_ref[j - n_steps], thr_ref[...], need_ref[...], seen_ref[...], tri_ref[...])
        seen_ref[...] = seen
        for g in range(G):
            k_t = jnp.concatenate([r[0, g] for r in kvpage_refs], axis=-1).astype(BF16)
            v_t = jnp.concatenate([r[0, G + g] for r in kvpage_refs], axis=-1).astype(BF16)
            update(g, _dot(queries(g), k_t), bias, v_t, True)

    @pl.when(j == 2 * n_steps - 1)
    def _():
        new = newkv_ref[0]
        pad = jnp.zeros((LANE - tq, HEAD_DIM), F32)
        bias, _ = _dsa_bias(key_ref[n_steps][:, :LANE], thr_ref[...], need_ref[...], seen_ref[...],
                            tri_ref[0:LANE, 0:LANE])
        outs = []
        for g in range(G):
            k = jnp.concatenate([new[:, g * HEAD_DIM:(g + 1) * HEAD_DIM], pad], axis=0).astype(BF16)
            v = jnp.concatenate([new[:, (G + g) * HEAD_DIM:(G + g + 1) * HEAD_DIM], pad], axis=0).astype(BF16)
            update(g, _dot_nt(queries(g), k), bias, v, False)
            o = acc_ref[g] * (1.0 / jnp.maximum(l_ref[g], 1e-30))
            outs.extend(o[h * tq:(h + 1) * tq] for h in range(hpg))
        o_ref[0] = jnp.concatenate(outs, axis=-1)


def _dsa_decode(q, qi, kw, kidx_pool, kv_pool, page_table, new_kidx, new_kv, *, n_new):
    bk, tq, qw = q.shape
    n_pages = page_table.shape[1]
    pp = min(PAGES_PER_STEP, n_pages)
    assert n_pages % pp == 0 and n_new <= tq
    n_steps = n_pages // pp
    keys = pp * PAGE_SIZE
    lk = n_pages * PAGE_SIZE + n_new
    n_sel = min(DSA_TOPK_MAX, lk // 4)
    G = DSA_KV_HEADS
    rows = (DSA_HEADS // G) * tq
    tri = jnp.triu(jnp.ones((TRI_TILE, TRI_TILE), BF16))
    assert keys % TRI_TILE == 0 and keys >= n_sel

    def ipage_map(i):
        return lambda b, j, pt: (pt[b, jnp.minimum(j, n_steps - 1) * pp + i], 0, 0, 0)

    def kvpage_map(i):
        return lambda b, j, pt: (pt[b, jnp.maximum(j - n_steps, 0) * pp + i], 0, 0, 0)

    const3 = lambda b, j, pt: (b, 0, 0)
    in_specs = [pl.BlockSpec((1, tq, qw), const3),
                pl.BlockSpec((1, tq, qi.shape[-1]), const3),
                pl.BlockSpec((1, tq, LANE), const3),
                pl.BlockSpec((1, tq, IDX_DIM), const3),
                pl.BlockSpec((1, tq, new_kv.shape[-1]), const3),
                pl.BlockSpec((TRI_TILE, TRI_TILE), lambda b, j, pt: (0, 0))]
    in_specs += [pl.BlockSpec((1,) + kidx_pool.shape[1:], ipage_map(i)) for i in range(pp)]
    in_specs += [pl.BlockSpec((1,) + kv_pool.shape[1:], kvpage_map(i)) for i in range(pp)]
    kern = functools.partial(_dsa_decode_kernel, pp=pp, n_steps=n_steps, tq=tq, n_new=n_new, n_sel=n_sel)
    return pl.pallas_call(
        kern,
        grid_spec=pltpu.PrefetchScalarGridSpec(
            num_scalar_prefetch=1, grid=(bk, 2 * n_steps), in_specs=in_specs,
            out_specs=pl.BlockSpec((1, tq, qw), const3),
            scratch_shapes=[pltpu.VMEM((n_steps + 1, tq, keys), I32),
                            pltpu.VMEM((tq, 1), I32), pltpu.VMEM((tq, 1), F32), pltpu.VMEM((tq, 1), F32),
                            pltpu.VMEM((G, rows, 1), F32), pltpu.VMEM((G, rows, 1), F32),
                            pltpu.VMEM((G, rows, HEAD_DIM), F32)]),
        out_shape=jax.ShapeDtypeStruct((bk, tq, qw), F32),
        compiler_params=_cparams(("parallel", "arbitrary")),
        name="dsa_decode",
    )(page_table, q, qi, kw, new_kidx, new_kv, tri, *([kidx_pool] * pp), *([kv_pool] * pp))


L0_SIZES = (512, 768, 24, 512, 512, 512)
L1_SIZES = (512, 256, 256, 512, 512, 128, 128, 256, 64, 4, 512)


def _l0_weight(w_in):
    d = w_in.shape[0]
    q, kv6, gl, z_a, x_b, z_b = jnp.split(w_in, np.cumsum(L0_SIZES)[:-1].tolist(), axis=1)
    pad = jnp.zeros((d, LANE - 12), w_in.dtype)
    w = jnp.concatenate([q, kv6, z_a, x_b, z_b, gl[:, :12], pad, gl[:, 12:], pad], axis=1)
    segs = [(0, 512), (512, 1024), (1024, 1280), (1280, 1792), (1792, 2304), (2304, 2816), (2816, 3072)]
    return w.astype(BF16), segs


def _l1_weight(w_in):
    d = w_in.shape[0]
    qc, kc, vc, z_c, qd, kd, vd, qi, ki, wi, z_d = jnp.split(w_in, np.cumsum(L1_SIZES)[:-1].tolist(), axis=1)
    pad = jnp.zeros((d, LANE - IDX_DIM - IDX_HEADS), w_in.dtype)
    w = jnp.concatenate([qc, kc, vc, z_c, qd, kd, vd, qi, z_d, ki, wi, pad], axis=1)
    segs = [(0, 512), (512, 1024), (1024, 1536), (1536, 2048), (2048, 2304), (2304, 2560), (2560, 3072),
            (3072, 3200)]
    return w.astype(BF16), segs


def _pad_rows(x, n):
    return jnp.pad(x, ((0, 0), (0, n - x.shape[1]), (0, 0)))


def _cols_pool(pool):
    npool, ps = pool.shape[:2]
    cw = pool.shape[-1]
    perm = (0,) + tuple(range(2, pool.ndim)) + (1,)
    return jnp.transpose(pool, perm).reshape(npool, -1, cw, ps)


def _layer0(x, mod, past, w, *, tl, tq):
    (norm_g, w_in, cmp_wk, cmp_wv, conv_w, conv_b, lru_wr, lru_br, lru_wi, lru_bi, lru_lambda, w_out) = w
    shift, scale, gate = mod
    b, l, d = x.shape
    w_p, segs = _l0_weight(w_in)
    flat = shift.shape[1] != 1
    xin = x.reshape(1, b * l, d) if flat else x
    q, kvp, kvw, z_a, x_b, z_b, gates = _project(xin, norm_g, shift, scale, w_p, segs, 6, tl)
    if flat:
        q, kvp, kvw, z_a, x_b, z_b, gates = (t.reshape(b, l, -1) for t in (q, kvp, kvw, z_a, x_b, z_b, gates))
    wk2 = jnp.concatenate([cmp_wk, cmp_wk], axis=0)
    wv2 = jnp.concatenate([cmp_wv, cmp_wv], axis=0)
    w2 = jnp.stack([wk2, wk2, wv2, wv2], axis=0)
    nsa = dict(cw=HEAD_DIM, c_all=8, n_out=4, n_cmp=4, cmp_w=w2)
    lq = _round_up(l, 8)
    if past is None:
        ppool, ptable = _identity_pages(kvp)
        ksel, cmp = _gather_chunks(ppool, "rows", ptable, None, **nsa)
        wpool, wtable = _identity_pages(kvw)
        kwin = _gather_chunks(wpool, "rows", wtable, None, cw=HEAD_DIM, c_all=4, n_out=4)
        kv_win = kvw
        hist = jnp.zeros((b, CONV_WIDTH - 1, x_b.shape[-1]), F32)
        h0 = jnp.zeros((b, x_b.shape[-1]), F32)
        nsp = _round_up(cmp.shape[3], LANE)
        cmp = jnp.pad(cmp, ((0, 0), (0, 0), (0, 0), (0, nsp - cmp.shape[3]), (0, 0)))
        cmp = cmp.reshape(b, 2, 2, 2, nsp, HEAD_DIM).transpose(0, 2, 1, 3, 4, 5).reshape(b, 4, 2, nsp, HEAD_DIM)
        o_a = _nsa_attention(_pad_rows(q, lq), _pad_rows(gates, lq), cmp, ksel, kwin,
                             tq=min(tq, lq), q_pos0=0, lk=l, win_pos0=0)[:, :l]
    else:
        pool, table, win_buf, hist, h0 = past
        assert lq <= tq
        kv_win = jnp.concatenate([win_buf.reshape(b, win_buf.shape[1], -1), kvw], axis=1)
        lw_pad = _win_span(lq)
        assert kv_win.shape[1] <= lw_pad and lw_pad % PAGE_SIZE == 0
        wpool, wtable = _identity_pages(_pad_rows(kv_win, lw_pad))
        kwin = _gather_chunks(wpool, "rows", wtable, None, cw=HEAD_DIM, c_all=4, n_out=4)
        o_a = _nsa_decode(_pad_rows(q, lq), _pad_rows(gates, lq), _cols_pool(pool), table, _pad_rows(kvp, lq),
                          w2, kwin, n_new=l, win_pos0=table.shape[1] * PAGE_SIZE - win_buf.shape[1])[:, :l]
    o_b, h_last = _conv_rglru(_pad_rows(x_b, lq), hist, h0, conv_w, conv_b, lru_wr, lru_br, lru_wi, lru_bi,
                              lru_lambda, tl=min(256, lq), n_valid=l)
    o_b = o_b[:, :l]
    fl = (lambda t: t.reshape(1, b * l, -1)) if flat else (lambda t: t)
    x_new = _out_project(fl(o_a), fl(z_a), fl(o_b), fl(z_b), xin, gate, w_out.astype(BF16),
                         jnp.ones((d,), F32), tl=tl, final_norm=False).reshape(b, l, d)
    win_keep = min(NSA_WINDOW, kv_win.shape[1])
    conv_src = jnp.concatenate([hist, x_b], axis=1) if l < CONV_WIDTH - 1 else x_b
    states = (kvp.reshape(b, l, 4, NSA_KV_HEADS, HEAD_DIM),
              kv_win[:, -win_keep:].reshape(b, win_keep, 2, NSA_KV_HEADS, HEAD_DIM),
              conv_src[:, -(CONV_WIDTH - 1):], h_last)
    return x_new, states


def _layer1(x, mod, past, w, final_g, *, tl, tq):
    (norm_g, w_in, lam_q1, lam_k1, lam_q2, lam_k2, subln_g, w_out) = w
    shift, scale, gate = mod
    b, l, d = x.shape
    w_p, segs = _l1_weight(w_in)
    flat = shift.shape[1] != 1
    xin = x.reshape(1, b * l, d) if flat else x
    qc, kvc, z_c, qd, kvd, qi, z_d, kiw = _project(xin, norm_g, shift, scale, w_p, segs, -1, tl)
    if flat:
        qc, kvc, z_c, qd, kvd, qi, z_d, kiw = (t.reshape(b, l, -1) for t in (qc, kvc, z_c, qd, kvd, qi, z_d, kiw))
    lq = _round_up(l, 8)
    diff_a = dict(cw=2 * DIFF_HALF, c_all=4, n_out=4)
    dsa_a = dict(cw=HEAD_DIM, c_all=4, n_out=4)
    kidx_a = dict(cw=IDX_DIM, c_all=1, n_out=1)
    if past is None:
        dpool, dtable = _identity_pages(kvc)
        diff_kv = _gather_chunks(dpool, "rows", dtable, None, **diff_a)
        spool, stable = _identity_pages(kvd)
        dsa_kv = _gather_chunks(spool, "rows", stable, None, **dsa_a)
        ipool, itable = _identity_pages(kiw)
        kidx = _gather_chunks(ipool, "rows", itable, None, **kidx_a)
        q_pos0, lk = 0, l
    else:
        diff_pool, dsa_pool, kidx_pool, table = past
        past_len = table.shape[1] * PAGE_SIZE
        diff_rows = diff_pool.reshape(diff_pool.shape[0], PAGE_SIZE * 4, 2 * DIFF_HALF)
    lamv = jnp.stack([lam_q1, lam_k1, lam_q2, lam_k2], axis=0)
    if past is None:
        o_c = _diff_attention(_pad_rows(qc, lq), diff_kv, lamv, subln_g, tq=min(tq, lq), q_pos0=0)[:, :l]
        o_d = _dsa_attention(_pad_rows(qd, lq), _pad_rows(qi, lq), _pad_rows(kiw, lq), kidx, dsa_kv,
                             tq=min(tq, lq), q_pos0=0, lk=l)[:, :l]
    else:
        assert lq <= tq
        o_c = _diff_decode(_pad_rows(qc, lq), diff_rows, table, _pad_rows(kvc, lq), lamv, subln_g, n_new=l)[:, :l]
        o_d = _dsa_decode(_pad_rows(qd, lq), _pad_rows(qi, lq), _pad_rows(kiw, lq), _cols_pool(kidx_pool),
                          _cols_pool(dsa_pool), table, _pad_rows(kiw[:, :, :IDX_DIM], lq), _pad_rows(kvd, lq),
                          n_new=l)[:, :l]
    fl = (lambda t: t.reshape(1, b * l, -1)) if flat else (lambda t: t)
    y = _out_project(fl(o_c), fl(z_c), fl(o_d), fl(z_d), xin, gate, w_out.astype(BF16), final_g,
                     tl=tl, final_norm=True).reshape(b, l, d)
    states = (kvc.reshape(b, l, 2, DIFF_KV_HEADS, 2 * DIFF_HALF),
              kvd.reshape(b, l, 2, DSA_KV_HEADS, HEAD_DIM), kiw[:, :, :IDX_DIM])
    return y, states


def kernel(x_prompt, x_sample, cache_l0_nsa_kv, state_l0_win_kv, state_l0_conv, state_l0_lru_h,
           cache_l1_diff_kv, cache_l1_dsa_kv, cache_l1_dsa_kidx, page_table, c_prompt, c_sample,
           l0_norm_g, l0_ada_w, l0_ada_b, l0_w_in, l0_cmp_wk, l0_cmp_wv, l0_conv_w, l0_conv_b,
           l0_lru_wr, l0_lru_br, l0_lru_wi, l0_lru_bi, l0_lru_lambda, l0_w_out,
           l1_norm_g, l1_ada_w, l1_ada_b, l1_w_in, l1_lam_q1, l1_lam_k1, l1_lam_q2, l1_lam_k2,
           l1_subln_g, l1_w_out, final_norm_g):
    bp, lp, d = x_prompt.shape
    bs, ls, _ = x_sample.shape
    c_all = jnp.concatenate([c_prompt, c_sample], axis=0)

    def mods(ada_w, ada_b):
        m = _modulation(c_all, ada_w, ada_b)
        mp = tuple(t[:, None] for t in jnp.split(m[:bp], 3, axis=-1))
        ms = tuple(jnp.repeat(t, ls, axis=0)[None] for t in jnp.split(m[bp:], 3, axis=-1))
        return mp, ms

    tl_p = min(512, lp)
    tl_s = bs * ls
    tq = min(Q_TILE, lp)
    w0 = (l0_norm_g, l0_w_in, l0_cmp_wk, l0_cmp_wv, l0_conv_w, l0_conv_b, l0_lru_wr, l0_lru_br,
          l0_lru_wi, l0_lru_bi, l0_lru_lambda, l0_w_out)
    mp0, ms0 = mods(l0_ada_w, l0_ada_b)
    xp, (nsa_kv_p, win_p, conv_p, h_p) = _layer0(x_prompt, mp0, None, w0, tl=tl_p, tq=tq)
    xs, (nsa_kv_s, win_s, conv_s, h_s) = _layer0(
        x_sample, ms0, (cache_l0_nsa_kv, page_table, state_l0_win_kv, state_l0_conv, state_l0_lru_h), w0,
        tl=tl_s, tq=tq)
    w1 = (l1_norm_g, l1_w_in, l1_lam_q1, l1_lam_k1, l1_lam_q2, l1_lam_k2, l1_subln_g, l1_w_out)
    mp1, ms1 = mods(l1_ada_w, l1_ada_b)
    y_p, (diff_kv_p, dsa_kv_p, kidx_p) = _layer1(xp, mp1, None, w1, final_norm_g, tl=tl_p, tq=tq)
    y_s, (diff_kv_s, dsa_kv_s, kidx_s) = _layer1(
        xs, ms1, (cache_l1_diff_kv, cache_l1_dsa_kv, cache_l1_dsa_kidx, page_table), w1, final_norm_g,
        tl=tl_s, tq=tq)
    return (y_p, y_s, nsa_kv_p, nsa_kv_s, win_p, win_s, conv_p, conv_s, h_p, h_s,
            diff_kv_p, diff_kv_s, dsa_kv_p, dsa_kv_s, kidx_p, kidx_s)
```

```python
import functools
import math

import jax
import jax.numpy as jnp
import numpy as np
from jax import lax
from jax.experimental import pallas as pl
from jax.experimental.pallas import tpu as pltpu

F32 = jnp.float32
BF16 = jnp.bfloat16
I32 = jnp.int32

PAGE_SIZE = 128
HEAD_DIM = 64
NSA_HEADS = 8
NSA_KV_HEADS = 2
NSA_CMP_BLOCK = 32
NSA_SEL_BLOCK = 64
NSA_TOPN = 16
NSA_WINDOW = 512
FORCE_SCORE = 1e4
LRU_BLOCKS = 8
LRU_C = 8.0
CONV_WIDTH = 4
DIFF_HALF = 64
DIFF_HEADS = 4
DIFF_KV_HEADS = 2
DIFF_LAMBDA_INIT = 0.8 - 0.6 * math.exp(-0.3 * 1)
DSA_HEADS = 8
DSA_KV_HEADS = 2
IDX_HEADS = 4
IDX_DIM = 64
DSA_TOPK_MAX = 256
NORM_EPS = 1e-6
NEG = -1e30
REMOVED = -3e38
INT_MIN = -2 ** 31
LOG2E = math.log2(math.e)

LANE = 128
VMEM_LIMIT = 56 * 1024 * 1024
KV_TILE = 1024
TRI_TILE = 256
Q_TILE = 256
MAX_KV_TILE = 2048
PROBE_BINADES = 3
COUNT_ROWS = 64
MAX_SWEEP_BRANCHES = 8


def _kv_tile(tq, lk_pad):
    t = min(KV_TILE * max(1, Q_TILE // tq), MAX_KV_TILE, lk_pad)
    while lk_pad % t:
        t -= TRI_TILE
    return t


def _win_span(tq):
    return _round_up(NSA_WINDOW + tq, LANE)
PAGES_PER_STEP = 32


def _cparams(sem):
    return pltpu.CompilerParams(dimension_semantics=sem, vmem_limit_bytes=VMEM_LIMIT)


def _dot(a, b):
    return jnp.dot(a, b, preferred_element_type=F32)


def _dot_nt(a, b):
    return lax.dot_general(a, b, (((1,), (1,)), ((), ())), preferred_element_type=F32)


def _round_up(x, m):
    return (x + m - 1) // m * m


def _mod_kernel(c_ref, w_ref, b_ref, o_ref):
    o_ref[...] = jnp.dot(c_ref[...], w_ref[...], preferred_element_type=F32,
                         precision=lax.Precision.HIGHEST) + b_ref[...]


def _modulation(c, w, b):
    bc, d = c.shape
    n = w.shape[1]
    tn = 512
    return pl.pallas_call(
        _mod_kernel,
        grid=(n // tn,),
        in_specs=[pl.BlockSpec((bc, d), lambda j: (0, 0)),
                  pl.BlockSpec((d, tn), lambda j: (0, j)),
                  pl.BlockSpec((1, tn), lambda j: (0, j))],
        out_specs=pl.BlockSpec((bc, tn), lambda j: (0, j)),
        out_shape=jax.ShapeDtypeStruct((bc, n), F32),
        compiler_params=_cparams(("arbitrary",)),
        name="modulation",
    )(c, w, b.reshape(1, n))


def _proj_kernel(x_ref, g_ref, sh_ref, sc_ref, w_ref, *o_refs, segs, sigmoid_seg):
    x = x_ref[0]
    y = x * lax.rsqrt(jnp.mean(x * x, axis=-1, keepdims=True) + NORM_EPS)
    h = (y * g_ref[...]) * (1.0 + sc_ref[0]) + sh_ref[0]
    hb = h.astype(BF16)
    for i, ((a, b), o_ref) in enumerate(zip(segs, o_refs)):
        r = _dot(hb, w_ref[:, a:b])
        if i == sigmoid_seg:
            r = jax.nn.sigmoid(r)
        o_ref[0] = r


def _project(x, g, shift, scale, w, segs, sigmoid_seg, tl):
    b, l, d = x.shape
    ts = shift.shape[1]
    tm = 1 if ts == 1 else tl
    mod_map = (lambda bi, li: (bi, 0, 0)) if ts == 1 else (lambda bi, li: (bi, li, 0))
    p = w.shape[1]
    kern = functools.partial(_proj_kernel, segs=tuple(segs), sigmoid_seg=sigmoid_seg)
    return pl.pallas_call(
        kern,
        grid=(b, l // tl),
        in_specs=[pl.BlockSpec((1, tl, d), lambda bi, li: (bi, li, 0)),
                  pl.BlockSpec((1, d), lambda bi, li: (0, 0)),
                  pl.BlockSpec((1, tm, d), mod_map),
                  pl.BlockSpec((1, tm, d), mod_map),
                  pl.BlockSpec((d, p), lambda bi, li: (0, 0))],
        out_specs=[pl.BlockSpec((1, tl, e - a), lambda bi, li: (bi, li, 0)) for a, e in segs],
        out_shape=[jax.ShapeDtypeStruct((b, l, e - a), F32) for a, e in segs],
        compiler_params=_cparams(("parallel", "arbitrary")),
        name="norm_mod_project",
    )(x, g.reshape(1, d), shift, scale, w)


def _gather_kernel(pt_ref, *refs, pp, n_page_steps, layout, cw, c_all, n_cmp, n_out, has_new):
    page_refs = refs[:pp]
    pos = pp
    new_ref = None
    if has_new:
        new_ref = refs[pos]
        pos += 1
    w2_ref = None
    if n_cmp:
        w2_ref = refs[pos]
        pos += 1
    out_ref = refs[pos]
    cmp_ref = refs[pos + 1] if n_cmp else None
    j = pl.program_id(1)
    ones_col = jnp.where(lax.broadcasted_iota(I32, (PAGE_SIZE, LANE - cw), 1) == 0, 1.0, 0.0) if cw < LANE else None

    def chunk(i, c):
        if layout == "rows":
            return page_refs[i][0, :, c * cw:(c + 1) * cw]
        if layout == "cols":
            return page_refs[i][0, c].T
        return page_refs[i][0, pl.ds(c, PAGE_SIZE, stride=c_all), :]

    def emit(i, c, x):
        if c < n_cmp:
            nb = PAGE_SIZE // NSA_SEL_BLOCK
            prod = x.reshape(nb, NSA_SEL_BLOCK, cw) * w2_ref[c][None]
            cmp_ref[0, c, 0, i * nb:(i + 1) * nb, :] = jnp.sum(prod[:, :NSA_CMP_BLOCK], axis=1)
            cmp_ref[0, c, 1, i * nb:(i + 1) * nb, :] = jnp.sum(prod[:, NSA_CMP_BLOCK:], axis=1)
        else:
            if ones_col is not None:
                x = jnp.concatenate([x, ones_col], axis=-1)
            out_ref[0, c - n_cmp, i * PAGE_SIZE:(i + 1) * PAGE_SIZE, :] = x.astype(BF16)

    def pages():
        for i in range(pp):
            for c in range(n_cmp + n_out):
                emit(i, c, chunk(i, c))

    if has_new:
        pl.when(j < n_page_steps)(pages)

        @pl.when(j >= n_page_steps)
        def _():
            new = new_ref[0]
            for c in range(n_cmp + n_out):
                xc = new[:, c * cw:(c + 1) * cw]
                emit(0, c, jnp.concatenate([xc, jnp.zeros((PAGE_SIZE - xc.shape[0], cw), F32)], axis=0))
                for i in range(1, pp):
                    emit(i, c, jnp.zeros((PAGE_SIZE, cw), F32))
    else:
        pages()


def _gather_chunks(pool, layout, page_table, new, *, cw, c_all, n_out, n_cmp=0, cmp_w=None):
    bk, n_pages = page_table.shape
    pp = min(PAGES_PER_STEP, n_pages)
    assert n_pages % pp == 0
    n_page_steps = n_pages // pp
    has_new = new is not None
    n_steps = n_page_steps + (1 if has_new else 0)
    rows = pp * PAGE_SIZE
    lk_pad = n_steps * rows
    page_block = (1,) + pool.shape[1:]
    zeros = (0,) * (len(page_block) - 1)

    def page_map(i):
        def f(b, j, pt):
            return (pt[b, jnp.minimum(j * pp + i, n_pages - 1)],) + zeros
        return f

    in_specs = [pl.BlockSpec(page_block, page_map(i)) for i in range(pp)]
    args = [pool] * pp
    if has_new:
        in_specs.append(pl.BlockSpec((1,) + new.shape[1:], lambda b, j, pt: (b, 0, 0)))
        args.append(new)
    if n_cmp:
        in_specs.append(pl.BlockSpec(cmp_w.shape, lambda b, j, pt: (0, 0, 0)))
        args.append(cmp_w)
    out_specs = [pl.BlockSpec((1, n_out, rows, LANE), lambda b, j, pt: (b, 0, j, 0))]
    out_shape = [jax.ShapeDtypeStruct((bk, n_out, lk_pad, LANE), BF16)]
    if n_cmp:
        nb = rows // NSA_SEL_BLOCK
        out_specs.append(pl.BlockSpec((1, n_cmp, 2, nb, HEAD_DIM), lambda b, j, pt: (b, 0, 0, j, 0)))
        out_shape.append(jax.ShapeDtypeStruct((bk, n_cmp, 2, lk_pad // NSA_SEL_BLOCK, HEAD_DIM), F32))
    kern = functools.partial(_gather_kernel, pp=pp, n_page_steps=n_page_steps, layout=layout, cw=cw,
                             c_all=c_all, n_cmp=n_cmp, n_out=n_out, has_new=has_new)
    outs = pl.pallas_call(
        kern,
        grid_spec=pltpu.PrefetchScalarGridSpec(
            num_scalar_prefetch=1, grid=(bk, n_steps), in_specs=in_specs, out_specs=out_specs),
        out_shape=out_shape,
        compiler_params=_cparams(("parallel", "arbitrary")),
        name="gather_pages",
    )(page_table, *args)
    return outs if n_cmp else outs[0]


def _identity_pages(x):
    b, l, w = x.shape
    n_pages = l // PAGE_SIZE
    pool = x.reshape(b * n_pages, PAGE_SIZE, w)
    table = jnp.arange(b * n_pages, dtype=I32).reshape(b, n_pages)
    return pool, table


def _flash_tile(s_all, bias, v_tile, carry, n_heads, tq, l_in_acc, v_t=False):
    m, l, acc = carry
    s = s_all
    if bias is not None:
        s = jnp.concatenate([s_all[h * tq:(h + 1) * tq] + bias for h in range(n_heads)], axis=0)
    m_new = jnp.maximum(m, jnp.max(s, axis=-1, keepdims=True))
    p = jnp.exp2(s - m_new)
    alpha = jnp.exp2(m - m_new)
    acc = alpha * acc + (_dot_nt if v_t else _dot)(p.astype(BF16), v_tile)
    if not l_in_acc:
        l = alpha * l + jnp.sum(p, axis=-1, keepdims=True)
    return m_new, l, acc


def _causal_sweep(step, carry, q0, tq, tk):
    assert tk % tq == 0 and tk // tq <= MAX_SWEEP_BRANCHES
    n_full = q0 // tk
    carry = lax.fori_loop(0, n_full, lambda j, c: step(pl.multiple_of(j * tk, tk), tk, False, c), carry)
    k0 = pl.multiple_of(n_full * tk, tk)
    branches = [functools.partial(step, k0, w, True) for w in range(tq, tk + tq, tq)]
    return lax.switch((q0 - k0) // tq, branches, carry)


def _flash_init(rows, dv):
    return (jnp.full((rows, 1), NEG, F32), jnp.zeros((rows, 1), F32), jnp.zeros((rows, dv), F32))


def _flash_out(acc):
    return acc[:, :HEAD_DIM] * (1.0 / jnp.maximum(acc[:, HEAD_DIM:HEAD_DIM + 1], 1e-30))


def _stack_heads(q, n, width):
    return jnp.concatenate([q[:, h * width:(h + 1) * width] for h in range(n)], axis=0)


def _pad_lanes(x):
    return jnp.concatenate([x, jnp.zeros((x.shape[0], LANE - x.shape[1]), x.dtype)], axis=-1)


def _nsa_prologue(qg, kw, vw, cmp4, *, q0, tq, nsp, n_top, win_pos0, start):
    hpg = NSA_HEADS // NSA_KV_HEADS
    rows = hpg * tq
    tqp = max(tq, LANE)
    qpos = q0 + lax.broadcasted_iota(I32, (tq, 1), 0)
    qpos_r = jnp.concatenate([qpos] * hpg, axis=0)
    blk = lax.broadcasted_iota(I32, (1, nsp), 1)
    blk_r = lax.broadcasted_iota(I32, (nsp, 1), 0)
    blk_rf = blk_r.astype(F32)
    cur_l = (q0 + lax.broadcasted_iota(I32, (1, tqp), 1)) // NSA_SEL_BLOCK
    vis_e = (blk * NSA_SEL_BLOCK + (NSA_CMP_BLOCK - 1)) <= qpos_r
    vis_o = (blk * NSA_SEL_BLOCK + (NSA_SEL_BLOCK - 1)) <= qpos_r
    kpos_w = win_pos0 + start + lax.broadcasted_iota(I32, (1, kw.shape[0]), 1)
    dlt = qpos - kpos_w
    bias_w = jnp.where(dlt >= 0, jnp.where(dlt < NSA_WINDOW, 0.0, NEG), NEG)

    qs64 = _stack_heads(qg, hpg, HEAD_DIM)
    qs = _pad_lanes(qs64).astype(BF16)
    qs64 = qs64.astype(BF16)

    _, _, acc_w = _flash_tile(_dot_nt(qs, kw), bias_w, vw, _flash_init(rows, LANE), hpg, tq, True)
    o_w = _flash_out(acc_w)

    kce, kco, vce, vco = (x.astype(BF16) for x in cmp4)
    s_e = jnp.where(vis_e, _dot_nt(qs64, kce), NEG)
    s_o = jnp.where(vis_o, _dot_nt(qs64, kco), NEG)
    m = jnp.maximum(jnp.max(s_e, axis=-1, keepdims=True), jnp.max(s_o, axis=-1, keepdims=True))
    p_e = jnp.where(vis_e, jnp.exp2(s_e - m), 0.0)
    p_o = jnp.where(vis_o, jnp.exp2(s_o - m), 0.0)
    den = jnp.sum(p_e, axis=-1, keepdims=True) + jnp.sum(p_o, axis=-1, keepdims=True)
    inv = 1.0 / jnp.maximum(den, 1e-30)
    p_e = p_e * inv
    p_o = p_o * inv
    o_c = _dot(p_e.astype(BF16), vce) + _dot(p_o.astype(BF16), vco)

    pe_h = sum(p_e[h * tq:(h + 1) * tq] for h in range(hpg))
    po_h = sum(p_o[h * tq:(h + 1) * tq] for h in range(hpg))
    imp = pe_h + po_h
    if tqp > tq:
        imp = jnp.concatenate([imp, jnp.zeros((tqp - tq, nsp), F32)], axis=0)
    imp = imp.T
    imp = jnp.where((blk_r == cur_l) | (blk_r == 0), FORCE_SCORE, imp)
    imp = jnp.where(blk_r <= cur_l, imp, NEG)
    sel = jnp.zeros((nsp, tqp), F32)
    for _ in range(n_top):
        mx = jnp.max(imp, axis=0, keepdims=True)
        first = jnp.min(jnp.where(imp == mx, blk_rf, float(nsp)), axis=0, keepdims=True)
        pick = blk_rf == first
        sel = jnp.where(pick & (mx > 0.5 * NEG), 1.0, sel)
        imp = jnp.where(pick, REMOVED, imp)
    return qs, qs64, o_c, o_w, sel.T[:tq].astype(BF16)


def _nsa_combine(gates, branches, tq):
    hpg = NSA_HEADS // NSA_KV_HEADS
    outs = []
    for g, (o_c, o_s, o_w) in enumerate(branches):
        for h in range(hpg):
            r = slice(h * tq, (h + 1) * tq)
            c = g * LANE + 3 * h
            outs.append(gates[:, c:c + 1] * o_c[r] + gates[:, c + 1:c + 2] * o_s[r]
                        + gates[:, c + 2:c + 3] * o_w[r])
    return jnp.concatenate(outs, axis=-1)


def _nsa_kernel(q_ref, g_ref, cmp_ref, ksel_ref, kwin_ref, o_ref, *,
                tq, tk, q_pos0, win_pos0, nsp, n_top):
    G = NSA_KV_HEADS
    hpg = NSA_HEADS // G
    gw = hpg * HEAD_DIM
    qi = pl.program_id(1)
    q0 = q_pos0 + qi * tq
    qpos = q0 + lax.broadcasted_iota(I32, (tq, 1), 0)
    rows = hpg * tq
    q_all = q_ref[0] * (HEAD_DIM ** -0.5 * LOG2E)
    start = pl.multiple_of(jnp.maximum(q0 - NSA_WINDOW - win_pos0, 0), 8)
    span = _win_span(tq)
    pro = [_nsa_prologue(q_all[:, g * gw:(g + 1) * gw],
                         kwin_ref[0, g, pl.ds(start, span), :], kwin_ref[0, G + g, pl.ds(start, span), :],
                         tuple(cmp_ref[0, 2 * g + kv, eo] for kv in range(2) for eo in range(2)),
                         q0=q0, tq=tq, nsp=nsp, n_top=n_top, win_pos0=win_pos0, start=start)
           for g in range(G)]

    blk_col = lax.broadcasted_iota(I32, (nsp, 1), 0)

    def sel_step(k0, w, diagonal, carry):
        kpos = k0 + lax.broadcasted_iota(I32, (1, w), 1)
        expand = jnp.where(blk_col == kpos // NSA_SEL_BLOCK, 1.0, 0.0).astype(BF16)
        out = []
        for g in range(G):
            bias = jnp.where(_dot(pro[g][4], expand) > 0.5, 0.0, NEG)
            if diagonal:
                bias = jnp.where(kpos <= qpos, bias, NEG)
            s = _dot_nt(pro[g][0], ksel_ref[0, g, pl.ds(k0, w), :])
            out.append(_flash_tile(s, bias, ksel_ref[0, G + g, pl.ds(k0, w), :], carry[g], hpg, tq, True))
        return tuple(out)

    res = _causal_sweep(sel_step, tuple(_flash_init(rows, LANE) for _ in range(G)), q0, tq, tk)
    o_ref[0] = _nsa_combine(g_ref[0], [(pro[g][2], _flash_out(res[g][2]), pro[g][3]) for g in range(G)], tq)


def _nsa_attention(q, gates, cmp, ksel, kwin, *, tq, q_pos0, lk, win_pos0):
    bk, lq, _ = q.shape
    G = NSA_KV_HEADS
    nsp = cmp.shape[3]
    lk_pad = ksel.shape[2]
    lw_pad = kwin.shape[2]
    tk = _kv_tile(tq, lk_pad)
    ns = -(-lk // NSA_SEL_BLOCK)
    assert lw_pad >= _win_span(tq) and q_pos0 + lq <= lk_pad
    kern = functools.partial(_nsa_kernel, tq=tq, tk=tk, q_pos0=q_pos0, win_pos0=win_pos0,
                             nsp=nsp, n_top=min(NSA_TOPN, ns))
    qw = NSA_HEADS * HEAD_DIM
    return pl.pallas_call(
        kern,
        grid=(bk, lq // tq),
        in_specs=[pl.BlockSpec((1, tq, qw), lambda b, i: (b, i, 0)),
                  pl.BlockSpec((1, tq, G * LANE), lambda b, i: (b, i, 0)),
                  pl.BlockSpec((1, 2 * G, 2, nsp, HEAD_DIM), lambda b, i: (b, 0, 0, 0, 0)),
                  pl.BlockSpec((1, 2 * G, lk_pad, LANE), lambda b, i: (b, 0, 0, 0)),
                  pl.BlockSpec((1, 2 * G, lw_pad, LANE), lambda b, i: (b, 0, 0, 0))],
        out_specs=pl.BlockSpec((1, tq, qw), lambda b, i: (b, i, 0)),
        out_shape=jax.ShapeDtypeStruct((bk, lq, qw), F32),
        compiler_params=_cparams(("parallel", "arbitrary")),
        name="nsa_attention",
    )(q, gates, cmp, ksel, kwin)


def _nsa_decode_kernel(pt_ref, q_ref, g_ref, new_ref, w2_ref, kwin_ref, *refs,
                       pp, n_steps, tq, n_new, past_len, win_pos0, nsp, n_top):
    page_refs = refs[:pp]
    o_ref, cmp_ref, oc_ref, ow_ref, sel_ref, m_ref, l_ref, acc_ref = refs[pp:]
    G = NSA_KV_HEADS
    hpg = NSA_HEADS // G
    gw = hpg * HEAD_DIM
    rows = hpg * tq
    j = pl.program_id(1)
    keys = pp * PAGE_SIZE
    nb = keys // NSA_SEL_BLOCK
    q_all = q_ref[0] * (HEAD_DIM ** -0.5 * LOG2E)

    def queries(g):
        qs64 = _stack_heads(q_all[:, g * gw:(g + 1) * gw], hpg, HEAD_DIM)
        return _pad_lanes(qs64).astype(BF16), qs64.astype(BF16)

    @pl.when(j == 0)
    def _():
        cmp_ref[...] = jnp.zeros(cmp_ref.shape, F32)

    @pl.when(j < n_steps)
    def _():
        r0 = pl.multiple_of(j * nb, nb)
        for c in range(2 * G):
            x = jnp.concatenate([r[0, c].T for r in page_refs], axis=0)
            prod = x.reshape(nb, NSA_SEL_BLOCK, HEAD_DIM) * w2_ref[c][None]
            cmp_ref[c, 0, pl.ds(r0, nb), :] = jnp.sum(prod[:, :NSA_CMP_BLOCK], axis=1)
            cmp_ref[c, 1, pl.ds(r0, nb), :] = jnp.sum(prod[:, NSA_CMP_BLOCK:], axis=1)

    @pl.when(j == n_steps - 1)
    def _():
        for g in range(G):
            _, _, o_c, o_w, sel = _nsa_prologue(
                q_all[:, g * gw:(g + 1) * gw], kwin_ref[0, g], kwin_ref[0, G + g],
                tuple(cmp_ref[g + G * kv, eo] for kv in range(2) for eo in range(2)),
                q0=past_len, tq=tq, nsp=nsp, n_top=n_top, win_pos0=win_pos0, start=0)
            oc_ref[g] = o_c
            ow_ref[g] = o_w
            sel_ref[g] = sel
        m_ref[...] = jnp.full(m_ref.shape, NEG, F32)
        l_ref[...] = jnp.zeros(l_ref.shape, F32)
        acc_ref[...] = jnp.zeros(acc_ref.shape, F32)

    def update(g, s, bias, v, v_t):
        carry = (m_ref[g], l_ref[g], acc_ref[g])
        m, l, acc = _flash_tile(s, bias, v, carry, hpg, tq, False, v_t=v_t)
        m_ref[g] = m
        l_ref[g] = l
        acc_ref[g] = acc

    @pl.when(j >= n_steps)
    def _():
        kpos = (j - n_steps) * keys + lax.broadcasted_iota(I32, (1, keys), 1)
        blk_col = lax.broadcasted_iota(I32, (nsp, 1), 0)
        expand = jnp.where(blk_col == kpos // NSA_SEL_BLOCK, 1.0, 0.0).astype(BF16)
        for g in range(G):
            _, qs64 = queries(g)
            k_t = jnp.concatenate([r[0, g] for r in page_refs], axis=-1).astype(BF16)
            v_t = jnp.concatenate([r[0, G + g] for r in page_refs], axis=-1).astype(BF16)
            bias = jnp.where(_dot(sel_ref[g], expand) > 0.5, 0.0, NEG)
            update(g, _dot(qs64, k_t), bias, v_t, True)

    @pl.when(j == 2 * n_steps - 1)
    def _():
        new = new_ref[0]
        pad = jnp.zeros((LANE - tq, HEAD_DIM), F32)
        row = lax.broadcasted_iota(I32, (tq, 1), 0)
        col = lax.broadcasted_iota(I32, (1, LANE), 1)
        own = past_len // NSA_SEL_BLOCK
        branches = []
        for g in range(G):
            _, qs64 = queries(g)
            k = jnp.concatenate([new[:, (2 * G + g) * HEAD_DIM:(2 * G + g + 1) * HEAD_DIM], pad], axis=0)
            v = jnp.concatenate([new[:, (3 * G + g) * HEAD_DIM:(3 * G + g + 1) * HEAD_DIM], pad], axis=0)
            picked = sel_ref[g][:, own:own + 1].astype(F32) > 0.5
            bias = jnp.where((col <= row) & (col < n_new) & picked, 0.0, NEG)
            update(g, _dot_nt(qs64, k.astype(BF16)), bias, v.astype(BF16), False)
            o_s = acc_ref[g] * (1.0 / jnp.maximum(l_ref[g], 1e-30))
            branches.append((oc_ref[g], o_s, ow_ref[g]))
        o_ref[0] = _nsa_combine(g_ref[0], branches, tq)


def _nsa_decode(q, gates, pool, page_table, new, w2, kwin, *, n_new, win_pos0):
    bk, tq, qw = q.shape
    n_pages = page_table.shape[1]
    pp = min(PAGES_PER_STEP, n_pages)
    past_len = n_pages * PAGE_SIZE
    assert n_pages % pp == 0 and past_len % NSA_SEL_BLOCK == 0 and n_new <= NSA_CMP_BLOCK
    assert kwin.shape[2] == _win_span(tq)
    n_steps = n_pages // pp
    G = NSA_KV_HEADS
    rows = (NSA_HEADS // G) * tq
    lk = past_len + n_new
    ns = -(-lk // NSA_SEL_BLOCK)
    nsp = _round_up(ns, LANE)

    def page_map(i):
        return lambda b, j, pt: (pt[b, (j % n_steps) * pp + i], j // n_steps, 0, 0)

    const3 = lambda b, j, pt: (b, 0, 0)
    in_specs = [pl.BlockSpec((1, tq, qw), const3),
                pl.BlockSpec((1, tq, G * LANE), const3),
                pl.BlockSpec((1, tq, new.shape[-1]), const3),
                pl.BlockSpec(w2.shape, lambda b, j, pt: (0, 0, 0)),
                pl.BlockSpec((1,) + kwin.shape[1:], lambda b, j, pt: (b, 0, 0, 0))]
    in_specs += [pl.BlockSpec((1, 2 * G, HEAD_DIM, PAGE_SIZE), page_map(i)) for i in range(pp)]
    kern = functools.partial(_nsa_decode_kernel, pp=pp, n_steps=n_steps, tq=tq, n_new=n_new, past_len=past_len,
                             win_pos0=win_pos0, nsp=nsp, n_top=min(NSA_TOPN, ns))
    return pl.pallas_call(
        kern,
        grid_spec=pltpu.PrefetchScalarGridSpec(
            num_scalar_prefetch=1, grid=(bk, 2 * n_steps), in_specs=in_specs,
            out_specs=pl.BlockSpec((1, tq, qw), const3),
            scratch_shapes=[pltpu.VMEM((2 * G, 2, nsp, HEAD_DIM), F32),
                            pltpu.VMEM((G, rows, HEAD_DIM), F32), pltpu.VMEM((G, rows, HEAD_DIM), F32),
                            pltpu.VMEM((G, tq, nsp), BF16),
                            pltpu.VMEM((G, rows, 1), F32), pltpu.VMEM((G, rows, 1), F32),
                            pltpu.VMEM((G, rows, HEAD_DIM), F32)]),
        out_shape=jax.ShapeDtypeStruct((bk, tq, qw), F32),
        compiler_params=_cparams(("parallel", "arbitrary")),
        name="nsa_decode",
    )(page_table, q, gates, new, w2, kwin, *([pool] * pp))


def _shift_rows(x, d, fill):
    rolled = pltpu.roll(x, d, axis=0)
    row = lax.broadcasted_iota(I32, x.shape, 0)
    return jnp.where(row >= d, rolled, fill)


def _lru_kernel(x_ref, hist_ref, h0_ref, cw_ref, cb_ref, wr_ref, br_ref, wi_ref, bi_ref, lam_ref,
                o_ref, hl_ref, tail_ref, h_ref, *, tl, last_row):
    li = pl.program_id(1)

    @pl.when(li == 0)
    def _():
        tail_ref[...] = jnp.concatenate(
            [jnp.zeros((8 - (CONV_WIDTH - 1), x_ref.shape[-1]), F32), hist_ref[0]], axis=0)
        h_ref[...] = h0_ref[0]

    x = x_ref[0]
    xp = jnp.concatenate([tail_ref[...], x], axis=0)
    cw = cw_ref[...]
    conv = sum(xp[8 - (CONV_WIDTH - 1) + j:8 - (CONV_WIDTH - 1) + j + tl] * cw[j:j + 1]
               for j in range(CONV_WIDTH))
    conv = cb_ref[...] + conv
    tail_ref[...] = x[tl - 8:tl]

    cb16 = conv.astype(BF16)
    r = jax.nn.sigmoid(_dot(cb16, wr_ref[...]) + br_ref[...])
    ig = jax.nn.sigmoid(_dot(cb16, wi_ref[...]) + bi_ref[...])
    log_a = -LRU_C * r * jax.nn.softplus(-lam_ref[...])
    a = jnp.exp(log_a)
    th = jnp.tanh(log_a)
    b = jnp.sqrt(-2.0 * th / (1.0 - th)) * (ig * conv)

    d = 1
    while d < tl:
        a_prev = _shift_rows(a, d, 1.0)
        b_prev = _shift_rows(b, d, 0.0)
        b = a * b_prev + b
        a = a * a_prev
        d *= 2
    h = a * h_ref[...] + b
    o_ref[0] = h
    h_ref[...] = h[tl - 1:tl]

    @pl.when(li == pl.num_programs(1) - 1)
    def _():
        hl_ref[0] = h[last_row:last_row + 1]


def _block_diag(w):
    nb, bw, _ = w.shape
    eye = jnp.eye(nb, dtype=w.dtype)
    return (eye[:, None, :, None] * w[:, :, None, :]).reshape(nb * bw, nb * bw)


def _conv_rglru(x_b, hist, h0, conv_w, conv_b, w_r, b_r, w_i, b_i, lam, *, tl, n_valid):
    b, l, w = x_b.shape
    assert tl >= 8 and l % tl == 0 and n_valid > l - tl
    kern = functools.partial(_lru_kernel, tl=tl, last_row=(n_valid - 1) % tl)
    vec = lambda: pl.BlockSpec((1, w), lambda bi, li: (0, 0))
    h, h_last = pl.pallas_call(
        kern,
        grid=(b, l // tl),
        in_specs=[pl.BlockSpec((1, tl, w), lambda bi, li: (bi, li, 0)),
                  pl.BlockSpec((1, CONV_WIDTH - 1, w), lambda bi, li: (bi, 0, 0)),
                  pl.BlockSpec((1, 1, w), lambda bi, li: (bi, 0, 0)),
                  pl.BlockSpec((CONV_WIDTH, w), lambda bi, li: (0, 0)),
                  vec(),
                  pl.BlockSpec((w, w), lambda bi, li: (0, 0)), vec(),
                  pl.BlockSpec((w, w), lambda bi, li: (0, 0)), vec(), vec()],
        out_specs=[pl.BlockSpec((1, tl, w), lambda bi, li: (bi, li, 0)),
                   pl.BlockSpec((1, 1, w), lambda bi, li: (bi, 0, 0))],
        out_shape=[jax.ShapeDtypeStruct((b, l, w), F32), jax.ShapeDtypeStruct((b, 1, w), F32)],
        scratch_shapes=[pltpu.VMEM((8, w), F32), pltpu.VMEM((1, w), F32)],
        compiler_params=_cparams(("parallel", "arbitrary")),
        name="conv_rglru",
    )(x_b, hist, h0.reshape(b, 1, w), conv_w, conv_b.reshape(1, w),
      _block_diag(w_r).astype(BF16), b_r.reshape(1, w), _block_diag(w_i).astype(BF16), b_i.reshape(1, w),
      lam.reshape(1, w))
    return h, h_last.reshape(b, w)


def _out_kernel(oa_ref, za_ref, ob_ref, zb_ref, x_ref, gate_ref, w_ref, fg_ref, o_ref, *, final_norm):
    half = oa_ref.shape[-1]
    ma = (oa_ref[0] * jax.nn.silu(za_ref[0])).astype(BF16)
    mb = (ob_ref[0] * jax.nn.silu(zb_ref[0])).astype(BF16)
    y = _dot(ma, w_ref[0:half, :]) + _dot(mb, w_ref[half:2 * half, :])
    out = x_ref[0] + gate_ref[0] * y
    if final_norm:
        out = out * lax.rsqrt(jnp.mean(out * out, axis=-1, keepdims=True) + NORM_EPS) * fg_ref[...]
    o_ref[0] = out


def _out_project(o_a, z_a, o_b, z_b, x, gate, w_out, final_g, *, tl, final_norm):
    b, l, d = x.shape
    half = o_a.shape[-1]
    ts = gate.shape[1]
    tm = 1 if ts == 1 else tl
    mod_map = (lambda bi, li: (bi, 0, 0)) if ts == 1 else (lambda bi, li: (bi, li, 0))
    act = lambda: pl.BlockSpec((1, tl, half), lambda bi, li: (bi, li, 0))
    return pl.pallas_call(
        functools.partial(_out_kernel, final_norm=final_norm),
        grid=(b, l // tl),
        in_specs=[act(), act(), act(), act(),
                  pl.BlockSpec((1, tl, d), lambda bi, li: (bi, li, 0)),
                  pl.BlockSpec((1, tm, d), mod_map),
                  pl.BlockSpec((2 * half, d), lambda bi, li: (0, 0)),
                  pl.BlockSpec((1, d), lambda bi, li: (0, 0))],
        out_specs=pl.BlockSpec((1, tl, d), lambda bi, li: (bi, li, 0)),
        out_shape=jax.ShapeDtypeStruct((b, l, d), F32),
        compiler_params=_cparams(("parallel", "arbitrary")),
        name="out_project",
    )(o_a, z_a, o_b, z_b, x, gate, w_out, final_g.reshape(1, d))


def _diff_queries(q, tq):
    hpg = DIFF_HEADS // DIFF_KV_HEADS
    q = q * (DIFF_HALF ** -0.5 * LOG2E)
    zero = jnp.zeros((tq, DIFF_HALF), F32)
    parts = []
    for mp in range(2):
        for h in range(hpg):
            qh = q[:, (2 * h + mp) * DIFF_HALF:(2 * h + mp + 1) * DIFF_HALF]
            parts.append(jnp.concatenate([qh, zero] if mp == 0 else [zero, qh], axis=-1))
    return jnp.concatenate(parts, axis=0).astype(BF16)


def _diff_finish(l, acc, lamv, subg, tq):
    hpg = DIFF_HEADS // DIFF_KV_HEADS
    o = acc * (1.0 / jnp.maximum(l, 1e-30))
    lam = (jnp.exp(jnp.sum(lamv[0:1] * lamv[1:2], axis=-1, keepdims=True))
           - jnp.exp(jnp.sum(lamv[2:3] * lamv[3:4], axis=-1, keepdims=True)) + DIFF_LAMBDA_INIT)
    half = hpg * tq
    od = o[0:half] - lam * o[half:2 * half]
    od = od * lax.rsqrt(jnp.mean(od * od, axis=-1, keepdims=True) + NORM_EPS)
    od = od * subg * (1.0 - DIFF_LAMBDA_INIT)
    return jnp.concatenate([od[h * tq:(h + 1) * tq] for h in range(hpg)], axis=-1)


def _diff_kernel(q_ref, k_ref, v_ref, lamv_ref, subg_ref, o_ref, *, tq, tk, q_pos0):
    hpg = DIFF_HEADS // DIFF_KV_HEADS
    qi = pl.program_id(2)
    q0 = q_pos0 + qi * tq
    n_maps = 2 * hpg
    rows = n_maps * tq
    qpos = q0 + lax.broadcasted_iota(I32, (tq, 1), 0)
    qs = _diff_queries(q_ref[0], tq)

    def step(k0, w, diagonal, carry):
        s = _dot_nt(qs, k_ref[0, 0, pl.ds(k0, w), :])
        bias = None
        if diagonal:
            kpos = k0 + lax.broadcasted_iota(I32, (1, w), 1)
            bias = jnp.where(kpos <= qpos, 0.0, NEG)
        return _flash_tile(s, bias, v_ref[0, 0, pl.ds(k0, w), :], carry, n_maps, tq, False)

    m, l, acc = _causal_sweep(step, _flash_init(rows, 2 * DIFF_HALF), q0, tq, tk)
    o_ref[0] = _diff_finish(l, acc, lamv_ref[...], subg_ref[...], tq)


def _diff_decode_kernel(pt_ref, q_ref, new_ref, lamv_ref, subg_ref, *refs, pp, tq, n_new):
    page_refs = refs[:pp]
    o_ref, m_ref, l_ref, acc_ref = refs[pp:]
    G = DIFF_KV_HEADS
    n_maps = 2 * (DIFF_HEADS // G)
    gw = n_maps * DIFF_HALF
    j = pl.program_id(1)

    @pl.when(j == 0)
    def _():
        m_ref[...] = jnp.full(m_ref.shape, NEG, F32)
        l_ref[...] = jnp.zeros(l_ref.shape, F32)
        acc_ref[...] = jnp.zeros(acc_ref.shape, F32)

    def update(g, qs, k, v, bias):
        carry = (m_ref[g], l_ref[g], acc_ref[g])
        m, l, acc = _flash_tile(_dot_nt(qs, k), bias, v, carry, n_maps, tq, False)
        m_ref[g] = m
        l_ref[g] = l
        acc_ref[g] = acc

    qs = [_diff_queries(q_ref[0][:, g * gw:(g + 1) * gw], tq) for g in range(G)]
    for g in range(G):
        k = jnp.concatenate([r[0, pl.ds(g, PAGE_SIZE, stride=2 * G), :] for r in page_refs], axis=0)
        v = jnp.concatenate([r[0, pl.ds(G + g, PAGE_SIZE, stride=2 * G), :] for r in page_refs], axis=0)
        update(g, qs[g], k.astype(BF16), v.astype(BF16), None)

    @pl.when(j == pl.num_programs(1) - 1)
    def _():
        new = new_ref[0]
        pad = jnp.zeros((LANE - tq, 2 * DIFF_HALF), F32)
        row = lax.broadcasted_iota(I32, (tq, 1), 0)
        col = lax.broadcasted_iota(I32, (1, LANE), 1)
        bias = jnp.where((col <= row) & (col < n_new), 0.0, NEG)
        outs = []
        for g in range(G):
            k = jnp.concatenate([new[:, g * 2 * DIFF_HALF:(g + 1) * 2 * DIFF_HALF], pad], axis=0)
            v = jnp.concatenate([new[:, (G + g) * 2 * DIFF_HALF:(G + g + 1) * 2 * DIFF_HALF], pad], axis=0)
            update(g, qs[g], k.astype(BF16), v.astype(BF16), bias)
            outs.append(_diff_finish(l_ref[g], acc_ref[g], lamv_ref[...], subg_ref[...], tq))
        o_ref[0] = jnp.concatenate(outs, axis=-1)


def _diff_decode(q, pool, page_table, new, lamv, subln_g, *, n_new):
    bk, tq, qw = q.shape
    n_pages = page_table.shape[1]
    pp = min(PAGES_PER_STEP, n_pages)
    assert n_pages % pp == 0
    G = DIFF_KV_HEADS
    rows = 2 * (DIFF_HEADS // G) * tq

    def page_map(i):
        return lambda b, j, pt: (pt[b, j * pp + i], 0, 0)

    in_specs = [pl.BlockSpec((1, tq, qw), lambda b, j, pt: (b, 0, 0)),
                pl.BlockSpec((1, tq, new.shape[-1]), lambda b, j, pt: (b, 0, 0)),
                pl.BlockSpec((4, DIFF_HALF), lambda b, j, pt: (0, 0)),
                pl.BlockSpec((1, 2 * DIFF_HALF), lambda b, j, pt: (0, 0))]
    in_specs += [pl.BlockSpec((1,) + pool.shape[1:], page_map(i)) for i in range(pp)]
    return pl.pallas_call(
        functools.partial(_diff_decode_kernel, pp=pp, tq=tq, n_new=n_new),
        grid_spec=pltpu.PrefetchScalarGridSpec(
            num_scalar_prefetch=1, grid=(bk, n_pages // pp), in_specs=in_specs,
            out_specs=pl.BlockSpec((1, tq, qw), lambda b, j, pt: (b, 0, 0)),
            scratch_shapes=[pltpu.VMEM((G, rows, 1), F32), pltpu.VMEM((G, rows, 1), F32),
                            pltpu.VMEM((G, rows, 2 * DIFF_HALF), F32)]),
        out_shape=jax.ShapeDtypeStruct((bk, tq, qw), F32),
        compiler_params=_cparams(("parallel", "arbitrary")),
        name="diff_decode",
    )(page_table, q, new, lamv, subln_g.reshape(1, 2 * DIFF_HALF), *([pool] * pp))


def _diff_attention(q, kv, lamv, subln_g, *, tq, q_pos0):
    bk, lq, _ = q.shape
    G = DIFF_KV_HEADS
    lk_pad = kv.shape[2]
    tk = _kv_tile(tq, lk_pad)
    gw = (DIFF_HEADS // G) * 2 * DIFF_HALF
    assert q_pos0 + lq <= lk_pad
    return pl.pallas_call(
        functools.partial(_diff_kernel, tq=tq, tk=tk, q_pos0=q_pos0),
        grid=(bk, G, lq // tq),
        in_specs=[pl.BlockSpec((1, tq, gw), lambda b, g, i: (b, i, g)),
                  pl.BlockSpec((1, 1, lk_pad, 2 * DIFF_HALF), lambda b, g, i: (b, g, 0, 0)),
                  pl.BlockSpec((1, 1, lk_pad, 2 * DIFF_HALF), lambda b, g, i: (b, G + g, 0, 0)),
                  pl.BlockSpec((4, DIFF_HALF), lambda b, g, i: (0, 0)),
                  pl.BlockSpec((1, 2 * DIFF_HALF), lambda b, g, i: (0, 0))],
        out_specs=pl.BlockSpec((1, tq, gw), lambda b, g, i: (b, i, g)),
        out_shape=jax.ShapeDtypeStruct((bk, lq, DIFF_HEADS * 2 * DIFF_HALF), F32),
        compiler_params=_cparams(("parallel", "parallel", "arbitrary")),
        name="diff_attention",
    )(q, kv, kv, lamv, subln_g.reshape(1, 2 * DIFF_HALF))


def _dsa_keys(s_all, wi, causal, tq):
    score = jnp.zeros((tq, s_all.shape[1]), F32)
    for h in range(IDX_HEADS):
        score = score + wi[:, h:h + 1] * jnp.maximum(s_all[h * tq:(h + 1) * tq], 0.0)
    bits = pltpu.bitcast(score, I32)
    key = jnp.where(bits < 0, bits ^ 0x7FFFFFFF, bits)
    key = jnp.where(score == 0.0, 0, key)
    return jnp.where(causal, jnp.where(score > 0.5 * NEG, key, INT_MIN), INT_MIN)


def _dsa_threshold(read, n_tiles, kmax, *, tq, tk, n_sel, transposed, unroll):
    lanes = tk // LANE

    def count(*bounds):
        def f(j, accs):
            keys = read(j)
            out = []
            for bound, acc in zip(bounds, accs):
                hit = jnp.where(keys >= bound, 1.0, 0.0)
                if transposed:
                    acc = acc + jnp.sum(hit.reshape(tk // COUNT_ROWS, COUNT_ROWS, tq), axis=0)
                else:
                    for c in range(lanes):
                        acc = acc + hit[:, c * LANE:(c + 1) * LANE]
                out.append(acc)
            return tuple(out)
        acc0 = jnp.zeros((COUNT_ROWS, tq) if transposed else (tq, LANE), F32)
        accs = lax.fori_loop(0, n_tiles, f, tuple(acc0 for _ in bounds), unroll=unroll)
        return [jnp.sum(acc, axis=0 if transposed else -1, keepdims=True) for acc in accs]

    k_f = float(n_sel)
    probe = jnp.maximum(kmax - (PROBE_BINADES << 23), 1)
    c_adm, c_nn, c_pos, c_probe = count(INT_MIN + 1, 0, 1, probe)
    few, pos, zero, high = c_adm < k_f, c_pos >= k_f, c_nn >= k_f, c_probe >= k_f
    lo0 = jnp.where(pos, jnp.where(high, probe, 1), jnp.where(zero, 0, INT_MIN))
    hi0 = jnp.where(pos, jnp.where(high, kmax, probe - 1), jnp.where(zero, 0, jnp.where(few, INT_MIN, -1)))

    def unfinished(lo_hi):
        lo, hi = lo_hi
        return jnp.max(jnp.where(lo < hi, 1.0, 0.0)) > 0.0

    def bisect(lo_hi):
        lo, hi = lo_hi
        mid = (lo >> 1) + (hi >> 1) + ((lo | hi) & 1)
        cnt, = count(mid)
        lo = jnp.where(cnt >= k_f, mid, lo)
        hi = jnp.where(cnt > k_f, hi, jnp.where(cnt == k_f, mid, mid - 1))
        return lo, hi

    thr, _ = lax.while_loop(unfinished, lambda s: bisect(bisect(s)), (lo0, hi0))
    thr = jnp.maximum(thr, INT_MIN + 1)
    n_gt, = count(thr + 1)
    return thr, k_f - n_gt


def _dsa_bias(key, thr, need, seen, tri):
    tied = key == thr
    tied_b = jnp.where(tied, 1.0, 0.0).astype(BF16)
    t = tri.shape[0]
    ranks = []
    for c in range(key.shape[1] // t):
        r = _dot(tied_b[:, c * t:(c + 1) * t], tri) + seen
        ranks.append(r)
        seen = r[:, t - 1:t]
    rank = jnp.concatenate(ranks, axis=-1)
    return jnp.where(key > thr, 0.0, jnp.where(tied, jnp.where(rank <= need, 0.0, NEG), NEG)), seen


def _dsa_kernel(q_ref, qi_ref, kw_ref, kidx_ref, kv_ref, tri_ref, o_ref, key_ref, keyt_ref, *,
                tq, tk, q_pos0, n_sel, one_block):
    G = DSA_KV_HEADS
    hpg = DSA_HEADS // G
    transposed = tq % LANE == 0
    q0 = q_pos0 + (0 if one_block else pl.program_id(1) * tq)
    qpos = q0 + lax.broadcasted_iota(I32, (tq, 1), 0)
    n_tiles = (q0 + tq - 1) // tk + 1
    lanes = tk // LANE

    qidx = _pad_lanes(_stack_heads(qi_ref[0] * (IDX_DIM ** -0.5), IDX_HEADS, IDX_DIM)).astype(BF16)
    wi = kw_ref[0][:, IDX_DIM:IDX_DIM + IDX_HEADS] * (IDX_HEADS ** -0.5)

    def score_step(j, kmax):
        k0 = pl.multiple_of(j * tk, tk)
        s_all = _dot_nt(qidx, kidx_ref[0, 0, pl.ds(k0, tk), :])
        kpos = k0 + lax.broadcasted_iota(I32, (1, tk), 1)
        key = _dsa_keys(s_all, wi, kpos <= qpos, tq)
        key_ref[j] = key
        if not transposed:
            return jnp.maximum(kmax, jnp.max(key, axis=-1, keepdims=True))
        key_t = key.T
        keyt_ref[j] = key_t
        return jnp.maximum(kmax, jnp.max(key_t, axis=0, keepdims=True))

    kmax = lax.fori_loop(0, n_tiles, score_step, jnp.full((1, tq) if transposed else (tq, 1), INT_MIN, I32))
    thr, need = _dsa_threshold((lambda j: keyt_ref[j]) if transposed else (lambda j: key_ref[j]), n_tiles, kmax,
                               tq=tq, tk=tk, n_sel=n_sel, transposed=transposed, unroll=one_block)

    def along_rows(v):
        rep = jnp.broadcast_to(v, (LANE, tq)).T
        return jnp.concatenate([rep] * lanes, axis=-1)

    if transposed:
        thr = along_rows(thr)
        need = along_rows(need)

    q = q_ref[0] * (HEAD_DIM ** -0.5 * LOG2E)
    qs = [_pad_lanes(_stack_heads(q[:, g * hpg * HEAD_DIM:(g + 1) * hpg * HEAD_DIM], hpg, HEAD_DIM)).astype(BF16)
          for g in range(G)]
    rows = hpg * tq

    def att_step(k0, w, diagonal, carry):
        seen, flash = carry
        cut = (lambda x: x[:, :w]) if x_is_wide else (lambda x: x)
        bias, seen = _dsa_bias(key_ref[k0 // tk, :, 0:w], cut(thr), cut(need), seen, tri_ref[...])
        out = []
        for g in range(G):
            s = _dot_nt(qs[g], kv_ref[0, g, pl.ds(k0, w), :])
            out.append(_flash_tile(s, bias, kv_ref[0, G + g, pl.ds(k0, w), :], flash[g], hpg, tq, True))
        return seen, tuple(out)

    x_is_wide = transposed
    init = (jnp.zeros((tq, 1), F32), tuple(_flash_init(rows, LANE) for _ in range(G)))
    if tk % tq == 0 and tk // tq <= MAX_SWEEP_BRANCHES:
        _, res = _causal_sweep(att_step, init, q0, tq, tk)
    else:
        _, res = lax.fori_loop(0, n_tiles, lambda j, c: att_step(pl.multiple_of(j * tk, tk), tk, True, c), init)
    outs = []
    for g in range(G):
        o = _flash_out(res[g][2])
        outs.extend(o[h * tq:(h + 1) * tq] for h in range(hpg))
    o_ref[0] = jnp.concatenate(outs, axis=-1)


def _dsa_attention(q, qi, kw, kidx, kv, *, tq, q_pos0, lk):
    bk, lq, _ = q.shape
    lk_pad = kv.shape[2]
    tk = _kv_tile(tq, lk_pad)
    n_sel = min(DSA_TOPK_MAX, lk // 4)
    assert q_pos0 + lq <= lk_pad and tk >= n_sel
    kern = functools.partial(_dsa_kernel, tq=tq, tk=tk, q_pos0=q_pos0, n_sel=n_sel, one_block=lq == tq)
    assert tk % TRI_TILE == 0
    tri = jnp.triu(jnp.ones((TRI_TILE, TRI_TILE), BF16))
    return pl.pallas_call(
        kern,
        grid=(bk, lq // tq),
        in_specs=[pl.BlockSpec((1, tq, DSA_HEADS * HEAD_DIM), lambda b, i: (b, i, 0)),
                  pl.BlockSpec((1, tq, IDX_HEADS * IDX_DIM), lambda b, i: (b, i, 0)),
                  pl.BlockSpec((1, tq, LANE), lambda b, i: (b, i, 0)),
                  pl.BlockSpec((1, 1, lk_pad, LANE), lambda b, i: (b, 0, 0, 0)),
                  pl.BlockSpec((1, 4, lk_pad, LANE), lambda b, i: (b, 0, 0, 0)),
                  pl.BlockSpec((TRI_TILE, TRI_TILE), lambda b, i: (0, 0))],
        out_specs=pl.BlockSpec((1, tq, DSA_HEADS * HEAD_DIM), lambda b, i: (b, i, 0)),
        out_shape=jax.ShapeDtypeStruct((bk, lq, DSA_HEADS * HEAD_DIM), F32),
        scratch_shapes=[pltpu.VMEM((lk_pad // tk, tq, tk), I32),
                        pltpu.VMEM((lk_pad // tk, tk, tq) if tq % LANE == 0 else (8, LANE), I32)],
        compiler_params=_cparams(("parallel", "arbitrary")),
        name="dsa_attention",
    )(q, qi, kw, kidx, kv, tri)


def _dsa_decode_kernel(pt_ref, q_ref, qi_ref, kw_ref, newi_ref, newkv_ref, tri_ref, *refs,
                       pp, n_steps, tq, n_new, n_sel):
    ipage_refs = refs[:pp]
    kvpage_refs = refs[pp:2 * pp]
    o_ref, key_ref, thr_ref, need_ref, seen_ref, m_ref, l_ref, acc_ref = refs[2 * pp:]
    G = DSA_KV_HEADS
    hpg = DSA_HEADS // G
    gw = hpg * HEAD_DIM
    j = pl.program_id(1)
    keys = pp * PAGE_SIZE
    qidx = _stack_heads(qi_ref[0] * (IDX_DIM ** -0.5), IDX_HEADS, IDX_DIM).astype(BF16)
    wi = kw_ref[0][:, IDX_DIM:IDX_DIM + IDX_HEADS] * (IDX_HEADS ** -0.5)
    q_all = q_ref[0] * (HEAD_DIM ** -0.5 * LOG2E)
    row = lax.broadcasted_iota(I32, (tq, 1), 0)
    col = lax.broadcasted_iota(I32, (1, LANE), 1)
    own = (col <= row) & (col < n_new)

    def queries(g):
        return _stack_heads(q_all[:, g * gw:(g + 1) * gw], hpg, HEAD_DIM).astype(BF16)

    @pl.when(j < n_steps)
    def _():
        k_t = jnp.concatenate([r[0, 0] for r in ipage_refs], axis=-1).astype(BF16)
        key_ref[j] = _dsa_keys(_dot(qidx, k_t), wi, jnp.full((tq, keys), True), tq)

    @pl.when(j == n_steps - 1)
    def _():
        pad = jnp.zeros((LANE - tq, IDX_DIM), F32)
        k_new = jnp.concatenate([newi_ref[0], pad], axis=0).astype(BF16)
        key_new = _dsa_keys(_dot_nt(qidx, k_new), wi, own, tq)
        key_ref[n_steps] = jnp.concatenate([key_new, jnp.full((tq, keys - LANE), INT_MIN, I32)], axis=-1)
        kmax = jnp.full((tq, 1), INT_MIN, I32)
        for t in range(n_steps + 1):
            kmax = jnp.maximum(kmax, jnp.max(key_ref[t], axis=-1, keepdims=True))
        thr, need = _dsa_threshold(lambda t: key_ref[t], n_steps + 1, kmax, tq=tq, tk=keys, n_sel=n_sel,
                                   transposed=False, unroll=True)
        thr_ref[...] = thr
        need_ref[...] = need
        seen_ref[...] = jnp.zeros(seen_ref.shape, F32)
        m_ref[...] = jnp.full(m_ref.shape, NEG, F32)
        l_ref[...] = jnp.zeros(l_ref.shape, F32)
        acc_ref[...] = jnp.zeros(acc_ref.shape, F32)

    def update(g, s, bias, v, v_t):
        carry = (m_ref[g], l_ref[g], acc_ref[g])
        m, l, acc = _flash_tile(s, bias, v, carry, hpg, tq, False, v_t=v_t)
        m_ref[g] = m
        l_ref[g] = l
        acc_ref[g] = acc

    @pl.when(j >= n_steps)
    def _():
        bias, seen = _dsa_bias(key_ref[j - n_steps], thr_ref[...], need_ref[...], seen_ref[...], tri_ref[...])
        seen_ref[...] = seen
        for g in range(G):
            k_t = jnp.concatenate([r[0, g] for r in kvpage_refs], axis=-1).astype(BF16)
            v_t = jnp.concatenate([r[0, G + g] for r in kvpage_refs], axis=-1).astype(BF16)
            update(g, _dot(queries(g), k_t), bias, v_t, True)

    @pl.when(j == 2 * n_steps - 1)
    def _():
        new = newkv_ref[0]
        pad = jnp.zeros((LANE - tq, HEAD_DIM), F32)
        bias, _ = _dsa_bias(key_ref[n_steps][:, :LANE], thr_ref[...], need_ref[...], seen_ref[...],
                            tri_ref[0:LANE, 0:LANE])
        outs = []
        for g in range(G):
            k = jnp.concatenate([new[:, g * HEAD_DIM:(g + 1) * HEAD_DIM], pad], axis=0).astype(BF16)
            v = jnp.concatenate([new[:, (G + g) * HEAD_DIM:(G + g + 1) * HEAD_DIM], pad], axis=0).astype(BF16)
            update(g, _dot_nt(queries(g), k), bias, v, False)
            o = acc_ref[g] * (1.0 / jnp.maximum(l_ref[g], 1e-30))
            outs.extend(o[h * tq:(h + 1) * tq] for h in range(hpg))
        o_ref[0] = jnp.concatenate(outs, axis=-1)


def _dsa_decode(q, qi, kw, kidx_pool, kv_pool, page_table, new_kidx, new_kv, *, n_new):
    bk, tq, qw = q.shape
    n_pages = page_table.shape[1]
    pp = min(PAGES_PER_STEP, n_pages)
    assert n_pages % pp == 0 and n_new <= tq
    n_steps = n_pages // pp
    keys = pp * PAGE_SIZE
    lk = n_pages * PAGE_SIZE + n_new
    n_sel = min(DSA_TOPK_MAX, lk // 4)
    G = DSA_KV_HEADS
    rows = (DSA_HEADS // G) * tq
    tri = jnp.triu(jnp.ones((TRI_TILE, TRI_TILE), BF16))
    assert keys % TRI_TILE == 0 and keys >= n_sel

    def ipage_map(i):
        return lambda b, j, pt: (pt[b, jnp.minimum(j, n_steps - 1) * pp + i], 0, 0, 0)

    def kvpage_map(i):
        return lambda b, j, pt: (pt[b, jnp.maximum(j - n_steps, 0) * pp + i], 0, 0, 0)

    const3 = lambda b, j, pt: (b, 0, 0)
    in_specs = [pl.BlockSpec((1, tq, qw), const3),
                pl.BlockSpec((1, tq, qi.shape[-1]), const3),
                pl.BlockSpec((1, tq, LANE), const3),
                pl.BlockSpec((1, tq, IDX_DIM), const3),
                pl.BlockSpec((1, tq, new_kv.shape[-1]), const3),
                pl.BlockSpec((TRI_TILE, TRI_TILE), lambda b, j, pt: (0, 0))]
    in_specs += [pl.BlockSpec((1,) + kidx_pool.shape[1:], ipage_map(i)) for i in range(pp)]
    in_specs += [pl.BlockSpec((1,) + kv_pool.shape[1:], kvpage_map(i)) for i in range(pp)]
    kern = functools.partial(_dsa_decode_kernel, pp=pp, n_steps=n_steps, tq=tq, n_new=n_new, n_sel=n_sel)
    return pl.pallas_call(
        kern,
        grid_spec=pltpu.PrefetchScalarGridSpec(
            num_scalar_prefetch=1, grid=(bk, 2 * n_steps), in_specs=in_specs,
            out_specs=pl.BlockSpec((1, tq, qw), const3),
            scratch_shapes=[pltpu.VMEM((n_steps + 1, tq, keys), I32),
                            pltpu.VMEM((tq, 1), I32), pltpu.VMEM((tq, 1), F32), pltpu.VMEM((tq, 1), F32),
                            pltpu.VMEM((G, rows, 1), F32), pltpu.VMEM((G, rows, 1), F32),
                            pltpu.VMEM((G, rows, HEAD_DIM), F32)]),
        out_shape=jax.ShapeDtypeStruct((bk, tq, qw), F32),
        compiler_params=_cparams(("parallel", "arbitrary")),
        name="dsa_decode",
    )(page_table, q, qi, kw, new_kidx, new_kv, tri, *([kidx_pool] * pp), *([kv_pool] * pp))


L0_SIZES = (512, 768, 24, 512, 512, 512)
L1_SIZES = (512, 256, 256, 512, 512, 128, 128, 256, 64, 4, 512)


def _l0_weight(w_in):
    d = w_in.shape[0]
    q, kv6, gl, z_a, x_b, z_b = jnp.split(w_in, np.cumsum(L0_SIZES)[:-1].tolist(), axis=1)
    pad = jnp.zeros((d, LANE - 12), w_in.dtype)
    w = jnp.concatenate([q, kv6, z_a, x_b, z_b, gl[:, :12], pad, gl[:, 12:], pad], axis=1)
    segs = [(0, 512), (512, 1024), (1024, 1280), (1280, 1792), (1792, 2304), (2304, 2816), (2816, 3072)]
    return w.astype(BF16), segs


def _l1_weight(w_in):
    d = w_in.shape[0]
    qc, kc, vc, z_c, qd, kd, vd, qi, ki, wi, z_d = jnp.split(w_in, np.cumsum(L1_SIZES)[:-1].tolist(), axis=1)
    pad = jnp.zeros((d, LANE - IDX_DIM - IDX_HEADS), w_in.dtype)
    w = jnp.concatenate([qc, kc, vc, z_c, qd, kd, vd, qi, z_d, ki, wi, pad], axis=1)
    segs = [(0, 512), (512, 1024), (1024, 1536), (1536, 2048), (2048, 2304), (2304, 2560), (2560, 3072),
            (3072, 3200)]
    return w.astype(BF16), segs


def _pad_rows(x, n):
    return jnp.pad(x, ((0, 0), (0, n - x.shape[1]), (0, 0)))


def _cols_pool(pool):
    npool, ps = pool.shape[:2]
    cw = pool.shape[-1]
    perm = (0,) + tuple(range(2, pool.ndim)) + (1,)
    return jnp.transpose(pool, perm).reshape(npool, -1, cw, ps)


def _layer0(x, mod, past, w, *, tl, tq):
    (norm_g, w_in, cmp_wk, cmp_wv, conv_w, conv_b, lru_wr, lru_br, lru_wi, lru_bi, lru_lambda, w_out) = w
    shift, scale, gate = mod
    b, l, d = x.shape
    w_p, segs = _l0_weight(w_in)
    flat = shift.shape[1] != 1
    xin = x.reshape(1, b * l, d) if flat else x
    q, kvp, kvw, z_a, x_b, z_b, gates = _project(xin, norm_g, shift, scale, w_p, segs, 6, tl)
    if flat:
        q, kvp, kvw, z_a, x_b, z_b, gates = (t.reshape(b, l, -1) for t in (q, kvp, kvw, z_a, x_b, z_b, gates))
    wk2 = jnp.concatenate([cmp_wk, cmp_wk], axis=0)
    wv2 = jnp.concatenate([cmp_wv, cmp_wv], axis=0)
    w2 = jnp.stack([wk2, wk2, wv2, wv2], axis=0)
    nsa = dict(cw=HEAD_DIM, c_all=8, n_out=4, n_cmp=4, cmp_w=w2)
    lq = _round_up(l, 8)
    if past is None:
        ppool, ptable = _identity_pages(kvp)
        ksel, cmp = _gather_chunks(ppool, "rows", ptable, None, **nsa)
        wpool, wtable = _identity_pages(kvw)
        kwin = _gather_chunks(wpool, "rows", wtable, None, cw=HEAD_DIM, c_all=4, n_out=4)
        kv_win = kvw
        hist = jnp.zeros((b, CONV_WIDTH - 1, x_b.shape[-1]), F32)
        h0 = jnp.zeros((b, x_b.shape[-1]), F32)
        nsp = _round_up(cmp.shape[3], LANE)
        cmp = jnp.pad(cmp, ((0, 0), (0, 0), (0, 0), (0, nsp - cmp.shape[3]), (0, 0)))
        cmp = cmp.reshape(b, 2, 2, 2, nsp, HEAD_DIM).transpose(0, 2, 1, 3, 4, 5).reshape(b, 4, 2, nsp, HEAD_DIM)
        o_a = _nsa_attention(_pad_rows(q, lq), _pad_rows(gates, lq), cmp, ksel, kwin,
                             tq=min(tq, lq), q_pos0=0, lk=l, win_pos0=0)[:, :l]
    else:
        pool, table, win_buf, hist, h0 = past
        assert lq <= tq
        kv_win = jnp.concatenate([win_buf.reshape(b, win_buf.shape[1], -1), kvw], axis=1)
        lw_pad = _win_span(lq)
        assert kv_win.shape[1] <= lw_pad and lw_pad % PAGE_SIZE == 0
        wpool, wtable = _identity_pages(_pad_rows(kv_win, lw_pad))
        kwin = _gather_chunks(wpool, "rows", wtable, None, cw=HEAD_DIM, c_all=4, n_out=4)
        o_a = _nsa_decode(_pad_rows(q, lq), _pad_rows(gates, lq), _cols_pool(pool), table, _pad_rows(kvp, lq),
                          w2, kwin, n_new=l, win_pos0=table.shape[1] * PAGE_SIZE - win_buf.shape[1])[:, :l]
    o_b, h_last = _conv_rglru(_pad_rows(x_b, lq), hist, h0, conv_w, conv_b, lru_wr, lru_br, lru_wi, lru_bi,
                              lru_lambda, tl=min(256, lq), n_valid=l)
    o_b = o_b[:, :l]
    fl = (lambda t: t.reshape(1, b * l, -1)) if flat else (lambda t: t)
    x_new = _out_project(fl(o_a), fl(z_a), fl(o_b), fl(z_b), xin, gate, w_out.astype(BF16),
                         jnp.ones((d,), F32), tl=tl, final_norm=False).reshape(b, l, d)
    win_keep = min(NSA_WINDOW, kv_win.shape[1])
    conv_src = jnp.concatenate([hist, x_b], axis=1) if l < CONV_WIDTH - 1 else x_b
    states = (kvp.reshape(b, l, 4, NSA_KV_HEADS, HEAD_DIM),
              kv_win[:, -win_keep:].reshape(b, win_keep, 2, NSA_KV_HEADS, HEAD_DIM),
              conv_src[:, -(CONV_WIDTH - 1):], h_last)
    return x_new, states


def _layer1(x, mod, past, w, final_g, *, tl, tq):
    (norm_g, w_in, lam_q1, lam_k1, lam_q2, lam_k2, subln_g, w_out) = w
    shift, scale, gate = mod
    b, l, d = x.shape
    w_p, segs = _l1_weight(w_in)
    flat = shift.shape[1] != 1
    xin = x.reshape(1, b * l, d) if flat else x
    qc, kvc, z_c, qd, kvd, qi, z_d, kiw = _project(xin, norm_g, shift, scale, w_p, segs, -1, tl)
    if flat:
        qc, kvc, z_c, qd, kvd, qi, z_d, kiw = (t.reshape(b, l, -1) for t in (qc, kvc, z_c, qd, kvd, qi, z_d, kiw))
    lq = _round_up(l, 8)
    diff_a = dict(cw=2 * DIFF_HALF, c_all=4, n_out=4)
    dsa_a = dict(cw=HEAD_DIM, c_all=4, n_out=4)
    kidx_a = dict(cw=IDX_DIM, c_all=1, n_out=1)
    if past is None:
        dpool, dtable = _identity_pages(kvc)
        diff_kv = _gather_chunks(dpool, "rows", dtable, None, **diff_a)
        spool, stable = _identity_pages(kvd)
        dsa_kv = _gather_chunks(spool, "rows", stable, None, **dsa_a)
        ipool, itable = _identity_pages(kiw)
        kidx = _gather_chunks(ipool, "rows", itable, None, **kidx_a)
        q_pos0, lk = 0, l
    else:
        diff_pool, dsa_pool, kidx_pool, table = past
        past_len = table.shape[1] * PAGE_SIZE
        diff_rows = diff_pool.reshape(diff_pool.shape[0], PAGE_SIZE * 4, 2 * DIFF_HALF)
    lamv = jnp.stack([lam_q1, lam_k1, lam_q2, lam_k2], axis=0)
    if past is None:
        o_c = _diff_attention(_pad_rows(qc, lq), diff_kv, lamv, subln_g, tq=min(tq, lq), q_pos0=0)[:, :l]
        o_d = _dsa_attention(_pad_rows(qd, lq), _pad_rows(qi, lq), _pad_rows(kiw, lq), kidx, dsa_kv,
                             tq=min(tq, lq), q_pos0=0, lk=l)[:, :l]
    else:
        assert lq <= tq
        o_c = _diff_decode(_pad_rows(qc, lq), diff_rows, table, _pad_rows(kvc, lq), lamv, subln_g, n_new=l)[:, :l]
        o_d = _dsa_decode(_pad_rows(qd, lq), _pad_rows(qi, lq), _pad_rows(kiw, lq), _cols_pool(kidx_pool),
                          _cols_pool(dsa_pool), table, _pad_rows(kiw[:, :, :IDX_DIM], lq), _pad_rows(kvd, lq),
                          n_new=l)[:, :l]
    fl = (lambda t: t.reshape(1, b * l, -1)) if flat else (lambda t: t)
    y = _out_project(fl(o_c), fl(z_c), fl(o_d), fl(z_d), xin, gate, w_out.astype(BF16), final_g,
                     tl=tl, final_norm=True).reshape(b, l, d)
    states = (kvc.reshape(b, l, 2, DIFF_KV_HEADS, 2 * DIFF_HALF),
              kvd.reshape(b, l, 2, DSA_KV_HEADS, HEAD_DIM), kiw[:, :, :IDX_DIM])
    return y, states


def kernel(x_prompt, x_sample, cache_l0_nsa_kv, state_l0_win_kv, state_l0_conv, state_l0_lru_h,
           cache_l1_diff_kv, cache_l1_dsa_kv, cache_l1_dsa_kidx, page_table, c_prompt, c_sample,
           l0_norm_g, l0_ada_w, l0_ada_b, l0_w_in, l0_cmp_wk, l0_cmp_wv, l0_conv_w, l0_conv_b,
           l0_lru_wr, l0_lru_br, l0_lru_wi, l0_lru_bi, l0_lru_lambda, l0_w_out,
           l1_norm_g, l1_ada_w, l1_ada_b, l1_w_in, l1_lam_q1, l1_lam_k1, l1_lam_q2, l1_lam_k2,
           l1_subln_g, l1_w_out, final_norm_g):
    bp, lp, d = x_prompt.shape
    bs, ls, _ = x_sample.shape
    c_all = jnp.concatenate([c_prompt, c_sample], axis=0)

    def mods(ada_w, ada_b):
        m = _modulation(c_all, ada_w, ada_b)
        mp = tuple(t[:, None] for t in jnp.split(m[:bp], 3, axis=-1))
        ms = tuple(jnp.repeat(t, ls, axis=0)[None] for t in jnp.split(m[bp:], 3, axis=-1))
        return mp, ms

    tl_p = min(512, lp)
    tl_s = bs * ls
    tq = min(Q_TILE, lp)
    w0 = (l0_norm_g, l0_w_in, l0_cmp_wk, l0_cmp_wv, l0_conv_w, l0_conv_b, l0_lru_wr, l0_lru_br,
          l0_lru_wi, l0_lru_bi, l0_lru_lambda, l0_w_out)
    mp0, ms0 = mods(l0_ada_w, l0_ada_b)
    xp, (nsa_kv_p, win_p, conv_p, h_p) = _layer0(x_prompt, mp0, None, w0, tl=tl_p, tq=tq)
    xs, (nsa_kv_s, win_s, conv_s, h_s) = _layer0(
        x_sample, ms0, (cache_l0_nsa_kv, page_table, state_l0_win_kv, state_l0_conv, state_l0_lru_h), w0,
        tl=tl_s, tq=tq)
    w1 = (l1_norm_g, l1_w_in, l1_lam_q1, l1_lam_k1, l1_lam_q2, l1_lam_k2, l1_subln_g, l1_w_out)
    mp1, ms1 = mods(l1_ada_w, l1_ada_b)
    y_p, (diff_kv_p, dsa_kv_p, kidx_p) = _layer1(xp, mp1, None, w1, final_norm_g, tl=tl_p, tq=tq)
    y_s, (diff_kv_s, dsa_kv_s, kidx_s) = _layer1(
        xs, ms1, (cache_l1_diff_kv, cache_l1_dsa_kv, cache_l1_dsa_kidx, page_table), w1, final_norm_g,
        tl=tl_s, tq=tq)
    return (y_p, y_s, nsa_kv_p, nsa_kv_s, win_p, win_s, conv_p, conv_s, h_p, h_s,
            diff_kv_p, diff_kv_s, dsa_kv_p, dsa_kv_s, kidx_p, kidx_s)
```

```python
import functools
import math

import jax
import jax.numpy as jnp
import numpy as np
from jax import lax
from jax.experimental import pallas as pl
from jax.experimental.pallas import tpu as pltpu

F32 = jnp.float32
BF16 = jnp.bfloat16
I32 = jnp.int32

PAGE_SIZE = 128
HEAD_DIM = 64
NSA_HEADS = 8
NSA_KV_HEADS = 2
NSA_CMP_BLOCK = 32
NSA_SEL_BLOCK = 64
NSA_TOPN = 16
NSA_WINDOW = 512
FORCE_SCORE = 1e4
LRU_BLOCKS = 8
LRU_C = 8.0
CONV_WIDTH = 4
DIFF_HALF = 64
DIFF_HEADS = 4
DIFF_KV_HEADS = 2
DIFF_LAMBDA_INIT = 0.8 - 0.6 * math.exp(-0.3 * 1)
DSA_HEADS = 8
DSA_KV_HEADS = 2
IDX_HEADS = 4
IDX_DIM = 64
DSA_TOPK_MAX = 256
NORM_EPS = 1e-6
NEG = -1e30
REMOVED = -3e38
INT_MIN = -2 ** 31
LOG2E = math.log2(math.e)

LANE = 128
VMEM_LIMIT = 56 * 1024 * 1024
KV_TILE = 1024
TRI_TILE = 256
Q_TILE = 256
PROBE_BINADES = 3
COUNT_ROWS = 64
MAX_SWEEP_BRANCHES = 8
PAGES_PER_STEP = 32
RELAYOUT_ROWS = 4096


def _kv_tile(lk):
    t = min(KV_TILE, lk)
    assert lk % t == 0 and t % TRI_TILE == 0
    return t


def _win_span(tq):
    return _round_up(NSA_WINDOW + tq, LANE)


def _cparams(sem):
    return pltpu.CompilerParams(dimension_semantics=sem, vmem_limit_bytes=VMEM_LIMIT)


def _dot(a, b):
    return jnp.dot(a, b, preferred_element_type=F32)


def _dot_nt(a, b):
    return lax.dot_general(a, b, (((1,), (1,)), ((), ())), preferred_element_type=F32)


def _round_up(x, m):
    return (x + m - 1) // m * m


def _mod_kernel(c_ref, w_ref, b_ref, o_ref):
    o_ref[...] = jnp.dot(c_ref[...], w_ref[...], preferred_element_type=F32,
                         precision=lax.Precision.HIGHEST) + b_ref[...]


def _modulation(c, w, b):
    bc, d = c.shape
    n = w.shape[1]
    tn = 512
    return pl.pallas_call(
        _mod_kernel,
        grid=(n // tn,),
        in_specs=[pl.BlockSpec((bc, d), lambda j: (0, 0)),
                  pl.BlockSpec((d, tn), lambda j: (0, j)),
                  pl.BlockSpec((1, tn), lambda j: (0, j))],
        out_specs=pl.BlockSpec((bc, tn), lambda j: (0, j)),
        out_shape=jax.ShapeDtypeStruct((bc, n), F32),
        compiler_params=_cparams(("arbitrary",)),
        name="modulation",
    )(c, w, b.reshape(1, n))


def _proj_kernel(x_ref, g_ref, sh_ref, sc_ref, w_ref, *o_refs, segs, sigmoid_seg):
    x = x_ref[0]
    y = x * lax.rsqrt(jnp.mean(x * x, axis=-1, keepdims=True) + NORM_EPS)
    h = (y * g_ref[...]) * (1.0 + sc_ref[0]) + sh_ref[0]
    hb = h.astype(BF16)
    for i, ((a, b), o_ref) in enumerate(zip(segs, o_refs)):
        r = _dot(hb, w_ref[:, a:b])
        if i == sigmoid_seg:
            r = jax.nn.sigmoid(r)
        o_ref[0] = r


def _project(x, g, shift, scale, w, segs, sigmoid_seg, tl):
    b, l, d = x.shape
    ts = shift.shape[1]
    tm = 1 if ts == 1 else tl
    mod_map = (lambda bi, li: (bi, 0, 0)) if ts == 1 else (lambda bi, li: (bi, li, 0))
    p = w.shape[1]
    kern = functools.partial(_proj_kernel, segs=tuple(segs), sigmoid_seg=sigmoid_seg)
    return pl.pallas_call(
        kern,
        grid=(b, l // tl),
        in_specs=[pl.BlockSpec((1, tl, d), lambda bi, li: (bi, li, 0)),
                  pl.BlockSpec((1, d), lambda bi, li: (0, 0)),
                  pl.BlockSpec((1, tm, d), mod_map),
                  pl.BlockSpec((1, tm, d), mod_map),
                  pl.BlockSpec((d, p), lambda bi, li: (0, 0))],
        out_specs=[pl.BlockSpec((1, tl, e - a), lambda bi, li: (bi, li, 0)) for a, e in segs],
        out_shape=[jax.ShapeDtypeStruct((b, l, e - a), F32) for a, e in segs],
        compiler_params=_cparams(("parallel", "arbitrary")),
        name="norm_mod_project",
    )(x, g.reshape(1, d), shift, scale, w)


def _chunk_kernel(x_ref, *refs, cw, n_cmp, n_out):
    w2_ref = refs[0] if n_cmp else None
    out_ref = refs[1 if n_cmp else 0]
    cmp_ref = refs[2] if n_cmp else None
    rows = x_ref.shape[1]
    ones_col = jnp.where(lax.broadcasted_iota(I32, (rows, LANE - cw), 1) == 0, 1.0, 0.0) if cw < LANE else None
    for c in range(n_cmp + n_out):
        x = x_ref[0, :, c * cw:(c + 1) * cw]
        if c < n_cmp:
            prod = x.reshape(rows // NSA_SEL_BLOCK, NSA_SEL_BLOCK, cw) * w2_ref[c][None]
            cmp_ref[0, c, 0] = jnp.sum(prod[:, :NSA_CMP_BLOCK], axis=1)
            cmp_ref[0, c, 1] = jnp.sum(prod[:, NSA_CMP_BLOCK:], axis=1)
        else:
            if ones_col is not None:
                x = jnp.concatenate([x, ones_col], axis=-1)
            out_ref[0, c - n_cmp] = x.astype(BF16)


def _split_chunks(x, *, cw, n_out, n_cmp=0, cmp_w=None):
    b, l, w = x.shape
    rows = min(RELAYOUT_ROWS, l)
    assert l % rows == 0 and rows % NSA_SEL_BLOCK == 0
    in_specs = [pl.BlockSpec((1, rows, w), lambda bi, j: (bi, j, 0))]
    args = [x]
    if n_cmp:
        in_specs.append(pl.BlockSpec(cmp_w.shape, lambda bi, j: (0, 0, 0)))
        args.append(cmp_w)
    out_specs = [pl.BlockSpec((1, n_out, rows, LANE), lambda bi, j: (bi, 0, j, 0))]
    out_shape = [jax.ShapeDtypeStruct((b, n_out, l, LANE), BF16)]
    if n_cmp:
        nb = rows // NSA_SEL_BLOCK
        out_specs.append(pl.BlockSpec((1, n_cmp, 2, nb, HEAD_DIM), lambda bi, j: (bi, 0, 0, j, 0)))
        out_shape.append(jax.ShapeDtypeStruct((b, n_cmp, 2, l // NSA_SEL_BLOCK, HEAD_DIM), F32))
    outs = pl.pallas_call(
        functools.partial(_chunk_kernel, cw=cw, n_cmp=n_cmp, n_out=n_out),
        grid=(b, l // rows),
        in_specs=in_specs, out_specs=out_specs, out_shape=out_shape,
        compiler_params=_cparams(("parallel", "arbitrary")),
        name="split_chunks",
    )(*args)
    return outs if n_cmp else outs[0]


def _flash_tile(s_all, bias, v_tile, carry, n_heads, tq, l_in_acc, v_t=False):
    m, l, acc = carry
    s = s_all
    if bias is not None:
        s = jnp.concatenate([s_all[h * tq:(h + 1) * tq] + bias for h in range(n_heads)], axis=0)
    m_new = jnp.maximum(m, jnp.max(s, axis=-1, keepdims=True))
    p = jnp.exp2(s - m_new)
    alpha = jnp.exp2(m - m_new)
    acc = alpha * acc + (_dot_nt if v_t else _dot)(p.astype(BF16), v_tile)
    if not l_in_acc:
        l = alpha * l + jnp.sum(p, axis=-1, keepdims=True)
    return m_new, l, acc


def _causal_sweep(step, carry, q0, tq, tk):
    assert tk % tq == 0 and tk // tq <= MAX_SWEEP_BRANCHES
    n_full = q0 // tk
    carry = lax.fori_loop(0, n_full, lambda j, c: step(pl.multiple_of(j * tk, tk), tk, False, c), carry)
    k0 = pl.multiple_of(n_full * tk, tk)
    branches = [functools.partial(step, k0, w, True) for w in range(tq, tk + tq, tq)]
    return lax.switch((q0 - k0) // tq, branches, carry)


def _flash_init(rows, dv):
    return (jnp.full((rows, 1), NEG, F32), jnp.zeros((rows, 1), F32), jnp.zeros((rows, dv), F32))


def _flash_out(acc):
    return acc[:, :HEAD_DIM] * (1.0 / jnp.maximum(acc[:, HEAD_DIM:HEAD_DIM + 1], 1e-30))


def _stack_heads(q, n, width):
    return jnp.concatenate([q[:, h * width:(h + 1) * width] for h in range(n)], axis=0)


def _pad_lanes(x):
    return jnp.concatenate([x, jnp.zeros((x.shape[0], LANE - x.shape[1]), x.dtype)], axis=-1)


def _nsa_prologue(qg, kw, vw, cmp4, *, q0, tq, nsp, n_top, win_pos0, start):
    hpg = NSA_HEADS // NSA_KV_HEADS
    rows = hpg * tq
    tqp = max(tq, LANE)
    qpos = q0 + lax.broadcasted_iota(I32, (tq, 1), 0)
    qpos_r = jnp.concatenate([qpos] * hpg, axis=0)
    blk = lax.broadcasted_iota(I32, (1, nsp), 1)
    blk_r = lax.broadcasted_iota(I32, (nsp, 1), 0)
    blk_rf = blk_r.astype(F32)
    cur_l = (q0 + lax.broadcasted_iota(I32, (1, tqp), 1)) // NSA_SEL_BLOCK
    vis_e = (blk * NSA_SEL_BLOCK + (NSA_CMP_BLOCK - 1)) <= qpos_r
    vis_o = (blk * NSA_SEL_BLOCK + (NSA_SEL_BLOCK - 1)) <= qpos_r
    kpos_w = win_pos0 + start + lax.broadcasted_iota(I32, (1, kw.shape[0]), 1)
    dlt = qpos - kpos_w
    bias_w = jnp.where(dlt >= 0, jnp.where(dlt < NSA_WINDOW, 0.0, NEG), NEG)

    qs64 = _stack_heads(qg, hpg, HEAD_DIM)
    qs = _pad_lanes(qs64).astype(BF16)
    qs64 = qs64.astype(BF16)

    _, _, acc_w = _flash_tile(_dot_nt(qs, kw), bias_w, vw, _flash_init(rows, LANE), hpg, tq, True)
    o_w = _flash_out(acc_w)

    kce, kco, vce, vco = (x.astype(BF16) for x in cmp4)
    s_e = jnp.where(vis_e, _dot_nt(qs64, kce), NEG)
    s_o = jnp.where(vis_o, _dot_nt(qs64, kco), NEG)
    m = jnp.maximum(jnp.max(s_e, axis=-1, keepdims=True), jnp.max(s_o, axis=-1, keepdims=True))
    p_e = jnp.where(vis_e, jnp.exp2(s_e - m), 0.0)
    p_o = jnp.where(vis_o, jnp.exp2(s_o - m), 0.0)
    den = jnp.sum(p_e, axis=-1, keepdims=True) + jnp.sum(p_o, axis=-1, keepdims=True)
    inv = 1.0 / jnp.maximum(den, 1e-30)
    p_e = p_e * inv
    p_o = p_o * inv
    o_c = _dot(p_e.astype(BF16), vce) + _dot(p_o.astype(BF16), vco)

    pe_h = sum(p_e[h * tq:(h + 1) * tq] for h in range(hpg))
    po_h = sum(p_o[h * tq:(h + 1) * tq] for h in range(hpg))
    imp = pe_h + po_h
    if tqp > tq:
        imp = jnp.concatenate([imp, jnp.zeros((tqp - tq, nsp), F32)], axis=0)
    imp = imp.T
    imp = jnp.where((blk_r == cur_l) | (blk_r == 0), FORCE_SCORE, imp)
    imp = jnp.where(blk_r <= cur_l, imp, NEG)
    sel = jnp.zeros((nsp, tqp), F32)
    for _ in range(n_top):
        mx = jnp.max(imp, axis=0, keepdims=True)
        first = jnp.min(jnp.where(imp == mx, blk_rf, float(nsp)), axis=0, keepdims=True)
        pick = blk_rf == first
        sel = jnp.where(pick & (mx > 0.5 * NEG), 1.0, sel)
        imp = jnp.where(pick, REMOVED, imp)
    return qs, qs64, o_c, o_w, sel.T[:tq].astype(BF16)


def _nsa_combine(gates, branches, tq):
    hpg = NSA_HEADS // NSA_KV_HEADS
    outs = []
    for g, (o_c, o_s, o_w) in enumerate(branches):
        for h in range(hpg):
            r = slice(h * tq, (h + 1) * tq)
            c = g * LANE + 3 * h
            outs.append(gates[:, c:c + 1] * o_c[r] + gates[:, c + 1:c + 2] * o_s[r]
                        + gates[:, c + 2:c + 3] * o_w[r])
    return jnp.concatenate(outs, axis=-1)


def _nsa_kernel(q_ref, g_ref, cmp_ref, ksel_ref, kwin_ref, o_ref, *,
                tq, tk, q_pos0, win_pos0, nsp, n_top):
    G = NSA_KV_HEADS
    hpg = NSA_HEADS // G
    gw = hpg * HEAD_DIM
    qi = pl.program_id(1)
    q0 = q_pos0 + qi * tq
    qpos = q0 + lax.broadcasted_iota(I32, (tq, 1), 0)
    rows = hpg * tq
    q_all = q_ref[0] * (HEAD_DIM ** -0.5 * LOG2E)
    start = pl.multiple_of(jnp.maximum(q0 - NSA_WINDOW - win_pos0, 0), 8)
    span = _win_span(tq)
    pro = [_nsa_prologue(q_all[:, g * gw:(g + 1) * gw],
                         kwin_ref[0, g, pl.ds(start, span), :], kwin_ref[0, G + g, pl.ds(start, span), :],
                         tuple(cmp_ref[0, 2 * g + kv, eo] for kv in range(2) for eo in range(2)),
                         q0=q0, tq=tq, nsp=nsp, n_top=n_top, win_pos0=win_pos0, start=start)
           for g in range(G)]

    blk_col = lax.broadcasted_iota(I32, (nsp, 1), 0)

    def sel_step(k0, w, diagonal, carry):
        kpos = k0 + lax.broadcasted_iota(I32, (1, w), 1)
        expand = jnp.where(blk_col == kpos // NSA_SEL_BLOCK, 1.0, 0.0).astype(BF16)
        out = []
        for g in range(G):
            bias = jnp.where(_dot(pro[g][4], expand) > 0.5, 0.0, NEG)
            if diagonal:
                bias = jnp.where(kpos <= qpos, bias, NEG)
            s = _dot_nt(pro[g][0], ksel_ref[0, g, pl.ds(k0, w), :])
            out.append(_flash_tile(s, bias, ksel_ref[0, G + g, pl.ds(k0, w), :], carry[g], hpg, tq, True))
        return tuple(out)

    res = _causal_sweep(sel_step, tuple(_flash_init(rows, LANE) for _ in range(G)), q0, tq, tk)
    o_ref[0] = _nsa_combine(g_ref[0], [(pro[g][2], _flash_out(res[g][2]), pro[g][3]) for g in range(G)], tq)


def _nsa_attention(q, gates, cmp, ksel, kwin, *, tq, q_pos0, lk, win_pos0):
    bk, lq, _ = q.shape
    G = NSA_KV_HEADS
    nsp = cmp.shape[3]
    lk_pad = ksel.shape[2]
    lw_pad = kwin.shape[2]
    tk = _kv_tile(lk_pad)
    ns = -(-lk // NSA_SEL_BLOCK)
    assert lw_pad >= _win_span(tq) and q_pos0 + lq <= lk_pad
    kern = functools.partial(_nsa_kernel, tq=tq, tk=tk, q_pos0=q_pos0, win_pos0=win_pos0,
                             nsp=nsp, n_top=min(NSA_TOPN, ns))
    qw = NSA_HEADS * HEAD_DIM
    return pl.pallas_call(
        kern,
        grid=(bk, lq // tq),
        in_specs=[pl.BlockSpec((1, tq, qw), lambda b, i: (b, i, 0)),
                  pl.BlockSpec((1, tq, G * LANE), lambda b, i: (b, i, 0)),
                  pl.BlockSpec((1, 2 * G, 2, nsp, HEAD_DIM), lambda b, i: (b, 0, 0, 0, 0)),
                  pl.BlockSpec((1, 2 * G, lk_pad, LANE), lambda b, i: (b, 0, 0, 0)),
                  pl.BlockSpec((1, 2 * G, lw_pad, LANE), lambda b, i: (b, 0, 0, 0))],
        out_specs=pl.BlockSpec((1, tq, qw), lambda b, i: (b, i, 0)),
        out_shape=jax.ShapeDtypeStruct((bk, lq, qw), F32),
        compiler_params=_cparams(("parallel", "arbitrary")),
        name="nsa_attention",
    )(q, gates, cmp, ksel, kwin)


def _nsa_decode_kernel(pt_ref, q_ref, g_ref, new_ref, w2_ref, kwin_ref, *refs,
                       pp, n_steps, tq, n_new, past_len, win_pos0, nsp, n_top):
    page_refs = refs[:pp]
    o_ref, cmp_ref, oc_ref, ow_ref, sel_ref, m_ref, l_ref, acc_ref = refs[pp:]
    G = NSA_KV_HEADS
    hpg = NSA_HEADS // G
    gw = hpg * HEAD_DIM
    rows = hpg * tq
    j = pl.program_id(1)
    keys = pp * PAGE_SIZE
    nb = keys // NSA_SEL_BLOCK
    q_all = q_ref[0] * (HEAD_DIM ** -0.5 * LOG2E)

    def queries(g):
        qs64 = _stack_heads(q_all[:, g * gw:(g + 1) * gw], hpg, HEAD_DIM)
        return _pad_lanes(qs64).astype(BF16), qs64.astype(BF16)

    @pl.when(j == 0)
    def _():
        cmp_ref[...] = jnp.zeros(cmp_ref.shape, F32)

    @pl.when(j < n_steps)
    def _():
        r0 = pl.multiple_of(j * nb, nb)
        for c in range(2 * G):
            x = jnp.concatenate([r[0, c].T for r in page_refs], axis=0)
            prod = x.reshape(nb, NSA_SEL_BLOCK, HEAD_DIM) * w2_ref[c][None]
            cmp_ref[c, 0, pl.ds(r0, nb), :] = jnp.sum(prod[:, :NSA_CMP_BLOCK], axis=1)
            cmp_ref[c, 1, pl.ds(r0, nb), :] = jnp.sum(prod[:, NSA_CMP_BLOCK:], axis=1)

    @pl.when(j == n_steps - 1)
    def _():
        for g in range(G):
            _, _, o_c, o_w, sel = _nsa_prologue(
                q_all[:, g * gw:(g + 1) * gw], kwin_ref[0, g], kwin_ref[0, G + g],
                tuple(cmp_ref[g + G * kv, eo] for kv in range(2) for eo in range(2)),
                q0=past_len, tq=tq, nsp=nsp, n_top=n_top, win_pos0=win_pos0, start=0)
            oc_ref[g] = o_c
            ow_ref[g] = o_w
            sel_ref[g] = sel
        m_ref[...] = jnp.full(m_ref.shape, NEG, F32)
        l_ref[...] = jnp.zeros(l_ref.shape, F32)
        acc_ref[...] = jnp.zeros(acc_ref.shape, F32)

    def update(g, s, bias, v, v_t):
        carry = (m_ref[g], l_ref[g], acc_ref[g])
        m, l, acc = _flash_tile(s, bias, v, carry, hpg, tq, False, v_t=v_t)
        m_ref[g] = m
        l_ref[g] = l
        acc_ref[g] = acc

    @pl.when(j >= n_steps)
    def _():
        kpos = (j - n_steps) * keys + lax.broadcasted_iota(I32, (1, keys), 1)
        blk_col = lax.broadcasted_iota(I32, (nsp, 1), 0)
        expand = jnp.where(blk_col == kpos // NSA_SEL_BLOCK, 1.0, 0.0).astype(BF16)
        for g in range(G):
            _, qs64 = queries(g)
            k_t = jnp.concatenate([r[0, g] for r in page_refs], axis=-1).astype(BF16)
            v_t = jnp.concatenate([r[0, G + g] for r in page_refs], axis=-1).astype(BF16)
            bias = jnp.where(_dot(sel_ref[g], expand) > 0.5, 0.0, NEG)
            update(g, _dot(qs64, k_t), bias, v_t, True)

    @pl.when(j == 2 * n_steps - 1)
    def _():
        new = new_ref[0]
        pad = jnp.zeros((LANE - tq, HEAD_DIM), F32)
        row = lax.broadcasted_iota(I32, (tq, 1), 0)
        col = lax.broadcasted_iota(I32, (1, LANE), 1)
        own = past_len // NSA_SEL_BLOCK
        branches = []
        for g in range(G):
            _, qs64 = queries(g)
            k = jnp.concatenate([new[:, (2 * G + g) * HEAD_DIM:(2 * G + g + 1) * HEAD_DIM], pad], axis=0)
            v = jnp.concatenate([new[:, (3 * G + g) * HEAD_DIM:(3 * G + g + 1) * HEAD_DIM], pad], axis=0)
            picked = sel_ref[g][:, own:own + 1].astype(F32) > 0.5
            bias = jnp.where((col <= row) & (col < n_new) & picked, 0.0, NEG)
            update(g, _dot_nt(qs64, k.astype(BF16)), bias, v.astype(BF16), False)
            o_s = acc_ref[g] * (1.0 / jnp.maximum(l_ref[g], 1e-30))
            branches.append((oc_ref[g], o_s, ow_ref[g]))
        o_ref[0] = _nsa_combine(g_ref[0], branches, tq)


def _nsa_decode(q, gates, pool, page_table, new, w2, kwin, *, n_new, win_pos0):
    bk, tq, qw = q.shape
    n_pages = page_table.shape[1]
    pp = min(PAGES_PER_STEP, n_pages)
    past_len = n_pages * PAGE_SIZE
    assert n_pages % pp == 0 and past_len % NSA_SEL_BLOCK == 0 and n_new <= NSA_CMP_BLOCK
    assert kwin.shape[2] == _win_span(tq)
    n_steps = n_pages // pp
    G = NSA_KV_HEADS
    rows = (NSA_HEADS // G) * tq
    lk = past_len + n_new
    ns = -(-lk // NSA_SEL_BLOCK)
    nsp = _round_up(ns, LANE)

    def page_map(i):
        return lambda b, j, pt: (pt[b, (j % n_steps) * pp + i], j // n_steps, 0, 0)

    const3 = lambda b, j, pt: (b, 0, 0)
    in_specs = [pl.BlockSpec((1, tq, qw), const3),
                pl.BlockSpec((1, tq, G * LANE), const3),
                pl.BlockSpec((1, tq, new.shape[-1]), const3),
                pl.BlockSpec(w2.shape, lambda b, j, pt: (0, 0, 0)),
                pl.BlockSpec((1,) + kwin.shape[1:], lambda b, j, pt: (b, 0, 0, 0))]
    in_specs += [pl.BlockSpec((1, 2 * G, HEAD_DIM, PAGE_SIZE), page_map(i)) for i in range(pp)]
    kern = functools.partial(_nsa_decode_kernel, pp=pp, n_steps=n_steps, tq=tq, n_new=n_new, past_len=past_len,
                             win_pos0=win_pos0, nsp=nsp, n_top=min(NSA_TOPN, ns))
    return pl.pallas_call(
        kern,
        grid_spec=pltpu.PrefetchScalarGridSpec(
            num_scalar_prefetch=1, grid=(bk, 2 * n_steps), in_specs=in_specs,
            out_specs=pl.BlockSpec((1, tq, qw), const3),
            scratch_shapes=[pltpu.VMEM((2 * G, 2, nsp, HEAD_DIM), F32),
                            pltpu.VMEM((G, rows, HEAD_DIM), F32), pltpu.VMEM((G, rows, HEAD_DIM), F32),
                            pltpu.VMEM((G, tq, nsp), BF16),
                            pltpu.VMEM((G, rows, 1), F32), pltpu.VMEM((G, rows, 1), F32),
                            pltpu.VMEM((G, rows, HEAD_DIM), F32)]),
        out_shape=jax.ShapeDtypeStruct((bk, tq, qw), F32),
        compiler_params=_cparams(("parallel", "arbitrary")),
        name="nsa_decode",
    )(page_table, q, gates, new, w2, kwin, *([pool] * pp))


def _shift_rows(x, d, fill):
    rolled = pltpu.roll(x, d, axis=0)
    row = lax.broadcasted_iota(I32, x.shape, 0)
    return jnp.where(row >= d, rolled, fill)


def _lru_kernel(x_ref, hist_ref, h0_ref, cw_ref, cb_ref, wr_ref, br_ref, wi_ref, bi_ref, lam_ref,
                o_ref, hl_ref, tail_ref, h_ref, *, tl, last_row):
    li = pl.program_id(1)

    @pl.when(li == 0)
    def _():
        tail_ref[...] = jnp.concatenate(
            [jnp.zeros((8 - (CONV_WIDTH - 1), x_ref.shape[-1]), F32), hist_ref[0]], axis=0)
        h_ref[...] = h0_ref[0]

    x = x_ref[0]
    xp = jnp.concatenate([tail_ref[...], x], axis=0)
    cw = cw_ref[...]
    conv = sum(xp[8 - (CONV_WIDTH - 1) + j:8 - (CONV_WIDTH - 1) + j + tl] * cw[j:j + 1]
               for j in range(CONV_WIDTH))
    conv = cb_ref[...] + conv
    tail_ref[...] = x[tl - 8:tl]

    cb16 = conv.astype(BF16)
    r = jax.nn.sigmoid(_dot(cb16, wr_ref[...]) + br_ref[...])
    ig = jax.nn.sigmoid(_dot(cb16, wi_ref[...]) + bi_ref[...])
    log_a = -LRU_C * r * jax.nn.softplus(-lam_ref[...])
    a = jnp.exp(log_a)
    th = jnp.tanh(log_a)
    b = jnp.sqrt(-2.0 * th / (1.0 - th)) * (ig * conv)

    d = 1
    while d < tl:
        a_prev = _shift_rows(a, d, 1.0)
        b_prev = _shift_rows(b, d, 0.0)
        b = a * b_prev + b
        a = a * a_prev
        d *= 2
    h = a * h_ref[...] + b
    o_ref[0] = h
    h_ref[...] = h[tl - 1:tl]

    @pl.when(li == pl.num_programs(1) - 1)
    def _():
        hl_ref[0] = h[last_row:last_row + 1]


def _block_diag(w):
    nb, bw, _ = w.shape
    eye = jnp.eye(nb, dtype=w.dtype)
    return (eye[:, None, :, None] * w[:, :, None, :]).reshape(nb * bw, nb * bw)


def _conv_rglru(x_b, hist, h0, conv_w, conv_b, w_r, b_r, w_i, b_i, lam, *, tl, n_valid):
    b, l, w = x_b.shape
    assert tl >= 8 and l % tl == 0 and n_valid > l - tl
    kern = functools.partial(_lru_kernel, tl=tl, last_row=(n_valid - 1) % tl)
    vec = lambda: pl.BlockSpec((1, w), lambda bi, li: (0, 0))
    h, h_last = pl.pallas_call(
        kern,
        grid=(b, l // tl),
        in_specs=[pl.BlockSpec((1, tl, w), lambda bi, li: (bi, li, 0)),
                  pl.BlockSpec((1, CONV_WIDTH - 1, w), lambda bi, li: (bi, 0, 0)),
                  pl.BlockSpec((1, 1, w), lambda bi, li: (bi, 0, 0)),
                  pl.BlockSpec((CONV_WIDTH, w), lambda bi, li: (0, 0)),
                  vec(),
                  pl.BlockSpec((w, w), lambda bi, li: (0, 0)), vec(),
                  pl.BlockSpec((w, w), lambda bi, li: (0, 0)), vec(), vec()],
        out_specs=[pl.BlockSpec((1, tl, w), lambda bi, li: (bi, li, 0)),
                   pl.BlockSpec((1, 1, w), lambda bi, li: (bi, 0, 0))],
        out_shape=[jax.ShapeDtypeStruct((b, l, w), F32), jax.ShapeDtypeStruct((b, 1, w), F32)],
        scratch_shapes=[pltpu.VMEM((8, w), F32), pltpu.VMEM((1, w), F32)],
        compiler_params=_cparams(("parallel", "arbitrary")),
        name="conv_rglru",
    )(x_b, hist, h0.reshape(b, 1, w), conv_w, conv_b.reshape(1, w),
      _block_diag(w_r).astype(BF16), b_r.reshape(1, w), _block_diag(w_i).astype(BF16), b_i.reshape(1, w),
      lam.reshape(1, w))
    return h, h_last.reshape(b, w)


def _out_kernel(oa_ref, za_ref, ob_ref, zb_ref, x_ref, gate_ref, w_ref, fg_ref, o_ref, *, final_norm):
    half = oa_ref.shape[-1]
    ma = (oa_ref[0] * jax.nn.silu(za_ref[0])).astype(BF16)
    mb = (ob_ref[0] * jax.nn.silu(zb_ref[0])).astype(BF16)
    y = _dot(ma, w_ref[0:half, :]) + _dot(mb, w_ref[half:2 * half, :])
    out = x_ref[0] + gate_ref[0] * y
    if final_norm:
        out = out * lax.rsqrt(jnp.mean(out * out, axis=-1, keepdims=True) + NORM_EPS) * fg_ref[...]
    o_ref[0] = out


def _out_project(o_a, z_a, o_b, z_b, x, gate, w_out, final_g, *, tl, final_norm):
    b, l, d = x.shape
    half = o_a.shape[-1]
    ts = gate.shape[1]
    tm = 1 if ts == 1 else tl
    mod_map = (lambda bi, li: (bi, 0, 0)) if ts == 1 else (lambda bi, li: (bi, li, 0))
    act = lambda: pl.BlockSpec((1, tl, half), lambda bi, li: (bi, li, 0))
    return pl.pallas_call(
        functools.partial(_out_kernel, final_norm=final_norm),
        grid=(b, l // tl),
        in_specs=[act(), act(), act(), act(),
                  pl.BlockSpec((1, tl, d), lambda bi, li: (bi, li, 0)),
                  pl.BlockSpec((1, tm, d), mod_map),
                  pl.BlockSpec((2 * half, d), lambda bi, li: (0, 0)),
                  pl.BlockSpec((1, d), lambda bi, li: (0, 0))],
        out_specs=pl.BlockSpec((1, tl, d), lambda bi, li: (bi, li, 0)),
        out_shape=jax.ShapeDtypeStruct((b, l, d), F32),
        compiler_params=_cparams(("parallel", "arbitrary")),
        name="out_project",
    )(o_a, z_a, o_b, z_b, x, gate, w_out, final_g.reshape(1, d))


def _diff_queries(q, tq):
    hpg = DIFF_HEADS // DIFF_KV_HEADS
    q = q * (DIFF_HALF ** -0.5 * LOG2E)
    zero = jnp.zeros((tq, DIFF_HALF), F32)
    parts = []
    for mp in range(2):
        for h in range(hpg):
            qh = q[:, (2 * h + mp) * DIFF_HALF:(2 * h + mp + 1) * DIFF_HALF]
            parts.append(jnp.concatenate([qh, zero] if mp == 0 else [zero, qh], axis=-1))
    return jnp.concatenate(parts, axis=0).astype(BF16)


def _diff_finish(l, acc, lamv, subg, tq):
    hpg = DIFF_HEADS // DIFF_KV_HEADS
    o = acc * (1.0 / jnp.maximum(l, 1e-30))
    lam = (jnp.exp(jnp.sum(lamv[0:1] * lamv[1:2], axis=-1, keepdims=True))
           - jnp.exp(jnp.sum(lamv[2:3] * lamv[3:4], axis=-1, keepdims=True)) + DIFF_LAMBDA_INIT)
    half = hpg * tq
    od = o[0:half] - lam * o[half:2 * half]
    od = od * lax.rsqrt(jnp.mean(od * od, axis=-1, keepdims=True) + NORM_EPS)
    od = od * subg * (1.0 - DIFF_LAMBDA_INIT)
    return jnp.concatenate([od[h * tq:(h + 1) * tq] for h in range(hpg)], axis=-1)


def _diff_kernel(q_ref, k_ref, v_ref, lamv_ref, subg_ref, o_ref, *, tq, tk, q_pos0):
    hpg = DIFF_HEADS // DIFF_KV_HEADS
    qi = pl.program_id(2)
    q0 = q_pos0 + qi * tq
    n_maps = 2 * hpg
    rows = n_maps * tq
    qpos = q0 + lax.broadcasted_iota(I32, (tq, 1), 0)
    qs = _diff_queries(q_ref[0], tq)

    def step(k0, w, diagonal, carry):
        s = _dot_nt(qs, k_ref[0, 0, pl.ds(k0, w), :])
        bias = None
        if diagonal:
            kpos = k0 + lax.broadcasted_iota(I32, (1, w), 1)
            bias = jnp.where(kpos <= qpos, 0.0, NEG)
        return _flash_tile(s, bias, v_ref[0, 0, pl.ds(k0, w), :], carry, n_maps, tq, False)

    m, l, acc = _causal_sweep(step, _flash_init(rows, 2 * DIFF_HALF), q0, tq, tk)
    o_ref[0] = _diff_finish(l, acc, lamv_ref[...], subg_ref[...], tq)


def _diff_decode_kernel(pt_ref, q_ref, new_ref, lamv_ref, subg_ref, *refs, pp, tq, n_new):
    page_refs = refs[:pp]
    o_ref, m_ref, l_ref, acc_ref = refs[pp:]
    G = DIFF_KV_HEADS
    n_maps = 2 * (DIFF_HEADS // G)
    gw = n_maps * DIFF_HALF
    j = pl.program_id(1)

    @pl.when(j == 0)
    def _():
        m_ref[...] = jnp.full(m_ref.shape, NEG, F32)
        l_ref[...] = jnp.zeros(l_ref.shape, F32)
        acc_ref[...] = jnp.zeros(acc_ref.shape, F32)

    def update(g, qs, k, v, bias):
        carry = (m_ref[g], l_ref[g], acc_ref[g])
        m, l, acc = _flash_tile(_dot_nt(qs, k), bias, v, carry, n_maps, tq, False)
        m_ref[g] = m
        l_ref[g] = l
        acc_ref[g] = acc

    qs = [_diff_queries(q_ref[0][:, g * gw:(g + 1) * gw], tq) for g in range(G)]
    for g in range(G):
        k = jnp.concatenate([r[0, pl.ds(g, PAGE_SIZE, stride=2 * G), :] for r in page_refs], axis=0)
        v = jnp.concatenate([r[0, pl.ds(G + g, PAGE_SIZE, stride=2 * G), :] for r in page_refs], axis=0)
        update(g, qs[g], k.astype(BF16), v.astype(BF16), None)

    @pl.when(j == pl.num_programs(1) - 1)
    def _():
        new = new_ref[0]
        pad = jnp.zeros((LANE - tq, 2 * DIFF_HALF), F32)
        row = lax.broadcasted_iota(I32, (tq, 1), 0)
        col = lax.broadcasted_iota(I32, (1, LANE), 1)
        bias = jnp.where((col <= row) & (col < n_new), 0.0, NEG)
        outs = []
        for g in range(G):
            k = jnp.concatenate([new[:, g * 2 * DIFF_HALF:(g + 1) * 2 * DIFF_HALF], pad], axis=0)
            v = jnp.concatenate([new[:, (G + g) * 2 * DIFF_HALF:(G + g + 1) * 2 * DIFF_HALF], pad], axis=0)
            update(g, qs[g], k.astype(BF16), v.astype(BF16), bias)
            outs.append(_diff_finish(l_ref[g], acc_ref[g], lamv_ref[...], subg_ref[...], tq))
        o_ref[0] = jnp.concatenate(outs, axis=-1)


def _diff_decode(q, pool, page_table, new, lamv, subln_g, *, n_new):
    bk, tq, qw = q.shape
    n_pages = page_table.shape[1]
    pp = min(PAGES_PER_STEP, n_pages)
    assert n_pages % pp == 0
    G = DIFF_KV_HEADS
    rows = 2 * (DIFF_HEADS // G) * tq

    def page_map(i):
        return lambda b, j, pt: (pt[b, j * pp + i], 0, 0)

    in_specs = [pl.BlockSpec((1, tq, qw), lambda b, j, pt: (b, 0, 0)),
                pl.BlockSpec((1, tq, new.shape[-1]), lambda b, j, pt: (b, 0, 0)),
                pl.BlockSpec((4, DIFF_HALF), lambda b, j, pt: (0, 0)),
                pl.BlockSpec((1, 2 * DIFF_HALF), lambda b, j, pt: (0, 0))]
    in_specs += [pl.BlockSpec((1,) + pool.shape[1:], page_map(i)) for i in range(pp)]
    return pl.pallas_call(
        functools.partial(_diff_decode_kernel, pp=pp, tq=tq, n_new=n_new),
        grid_spec=pltpu.PrefetchScalarGridSpec(
            num_scalar_prefetch=1, grid=(bk, n_pages // pp), in_specs=in_specs,
            out_specs=pl.BlockSpec((1, tq, qw), lambda b, j, pt: (b, 0, 0)),
            scratch_shapes=[pltpu.VMEM((G, rows, 1), F32), pltpu.VMEM((G, rows, 1), F32),
                            pltpu.VMEM((G, rows, 2 * DIFF_HALF), F32)]),
        out_shape=jax.ShapeDtypeStruct((bk, tq, qw), F32),
        compiler_params=_cparams(("parallel", "arbitrary")),
        name="diff_decode",
    )(page_table, q, new, lamv, subln_g.reshape(1, 2 * DIFF_HALF), *([pool] * pp))


def _diff_attention(q, kv, lamv, subln_g, *, tq, q_pos0):
    bk, lq, _ = q.shape
    G = DIFF_KV_HEADS
    lk_pad = kv.shape[2]
    tk = _kv_tile(lk_pad)
    gw = (DIFF_HEADS // G) * 2 * DIFF_HALF
    assert q_pos0 + lq <= lk_pad
    return pl.pallas_call(
        functools.partial(_diff_kernel, tq=tq, tk=tk, q_pos0=q_pos0),
        grid=(bk, G, lq // tq),
        in_specs=[pl.BlockSpec((1, tq, gw), lambda b, g, i: (b, i, g)),
                  pl.BlockSpec((1, 1, lk_pad, 2 * DIFF_HALF), lambda b, g, i: (b, g, 0, 0)),
                  pl.BlockSpec((1, 1, lk_pad, 2 * DIFF_HALF), lambda b, g, i: (b, G + g, 0, 0)),
                  pl.BlockSpec((4, DIFF_HALF), lambda b, g, i: (0, 0)),
                  pl.BlockSpec((1, 2 * DIFF_HALF), lambda b, g, i: (0, 0))],
        out_specs=pl.BlockSpec((1, tq, gw), lambda b, g, i: (b, i, g)),
        out_shape=jax.ShapeDtypeStruct((bk, lq, DIFF_HEADS * 2 * DIFF_HALF), F32),
        compiler_params=_cparams(("parallel", "parallel", "arbitrary")),
        name="diff_attention",
    )(q, kv, kv, lamv, subln_g.reshape(1, 2 * DIFF_HALF))


def _dsa_keys(s_all, wi, causal, tq):
    score = jnp.zeros((tq, s_all.shape[1]), F32)
    for h in range(IDX_HEADS):
        score = score + wi[:, h:h + 1] * jnp.maximum(s_all[h * tq:(h + 1) * tq], 0.0)
    bits = pltpu.bitcast(score, I32)
    key = jnp.where(bits < 0, bits ^ 0x7FFFFFFF, bits)
    key = jnp.where(score == 0.0, 0, key)
    return jnp.where(causal, jnp.where(score > 0.5 * NEG, key, INT_MIN), INT_MIN)


def _dsa_threshold(read, n_tiles, kmax, *, tq, tk, n_sel, transposed, unroll):
    lanes = tk // LANE

    def count(*bounds):
        def f(j, accs):
            keys = read(j)
            out = []
            for bound, acc in zip(bounds, accs):
                hit = jnp.where(keys >= bound, 1.0, 0.0)
                if transposed:
                    acc = acc + jnp.sum(hit.reshape(tk // COUNT_ROWS, COUNT_ROWS, tq), axis=0)
                else:
                    for c in range(lanes):
                        acc = acc + hit[:, c * LANE:(c + 1) * LANE]
                out.append(acc)
            return tuple(out)
        acc0 = jnp.zeros((COUNT_ROWS, tq) if transposed else (tq, LANE), F32)
        accs = lax.fori_loop(0, n_tiles, f, tuple(acc0 for _ in bounds), unroll=unroll)
        return [jnp.sum(acc, axis=0 if transposed else -1, keepdims=True) for acc in accs]

    k_f = float(n_sel)
    probe = jnp.maximum(kmax - (PROBE_BINADES << 23), 1)
    c_adm, c_nn, c_pos, c_probe = count(INT_MIN + 1, 0, 1, probe)
    few, pos, zero, high = c_adm < k_f, c_pos >= k_f, c_nn >= k_f, c_probe >= k_f
    lo0 = jnp.where(pos, jnp.where(high, probe, 1), jnp.where(zero, 0, INT_MIN))
    hi0 = jnp.where(pos, jnp.where(high, kmax, probe - 1), jnp.where(zero, 0, jnp.where(few, INT_MIN, -1)))

    def unfinished(lo_hi):
        lo, hi = lo_hi
        return jnp.max(jnp.where(lo < hi, 1.0, 0.0)) > 0.0

    def bisect(lo_hi):
        lo, hi = lo_hi
        mid = (lo >> 1) + (hi >> 1) + ((lo | hi) & 1)
        cnt, = count(mid)
        lo = jnp.where(cnt >= k_f, mid, lo)
        hi = jnp.where(cnt > k_f, hi, jnp.where(cnt == k_f, mid, mid - 1))
        return lo, hi

    thr, _ = lax.while_loop(unfinished, lambda s: bisect(bisect(s)), (lo0, hi0))
    thr = jnp.maximum(thr, INT_MIN + 1)
    n_gt, = count(thr + 1)
    return thr, k_f - n_gt


def _dsa_bias(key, thr, need, seen, tri):
    tied = key == thr
    tied_b = jnp.where(tied, 1.0, 0.0).astype(BF16)
    t = tri.shape[0]
    ranks = []
    for c in range(key.shape[1] // t):
        r = _dot(tied_b[:, c * t:(c + 1) * t], tri) + seen
        ranks.append(r)
        seen = r[:, t - 1:t]
    rank = jnp.concatenate(ranks, axis=-1)
    return jnp.where(key > thr, 0.0, jnp.where(tied, jnp.where(rank <= need, 0.0, NEG), NEG)), seen


def _dsa_kernel(q_ref, qi_ref, kw_ref, kidx_ref, kv_ref, tri_ref, o_ref, key_ref, keyt_ref, *,
                tq, tk, q_pos0, n_sel):
    G = DSA_KV_HEADS
    hpg = DSA_HEADS // G
    q0 = q_pos0 + pl.program_id(1) * tq
    qpos = q0 + lax.broadcasted_iota(I32, (tq, 1), 0)
    n_tiles = (q0 + tq - 1) // tk + 1
    lanes = tk // LANE

    qidx = _pad_lanes(_stack_heads(qi_ref[0] * (IDX_DIM ** -0.5), IDX_HEADS, IDX_DIM)).astype(BF16)
    wi = kw_ref[0][:, IDX_DIM:IDX_DIM + IDX_HEADS] * (IDX_HEADS ** -0.5)

    def score_step(j, kmax):
        k0 = pl.multiple_of(j * tk, tk)
        s_all = _dot_nt(qidx, kidx_ref[0, 0, pl.ds(k0, tk), :])
        kpos = k0 + lax.broadcasted_iota(I32, (1, tk), 1)
        key = _dsa_keys(s_all, wi, kpos <= qpos, tq)
        key_ref[j] = key
        key_t = key.T
        keyt_ref[j] = key_t
        return jnp.maximum(kmax, jnp.max(key_t, axis=0, keepdims=True))

    kmax = lax.fori_loop(0, n_tiles, score_step, jnp.full((1, tq), INT_MIN, I32))
    thr, need = _dsa_threshold(lambda j: keyt_ref[j], n_tiles, kmax,
                               tq=tq, tk=tk, n_sel=n_sel, transposed=True, unroll=False)

    def along_rows(v):
        rep = jnp.broadcast_to(v, (LANE, tq)).T
        return jnp.concatenate([rep] * lanes, axis=-1)

    thr = along_rows(thr)
    need = along_rows(need)

    q = q_ref[0] * (HEAD_DIM ** -0.5 * LOG2E)
    qs = [_pad_lanes(_stack_heads(q[:, g * hpg * HEAD_DIM:(g + 1) * hpg * HEAD_DIM], hpg, HEAD_DIM)).astype(BF16)
          for g in range(G)]
    rows = hpg * tq

    def att_step(k0, w, diagonal, carry):
        seen, flash = carry
        bias, seen = _dsa_bias(key_ref[k0 // tk, :, 0:w], thr[:, :w], need[:, :w], seen, tri_ref[...])
        out = []
        for g in range(G):
            s = _dot_nt(qs[g], kv_ref[0, g, pl.ds(k0, w), :])
            out.append(_flash_tile(s, bias, kv_ref[0, G + g, pl.ds(k0, w), :], flash[g], hpg, tq, True))
        return seen, tuple(out)

    init = (jnp.zeros((tq, 1), F32), tuple(_flash_init(rows, LANE) for _ in range(G)))
    _, res = _causal_sweep(att_step, init, q0, tq, tk)
    outs = []
    for g in range(G):
        o = _flash_out(res[g][2])
        outs.extend(o[h * tq:(h + 1) * tq] for h in range(hpg))
    o_ref[0] = jnp.concatenate(outs, axis=-1)


def _dsa_attention(q, qi, kw, kidx, kv, *, tq, q_pos0, lk):
    bk, lq, _ = q.shape
    lk_pad = kv.shape[2]
    tk = _kv_tile(lk_pad)
    n_sel = min(DSA_TOPK_MAX, lk // 4)
    assert q_pos0 + lq <= lk_pad and tk >= n_sel and tq % LANE == 0
    kern = functools.partial(_dsa_kernel, tq=tq, tk=tk, q_pos0=q_pos0, n_sel=n_sel)
    tri = jnp.triu(jnp.ones((TRI_TILE, TRI_TILE), BF16))
    return pl.pallas_call(
        kern,
        grid=(bk, lq // tq),
        in_specs=[pl.BlockSpec((1, tq, DSA_HEADS * HEAD_DIM), lambda b, i: (b, i, 0)),
                  pl.BlockSpec((1, tq, IDX_HEADS * IDX_DIM), lambda b, i: (b, i, 0)),
                  pl.BlockSpec((1, tq, LANE), lambda b, i: (b, i, 0)),
                  pl.BlockSpec((1, 1, lk_pad, LANE), lambda b, i: (b, 0, 0, 0)),
                  pl.BlockSpec((1, 4, lk_pad, LANE), lambda b, i: (b, 0, 0, 0)),
                  pl.BlockSpec((TRI_TILE, TRI_TILE), lambda b, i: (0, 0))],
        out_specs=pl.BlockSpec((1, tq, DSA_HEADS * HEAD_DIM), lambda b, i: (b, i, 0)),
        out_shape=jax.ShapeDtypeStruct((bk, lq, DSA_HEADS * HEAD_DIM), F32),
        scratch_shapes=[pltpu.VMEM((lk_pad // tk, tq, tk), I32), pltpu.VMEM((lk_pad // tk, tk, tq), I32)],
        compiler_params=_cparams(("parallel", "arbitrary")),
        name="dsa_attention",
    )(q, qi, kw, kidx, kv, tri)


def _dsa_decode_kernel(pt_ref, q_ref, qi_ref, kw_ref, newi_ref, newkv_ref, tri_ref, *refs,
                       pp, n_steps, tq, n_new, n_sel):
    ipage_refs = refs[:pp]
    kvpage_refs = refs[pp:2 * pp]
    o_ref, key_ref, thr_ref, need_ref, seen_ref, m_ref, l_ref, acc_ref = refs[2 * pp:]
    G = DSA_KV_HEADS
    hpg = DSA_HEADS // G
    gw = hpg * HEAD_DIM
    j = pl.program_id(1)
    keys = pp * PAGE_SIZE
    qidx = _stack_heads(qi_ref[0] * (IDX_DIM ** -0.5), IDX_HEADS, IDX_DIM).astype(BF16)
    wi = kw_ref[0][:, IDX_DIM:IDX_DIM + IDX_HEADS] * (IDX_HEADS ** -0.5)
    q_all = q_ref[0] * (HEAD_DIM ** -0.5 * LOG2E)
    row = lax.broadcasted_iota(I32, (tq, 1), 0)
    col = lax.broadcasted_iota(I32, (1, LANE), 1)
    own = (col <= row) & (col < n_new)

    def queries(g):
        return _stack_heads(q_all[:, g * gw:(g + 1) * gw], hpg, HEAD_DIM).astype(BF16)

    @pl.when(j < n_steps)
    def _():
        k_t = jnp.concatenate([r[0, 0] for r in ipage_refs], axis=-1).astype(BF16)
        key_ref[j] = _dsa_keys(_dot(qidx, k_t), wi, jnp.full((tq, keys), True), tq)

    @pl.when(j == n_steps - 1)
    def _():
        pad = jnp.zeros((LANE - tq, IDX_DIM), F32)
        k_new = jnp.concatenate([newi_ref[0], pad], axis=0).astype(BF16)
        key_new = _dsa_keys(_dot_nt(qidx, k_new), wi, own, tq)
        key_ref[n_steps] = jnp.concatenate([key_new, jnp.full((tq, keys - LANE), INT_MIN, I32)], axis=-1)
        kmax = jnp.full((tq, 1), INT_MIN, I32)
        for t in range(n_steps + 1):
            kmax = jnp.maximum(kmax, jnp.max(key_ref[t], axis=-1, keepdims=True))
        thr, need = _dsa_threshold(lambda t: key_ref[t], n_steps + 1, kmax, tq=tq, tk=keys, n_sel=n_sel,
                                   transposed=False, unroll=True)
        thr_ref[...] = thr
        need_ref[...] = need
        seen_ref[...] = jnp.zeros(seen_ref.shape, F32)
        m_ref[...] = jnp.full(m_ref.shape, NEG, F32)
        l_ref[...] = jnp.zeros(l_ref.shape, F32)
        acc_ref[...] = jnp.zeros(acc_ref.shape, F32)

    def update(g, s, bias, v, v_t):
        carry = (m_ref[g], l_ref[g], acc_ref[g])
        m, l, acc = _flash_tile(s, bias, v, carry, hpg, tq, False, v_t=v_t)
        m_ref[g] = m
        l_ref[g] = l
        acc_ref[g] = acc

    @pl.when(j >= n_steps)
    def _():
        bias, seen = _dsa_bias(key_ref[j - n_steps], thr_ref[...], need_ref[...], seen_ref[...], tri_ref[...])
        seen_ref[...] = seen
        for g in range(G):
            k_t = jnp.concatenate([r[0, g] for r in kvpage_refs], axis=-1).astype(BF16)
            v_t = jnp.concatenate([r[0, G + g] for r in kvpage_refs], axis=-1).astype(BF16)
            update(g, _dot(queries(g), k_t), bias, v_t, True)

    @pl.when(j == 2 * n_steps - 1)
    def _():
        new = newkv_ref[0]
        pad = jnp.zeros((LANE - tq, HEAD_DIM), F32)
        bias, _ = _dsa_bias(key_ref[n_steps][:, :LANE], thr_ref[...], need_ref[...], seen_ref[...],
                            tri_ref[0:LANE, 0:LANE])
        outs = []
        for g in range(G):
            k = jnp.concatenate([new[:, g * HEAD_DIM:(g + 1) * HEAD_DIM], pad], axis=0).astype(BF16)
            v = jnp.concatenate([new[:, (G + g) * HEAD_DIM:(G + g + 1) * HEAD_DIM], pad], axis=0).astype(BF16)
            update(g, _dot_nt(queries(g), k), bias, v, False)
            o = acc_ref[g] * (1.0 / jnp.maximum(l_ref[g], 1e-30))
            outs.extend(o[h * tq:(h + 1) * tq] for h in range(hpg))
        o_ref[0] = jnp.concatenate(outs, axis=-1)


def _dsa_decode(q, qi, kw, kidx_pool, kv_pool, page_table, new_kidx, new_kv, *, n_new):
    bk, tq, qw = q.shape
    n_pages = page_table.shape[1]
    pp = min(PAGES_PER_STEP, n_pages)
    assert n_pages % pp == 0 and n_new <= tq
    n_steps = n_pages // pp
    keys = pp * PAGE_SIZE
    lk = n_pages * PAGE_SIZE + n_new
    n_sel = min(DSA_TOPK_MAX, lk // 4)
    G = DSA_KV_HEADS
    rows = (DSA_HEADS // G) * tq
    tri = jnp.triu(jnp.ones((TRI_TILE, TRI_TILE), BF16))
    assert keys % TRI_TILE == 0 and keys >= n_sel

    def ipage_map(i):
        return lambda b, j, pt: (pt[b, jnp.minimum(j, n_steps - 1) * pp + i], 0, 0, 0)

    def kvpage_map(i):
        return lambda b, j, pt: (pt[b, jnp.maximum(j - n_steps, 0) * pp + i], 0, 0, 0)

    const3 = lambda b, j, pt: (b, 0, 0)
    in_specs = [pl.BlockSpec((1, tq, qw), const3),
                pl.BlockSpec((1, tq, qi.shape[-1]), const3),
                pl.BlockSpec((1, tq, LANE), const3),
                pl.BlockSpec((1, tq, IDX_DIM), const3),
                pl.BlockSpec((1, tq, new_kv.shape[-1]), const3),
                pl.BlockSpec((TRI_TILE, TRI_TILE), lambda b, j, pt: (0, 0))]
    in_specs += [pl.BlockSpec((1,) + kidx_pool.shape[1:], ipage_map(i)) for i in range(pp)]
    in_specs += [pl.BlockSpec((1,) + kv_pool.shape[1:], kvpage_map(i)) for i in range(pp)]
    kern = functools.partial(_dsa_decode_kernel, pp=pp, n_steps=n_steps, tq=tq, n_new=n_new, n_sel=n_sel)
    return pl.pallas_call(
        kern,
        grid_spec=pltpu.PrefetchScalarGridSpec(
            num_scalar_prefetch=1, grid=(bk, 2 * n_steps), in_specs=in_specs,
            out_specs=pl.BlockSpec((1, tq, qw), const3),
            scratch_shapes=[pltpu.VMEM((n_steps + 1, tq, keys), I32),
                            pltpu.VMEM((tq, 1), I32), pltpu.VMEM((tq, 1), F32), pltpu.VMEM((tq, 1), F32),
                            pltpu.VMEM((G, rows, 1), F32), pltpu.VMEM((G, rows, 1), F32),
                            pltpu.VMEM((G, rows, HEAD_DIM), F32)]),
        out_shape=jax.ShapeDtypeStruct((bk, tq, qw), F32),
        compiler_params=_cparams(("parallel", "arbitrary")),
        name="dsa_decode",
    )(page_table, q, qi, kw, new_kidx, new_kv, tri, *([kidx_pool] * pp), *([kv_pool] * pp))


L0_SIZES = (512, 768, 24, 512, 512, 512)
L1_SIZES = (512, 256, 256, 512, 512, 128, 128, 256, 64, 4, 512)


def _l0_weight(w_in):
    d = w_in.shape[0]
    q, kv6, gl, z_a, x_b, z_b = jnp.split(w_in, np.cumsum(L0_SIZES)[:-1].tolist(), axis=1)
    pad = jnp.zeros((d, LANE - 12), w_in.dtype)
    w = jnp.concatenate([q, kv6, z_a, x_b, z_b, gl[:, :12], pad, gl[:, 12:], pad], axis=1)
    segs = [(0, 512), (512, 1024), (1024, 1280), (1280, 1792), (1792, 2304), (2304, 2816), (2816, 3072)]
    return w.astype(BF16), segs


def _l1_weight(w_in):
    d = w_in.shape[0]
    qc, kc, vc, z_c, qd, kd, vd, qi, ki, wi, z_d = jnp.split(w_in, np.cumsum(L1_SIZES)[:-1].tolist(), axis=1)
    pad = jnp.zeros((d, LANE - IDX_DIM - IDX_HEADS), w_in.dtype)
    w = jnp.concatenate([qc, kc, vc, z_c, qd, kd, vd, qi, z_d, ki, wi, pad], axis=1)
    segs = [(0, 512), (512, 1024), (1024, 1536), (1536, 2048), (2048, 2304), (2304, 2560), (2560, 3072),
            (3072, 3200)]
    return w.astype(BF16), segs


def _pad_rows(x, n):
    return jnp.pad(x, ((0, 0), (0, n - x.shape[1]), (0, 0)))


def _cols_pool(pool):
    npool, ps = pool.shape[:2]
    cw = pool.shape[-1]
    perm = (0,) + tuple(range(2, pool.ndim)) + (1,)
    return jnp.transpose(pool, perm).reshape(npool, -1, cw, ps)


def _layer0(x, mod, past, w, *, tl, tq):
    (norm_g, w_in, cmp_wk, cmp_wv, conv_w, conv_b, lru_wr, lru_br, lru_wi, lru_bi, lru_lambda, w_out) = w
    shift, scale, gate = mod
    b, l, d = x.shape
    w_p, segs = _l0_weight(w_in)
    flat = shift.shape[1] != 1
    xin = x.reshape(1, b * l, d) if flat else x
    q, kvp, kvw, z_a, x_b, z_b, gates = _project(xin, norm_g, shift, scale, w_p, segs, 6, tl)
    if flat:
        q, kvp, kvw, z_a, x_b, z_b, gates = (t.reshape(b, l, -1) for t in (q, kvp, kvw, z_a, x_b, z_b, gates))
    wk2 = jnp.concatenate([cmp_wk, cmp_wk], axis=0)
    wv2 = jnp.concatenate([cmp_wv, cmp_wv], axis=0)
    w2 = jnp.stack([wk2, wk2, wv2, wv2], axis=0)
    lq = _round_up(l, 8)
    if past is None:
        ksel, cmp = _split_chunks(kvp, cw=HEAD_DIM, n_out=4, n_cmp=4, cmp_w=w2)
        kwin = _split_chunks(kvw, cw=HEAD_DIM, n_out=4)
        kv_win = kvw
        hist = jnp.zeros((b, CONV_WIDTH - 1, x_b.shape[-1]), F32)
        h0 = jnp.zeros((b, x_b.shape[-1]), F32)
        nsp = _round_up(cmp.shape[3], LANE)
        cmp = jnp.pad(cmp, ((0, 0), (0, 0), (0, 0), (0, nsp - cmp.shape[3]), (0, 0)))
        cmp = cmp.reshape(b, 2, 2, 2, nsp, HEAD_DIM).transpose(0, 2, 1, 3, 4, 5).reshape(b, 4, 2, nsp, HEAD_DIM)
        o_a = _nsa_attention(_pad_rows(q, lq), _pad_rows(gates, lq), cmp, ksel, kwin,
                             tq=min(tq, lq), q_pos0=0, lk=l, win_pos0=0)[:, :l]
    else:
        pool, table, win_buf, hist, h0 = past
        assert lq <= tq
        kv_win = jnp.concatenate([win_buf.reshape(b, win_buf.shape[1], -1), kvw], axis=1)
        lw_pad = _win_span(lq)
        assert kv_win.shape[1] <= lw_pad
        kwin = _split_chunks(_pad_rows(kv_win, lw_pad), cw=HEAD_DIM, n_out=4)
        o_a = _nsa_decode(_pad_rows(q, lq), _pad_rows(gates, lq), _cols_pool(pool), table, _pad_rows(kvp, lq),
                          w2, kwin, n_new=l, win_pos0=table.shape[1] * PAGE_SIZE - win_buf.shape[1])[:, :l]
    o_b, h_last = _conv_rglru(_pad_rows(x_b, lq), hist, h0, conv_w, conv_b, lru_wr, lru_br, lru_wi, lru_bi,
                              lru_lambda, tl=min(256, lq), n_valid=l)
    o_b = o_b[:, :l]
    fl = (lambda t: t.reshape(1, b * l, -1)) if flat else (lambda t: t)
    x_new = _out_project(fl(o_a), fl(z_a), fl(o_b), fl(z_b), xin, gate, w_out.astype(BF16),
                         jnp.ones((d,), F32), tl=tl, final_norm=False).reshape(b, l, d)
    win_keep = min(NSA_WINDOW, kv_win.shape[1])
    conv_src = jnp.concatenate([hist, x_b], axis=1) if l < CONV_WIDTH - 1 else x_b
    states = (kvp.reshape(b, l, 4, NSA_KV_HEADS, HEAD_DIM),
              kv_win[:, -win_keep:].reshape(b, win_keep, 2, NSA_KV_HEADS, HEAD_DIM),
              conv_src[:, -(CONV_WIDTH - 1):], h_last)
    return x_new, states


def _layer1(x, mod, past, w, final_g, *, tl, tq):
    (norm_g, w_in, lam_q1, lam_k1, lam_q2, lam_k2, subln_g, w_out) = w
    shift, scale, gate = mod
    b, l, d = x.shape
    w_p, segs = _l1_weight(w_in)
    flat = shift.shape[1] != 1
    xin = x.reshape(1, b * l, d) if flat else x
    qc, kvc, z_c, qd, kvd, qi, z_d, kiw = _project(xin, norm_g, shift, scale, w_p, segs, -1, tl)
    if flat:
        qc, kvc, z_c, qd, kvd, qi, z_d, kiw = (t.reshape(b, l, -1) for t in (qc, kvc, z_c, qd, kvd, qi, z_d, kiw))
    lq = _round_up(l, 8)
    lamv = jnp.stack([lam_q1, lam_k1, lam_q2, lam_k2], axis=0)
    if past is None:
        diff_kv = _split_chunks(kvc, cw=2 * DIFF_HALF, n_out=4)
        dsa_kv = _split_chunks(kvd, cw=HEAD_DIM, n_out=4)
        kidx = _split_chunks(kiw, cw=IDX_DIM, n_out=1)
        o_c = _diff_attention(_pad_rows(qc, lq), diff_kv, lamv, subln_g, tq=min(tq, lq), q_pos0=0)[:, :l]
        o_d = _dsa_attention(_pad_rows(qd, lq), _pad_rows(qi, lq), _pad_rows(kiw, lq), kidx, dsa_kv,
                             tq=min(tq, lq), q_pos0=0, lk=l)[:, :l]
    else:
        diff_pool, dsa_pool, kidx_pool, table = past
        assert lq <= tq
        diff_rows = diff_pool.reshape(diff_pool.shape[0], PAGE_SIZE * 4, 2 * DIFF_HALF)
        o_c = _diff_decode(_pad_rows(qc, lq), diff_rows, table, _pad_rows(kvc, lq), lamv, subln_g, n_new=l)[:, :l]
        o_d = _dsa_decode(_pad_rows(qd, lq), _pad_rows(qi, lq), _pad_rows(kiw, lq), _cols_pool(kidx_pool),
                          _cols_pool(dsa_pool), table, _pad_rows(kiw[:, :, :IDX_DIM], lq), _pad_rows(kvd, lq),
                          n_new=l)[:, :l]
    fl = (lambda t: t.reshape(1, b * l, -1)) if flat else (lambda t: t)
    y = _out_project(fl(o_c), fl(z_c), fl(o_d), fl(z_d), xin, gate, w_out.astype(BF16), final_g,
                     tl=tl, final_norm=True).reshape(b, l, d)
    states = (kvc.reshape(b, l, 2, DIFF_KV_HEADS, 2 * DIFF_HALF),
              kvd.reshape(b, l, 2, DSA_KV_HEADS, HEAD_DIM), kiw[:, :, :IDX_DIM])
    return y, states


def kernel(x_prompt, x_sample, cache_l0_nsa_kv, state_l0_win_kv, state_l0_conv, state_l0_lru_h,
           cache_l1_diff_kv, cache_l1_dsa_kv, cache_l1_dsa_kidx, page_table, c_prompt, c_sample,
           l0_norm_g, l0_ada_w, l0_ada_b, l0_w_in, l0_cmp_wk, l0_cmp_wv, l0_conv_w, l0_conv_b,
           l0_lru_wr, l0_lru_br, l0_lru_wi, l0_lru_bi, l0_lru_lambda, l0_w_out,
           l1_norm_g, l1_ada_w, l1_ada_b, l1_w_in, l1_lam_q1, l1_lam_k1, l1_lam_q2, l1_lam_k2,
           l1_subln_g, l1_w_out, final_norm_g):
    bp, lp, d = x_prompt.shape
    bs, ls, _ = x_sample.shape
    c_all = jnp.concatenate([c_prompt, c_sample], axis=0)

    def mods(ada_w, ada_b):
        m = _modulation(c_all, ada_w, ada_b)
        mp = tuple(t[:, None] for t in jnp.split(m[:bp], 3, axis=-1))
        ms = tuple(jnp.repeat(t, ls, axis=0)[None] for t in jnp.split(m[bp:], 3, axis=-1))
        return mp, ms

    tl_p = min(512, lp)
    tl_s = bs * ls
    tq = min(Q_TILE, lp)
    w0 = (l0_norm_g, l0_w_in, l0_cmp_wk, l0_cmp_wv, l0_conv_w, l0_conv_b, l0_lru_wr, l0_lru_br,
          l0_lru_wi, l0_lru_bi, l0_lru_lambda, l0_w_out)
    mp0, ms0 = mods(l0_ada_w, l0_ada_b)
    xp, (nsa_kv_p, win_p, conv_p, h_p) = _layer0(x_prompt, mp0, None, w0, tl=tl_p, tq=tq)
    xs, (nsa_kv_s, win_s, conv_s, h_s) = _layer0(
        x_sample, ms0, (cache_l0_nsa_kv, page_table, state_l0_win_kv, state_l0_conv, state_l0_lru_h), w0,
        tl=tl_s, tq=tq)
    w1 = (l1_norm_g, l1_w_in, l1_lam_q1, l1_lam_k1, l1_lam_q2, l1_lam_k2, l1_subln_g, l1_w_out)
    mp1, ms1 = mods(l1_ada_w, l1_ada_b)
    y_p, (diff_kv_p, dsa_kv_p, kidx_p) = _layer1(xp, mp1, None, w1, final_norm_g, tl=tl_p, tq=tq)
    y_s, (diff_kv_s, dsa_kv_s, kidx_s) = _layer1(
        xs, ms1, (cache_l1_diff_kv, cache_l1_dsa_kv, cache_l1_dsa_kidx, page_table), w1, final_norm_g,
        tl=tl_s, tq=tq)
    return (y_p, y_s, nsa_kv_p, nsa_kv_s, win_p, win_s, conv_p, conv_s, h_p, h_s,
            diff_kv_p, diff_kv_s, dsa_kv_p, dsa_kv_s, kidx_p, kidx_s)
```

```python
import functools
import math

import jax
import jax.numpy as jnp
import numpy as np
from jax import lax
from jax.experimental import pallas as pl
from jax.experimental.pallas import tpu as pltpu

F32 = jnp.float32
BF16 = jnp.bfloat16
I32 = jnp.int32

PAGE_SIZE = 128
HEAD_DIM = 64
NSA_HEADS = 8
NSA_KV_HEADS = 2
NSA_CMP_BLOCK = 32
NSA_SEL_BLOCK = 64
NSA_TOPN = 16
NSA_WINDOW = 512
FORCE_SCORE = 1e4
LRU_BLOCKS = 8
LRU_C = 8.0
CONV_WIDTH = 4
DIFF_HALF = 64
DIFF_HEADS = 4
DIFF_KV_HEADS = 2
DIFF_LAMBDA_INIT = 0.8 - 0.6 * math.exp(-0.3 * 1)
DSA_HEADS = 8
DSA_KV_HEADS = 2
IDX_HEADS = 4
IDX_DIM = 64
DSA_TOPK_MAX = 256
NORM_EPS = 1e-6
NEG = -1e30
REMOVED = -3e38
INT_MIN = -2 ** 31
LOG2E = math.log2(math.e)

LANE = 128
VMEM_LIMIT = 56 * 1024 * 1024
KV_TILE = 1024
TRI_TILE = 256
Q_TILE = 256
PROBE_BINADES = 3
COUNT_ROWS = 64
MAX_SWEEP_BRANCHES = 8
PAGES_PER_STEP = 32
HALF_PAGES_PER_STEP = 64
RELAYOUT_ROWS = 4096


def _kv_tile(lk):
    t = min(KV_TILE, lk)
    assert lk % t == 0 and t % TRI_TILE == 0
    return t


def _win_span(tq):
    return _round_up(NSA_WINDOW + tq, LANE)


def _cparams(sem):
    return pltpu.CompilerParams(dimension_semantics=sem, vmem_limit_bytes=VMEM_LIMIT)


def _dot(a, b):
    return jnp.dot(a, b, preferred_element_type=F32)


def _dot_nt(a, b):
    return lax.dot_general(a, b, (((1,), (1,)), ((), ())), preferred_element_type=F32)


def _round_up(x, m):
    return (x + m - 1) // m * m


def _mod_kernel(c_ref, w_ref, b_ref, o_ref):
    o_ref[...] = jnp.dot(c_ref[...], w_ref[...], preferred_element_type=F32,
                         precision=lax.Precision.HIGHEST) + b_ref[...]


def _modulation(c, w, b):
    bc, d = c.shape
    n = w.shape[1]
    tn = 512
    return pl.pallas_call(
        _mod_kernel,
        grid=(n // tn,),
        in_specs=[pl.BlockSpec((bc, d), lambda j: (0, 0)),
                  pl.BlockSpec((d, tn), lambda j: (0, j)),
                  pl.BlockSpec((1, tn), lambda j: (0, j))],
        out_specs=pl.BlockSpec((bc, tn), lambda j: (0, j)),
        out_shape=jax.ShapeDtypeStruct((bc, n), F32),
        compiler_params=_cparams(("arbitrary",)),
        name="modulation",
    )(c, w, b.reshape(1, n))


def _proj_kernel(x_ref, g_ref, sh_ref, sc_ref, w_ref, *o_refs, segs, sigmoid_seg):
    x = x_ref[0]
    y = x * lax.rsqrt(jnp.mean(x * x, axis=-1, keepdims=True) + NORM_EPS)
    h = (y * g_ref[...]) * (1.0 + sc_ref[0]) + sh_ref[0]
    hb = h.astype(BF16)
    for i, ((a, b), o_ref) in enumerate(zip(segs, o_refs)):
        r = _dot(hb, w_ref[:, a:b])
        if i == sigmoid_seg:
            r = jax.nn.sigmoid(r)
        o_ref[0] = r


def _project(x, g, shift, scale, w, segs, sigmoid_seg, tl):
    b, l, d = x.shape
    ts = shift.shape[1]
    tm = 1 if ts == 1 else tl
    mod_map = (lambda bi, li: (bi, 0, 0)) if ts == 1 else (lambda bi, li: (bi, li, 0))
    p = w.shape[1]
    kern = functools.partial(_proj_kernel, segs=tuple(segs), sigmoid_seg=sigmoid_seg)
    return pl.pallas_call(
        kern,
        grid=(b, l // tl),
        in_specs=[pl.BlockSpec((1, tl, d), lambda bi, li: (bi, li, 0)),
                  pl.BlockSpec((1, d), lambda bi, li: (0, 0)),
                  pl.BlockSpec((1, tm, d), mod_map),
                  pl.BlockSpec((1, tm, d), mod_map),
                  pl.BlockSpec((d, p), lambda bi, li: (0, 0))],
        out_specs=[pl.BlockSpec((1, tl, e - a), lambda bi, li: (bi, li, 0)) for a, e in segs],
        out_shape=[jax.ShapeDtypeStruct((b, l, e - a), F32) for a, e in segs],
        compiler_params=_cparams(("parallel", "arbitrary")),
        name="norm_mod_project",
    )(x, g.reshape(1, d), shift, scale, w)


def _chunk_kernel(x_ref, *refs, cw, n_cmp, n_out):
    w2_ref = refs[0] if n_cmp else None
    out_ref = refs[1 if n_cmp else 0]
    cmp_ref = refs[2] if n_cmp else None
    rows = x_ref.shape[1]
    ones_col = jnp.where(lax.broadcasted_iota(I32, (rows, LANE - cw), 1) == 0, 1.0, 0.0) if cw < LANE else None
    for c in range(n_cmp + n_out):
        x = x_ref[0, :, c * cw:(c + 1) * cw]
        if c < n_cmp:
            prod = x.reshape(rows // NSA_SEL_BLOCK, NSA_SEL_BLOCK, cw) * w2_ref[c][None]
            cmp_ref[0, c, 0] = jnp.sum(prod[:, :NSA_CMP_BLOCK], axis=1)
            cmp_ref[0, c, 1] = jnp.sum(prod[:, NSA_CMP_BLOCK:], axis=1)
        else:
            if ones_col is not None:
                x = jnp.concatenate([x, ones_col], axis=-1)
            out_ref[0, c - n_cmp] = x.astype(BF16)


def _split_chunks(x, *, cw, n_out, n_cmp=0, cmp_w=None):
    b, l, w = x.shape
    rows = min(RELAYOUT_ROWS, l)
    assert l % rows == 0 and rows % NSA_SEL_BLOCK == 0
    in_specs = [pl.BlockSpec((1, rows, w), lambda bi, j: (bi, j, 0))]
    args = [x]
    if n_cmp:
        in_specs.append(pl.BlockSpec(cmp_w.shape, lambda bi, j: (0, 0, 0)))
        args.append(cmp_w)
    out_specs = [pl.BlockSpec((1, n_out, rows, LANE), lambda bi, j: (bi, 0, j, 0))]
    out_shape = [jax.ShapeDtypeStruct((b, n_out, l, LANE), BF16)]
    if n_cmp:
        nb = rows // NSA_SEL_BLOCK
        out_specs.append(pl.BlockSpec((1, n_cmp, 2, nb, HEAD_DIM), lambda bi, j: (bi, 0, 0, j, 0)))
        out_shape.append(jax.ShapeDtypeStruct((b, n_cmp, 2, l // NSA_SEL_BLOCK, HEAD_DIM), F32))
    outs = pl.pallas_call(
        functools.partial(_chunk_kernel, cw=cw, n_cmp=n_cmp, n_out=n_out),
        grid=(b, l // rows),
        in_specs=in_specs, out_specs=out_specs, out_shape=out_shape,
        compiler_params=_cparams(("parallel", "arbitrary")),
        name="split_chunks",
    )(*args)
    return outs if n_cmp else outs[0]


def _flash_tile(s_all, bias, v_tile, carry, n_heads, tq, l_in_acc, v_t=False):
    m, l, acc = carry
    s = s_all
    if bias is not None:
        s = jnp.concatenate([s_all[h * tq:(h + 1) * tq] + bias for h in range(n_heads)], axis=0)
    m_new = jnp.maximum(m, jnp.max(s, axis=-1, keepdims=True))
    p = jnp.exp2(s - m_new)
    alpha = jnp.exp2(m - m_new)
    acc = alpha * acc + (_dot_nt if v_t else _dot)(p.astype(BF16), v_tile)
    if not l_in_acc:
        l = alpha * l + jnp.sum(p, axis=-1, keepdims=True)
    return m_new, l, acc


def _causal_sweep(step, carry, q0, tq, tk):
    assert tk % tq == 0 and tk // tq <= MAX_SWEEP_BRANCHES
    n_full = q0 // tk
    carry = lax.fori_loop(0, n_full, lambda j, c: step(pl.multiple_of(j * tk, tk), tk, False, c), carry)
    k0 = pl.multiple_of(n_full * tk, tk)
    branches = [functools.partial(step, k0, w, True) for w in range(tq, tk + tq, tq)]
    return lax.switch((q0 - k0) // tq, branches, carry)


def _flash_init(rows, dv):
    return (jnp.full((rows, 1), NEG, F32), jnp.zeros((rows, 1), F32), jnp.zeros((rows, dv), F32))


def _flash_out(acc):
    return acc[:, :HEAD_DIM] * (1.0 / jnp.maximum(acc[:, HEAD_DIM:HEAD_DIM + 1], 1e-30))


def _stack_heads(q, n, width):
    return jnp.concatenate([q[:, h * width:(h + 1) * width] for h in range(n)], axis=0)


def _pad_lanes(x):
    return jnp.concatenate([x, jnp.zeros((x.shape[0], LANE - x.shape[1]), x.dtype)], axis=-1)


def _nsa_prologue(qg, kw, vw, cmp4, *, q0, tq, nsp, n_top, win_pos0, start):
    hpg = NSA_HEADS // NSA_KV_HEADS
    rows = hpg * tq
    tqp = max(tq, LANE)
    qpos = q0 + lax.broadcasted_iota(I32, (tq, 1), 0)
    qpos_r = jnp.concatenate([qpos] * hpg, axis=0)
    blk = lax.broadcasted_iota(I32, (1, nsp), 1)
    blk_r = lax.broadcasted_iota(I32, (nsp, 1), 0)
    blk_rf = blk_r.astype(F32)
    cur_l = (q0 + lax.broadcasted_iota(I32, (1, tqp), 1)) // NSA_SEL_BLOCK
    vis_e = (blk * NSA_SEL_BLOCK + (NSA_CMP_BLOCK - 1)) <= qpos_r
    vis_o = (blk * NSA_SEL_BLOCK + (NSA_SEL_BLOCK - 1)) <= qpos_r
    kpos_w = win_pos0 + start + lax.broadcasted_iota(I32, (1, kw.shape[0]), 1)
    dlt = qpos - kpos_w
    bias_w = jnp.where(dlt >= 0, jnp.where(dlt < NSA_WINDOW, 0.0, NEG), NEG)

    qs64 = _stack_heads(qg, hpg, HEAD_DIM)
    qs = _pad_lanes(qs64).astype(BF16)
    qs64 = qs64.astype(BF16)

    _, _, acc_w = _flash_tile(_dot_nt(qs, kw), bias_w, vw, _flash_init(rows, LANE), hpg, tq, True)
    o_w = _flash_out(acc_w)

    kce, kco, vce, vco = (x.astype(BF16) for x in cmp4)
    s_e = jnp.where(vis_e, _dot_nt(qs64, kce), NEG)
    s_o = jnp.where(vis_o, _dot_nt(qs64, kco), NEG)
    m = jnp.maximum(jnp.max(s_e, axis=-1, keepdims=True), jnp.max(s_o, axis=-1, keepdims=True))
    p_e = jnp.where(vis_e, jnp.exp2(s_e - m), 0.0)
    p_o = jnp.where(vis_o, jnp.exp2(s_o - m), 0.0)
    den = jnp.sum(p_e, axis=-1, keepdims=True) + jnp.sum(p_o, axis=-1, keepdims=True)
    inv = 1.0 / jnp.maximum(den, 1e-30)
    p_e = p_e * inv
    p_o = p_o * inv
    o_c = _dot(p_e.astype(BF16), vce) + _dot(p_o.astype(BF16), vco)

    pe_h = sum(p_e[h * tq:(h + 1) * tq] for h in range(hpg))
    po_h = sum(p_o[h * tq:(h + 1) * tq] for h in range(hpg))
    imp = pe_h + po_h
    if tqp > tq:
        imp = jnp.concatenate([imp, jnp.zeros((tqp - tq, nsp), F32)], axis=0)
    imp = imp.T
    imp = jnp.where((blk_r == cur_l) | (blk_r == 0), FORCE_SCORE, imp)
    imp = jnp.where(blk_r <= cur_l, imp, NEG)
    sel = jnp.zeros((nsp, tqp), F32)
    for _ in range(n_top):
        mx = jnp.max(imp, axis=0, keepdims=True)
        first = jnp.min(jnp.where(imp == mx, blk_rf, float(nsp)), axis=0, keepdims=True)
        pick = blk_rf == first
        sel = jnp.where(pick & (mx > 0.5 * NEG), 1.0, sel)
        imp = jnp.where(pick, REMOVED, imp)
    return qs, qs64, o_c, o_w, sel.T[:tq].astype(BF16)


def _nsa_combine(gates, branches, tq):
    hpg = NSA_HEADS // NSA_KV_HEADS
    outs = []
    for g, (o_c, o_s, o_w) in enumerate(branches):
        for h in range(hpg):
            r = slice(h * tq, (h + 1) * tq)
            c = g * LANE + 3 * h
            outs.append(gates[:, c:c + 1] * o_c[r] + gates[:, c + 1:c + 2] * o_s[r]
                        + gates[:, c + 2:c + 3] * o_w[r])
    return jnp.concatenate(outs, axis=-1)


def _nsa_kernel(q_ref, g_ref, cmp_ref, ksel_ref, kwin_ref, o_ref, *,
                tq, tk, q_pos0, win_pos0, nsp, n_top):
    G = NSA_KV_HEADS
    hpg = NSA_HEADS // G
    gw = hpg * HEAD_DIM
    qi = pl.program_id(1)
    q0 = q_pos0 + qi * tq
    qpos = q0 + lax.broadcasted_iota(I32, (tq, 1), 0)
    rows = hpg * tq
    q_all = q_ref[0] * (HEAD_DIM ** -0.5 * LOG2E)
    start = pl.multiple_of(jnp.maximum(q0 - NSA_WINDOW - win_pos0, 0), 8)
    span = _win_span(tq)
    pro = [_nsa_prologue(q_all[:, g * gw:(g + 1) * gw],
                         kwin_ref[0, g, pl.ds(start, span), :], kwin_ref[0, G + g, pl.ds(start, span), :],
                         tuple(cmp_ref[0, 2 * g + kv, eo] for kv in range(2) for eo in range(2)),
                         q0=q0, tq=tq, nsp=nsp, n_top=n_top, win_pos0=win_pos0, start=start)
           for g in range(G)]

    blk_col = lax.broadcasted_iota(I32, (nsp, 1), 0)

    def sel_step(k0, w, diagonal, carry):
        kpos = k0 + lax.broadcasted_iota(I32, (1, w), 1)
        expand = jnp.where(blk_col == kpos // NSA_SEL_BLOCK, 1.0, 0.0).astype(BF16)
        out = []
        for g in range(G):
            bias = jnp.where(_dot(pro[g][4], expand) > 0.5, 0.0, NEG)
            if diagonal:
                bias = jnp.where(kpos <= qpos, bias, NEG)
            s = _dot_nt(pro[g][0], ksel_ref[0, g, pl.ds(k0, w), :])
            out.append(_flash_tile(s, bias, ksel_ref[0, G + g, pl.ds(k0, w), :], carry[g], hpg, tq, True))
        return tuple(out)

    res = _causal_sweep(sel_step, tuple(_flash_init(rows, LANE) for _ in range(G)), q0, tq, tk)
    o_ref[0] = _nsa_combine(g_ref[0], [(pro[g][2], _flash_out(res[g][2]), pro[g][3]) for g in range(G)], tq)


def _nsa_attention(q, gates, cmp, ksel, kwin, *, tq, q_pos0, lk, win_pos0):
    bk, lq, _ = q.shape
    G = NSA_KV_HEADS
    nsp = cmp.shape[3]
    lk_pad = ksel.shape[2]
    lw_pad = kwin.shape[2]
    tk = _kv_tile(lk_pad)
    ns = -(-lk // NSA_SEL_BLOCK)
    assert lw_pad >= _win_span(tq) and q_pos0 + lq <= lk_pad
    kern = functools.partial(_nsa_kernel, tq=tq, tk=tk, q_pos0=q_pos0, win_pos0=win_pos0,
                             nsp=nsp, n_top=min(NSA_TOPN, ns))
    qw = NSA_HEADS * HEAD_DIM
    return pl.pallas_call(
        kern,
        grid=(bk, lq // tq),
        in_specs=[pl.BlockSpec((1, tq, qw), lambda b, i: (b, i, 0)),
                  pl.BlockSpec((1, tq, G * LANE), lambda b, i: (b, i, 0)),
                  pl.BlockSpec((1, 2 * G, 2, nsp, HEAD_DIM), lambda b, i: (b, 0, 0, 0, 0)),
                  pl.BlockSpec((1, 2 * G, lk_pad, LANE), lambda b, i: (b, 0, 0, 0)),
                  pl.BlockSpec((1, 2 * G, lw_pad, LANE), lambda b, i: (b, 0, 0, 0))],
        out_specs=pl.BlockSpec((1, tq, qw), lambda b, i: (b, i, 0)),
        out_shape=jax.ShapeDtypeStruct((bk, lq, qw), F32),
        compiler_params=_cparams(("parallel", "arbitrary")),
        name="nsa_attention",
    )(q, gates, cmp, ksel, kwin)


def _nsa_decode_kernel(pt_ref, q_ref, g_ref, new_ref, w2_ref, kwin_ref, *refs,
                       pp, n_steps, tq, n_new, past_len, win_pos0, nsp, n_top):
    page_refs = refs[:pp]
    o_ref, cmp_ref, oc_ref, ow_ref, sel_ref, m_ref, l_ref, acc_ref = refs[pp:]
    G = NSA_KV_HEADS
    hpg = NSA_HEADS // G
    gw = hpg * HEAD_DIM
    rows = hpg * tq
    j = pl.program_id(1)
    keys = pp * PAGE_SIZE
    nb = keys // NSA_SEL_BLOCK
    q_all = q_ref[0] * (HEAD_DIM ** -0.5 * LOG2E)

    def queries(g):
        qs64 = _stack_heads(q_all[:, g * gw:(g + 1) * gw], hpg, HEAD_DIM)
        return _pad_lanes(qs64).astype(BF16), qs64.astype(BF16)

    @pl.when(j == 0)
    def _():
        cmp_ref[...] = jnp.zeros(cmp_ref.shape, F32)

    @pl.when(j < n_steps)
    def _():
        r0 = pl.multiple_of(j * nb, nb)
        for c in range(2 * G):
            x = jnp.concatenate([r[0, c].T for r in page_refs], axis=0)
            prod = x.reshape(nb, NSA_SEL_BLOCK, HEAD_DIM) * w2_ref[c][None]
            cmp_ref[c, 0, pl.ds(r0, nb), :] = jnp.sum(prod[:, :NSA_CMP_BLOCK], axis=1)
            cmp_ref[c, 1, pl.ds(r0, nb), :] = jnp.sum(prod[:, NSA_CMP_BLOCK:], axis=1)

    @pl.when(j == n_steps - 1)
    def _():
        for g in range(G):
            _, _, o_c, o_w, sel = _nsa_prologue(
                q_all[:, g * gw:(g + 1) * gw], kwin_ref[0, g], kwin_ref[0, G + g],
                tuple(cmp_ref[g + G * kv, eo] for kv in range(2) for eo in range(2)),
                q0=past_len, tq=tq, nsp=nsp, n_top=n_top, win_pos0=win_pos0, start=0)
            oc_ref[g] = o_c
            ow_ref[g] = o_w
            sel_ref[g] = sel
        m_ref[...] = jnp.full(m_ref.shape, NEG, F32)
        l_ref[...] = jnp.zeros(l_ref.shape, F32)
        acc_ref[...] = jnp.zeros(acc_ref.shape, F32)

    def update(g, s, bias, v, v_t):
        carry = (m_ref[g], l_ref[g], acc_ref[g])
        m, l, acc = _flash_tile(s, bias, v, carry, hpg, tq, False, v_t=v_t)
        m_ref[g] = m
        l_ref[g] = l
        acc_ref[g] = acc

    @pl.when(j >= n_steps)
    def _():
        kpos = (j - n_steps) * keys + lax.broadcasted_iota(I32, (1, keys), 1)
        blk_col = lax.broadcasted_iota(I32, (nsp, 1), 0)
        expand = jnp.where(blk_col == kpos // NSA_SEL_BLOCK, 1.0, 0.0).astype(BF16)
        for g in range(G):
            _, qs64 = queries(g)
            k_t = jnp.concatenate([r[0, g] for r in page_refs], axis=-1).astype(BF16)
            v_t = jnp.concatenate([r[0, G + g] for r in page_refs], axis=-1).astype(BF16)
            bias = jnp.where(_dot(sel_ref[g], expand) > 0.5, 0.0, NEG)
            update(g, _dot(qs64, k_t), bias, v_t, True)

    @pl.when(j == 2 * n_steps - 1)
    def _():
        new = new_ref[0]
        pad = jnp.zeros((LANE - tq, HEAD_DIM), F32)
        row = lax.broadcasted_iota(I32, (tq, 1), 0)
        col = lax.broadcasted_iota(I32, (1, LANE), 1)
        own = past_len // NSA_SEL_BLOCK
        branches = []
        for g in range(G):
            _, qs64 = queries(g)
            k = jnp.concatenate([new[:, (2 * G + g) * HEAD_DIM:(2 * G + g + 1) * HEAD_DIM], pad], axis=0)
            v = jnp.concatenate([new[:, (3 * G + g) * HEAD_DIM:(3 * G + g + 1) * HEAD_DIM], pad], axis=0)
            picked = sel_ref[g][:, own:own + 1].astype(F32) > 0.5
            bias = jnp.where((col <= row) & (col < n_new) & picked, 0.0, NEG)
            update(g, _dot_nt(qs64, k.astype(BF16)), bias, v.astype(BF16), False)
            o_s = acc_ref[g] * (1.0 / jnp.maximum(l_ref[g], 1e-30))
            branches.append((oc_ref[g], o_s, ow_ref[g]))
        o_ref[0] = _nsa_combine(g_ref[0], branches, tq)


def _nsa_decode(q, gates, pool, page_table, new, w2, kwin, *, n_new, win_pos0):
    bk, tq, qw = q.shape
    n_pages = page_table.shape[1]
    pp = min(HALF_PAGES_PER_STEP, n_pages)
    past_len = n_pages * PAGE_SIZE
    assert n_pages % pp == 0 and past_len % NSA_SEL_BLOCK == 0 and n_new <= NSA_CMP_BLOCK
    assert kwin.shape[2] == _win_span(tq)
    n_steps = n_pages // pp
    G = NSA_KV_HEADS
    rows = (NSA_HEADS // G) * tq
    lk = past_len + n_new
    ns = -(-lk // NSA_SEL_BLOCK)
    nsp = _round_up(ns, LANE)

    def page_map(i):
        return lambda b, j, pt: (pt[b, (j % n_steps) * pp + i], j // n_steps, 0, 0)

    const3 = lambda b, j, pt: (b, 0, 0)
    in_specs = [pl.BlockSpec((1, tq, qw), const3),
                pl.BlockSpec((1, tq, G * LANE), const3),
                pl.BlockSpec((1, tq, new.shape[-1]), const3),
                pl.BlockSpec(w2.shape, lambda b, j, pt: (0, 0, 0)),
                pl.BlockSpec((1,) + kwin.shape[1:], lambda b, j, pt: (b, 0, 0, 0))]
    in_specs += [pl.BlockSpec((1, 2 * G, HEAD_DIM, PAGE_SIZE), page_map(i)) for i in range(pp)]
    kern = functools.partial(_nsa_decode_kernel, pp=pp, n_steps=n_steps, tq=tq, n_new=n_new, past_len=past_len,
                             win_pos0=win_pos0, nsp=nsp, n_top=min(NSA_TOPN, ns))
    return pl.pallas_call(
        kern,
        grid_spec=pltpu.PrefetchScalarGridSpec(
            num_scalar_prefetch=1, grid=(bk, 2 * n_steps), in_specs=in_specs,
            out_specs=pl.BlockSpec((1, tq, qw), const3),
            scratch_shapes=[pltpu.VMEM((2 * G, 2, nsp, HEAD_DIM), F32),
                            pltpu.VMEM((G, rows, HEAD_DIM), F32), pltpu.VMEM((G, rows, HEAD_DIM), F32),
                            pltpu.VMEM((G, tq, nsp), BF16),
                            pltpu.VMEM((G, rows, 1), F32), pltpu.VMEM((G, rows, 1), F32),
                            pltpu.VMEM((G, rows, HEAD_DIM), F32)]),
        out_shape=jax.ShapeDtypeStruct((bk, tq, qw), F32),
        compiler_params=_cparams(("parallel", "arbitrary")),
        name="nsa_decode",
    )(page_table, q, gates, new, w2, kwin, *([pool] * pp))


def _shift_rows(x, d, fill):
    rolled = pltpu.roll(x, d, axis=0)
    row = lax.broadcasted_iota(I32, x.shape, 0)
    return jnp.where(row >= d, rolled, fill)


def _lru_kernel(x_ref, hist_ref, h0_ref, cw_ref, cb_ref, wr_ref, br_ref, wi_ref, bi_ref, lam_ref,
                o_ref, hl_ref, tail_ref, h_ref, *, tl, last_row):
    li = pl.program_id(1)

    @pl.when(li == 0)
    def _():
        tail_ref[...] = jnp.concatenate(
            [jnp.zeros((8 - (CONV_WIDTH - 1), x_ref.shape[-1]), F32), hist_ref[0]], axis=0)
        h_ref[...] = h0_ref[0]

    x = x_ref[0]
    xp = jnp.concatenate([tail_ref[...], x], axis=0)
    cw = cw_ref[...]
    conv = sum(xp[8 - (CONV_WIDTH - 1) + j:8 - (CONV_WIDTH - 1) + j + tl] * cw[j:j + 1]
               for j in range(CONV_WIDTH))
    conv = cb_ref[...] + conv
    tail_ref[...] = x[tl - 8:tl]

    cb16 = conv.astype(BF16)
    r = jax.nn.sigmoid(_dot(cb16, wr_ref[...]) + br_ref[...])
    ig = jax.nn.sigmoid(_dot(cb16, wi_ref[...]) + bi_ref[...])
    log_a = -LRU_C * r * jax.nn.softplus(-lam_ref[...])
    a = jnp.exp(log_a)
    th = jnp.tanh(log_a)
    b = jnp.sqrt(-2.0 * th / (1.0 - th)) * (ig * conv)

    d = 1
    while d < tl:
        a_prev = _shift_rows(a, d, 1.0)
        b_prev = _shift_rows(b, d, 0.0)
        b = a * b_prev + b
        a = a * a_prev
        d *= 2
    h = a * h_ref[...] + b
    o_ref[0] = h
    h_ref[...] = h[tl - 1:tl]

    @pl.when(li == pl.num_programs(1) - 1)
    def _():
        hl_ref[0] = h[last_row:last_row + 1]


def _block_diag(w):
    nb, bw, _ = w.shape
    eye = jnp.eye(nb, dtype=w.dtype)
    return (eye[:, None, :, None] * w[:, :, None, :]).reshape(nb * bw, nb * bw)


def _conv_rglru(x_b, hist, h0, conv_w, conv_b, w_r, b_r, w_i, b_i, lam, *, tl, n_valid):
    b, l, w = x_b.shape
    assert tl >= 8 and l % tl == 0 and n_valid > l - tl
    kern = functools.partial(_lru_kernel, tl=tl, last_row=(n_valid - 1) % tl)
    vec = lambda: pl.BlockSpec((1, w), lambda bi, li: (0, 0))
    h, h_last = pl.pallas_call(
        kern,
        grid=(b, l // tl),
        in_specs=[pl.BlockSpec((1, tl, w), lambda bi, li: (bi, li, 0)),
                  pl.BlockSpec((1, CONV_WIDTH - 1, w), lambda bi, li: (bi, 0, 0)),
                  pl.BlockSpec((1, 1, w), lambda bi, li: (bi, 0, 0)),
                  pl.BlockSpec((CONV_WIDTH, w), lambda bi, li: (0, 0)),
                  vec(),
                  pl.BlockSpec((w, w), lambda bi, li: (0, 0)), vec(),
                  pl.BlockSpec((w, w), lambda bi, li: (0, 0)), vec(), vec()],
        out_specs=[pl.BlockSpec((1, tl, w), lambda bi, li: (bi, li, 0)),
                   pl.BlockSpec((1, 1, w), lambda bi, li: (bi, 0, 0))],
        out_shape=[jax.ShapeDtypeStruct((b, l, w), F32), jax.ShapeDtypeStruct((b, 1, w), F32)],
        scratch_shapes=[pltpu.VMEM((8, w), F32), pltpu.VMEM((1, w), F32)],
        compiler_params=_cparams(("parallel", "arbitrary")),
        name="conv_rglru",
    )(x_b, hist, h0.reshape(b, 1, w), conv_w, conv_b.reshape(1, w),
      _block_diag(w_r).astype(BF16), b_r.reshape(1, w), _block_diag(w_i).astype(BF16), b_i.reshape(1, w),
      lam.reshape(1, w))
    return h, h_last.reshape(b, w)


def _out_kernel(oa_ref, za_ref, ob_ref, zb_ref, x_ref, gate_ref, w_ref, fg_ref, o_ref, *, final_norm):
    half = oa_ref.shape[-1]
    ma = (oa_ref[0] * jax.nn.silu(za_ref[0])).astype(BF16)
    mb = (ob_ref[0] * jax.nn.silu(zb_ref[0])).astype(BF16)
    y = _dot(ma, w_ref[0:half, :]) + _dot(mb, w_ref[half:2 * half, :])
    out = x_ref[0] + gate_ref[0] * y
    if final_norm:
        out = out * lax.rsqrt(jnp.mean(out * out, axis=-1, keepdims=True) + NORM_EPS) * fg_ref[...]
    o_ref[0] = out


def _out_project(o_a, z_a, o_b, z_b, x, gate, w_out, final_g, *, tl, final_norm):
    b, l, d = x.shape
    half = o_a.shape[-1]
    ts = gate.shape[1]
    tm = 1 if ts == 1 else tl
    mod_map = (lambda bi, li: (bi, 0, 0)) if ts == 1 else (lambda bi, li: (bi, li, 0))
    act = lambda: pl.BlockSpec((1, tl, half), lambda bi, li: (bi, li, 0))
    return pl.pallas_call(
        functools.partial(_out_kernel, final_norm=final_norm),
        grid=(b, l // tl),
        in_specs=[act(), act(), act(), act(),
                  pl.BlockSpec((1, tl, d), lambda bi, li: (bi, li, 0)),
                  pl.BlockSpec((1, tm, d), mod_map),
                  pl.BlockSpec((2 * half, d), lambda bi, li: (0, 0)),
                  pl.BlockSpec((1, d), lambda bi, li: (0, 0))],
        out_specs=pl.BlockSpec((1, tl, d), lambda bi, li: (bi, li, 0)),
        out_shape=jax.ShapeDtypeStruct((b, l, d), F32),
        compiler_params=_cparams(("parallel", "arbitrary")),
        name="out_project",
    )(o_a, z_a, o_b, z_b, x, gate, w_out, final_g.reshape(1, d))


def _diff_queries(q, tq):
    hpg = DIFF_HEADS // DIFF_KV_HEADS
    q = q * (DIFF_HALF ** -0.5 * LOG2E)
    zero = jnp.zeros((tq, DIFF_HALF), F32)
    parts = []
    for mp in range(2):
        for h in range(hpg):
            qh = q[:, (2 * h + mp) * DIFF_HALF:(2 * h + mp + 1) * DIFF_HALF]
            parts.append(jnp.concatenate([qh, zero] if mp == 0 else [zero, qh], axis=-1))
    return jnp.concatenate(parts, axis=0).astype(BF16)


def _diff_finish(l, acc, lamv, subg, tq):
    hpg = DIFF_HEADS // DIFF_KV_HEADS
    o = acc * (1.0 / jnp.maximum(l, 1e-30))
    lam = (jnp.exp(jnp.sum(lamv[0:1] * lamv[1:2], axis=-1, keepdims=True))
           - jnp.exp(jnp.sum(lamv[2:3] * lamv[3:4], axis=-1, keepdims=True)) + DIFF_LAMBDA_INIT)
    half = hpg * tq
    od = o[0:half] - lam * o[half:2 * half]
    od = od * lax.rsqrt(jnp.mean(od * od, axis=-1, keepdims=True) + NORM_EPS)
    od = od * subg * (1.0 - DIFF_LAMBDA_INIT)
    return jnp.concatenate([od[h * tq:(h + 1) * tq] for h in range(hpg)], axis=-1)


def _diff_kernel(q_ref, k_ref, v_ref, lamv_ref, subg_ref, o_ref, *, tq, tk, q_pos0):
    hpg = DIFF_HEADS // DIFF_KV_HEADS
    qi = pl.program_id(2)
    q0 = q_pos0 + qi * tq
    n_maps = 2 * hpg
    rows = n_maps * tq
    qpos = q0 + lax.broadcasted_iota(I32, (tq, 1), 0)
    qs = _diff_queries(q_ref[0], tq)

    def step(k0, w, diagonal, carry):
        s = _dot_nt(qs, k_ref[0, 0, pl.ds(k0, w), :])
        bias = None
        if diagonal:
            kpos = k0 + lax.broadcasted_iota(I32, (1, w), 1)
            bias = jnp.where(kpos <= qpos, 0.0, NEG)
        return _flash_tile(s, bias, v_ref[0, 0, pl.ds(k0, w), :], carry, n_maps, tq, False)

    m, l, acc = _causal_sweep(step, _flash_init(rows, 2 * DIFF_HALF), q0, tq, tk)
    o_ref[0] = _diff_finish(l, acc, lamv_ref[...], subg_ref[...], tq)


def _diff_decode_kernel(pt_ref, q_ref, new_ref, lamv_ref, subg_ref, *refs, pp, tq, n_new):
    page_refs = refs[:pp]
    o_ref, m_ref, l_ref, acc_ref = refs[pp:]
    G = DIFF_KV_HEADS
    n_maps = 2 * (DIFF_HEADS // G)
    gw = n_maps * DIFF_HALF
    j = pl.program_id(1)

    @pl.when(j == 0)
    def _():
        m_ref[...] = jnp.full(m_ref.shape, NEG, F32)
        l_ref[...] = jnp.zeros(l_ref.shape, F32)
        acc_ref[...] = jnp.zeros(acc_ref.shape, F32)

    def update(g, qs, k, v, bias):
        carry = (m_ref[g], l_ref[g], acc_ref[g])
        m, l, acc = _flash_tile(_dot_nt(qs, k), bias, v, carry, n_maps, tq, False)
        m_ref[g] = m
        l_ref[g] = l
        acc_ref[g] = acc

    qs = [_diff_queries(q_ref[0][:, g * gw:(g + 1) * gw], tq) for g in range(G)]
    for g in range(G):
        k = jnp.concatenate([r[0, pl.ds(g, PAGE_SIZE, stride=2 * G), :] for r in page_refs], axis=0)
        v = jnp.concatenate([r[0, pl.ds(G + g, PAGE_SIZE, stride=2 * G), :] for r in page_refs], axis=0)
        update(g, qs[g], k.astype(BF16), v.astype(BF16), None)

    @pl.when(j == pl.num_programs(1) - 1)
    def _():
        new = new_ref[0]
        pad = jnp.zeros((LANE - tq, 2 * DIFF_HALF), F32)
        row = lax.broadcasted_iota(I32, (tq, 1), 0)
        col = lax.broadcasted_iota(I32, (1, LANE), 1)
        bias = jnp.where((col <= row) & (col < n_new), 0.0, NEG)
        outs = []
        for g in range(G):
            k = jnp.concatenate([new[:, g * 2 * DIFF_HALF:(g + 1) * 2 * DIFF_HALF], pad], axis=0)
            v = jnp.concatenate([new[:, (G + g) * 2 * DIFF_HALF:(G + g + 1) * 2 * DIFF_HALF], pad], axis=0)
            update(g, qs[g], k.astype(BF16), v.astype(BF16), bias)
            outs.append(_diff_finish(l_ref[g], acc_ref[g], lamv_ref[...], subg_ref[...], tq))
        o_ref[0] = jnp.concatenate(outs, axis=-1)


def _diff_decode(q, pool, page_table, new, lamv, subln_g, *, n_new):
    bk, tq, qw = q.shape
    n_pages = page_table.shape[1]
    pp = min(PAGES_PER_STEP, n_pages)
    assert n_pages % pp == 0
    G = DIFF_KV_HEADS
    rows = 2 * (DIFF_HEADS // G) * tq

    def page_map(i):
        return lambda b, j, pt: (pt[b, j * pp + i], 0, 0)

    in_specs = [pl.BlockSpec((1, tq, qw), lambda b, j, pt: (b, 0, 0)),
                pl.BlockSpec((1, tq, new.shape[-1]), lambda b, j, pt: (b, 0, 0)),
                pl.BlockSpec((4, DIFF_HALF), lambda b, j, pt: (0, 0)),
                pl.BlockSpec((1, 2 * DIFF_HALF), lambda b, j, pt: (0, 0))]
    in_specs += [pl.BlockSpec((1,) + pool.shape[1:], page_map(i)) for i in range(pp)]
    return pl.pallas_call(
        functools.partial(_diff_decode_kernel, pp=pp, tq=tq, n_new=n_new),
        grid_spec=pltpu.PrefetchScalarGridSpec(
            num_scalar_prefetch=1, grid=(bk, n_pages // pp), in_specs=in_specs,
            out_specs=pl.BlockSpec((1, tq, qw), lambda b, j, pt: (b, 0, 0)),
            scratch_shapes=[pltpu.VMEM((G, rows, 1), F32), pltpu.VMEM((G, rows, 1), F32),
                            pltpu.VMEM((G, rows, 2 * DIFF_HALF), F32)]),
        out_shape=jax.ShapeDtypeStruct((bk, tq, qw), F32),
        compiler_params=_cparams(("parallel", "arbitrary")),
        name="diff_decode",
    )(page_table, q, new, lamv, subln_g.reshape(1, 2 * DIFF_HALF), *([pool] * pp))


def _diff_attention(q, kv, lamv, subln_g, *, tq, q_pos0):
    bk, lq, _ = q.shape
    G = DIFF_KV_HEADS
    lk_pad = kv.shape[2]
    tk = _kv_tile(lk_pad)
    gw = (DIFF_HEADS // G) * 2 * DIFF_HALF
    assert q_pos0 + lq <= lk_pad
    return pl.pallas_call(
        functools.partial(_diff_kernel, tq=tq, tk=tk, q_pos0=q_pos0),
        grid=(bk, G, lq // tq),
        in_specs=[pl.BlockSpec((1, tq, gw), lambda b, g, i: (b, i, g)),
                  pl.BlockSpec((1, 1, lk_pad, 2 * DIFF_HALF), lambda b, g, i: (b, g, 0, 0)),
                  pl.BlockSpec((1, 1, lk_pad, 2 * DIFF_HALF), lambda b, g, i: (b, G + g, 0, 0)),
                  pl.BlockSpec((4, DIFF_HALF), lambda b, g, i: (0, 0)),
                  pl.BlockSpec((1, 2 * DIFF_HALF), lambda b, g, i: (0, 0))],
        out_specs=pl.BlockSpec((1, tq, gw), lambda b, g, i: (b, i, g)),
        out_shape=jax.ShapeDtypeStruct((bk, lq, DIFF_HEADS * 2 * DIFF_HALF), F32),
        compiler_params=_cparams(("parallel", "parallel", "arbitrary")),
        name="diff_attention",
    )(q, kv, kv, lamv, subln_g.reshape(1, 2 * DIFF_HALF))


def _dsa_keys(s_all, wi, causal, tq):
    score = jnp.zeros((tq, s_all.shape[1]), F32)
    for h in range(IDX_HEADS):
        score = score + wi[:, h:h + 1] * jnp.maximum(s_all[h * tq:(h + 1) * tq], 0.0)
    bits = pltpu.bitcast(score, I32)
    key = jnp.where(bits < 0, bits ^ 0x7FFFFFFF, bits)
    key = jnp.where(score == 0.0, 0, key)
    key = jnp.where(score > 0.5 * NEG, key, INT_MIN)
    return key if causal is None else jnp.where(causal, key, INT_MIN)


def _dsa_threshold(read, n_tiles, kmax, *, tq, tk, n_sel, transposed, unroll):
    lanes = tk // LANE

    def count(*bounds):
        def f(j, accs):
            keys = read(j)
            out = []
            for bound, acc in zip(bounds, accs):
                hit = jnp.where(keys >= bound, 1.0, 0.0)
                if transposed:
                    acc = acc + jnp.sum(hit.reshape(tk // COUNT_ROWS, COUNT_ROWS, tq), axis=0)
                else:
                    for c in range(lanes):
                        acc = acc + hit[:, c * LANE:(c + 1) * LANE]
                out.append(acc)
            return tuple(out)
        acc0 = jnp.zeros((COUNT_ROWS, tq) if transposed else (tq, LANE), F32)
        accs = lax.fori_loop(0, n_tiles, f, tuple(acc0 for _ in bounds), unroll=unroll)
        return [jnp.sum(acc, axis=0 if transposed else -1, keepdims=True) for acc in accs]

    k_f = float(n_sel)
    probe = jnp.maximum(kmax - (PROBE_BINADES << 23), 1)
    c_adm, c_nn, c_pos, c_probe = count(INT_MIN + 1, 0, 1, probe)
    few, pos, zero, high = c_adm < k_f, c_pos >= k_f, c_nn >= k_f, c_probe >= k_f
    lo0 = jnp.where(pos, jnp.where(high, probe, 1), jnp.where(zero, 0, INT_MIN))
    hi0 = jnp.where(pos, jnp.where(high, kmax, probe - 1), jnp.where(zero, 0, jnp.where(few, INT_MIN, -1)))

    def unfinished(lo_hi):
        lo, hi = lo_hi
        return jnp.max(jnp.where(lo < hi, 1.0, 0.0)) > 0.0

    def bisect(lo_hi):
        lo, hi = lo_hi
        mid = (lo >> 1) + (hi >> 1) + ((lo | hi) & 1)
        cnt, = count(mid)
        lo = jnp.where(cnt >= k_f, mid, lo)
        hi = jnp.where(cnt > k_f, hi, jnp.where(cnt == k_f, mid, mid - 1))
        return lo, hi

    thr, _ = lax.while_loop(unfinished, lambda s: bisect(bisect(s)), (lo0, hi0))
    thr = jnp.maximum(thr, INT_MIN + 1)
    n_gt, = count(thr + 1)
    return thr, k_f - n_gt


def _dsa_bias(key, thr, need, seen, tri):
    tied = key == thr
    tied_b = jnp.where(tied, 1.0, 0.0).astype(BF16)
    t = tri.shape[0]
    ranks = []
    for c in range(key.shape[1] // t):
        r = _dot(tied_b[:, c * t:(c + 1) * t], tri) + seen
        ranks.append(r)
        seen = r[:, t - 1:t]
    rank = jnp.concatenate(ranks, axis=-1)
    return jnp.where(key > thr, 0.0, jnp.where(tied, jnp.where(rank <= need, 0.0, NEG), NEG)), seen


def _dsa_kernel(q_ref, qi_ref, kw_ref, kidx_ref, kv_ref, tri_ref, o_ref, key_ref, keyt_ref, *,
                tq, tk, q_pos0, n_sel):
    G = DSA_KV_HEADS
    hpg = DSA_HEADS // G
    q0 = q_pos0 + pl.program_id(1) * tq
    qpos = q0 + lax.broadcasted_iota(I32, (tq, 1), 0)
    n_tiles = (q0 + tq - 1) // tk + 1
    lanes = tk // LANE

    qidx = _pad_lanes(_stack_heads(qi_ref[0] * (IDX_DIM ** -0.5), IDX_HEADS, IDX_DIM)).astype(BF16)
    wi = kw_ref[0][:, IDX_DIM:IDX_DIM + IDX_HEADS] * (IDX_HEADS ** -0.5)

    def score_step(j, kmax, diagonal):
        k0 = pl.multiple_of(j * tk, tk)
        s_all = _dot_nt(qidx, kidx_ref[0, 0, pl.ds(k0, tk), :])
        causal = (k0 + lax.broadcasted_iota(I32, (1, tk), 1)) <= qpos if diagonal else None
        key = _dsa_keys(s_all, wi, causal, tq)
        key_ref[j] = key
        key_t = key.T
        keyt_ref[j] = key_t
        return jnp.maximum(kmax, jnp.max(key_t, axis=0, keepdims=True))

    kmax = lax.fori_loop(0, q0 // tk, functools.partial(score_step, diagonal=False), jnp.full((1, tq), INT_MIN, I32))
    kmax = lax.fori_loop(q0 // tk, n_tiles, functools.partial(score_step, diagonal=True), kmax)
    thr, need = _dsa_threshold(lambda j: keyt_ref[j], n_tiles, kmax,
                               tq=tq, tk=tk, n_sel=n_sel, transposed=True, unroll=False)

    def along_rows(v):
        rep = jnp.broadcast_to(v, (LANE, tq)).T
        return jnp.concatenate([rep] * lanes, axis=-1)

    thr = along_rows(thr)
    need = along_rows(need)

    q = q_ref[0] * (HEAD_DIM ** -0.5 * LOG2E)
    qs = [_pad_lanes(_stack_heads(q[:, g * hpg * HEAD_DIM:(g + 1) * hpg * HEAD_DIM], hpg, HEAD_DIM)).astype(BF16)
          for g in range(G)]
    rows = hpg * tq

    def att_step(k0, w, diagonal, carry):
        seen, flash = carry
        bias, seen = _dsa_bias(key_ref[k0 // tk, :, 0:w], thr[:, :w], need[:, :w], seen, tri_ref[...])
        out = []
        for g in range(G):
            s = _dot_nt(qs[g], kv_ref[0, g, pl.ds(k0, w), :])
            out.append(_flash_tile(s, bias, kv_ref[0, G + g, pl.ds(k0, w), :], flash[g], hpg, tq, True))
        return seen, tuple(out)

    init = (jnp.zeros((tq, 1), F32), tuple(_flash_init(rows, LANE) for _ in range(G)))
    _, res = _causal_sweep(att_step, init, q0, tq, tk)
    outs = []
    for g in range(G):
        o = _flash_out(res[g][2])
        outs.extend(o[h * tq:(h + 1) * tq] for h in range(hpg))
    o_ref[0] = jnp.concatenate(outs, axis=-1)


def _dsa_attention(q, qi, kw, kidx, kv, *, tq, q_pos0, lk):
    bk, lq, _ = q.shape
    lk_pad = kv.shape[2]
    tk = _kv_tile(lk_pad)
    n_sel = min(DSA_TOPK_MAX, lk // 4)
    assert q_pos0 + lq <= lk_pad and tk >= n_sel and tq % LANE == 0
    kern = functools.partial(_dsa_kernel, tq=tq, tk=tk, q_pos0=q_pos0, n_sel=n_sel)
    tri = jnp.triu(jnp.ones((TRI_TILE, TRI_TILE), BF16))
    return pl.pallas_call(
        kern,
        grid=(bk, lq // tq),
        in_specs=[pl.BlockSpec((1, tq, DSA_HEADS * HEAD_DIM), lambda b, i: (b, i, 0)),
                  pl.BlockSpec((1, tq, IDX_HEADS * IDX_DIM), lambda b, i: (b, i, 0)),
                  pl.BlockSpec((1, tq, LANE), lambda b, i: (b, i, 0)),
                  pl.BlockSpec((1, 1, lk_pad, LANE), lambda b, i: (b, 0, 0, 0)),
                  pl.BlockSpec((1, 4, lk_pad, LANE), lambda b, i: (b, 0, 0, 0)),
                  pl.BlockSpec((TRI_TILE, TRI_TILE), lambda b, i: (0, 0))],
        out_specs=pl.BlockSpec((1, tq, DSA_HEADS * HEAD_DIM), lambda b, i: (b, i, 0)),
        out_shape=jax.ShapeDtypeStruct((bk, lq, DSA_HEADS * HEAD_DIM), F32),
        scratch_shapes=[pltpu.VMEM((lk_pad // tk, tq, tk), I32), pltpu.VMEM((lk_pad // tk, tk, tq), I32)],
        compiler_params=_cparams(("parallel", "arbitrary")),
        name="dsa_attention",
    )(q, qi, kw, kidx, kv, tri)


def _dsa_decode_kernel(pt_ref, q_ref, qi_ref, kw_ref, newi_ref, newkv_ref, tri_ref, *refs,
                       pp, n_steps, tq, n_new, n_sel):
    ipage_refs = refs[:pp]
    kvpage_refs = refs[pp:2 * pp]
    o_ref, key_ref, thr_ref, need_ref, seen_ref, m_ref, l_ref, acc_ref = refs[2 * pp:]
    G = DSA_KV_HEADS
    hpg = DSA_HEADS // G
    gw = hpg * HEAD_DIM
    j = pl.program_id(1)
    keys = pp * PAGE_SIZE
    qidx = _stack_heads(qi_ref[0] * (IDX_DIM ** -0.5), IDX_HEADS, IDX_DIM).astype(BF16)
    wi = kw_ref[0][:, IDX_DIM:IDX_DIM + IDX_HEADS] * (IDX_HEADS ** -0.5)
    q_all = q_ref[0] * (HEAD_DIM ** -0.5 * LOG2E)
    row = lax.broadcasted_iota(I32, (tq, 1), 0)
    col = lax.broadcasted_iota(I32, (1, LANE), 1)
    own = (col <= row) & (col < n_new)

    def queries(g):
        return _stack_heads(q_all[:, g * gw:(g + 1) * gw], hpg, HEAD_DIM).astype(BF16)

    @pl.when(j < n_steps)
    def _():
        k_t = jnp.concatenate([r[0, 0] for r in ipage_refs], axis=-1).astype(BF16)
        key_ref[j] = _dsa_keys(_dot(qidx, k_t), wi, None, tq)

    @pl.when(j == n_steps - 1)
    def _():
        pad = jnp.zeros((LANE - tq, IDX_DIM), F32)
        k_new = jnp.concatenate([newi_ref[0], pad], axis=0).astype(BF16)
        key_new = _dsa_keys(_dot_nt(qidx, k_new), wi, own, tq)
        key_ref[n_steps] = jnp.concatenate([key_new, jnp.full((tq, keys - LANE), INT_MIN, I32)], axis=-1)
        kmax = jnp.full((tq, 1), INT_MIN, I32)
        for t in range(n_steps + 1):
            kmax = jnp.maximum(kmax, jnp.max(key_ref[t], axis=-1, keepdims=True))
        thr, need = _dsa_threshold(lambda t: key_ref[t], n_steps + 1, kmax, tq=tq, tk=keys, n_sel=n_sel,
                                   transposed=False, unroll=True)
        thr_ref[...] = thr
        need_ref[...] = need
        seen_ref[...] = jnp.zeros(seen_ref.shape, F32)
        m_ref[...] = jnp.full(m_ref.shape, NEG, F32)
        l_ref[...] = jnp.zeros(l_ref.shape, F32)
        acc_ref[...] = jnp.zeros(acc_ref.shape, F32)

    def update(g, s, bias, v, v_t):
        carry = (m_ref[g], l_ref[g], acc_ref[g])
        m, l, acc = _flash_tile(s, bias, v, carry, hpg, tq, False, v_t=v_t)
        m_ref[g] = m
        l_ref[g] = l
        acc_ref[g] = acc

    @pl.when(j >= n_steps)
    def _():
        bias, seen = _dsa_bias(key_ref[j - n_steps], thr_ref[...], need_ref[...], seen_ref[...], tri_ref[...])
        seen_ref[...] = seen
        for g in range(G):
            k_t = jnp.concatenate([r[0, g] for r in kvpage_refs], axis=-1).astype(BF16)
            v_t = jnp.concatenate([r[0, G + g] for r in kvpage_refs], axis=-1).astype(BF16)
            update(g, _dot(queries(g), k_t), bias, v_t, True)

    @pl.when(j == 2 * n_steps - 1)
    def _():
        new = newkv_ref[0]
        pad = jnp.zeros((LANE - tq, HEAD_DIM), F32)
        bias, _ = _dsa_bias(key_ref[n_steps][:, :LANE], thr_ref[...], need_ref[...], seen_ref[...],
                            tri_ref[0:LANE, 0:LANE])
        outs = []
        for g in range(G):
            k = jnp.concatenate([new[:, g * HEAD_DIM:(g + 1) * HEAD_DIM], pad], axis=0).astype(BF16)
            v = jnp.concatenate([new[:, (G + g) * HEAD_DIM:(G + g + 1) * HEAD_DIM], pad], axis=0).astype(BF16)
            update(g, _dot_nt(queries(g), k), bias, v, False)
            o = acc_ref[g] * (1.0 / jnp.maximum(l_ref[g], 1e-30))
            outs.extend(o[h * tq:(h + 1) * tq] for h in range(hpg))
        o_ref[0] = jnp.concatenate(outs, axis=-1)


def _dsa_decode(q, qi, kw, kidx_pool, kv_pool, page_table, new_kidx, new_kv, *, n_new):
    bk, tq, qw = q.shape
    n_pages = page_table.shape[1]
    pp = min(HALF_PAGES_PER_STEP, n_pages)
    assert n_pages % pp == 0 and n_new <= tq
    n_steps = n_pages // pp
    keys = pp * PAGE_SIZE
    lk = n_pages * PAGE_SIZE + n_new
    n_sel = min(DSA_TOPK_MAX, lk // 4)
    G = DSA_KV_HEADS
    rows = (DSA_HEADS // G) * tq
    tri = jnp.triu(jnp.ones((TRI_TILE, TRI_TILE), BF16))
    assert keys % TRI_TILE == 0 and keys >= n_sel

    def ipage_map(i):
        return lambda b, j, pt: (pt[b, jnp.minimum(j, n_steps - 1) * pp + i], 0, 0, 0)

    def kvpage_map(i):
        return lambda b, j, pt: (pt[b, jnp.maximum(j - n_steps, 0) * pp + i], 0, 0, 0)

    const3 = lambda b, j, pt: (b, 0, 0)
    in_specs = [pl.BlockSpec((1, tq, qw), const3),
                pl.BlockSpec((1, tq, qi.shape[-1]), const3),
                pl.BlockSpec((1, tq, LANE), const3),
                pl.BlockSpec((1, tq, IDX_DIM), const3),
                pl.BlockSpec((1, tq, new_kv.shape[-1]), const3),
                pl.BlockSpec((TRI_TILE, TRI_TILE), lambda b, j, pt: (0, 0))]
    in_specs += [pl.BlockSpec((1,) + kidx_pool.shape[1:], ipage_map(i)) for i in range(pp)]
    in_specs += [pl.BlockSpec((1,) + kv_pool.shape[1:], kvpage_map(i)) for i in range(pp)]
    kern = functools.partial(_dsa_decode_kernel, pp=pp, n_steps=n_steps, tq=tq, n_new=n_new, n_sel=n_sel)
    return pl.pallas_call(
        kern,
        grid_spec=pltpu.PrefetchScalarGridSpec(
            num_scalar_prefetch=1, grid=(bk, 2 * n_steps), in_specs=in_specs,
            out_specs=pl.BlockSpec((1, tq, qw), const3),
            scratch_shapes=[pltpu.VMEM((n_steps + 1, tq, keys), I32),
                            pltpu.VMEM((tq, 1), I32), pltpu.VMEM((tq, 1), F32), pltpu.VMEM((tq, 1), F32),
                            pltpu.VMEM((G, rows, 1), F32), pltpu.VMEM((G, rows, 1), F32),
                            pltpu.VMEM((G, rows, HEAD_DIM), F32)]),
        out_shape=jax.ShapeDtypeStruct((bk, tq, qw), F32),
        compiler_params=_cparams(("parallel", "arbitrary")),
        name="dsa_decode",
    )(page_table, q, qi, kw, new_kidx, new_kv, tri, *([kidx_pool] * pp), *([kv_pool] * pp))


L0_SIZES = (512, 768, 24, 512, 512, 512)
L1_SIZES = (512, 256, 256, 512, 512, 128, 128, 256, 64, 4, 512)


def _l0_weight(w_in):
    d = w_in.shape[0]
    q, kv6, gl, z_a, x_b, z_b = jnp.split(w_in, np.cumsum(L0_SIZES)[:-1].tolist(), axis=1)
    pad = jnp.zeros((d, LANE - 12), w_in.dtype)
    w = jnp.concatenate([q, kv6, z_a, x_b, z_b, gl[:, :12], pad, gl[:, 12:], pad], axis=1)
    segs = [(0, 512), (512, 1024), (1024, 1280), (1280, 1792), (1792, 2304), (2304, 2816), (2816, 3072)]
    return w.astype(BF16), segs


def _l1_weight(w_in):
    d = w_in.shape[0]
    qc, kc, vc, z_c, qd, kd, vd, qi, ki, wi, z_d = jnp.split(w_in, np.cumsum(L1_SIZES)[:-1].tolist(), axis=1)
    pad = jnp.zeros((d, LANE - IDX_DIM - IDX_HEADS), w_in.dtype)
    w = jnp.concatenate([qc, kc, vc, z_c, qd, kd, vd, qi, z_d, ki, wi, pad], axis=1)
    segs = [(0, 512), (512, 1024), (1024, 1536), (1536, 2048), (2048, 2304), (2304, 2560), (2560, 3072),
            (3072, 3200)]
    return w.astype(BF16), segs


def _pad_rows(x, n):
    return jnp.pad(x, ((0, 0), (0, n - x.shape[1]), (0, 0)))


def _cols_pool(pool):
    npool, ps = pool.shape[:2]
    cw = pool.shape[-1]
    perm = (0,) + tuple(range(2, pool.ndim)) + (1,)
    return jnp.transpose(pool, perm).reshape(npool, -1, cw, ps)


def _layer0(x, mod, past, w, *, tl, tq):
    (norm_g, w_in, cmp_wk, cmp_wv, conv_w, conv_b, lru_wr, lru_br, lru_wi, lru_bi, lru_lambda, w_out) = w
    shift, scale, gate = mod
    b, l, d = x.shape
    w_p, segs = _l0_weight(w_in)
    flat = shift.shape[1] != 1
    xin = x.reshape(1, b * l, d) if flat else x
    q, kvp, kvw, z_a, x_b, z_b, gates = _project(xin, norm_g, shift, scale, w_p, segs, 6, tl)
    if flat:
        q, kvp, kvw, z_a, x_b, z_b, gates = (t.reshape(b, l, -1) for t in (q, kvp, kvw, z_a, x_b, z_b, gates))
    wk2 = jnp.concatenate([cmp_wk, cmp_wk], axis=0)
    wv2 = jnp.concatenate([cmp_wv, cmp_wv], axis=0)
    w2 = jnp.stack([wk2, wk2, wv2, wv2], axis=0)
    lq = _round_up(l, 8)
    if past is None:
        ksel, cmp = _split_chunks(kvp, cw=HEAD_DIM, n_out=4, n_cmp=4, cmp_w=w2)
        kwin = _split_chunks(kvw, cw=HEAD_DIM, n_out=4)
        kv_win = kvw
        hist = jnp.zeros((b, CONV_WIDTH - 1, x_b.shape[-1]), F32)
        h0 = jnp.zeros((b, x_b.shape[-1]), F32)
        nsp = _round_up(cmp.shape[3], LANE)
        cmp = jnp.pad(cmp, ((0, 0), (0, 0), (0, 0), (0, nsp - cmp.shape[3]), (0, 0)))
        cmp = cmp.reshape(b, 2, 2, 2, nsp, HEAD_DIM).transpose(0, 2, 1, 3, 4, 5).reshape(b, 4, 2, nsp, HEAD_DIM)
        o_a = _nsa_attention(_pad_rows(q, lq), _pad_rows(gates, lq), cmp, ksel, kwin,
                             tq=min(tq, lq), q_pos0=0, lk=l, win_pos0=0)[:, :l]
    else:
        pool, table, win_buf, hist, h0 = past
        assert lq <= tq
        kv_win = jnp.concatenate([win_buf.reshape(b, win_buf.shape[1], -1), kvw], axis=1)
        lw_pad = _win_span(lq)
        assert kv_win.shape[1] <= lw_pad
        kwin = _split_chunks(_pad_rows(kv_win, lw_pad), cw=HEAD_DIM, n_out=4)
        o_a = _nsa_decode(_pad_rows(q, lq), _pad_rows(gates, lq), _cols_pool(pool), table, _pad_rows(kvp, lq),
                          w2, kwin, n_new=l, win_pos0=table.shape[1] * PAGE_SIZE - win_buf.shape[1])[:, :l]
    o_b, h_last = _conv_rglru(_pad_rows(x_b, lq), hist, h0, conv_w, conv_b, lru_wr, lru_br, lru_wi, lru_bi,
                              lru_lambda, tl=min(256, lq), n_valid=l)
    o_b = o_b[:, :l]
    fl = (lambda t: t.reshape(1, b * l, -1)) if flat else (lambda t: t)
    x_new = _out_project(fl(o_a), fl(z_a), fl(o_b), fl(z_b), xin, gate, w_out.astype(BF16),
                         jnp.ones((d,), F32), tl=tl, final_norm=False).reshape(b, l, d)
    win_keep = min(NSA_WINDOW, kv_win.shape[1])
    conv_src = jnp.concatenate([hist, x_b], axis=1) if l < CONV_WIDTH - 1 else x_b
    states = (kvp.reshape(b, l, 4, NSA_KV_HEADS, HEAD_DIM),
              kv_win[:, -win_keep:].reshape(b, win_keep, 2, NSA_KV_HEADS, HEAD_DIM),
              conv_src[:, -(CONV_WIDTH - 1):], h_last)
    return x_new, states


def _layer1(x, mod, past, w, final_g, *, tl, tq):
    (norm_g, w_in, lam_q1, lam_k1, lam_q2, lam_k2, subln_g, w_out) = w
    shift, scale, gate = mod
    b, l, d = x.shape
    w_p, segs = _l1_weight(w_in)
    flat = shift.shape[1] != 1
    xin = x.reshape(1, b * l, d) if flat else x
    qc, kvc, z_c, qd, kvd, qi, z_d, kiw = _project(xin, norm_g, shift, scale, w_p, segs, -1, tl)
    if flat:
        qc, kvc, z_c, qd, kvd, qi, z_d, kiw = (t.reshape(b, l, -1) for t in (qc, kvc, z_c, qd, kvd, qi, z_d, kiw))
    lq = _round_up(l, 8)
    lamv = jnp.stack([lam_q1, lam_k1, lam_q2, lam_k2], axis=0)
    if past is None:
        diff_kv = _split_chunks(kvc, cw=2 * DIFF_HALF, n_out=4)
        dsa_kv = _split_chunks(kvd, cw=HEAD_DIM, n_out=4)
        kidx = _split_chunks(kiw, cw=IDX_DIM, n_out=1)
        o_c = _diff_attention(_pad_rows(qc, lq), diff_kv, lamv, subln_g, tq=min(tq, lq), q_pos0=0)[:, :l]
        o_d = _dsa_attention(_pad_rows(qd, lq), _pad_rows(qi, lq), _pad_rows(kiw, lq), kidx, dsa_kv,
                             tq=min(tq, lq), q_pos0=0, lk=l)[:, :l]
    else:
        diff_pool, dsa_pool, kidx_pool, table = past
        assert lq <= tq
        diff_rows = diff_pool.reshape(diff_pool.shape[0], PAGE_SIZE * 4, 2 * DIFF_HALF)
        o_c = _diff_decode(_pad_rows(qc, lq), diff_rows, table, _pad_rows(kvc, lq), lamv, subln_g, n_new=l)[:, :l]
        o_d = _dsa_decode(_pad_rows(qd, lq), _pad_rows(qi, lq), _pad_rows(kiw, lq), _cols_pool(kidx_pool),
                          _cols_pool(dsa_pool), table, _pad_rows(kiw[:, :, :IDX_DIM], lq), _pad_rows(kvd, lq),
                          n_new=l)[:, :l]
    fl = (lambda t: t.reshape(1, b * l, -1)) if flat else (lambda t: t)
    y = _out_project(fl(o_c), fl(z_c), fl(o_d), fl(z_d), xin, gate, w_out.astype(BF16), final_g,
                     tl=tl, final_norm=True).reshape(b, l, d)
    states = (kvc.reshape(b, l, 2, DIFF_KV_HEADS, 2 * DIFF_HALF),
              kvd.reshape(b, l, 2, DSA_KV_HEADS, HEAD_DIM), kiw[:, :, :IDX_DIM])
    return y, states


def kernel(x_prompt, x_sample, cache_l0_nsa_kv, state_l0_win_kv, state_l0_conv, state_l0_lru_h,
           cache_l1_diff_kv, cache_l1_dsa_kv, cache_l1_dsa_kidx, page_table, c_prompt, c_sample,
           l0_norm_g, l0_ada_w, l0_ada_b, l0_w_in, l0_cmp_wk, l0_cmp_wv, l0_conv_w, l0_conv_b,
           l0_lru_wr, l0_lru_br, l0_lru_wi, l0_lru_bi, l0_lru_lambda, l0_w_out,
           l1_norm_g, l1_ada_w, l1_ada_b, l1_w_in, l1_lam_q1, l1_lam_k1, l1_lam_q2, l1_lam_k2,
           l1_subln_g, l1_w_out, final_norm_g):
    bp, lp, d = x_prompt.shape
    bs, ls, _ = x_sample.shape
    c_all = jnp.concatenate([c_prompt, c_sample], axis=0)

    def mods(ada_w, ada_b):
        m = _modulation(c_all, ada_w, ada_b)
        mp = tuple(t[:, None] for t in jnp.split(m[:bp], 3, axis=-1))
        ms = tuple(jnp.repeat(t, ls, axis=0)[None] for t in jnp.split(m[bp:], 3, axis=-1))
        return mp, ms

    tl_p = min(512, lp)
    tl_s = bs * ls
    tq = min(Q_TILE, lp)
    w0 = (l0_norm_g, l0_w_in, l0_cmp_wk, l0_cmp_wv, l0_conv_w, l0_conv_b, l0_lru_wr, l0_lru_br,
          l0_lru_wi, l0_lru_bi, l0_lru_lambda, l0_w_out)
    mp0, ms0 = mods(l0_ada_w, l0_ada_b)
    xp, (nsa_kv_p, win_p, conv_p, h_p) = _layer0(x_prompt, mp0, None, w0, tl=tl_p, tq=tq)
    xs, (nsa_kv_s, win_s, conv_s, h_s) = _layer0(
        x_sample, ms0, (cache_l0_nsa_kv, page_table, state_l0_win_kv, state_l0_conv, state_l0_lru_h), w0,
        tl=tl_s, tq=tq)
    w1 = (l1_norm_g, l1_w_in, l1_lam_q1, l1_lam_k1, l1_lam_q2, l1_lam_k2, l1_subln_g, l1_w_out)
    mp1, ms1 = mods(l1_ada_w, l1_ada_b)
    y_p, (diff_kv_p, dsa_kv_p, kidx_p) = _layer1(xp, mp1, None, w1, final_norm_g, tl=tl_p, tq=tq)
    y_s, (diff_kv_s, dsa_kv_s, kidx_s) = _layer1(
        xs, ms1, (cache_l1_diff_kv, cache_l1_dsa_kv, cache_l1_dsa_kidx, page_table), w1, final_norm_g,
        tl=tl_s, tq=tq)
    return (y_p, y_s, nsa_kv_p, nsa_kv_s, win_p, win_s, conv_p, conv_s, h_p, h_s,
            diff_kv_p, diff_kv_s, dsa_kv_p, dsa_kv_s, kidx_p, kidx_s)
```

```python
import functools
import math

import jax
import jax.numpy as jnp
import numpy as np
from jax import lax
from jax.experimental import pallas as pl
from jax.experimental.pallas import tpu as pltpu

F32 = jnp.float32
BF16 = jnp.bfloat16
I32 = jnp.int32

PAGE_SIZE = 128
HEAD_DIM = 64
NSA_HEADS = 8
NSA_KV_HEADS = 2
NSA_CMP_BLOCK = 32
NSA_SEL_BLOCK = 64
NSA_TOPN = 16
NSA_WINDOW = 512
FORCE_SCORE = 1e4
LRU_BLOCKS = 8
LRU_C = 8.0
CONV_WIDTH = 4
DIFF_HALF = 64
DIFF_HEADS = 4
DIFF_KV_HEADS = 2
DIFF_LAMBDA_INIT = 0.8 - 0.6 * math.exp(-0.3 * 1)
DSA_HEADS = 8
DSA_KV_HEADS = 2
IDX_HEADS = 4
IDX_DIM = 64
DSA_TOPK_MAX = 256
NORM_EPS = 1e-6
NEG = -1e30
REMOVED = -3e38
INT_MIN = -2 ** 31
LOG2E = math.log2(math.e)

LANE = 128
VMEM_LIMIT = 56 * 1024 * 1024
KV_TILE = 2048
DSA_KV_TILE = 1024
TRI_TILE = 256
Q_TILE = 256
PROBE_BINADES = 3
COUNT_ROWS = 64
MAX_SWEEP_BRANCHES = 8
PAGES_PER_STEP = 32
HALF_PAGES_PER_STEP = 64
RELAYOUT_ROWS = 4096


def _kv_tile(lk, tile=KV_TILE):
    t = min(tile, lk)
    assert lk % t == 0 and t % TRI_TILE == 0
    return t


def _win_span(tq):
    return _round_up(NSA_WINDOW + tq, LANE)


def _cparams(sem):
    return pltpu.CompilerParams(dimension_semantics=sem, vmem_limit_bytes=VMEM_LIMIT)


def _dot(a, b):
    return jnp.dot(a, b, preferred_element_type=F32)


def _dot_nt(a, b):
    return lax.dot_general(a, b, (((1,), (1,)), ((), ())), preferred_element_type=F32)


def _round_up(x, m):
    return (x + m - 1) // m * m


def _mod_kernel(c_ref, w_ref, b_ref, o_ref):
    o_ref[...] = jnp.dot(c_ref[...], w_ref[...], preferred_element_type=F32,
                         precision=lax.Precision.HIGHEST) + b_ref[...]


def _modulation(c, w, b):
    bc, d = c.shape
    n = w.shape[1]
    tn = 512
    return pl.pallas_call(
        _mod_kernel,
        grid=(n // tn,),
        in_specs=[pl.BlockSpec((bc, d), lambda j: (0, 0)),
                  pl.BlockSpec((d, tn), lambda j: (0, j)),
                  pl.BlockSpec((1, tn), lambda j: (0, j))],
        out_specs=pl.BlockSpec((bc, tn), lambda j: (0, j)),
        out_shape=jax.ShapeDtypeStruct((bc, n), F32),
        compiler_params=_cparams(("arbitrary",)),
        name="modulation",
    )(c, w, b.reshape(1, n))


def _proj_kernel(x_ref, g_ref, sh_ref, sc_ref, w_ref, *o_refs, segs, sigmoid_seg):
    x = x_ref[0]
    y = x * lax.rsqrt(jnp.mean(x * x, axis=-1, keepdims=True) + NORM_EPS)
    h = (y * g_ref[...]) * (1.0 + sc_ref[0]) + sh_ref[0]
    hb = h.astype(BF16)
    for i, ((a, b), o_ref) in enumerate(zip(segs, o_refs)):
        r = _dot(hb, w_ref[:, a:b])
        if i == sigmoid_seg:
            r = jax.nn.sigmoid(r)
        o_ref[0] = r


def _project(x, g, shift, scale, w, segs, sigmoid_seg, tl):
    b, l, d = x.shape
    ts = shift.shape[1]
    tm = 1 if ts == 1 else tl
    mod_map = (lambda bi, li: (bi, 0, 0)) if ts == 1 else (lambda bi, li: (bi, li, 0))
    p = w.shape[1]
    kern = functools.partial(_proj_kernel, segs=tuple(segs), sigmoid_seg=sigmoid_seg)
    return pl.pallas_call(
        kern,
        grid=(b, l // tl),
        in_specs=[pl.BlockSpec((1, tl, d), lambda bi, li: (bi, li, 0)),
                  pl.BlockSpec((1, d), lambda bi, li: (0, 0)),
                  pl.BlockSpec((1, tm, d), mod_map),
                  pl.BlockSpec((1, tm, d), mod_map),
                  pl.BlockSpec((d, p), lambda bi, li: (0, 0))],
        out_specs=[pl.BlockSpec((1, tl, e - a), lambda bi, li: (bi, li, 0)) for a, e in segs],
        out_shape=[jax.ShapeDtypeStruct((b, l, e - a), F32) for a, e in segs],
        compiler_params=_cparams(("parallel", "arbitrary")),
        name="norm_mod_project",
    )(x, g.reshape(1, d), shift, scale, w)


def _chunk_kernel(x_ref, *refs, cw, n_cmp, n_out):
    w2_ref = refs[0] if n_cmp else None
    out_ref = refs[1 if n_cmp else 0]
    cmp_ref = refs[2] if n_cmp else None
    rows = x_ref.shape[1]
    ones_col = jnp.where(lax.broadcasted_iota(I32, (rows, LANE - cw), 1) == 0, 1.0, 0.0) if cw < LANE else None
    for c in range(n_cmp + n_out):
        x = x_ref[0, :, c * cw:(c + 1) * cw]
        if c < n_cmp:
            prod = x.reshape(rows // NSA_SEL_BLOCK, NSA_SEL_BLOCK, cw) * w2_ref[c][None]
            cmp_ref[0, c, 0] = jnp.sum(prod[:, :NSA_CMP_BLOCK], axis=1)
            cmp_ref[0, c, 1] = jnp.sum(prod[:, NSA_CMP_BLOCK:], axis=1)
        else:
            if ones_col is not None:
                x = jnp.concatenate([x, ones_col], axis=-1)
            out_ref[0, c - n_cmp] = x.astype(BF16)


def _split_chunks(x, *, cw, n_out, n_cmp=0, cmp_w=None):
    b, l, w = x.shape
    rows = min(RELAYOUT_ROWS, l)
    assert l % rows == 0 and rows % NSA_SEL_BLOCK == 0
    in_specs = [pl.BlockSpec((1, rows, w), lambda bi, j: (bi, j, 0))]
    args = [x]
    if n_cmp:
        in_specs.append(pl.BlockSpec(cmp_w.shape, lambda bi, j: (0, 0, 0)))
        args.append(cmp_w)
    out_specs = [pl.BlockSpec((1, n_out, rows, LANE), lambda bi, j: (bi, 0, j, 0))]
    out_shape = [jax.ShapeDtypeStruct((b, n_out, l, LANE), BF16)]
    if n_cmp:
        nb = rows // NSA_SEL_BLOCK
        out_specs.append(pl.BlockSpec((1, n_cmp, 2, nb, HEAD_DIM), lambda bi, j: (bi, 0, 0, j, 0)))
        out_shape.append(jax.ShapeDtypeStruct((b, n_cmp, 2, l // NSA_SEL_BLOCK, HEAD_DIM), F32))
    outs = pl.pallas_call(
        functools.partial(_chunk_kernel, cw=cw, n_cmp=n_cmp, n_out=n_out),
        grid=(b, l // rows),
        in_specs=in_specs, out_specs=out_specs, out_shape=out_shape,
        compiler_params=_cparams(("parallel", "arbitrary")),
        name="split_chunks",
    )(*args)
    return outs if n_cmp else outs[0]


def _flash_tile(s_all, bias, v_tile, carry, n_heads, tq, l_in_acc, v_t=False):
    m, l, acc = carry
    s = s_all
    if bias is not None:
        s = jnp.concatenate([s_all[h * tq:(h + 1) * tq] + bias for h in range(n_heads)], axis=0)
    m_new = jnp.maximum(m, jnp.max(s, axis=-1, keepdims=True))
    p = jnp.exp2(s - m_new)
    alpha = jnp.exp2(m - m_new)
    acc = alpha * acc + (_dot_nt if v_t else _dot)(p.astype(BF16), v_tile)
    if not l_in_acc:
        l = alpha * l + jnp.sum(p, axis=-1, keepdims=True)
    return m_new, l, acc


def _causal_sweep(step, carry, q0, tq, tk):
    assert tk % tq == 0 and tk // tq <= MAX_SWEEP_BRANCHES
    n_full = q0 // tk
    carry = lax.fori_loop(0, n_full, lambda j, c: step(pl.multiple_of(j * tk, tk), tk, False, c), carry)
    k0 = pl.multiple_of(n_full * tk, tk)
    branches = [functools.partial(step, k0, w, True) for w in range(tq, tk + tq, tq)]
    return lax.switch((q0 - k0) // tq, branches, carry)


def _flash_init(rows, dv):
    return (jnp.full((rows, 1), NEG, F32), jnp.zeros((rows, 1), F32), jnp.zeros((rows, dv), F32))


def _flash_out(acc):
    return acc[:, :HEAD_DIM] * (1.0 / jnp.maximum(acc[:, HEAD_DIM:HEAD_DIM + 1], 1e-30))


def _stack_heads(q, n, width):
    return jnp.concatenate([q[:, h * width:(h + 1) * width] for h in range(n)], axis=0)


def _pad_lanes(x):
    return jnp.concatenate([x, jnp.zeros((x.shape[0], LANE - x.shape[1]), x.dtype)], axis=-1)


def _nsa_prologue(qg, kw, vw, cmp4, *, q0, tq, nsp, n_top, win_pos0, start):
    hpg = NSA_HEADS // NSA_KV_HEADS
    rows = hpg * tq
    tqp = max(tq, LANE)
    qpos = q0 + lax.broadcasted_iota(I32, (tq, 1), 0)
    qpos_r = jnp.concatenate([qpos] * hpg, axis=0)
    blk = lax.broadcasted_iota(I32, (1, nsp), 1)
    blk_r = lax.broadcasted_iota(I32, (nsp, 1), 0)
    blk_rf = blk_r.astype(F32)
    cur_l = (q0 + lax.broadcasted_iota(I32, (1, tqp), 1)) // NSA_SEL_BLOCK
    vis_e = (blk * NSA_SEL_BLOCK + (NSA_CMP_BLOCK - 1)) <= qpos_r
    vis_o = (blk * NSA_SEL_BLOCK + (NSA_SEL_BLOCK - 1)) <= qpos_r
    kpos_w = win_pos0 + start + lax.broadcasted_iota(I32, (1, kw.shape[0]), 1)
    dlt = qpos - kpos_w
    bias_w = jnp.where(dlt >= 0, jnp.where(dlt < NSA_WINDOW, 0.0, NEG), NEG)

    qs64 = _stack_heads(qg, hpg, HEAD_DIM)
    qs = _pad_lanes(qs64).astype(BF16)
    qs64 = qs64.astype(BF16)

    _, _, acc_w = _flash_tile(_dot_nt(qs, kw), bias_w, vw, _flash_init(rows, LANE), hpg, tq, True)
    o_w = _flash_out(acc_w)

    kce, kco, vce, vco = (x.astype(BF16) for x in cmp4)
    s_e = jnp.where(vis_e, _dot_nt(qs64, kce), NEG)
    s_o = jnp.where(vis_o, _dot_nt(qs64, kco), NEG)
    m = jnp.maximum(jnp.max(s_e, axis=-1, keepdims=True), jnp.max(s_o, axis=-1, keepdims=True))
    p_e = jnp.where(vis_e, jnp.exp2(s_e - m), 0.0)
    p_o = jnp.where(vis_o, jnp.exp2(s_o - m), 0.0)
    den = jnp.sum(p_e, axis=-1, keepdims=True) + jnp.sum(p_o, axis=-1, keepdims=True)
    inv = 1.0 / jnp.maximum(den, 1e-30)
    p_e = p_e * inv
    p_o = p_o * inv
    o_c = _dot(p_e.astype(BF16), vce) + _dot(p_o.astype(BF16), vco)

    pe_h = sum(p_e[h * tq:(h + 1) * tq] for h in range(hpg))
    po_h = sum(p_o[h * tq:(h + 1) * tq] for h in range(hpg))
    imp = pe_h + po_h
    if tqp > tq:
        imp = jnp.concatenate([imp, jnp.zeros((tqp - tq, nsp), F32)], axis=0)
    imp = imp.T
    imp = jnp.where((blk_r == cur_l) | (blk_r == 0), FORCE_SCORE, imp)
    imp = jnp.where(blk_r <= cur_l, imp, NEG)
    sel = jnp.zeros((nsp, tqp), F32)
    for _ in range(n_top):
        mx = jnp.max(imp, axis=0, keepdims=True)
        first = jnp.min(jnp.where(imp == mx, blk_rf, float(nsp)), axis=0, keepdims=True)
        pick = blk_rf == first
        sel = jnp.where(pick & (mx > 0.5 * NEG), 1.0, sel)
        imp = jnp.where(pick, REMOVED, imp)
    return qs, qs64, o_c, o_w, sel.T[:tq].astype(BF16)


def _nsa_combine(gates, branches, tq):
    hpg = NSA_HEADS // NSA_KV_HEADS
    outs = []
    for g, (o_c, o_s, o_w) in enumerate(branches):
        for h in range(hpg):
            r = slice(h * tq, (h + 1) * tq)
            c = g * LANE + 3 * h
            outs.append(gates[:, c:c + 1] * o_c[r] + gates[:, c + 1:c + 2] * o_s[r]
                        + gates[:, c + 2:c + 3] * o_w[r])
    return jnp.concatenate(outs, axis=-1)


def _nsa_kernel(q_ref, g_ref, cmp_ref, ksel_ref, kwin_ref, o_ref, *,
                tq, tk, q_pos0, win_pos0, nsp, n_top):
    G = NSA_KV_HEADS
    hpg = NSA_HEADS // G
    gw = hpg * HEAD_DIM
    qi = pl.program_id(1)
    q0 = q_pos0 + qi * tq
    qpos = q0 + lax.broadcasted_iota(I32, (tq, 1), 0)
    rows = hpg * tq
    q_all = q_ref[0] * (HEAD_DIM ** -0.5 * LOG2E)
    start = pl.multiple_of(jnp.maximum(q0 - NSA_WINDOW - win_pos0, 0), 8)
    span = _win_span(tq)
    pro = [_nsa_prologue(q_all[:, g * gw:(g + 1) * gw],
                         kwin_ref[0, g, pl.ds(start, span), :], kwin_ref[0, G + g, pl.ds(start, span), :],
                         tuple(cmp_ref[0, 2 * g + kv, eo] for kv in range(2) for eo in range(2)),
                         q0=q0, tq=tq, nsp=nsp, n_top=n_top, win_pos0=win_pos0, start=start)
           for g in range(G)]

    blk_col = lax.broadcasted_iota(I32, (nsp, 1), 0)

    def sel_step(k0, w, diagonal, carry):
        kpos = k0 + lax.broadcasted_iota(I32, (1, w), 1)
        expand = jnp.where(blk_col == kpos // NSA_SEL_BLOCK, 1.0, 0.0).astype(BF16)
        out = []
        for g in range(G):
            bias = jnp.where(_dot(pro[g][4], expand) > 0.5, 0.0, NEG)
            if diagonal:
                bias = jnp.where(kpos <= qpos, bias, NEG)
            s = _dot_nt(pro[g][0], ksel_ref[0, g, pl.ds(k0, w), :])
            out.append(_flash_tile(s, bias, ksel_ref[0, G + g, pl.ds(k0, w), :], carry[g], hpg, tq, True))
        return tuple(out)

    res = _causal_sweep(sel_step, tuple(_flash_init(rows, LANE) for _ in range(G)), q0, tq, tk)
    o_ref[0] = _nsa_combine(g_ref[0], [(pro[g][2], _flash_out(res[g][2]), pro[g][3]) for g in range(G)], tq)


def _nsa_attention(q, gates, cmp, ksel, kwin, *, tq, q_pos0, lk, win_pos0):
    bk, lq, _ = q.shape
    G = NSA_KV_HEADS
    nsp = cmp.shape[3]
    lk_pad = ksel.shape[2]
    lw_pad = kwin.shape[2]
    tk = _kv_tile(lk_pad)
    ns = -(-lk // NSA_SEL_BLOCK)
    assert lw_pad >= _win_span(tq) and q_pos0 + lq <= lk_pad
    kern = functools.partial(_nsa_kernel, tq=tq, tk=tk, q_pos0=q_pos0, win_pos0=win_pos0,
                             nsp=nsp, n_top=min(NSA_TOPN, ns))
    qw = NSA_HEADS * HEAD_DIM
    return pl.pallas_call(
        kern,
        grid=(bk, lq // tq),
        in_specs=[pl.BlockSpec((1, tq, qw), lambda b, i: (b, i, 0)),
                  pl.BlockSpec((1, tq, G * LANE), lambda b, i: (b, i, 0)),
                  pl.BlockSpec((1, 2 * G, 2, nsp, HEAD_DIM), lambda b, i: (b, 0, 0, 0, 0)),
                  pl.BlockSpec((1, 2 * G, lk_pad, LANE), lambda b, i: (b, 0, 0, 0)),
                  pl.BlockSpec((1, 2 * G, lw_pad, LANE), lambda b, i: (b, 0, 0, 0))],
        out_specs=pl.BlockSpec((1, tq, qw), lambda b, i: (b, i, 0)),
        out_shape=jax.ShapeDtypeStruct((bk, lq, qw), F32),
        compiler_params=_cparams(("parallel", "arbitrary")),
        name="nsa_attention",
    )(q, gates, cmp, ksel, kwin)


def _nsa_decode_kernel(pt_ref, q_ref, g_ref, new_ref, w2_ref, kwin_ref, *refs,
                       pp, n_steps, tq, n_new, past_len, win_pos0, nsp, n_top):
    page_refs = refs[:pp]
    o_ref, cmp_ref, oc_ref, ow_ref, sel_ref, m_ref, l_ref, acc_ref = refs[pp:]
    G = NSA_KV_HEADS
    hpg = NSA_HEADS // G
    gw = hpg * HEAD_DIM
    rows = hpg * tq
    j = pl.program_id(1)
    keys = pp * PAGE_SIZE
    nb = keys // NSA_SEL_BLOCK
    q_all = q_ref[0] * (HEAD_DIM ** -0.5 * LOG2E)

    def queries(g):
        qs64 = _stack_heads(q_all[:, g * gw:(g + 1) * gw], hpg, HEAD_DIM)
        return _pad_lanes(qs64).astype(BF16), qs64.astype(BF16)

    @pl.when(j == 0)
    def _():
        cmp_ref[...] = jnp.zeros(cmp_ref.shape, F32)

    @pl.when(j < n_steps)
    def _():
        r0 = pl.multiple_of(j * nb, nb)
        for c in range(2 * G):
            x = jnp.concatenate([r[0, c].T for r in page_refs], axis=0)
            prod = x.reshape(nb, NSA_SEL_BLOCK, HEAD_DIM) * w2_ref[c][None]
            cmp_ref[c, 0, pl.ds(r0, nb), :] = jnp.sum(prod[:, :NSA_CMP_BLOCK], axis=1)
            cmp_ref[c, 1, pl.ds(r0, nb), :] = jnp.sum(prod[:, NSA_CMP_BLOCK:], axis=1)

    @pl.when(j == n_steps - 1)
    def _():
        for g in range(G):
            _, _, o_c, o_w, sel = _nsa_prologue(
                q_all[:, g * gw:(g + 1) * gw], kwin_ref[0, g], kwin_ref[0, G + g],
                tuple(cmp_ref[g + G * kv, eo] for kv in range(2) for eo in range(2)),
                q0=past_len, tq=tq, nsp=nsp, n_top=n_top, win_pos0=win_pos0, start=0)
            oc_ref[g] = o_c
            ow_ref[g] = o_w
            sel_ref[g] = sel
        m_ref[...] = jnp.full(m_ref.shape, NEG, F32)
        l_ref[...] = jnp.zeros(l_ref.shape, F32)
        acc_ref[...] = jnp.zeros(acc_ref.shape, F32)

    def update(g, s, bias, v, v_t):
        carry = (m_ref[g], l_ref[g], acc_ref[g])
        m, l, acc = _flash_tile(s, bias, v, carry, hpg, tq, False, v_t=v_t)
        m_ref[g] = m
        l_ref[g] = l
        acc_ref[g] = acc

    @pl.when(j >= n_steps)
    def _():
        kpos = (j - n_steps) * keys + lax.broadcasted_iota(I32, (1, keys), 1)
        blk_col = lax.broadcasted_iota(I32, (nsp, 1), 0)
        expand = jnp.where(blk_col == kpos // NSA_SEL_BLOCK, 1.0, 0.0).astype(BF16)
        for g in range(G):
            _, qs64 = queries(g)
            k_t = jnp.concatenate([r[0, g] for r in page_refs], axis=-1).astype(BF16)
            v_t = jnp.concatenate([r[0, G + g] for r in page_refs], axis=-1).astype(BF16)
            bias = jnp.where(_dot(sel_ref[g], expand) > 0.5, 0.0, NEG)
            update(g, _dot(qs64, k_t), bias, v_t, True)

    @pl.when(j == 2 * n_steps - 1)
    def _():
        new = new_ref[0]
        pad = jnp.zeros((LANE - tq, HEAD_DIM), F32)
        row = lax.broadcasted_iota(I32, (tq, 1), 0)
        col = lax.broadcasted_iota(I32, (1, LANE), 1)
        own = past_len // NSA_SEL_BLOCK
        branches = []
        for g in range(G):
            _, qs64 = queries(g)
            k = jnp.concatenate([new[:, (2 * G + g) * HEAD_DIM:(2 * G + g + 1) * HEAD_DIM], pad], axis=0)
            v = jnp.concatenate([new[:, (3 * G + g) * HEAD_DIM:(3 * G + g + 1) * HEAD_DIM], pad], axis=0)
            picked = sel_ref[g][:, own:own + 1].astype(F32) > 0.5
            bias = jnp.where((col <= row) & (col < n_new) & picked, 0.0, NEG)
            update(g, _dot_nt(qs64, k.astype(BF16)), bias, v.astype(BF16), False)
            o_s = acc_ref[g] * (1.0 / jnp.maximum(l_ref[g], 1e-30))
            branches.append((oc_ref[g], o_s, ow_ref[g]))
        o_ref[0] = _nsa_combine(g_ref[0], branches, tq)


def _nsa_decode(q, gates, pool, page_table, new, w2, kwin, *, n_new, win_pos0):
    bk, tq, qw = q.shape
    n_pages = page_table.shape[1]
    pp = min(HALF_PAGES_PER_STEP, n_pages)
    past_len = n_pages * PAGE_SIZE
    assert n_pages % pp == 0 and past_len % NSA_SEL_BLOCK == 0 and n_new <= NSA_CMP_BLOCK
    assert kwin.shape[2] == _win_span(tq)
    n_steps = n_pages // pp
    G = NSA_KV_HEADS
    rows = (NSA_HEADS // G) * tq
    lk = past_len + n_new
    ns = -(-lk // NSA_SEL_BLOCK)
    nsp = _round_up(ns, LANE)

    def page_map(i):
        return lambda b, j, pt: (pt[b, (j % n_steps) * pp + i], j // n_steps, 0, 0)

    const3 = lambda b, j, pt: (b, 0, 0)
    in_specs = [pl.BlockSpec((1, tq, qw), const3),
                pl.BlockSpec((1, tq, G * LANE), const3),
                pl.BlockSpec((1, tq, new.shape[-1]), const3),
                pl.BlockSpec(w2.shape, lambda b, j, pt: (0, 0, 0)),
                pl.BlockSpec((1,) + kwin.shape[1:], lambda b, j, pt: (b, 0, 0, 0))]
    in_specs += [pl.BlockSpec((1, 2 * G, HEAD_DIM, PAGE_SIZE), page_map(i)) for i in range(pp)]
    kern = functools.partial(_nsa_decode_kernel, pp=pp, n_steps=n_steps, tq=tq, n_new=n_new, past_len=past_len,
                             win_pos0=win_pos0, nsp=nsp, n_top=min(NSA_TOPN, ns))
    return pl.pallas_call(
        kern,
        grid_spec=pltpu.PrefetchScalarGridSpec(
            num_scalar_prefetch=1, grid=(bk, 2 * n_steps), in_specs=in_specs,
            out_specs=pl.BlockSpec((1, tq, qw), const3),
            scratch_shapes=[pltpu.VMEM((2 * G, 2, nsp, HEAD_DIM), F32),
                            pltpu.VMEM((G, rows, HEAD_DIM), F32), pltpu.VMEM((G, rows, HEAD_DIM), F32),
                            pltpu.VMEM((G, tq, nsp), BF16),
                            pltpu.VMEM((G, rows, 1), F32), pltpu.VMEM((G, rows, 1), F32),
                            pltpu.VMEM((G, rows, HEAD_DIM), F32)]),
        out_shape=jax.ShapeDtypeStruct((bk, tq, qw), F32),
        compiler_params=_cparams(("parallel", "arbitrary")),
        name="nsa_decode",
    )(page_table, q, gates, new, w2, kwin, *([pool] * pp))


def _shift_rows(x, d, fill):
    rolled = pltpu.roll(x, d, axis=0)
    row = lax.broadcasted_iota(I32, x.shape, 0)
    return jnp.where(row >= d, rolled, fill)


def _lru_kernel(x_ref, hist_ref, h0_ref, cw_ref, cb_ref, wr_ref, br_ref, wi_ref, bi_ref, lam_ref,
                o_ref, hl_ref, tail_ref, h_ref, *, tl, last_row):
    li = pl.program_id(1)

    @pl.when(li == 0)
    def _():
        tail_ref[...] = jnp.concatenate(
            [jnp.zeros((8 - (CONV_WIDTH - 1), x_ref.shape[-1]), F32), hist_ref[0]], axis=0)
        h_ref[...] = h0_ref[0]

    x = x_ref[0]
    xp = jnp.concatenate([tail_ref[...], x], axis=0)
    cw = cw_ref[...]
    conv = sum(xp[8 - (CONV_WIDTH - 1) + j:8 - (CONV_WIDTH - 1) + j + tl] * cw[j:j + 1]
               for j in range(CONV_WIDTH))
    conv = cb_ref[...] + conv
    tail_ref[...] = x[tl - 8:tl]

    cb16 = conv.astype(BF16)
    r = jax.nn.sigmoid(_dot(cb16, wr_ref[...]) + br_ref[...])
    ig = jax.nn.sigmoid(_dot(cb16, wi_ref[...]) + bi_ref[...])
    log_a = -LRU_C * r * jax.nn.softplus(-lam_ref[...])
    a = jnp.exp(log_a)
    th = jnp.tanh(log_a)
    b = jnp.sqrt(-2.0 * th / (1.0 - th)) * (ig * conv)

    d = 1
    while d < tl:
        a_prev = _shift_rows(a, d, 1.0)
        b_prev = _shift_rows(b, d, 0.0)
        b = a * b_prev + b
        a = a * a_prev
        d *= 2
    h = a * h_ref[...] + b
    o_ref[0] = h
    h_ref[...] = h[tl - 1:tl]

    @pl.when(li == pl.num_programs(1) - 1)
    def _():
        hl_ref[0] = h[last_row:last_row + 1]


def _block_diag(w):
    nb, bw, _ = w.shape
    eye = jnp.eye(nb, dtype=w.dtype)
    return (eye[:, None, :, None] * w[:, :, None, :]).reshape(nb * bw, nb * bw)


def _conv_rglru(x_b, hist, h0, conv_w, conv_b, w_r, b_r, w_i, b_i, lam, *, tl, n_valid):
    b, l, w = x_b.shape
    assert tl >= 8 and l % tl == 0 and n_valid > l - tl
    kern = functools.partial(_lru_kernel, tl=tl, last_row=(n_valid - 1) % tl)
    vec = lambda: pl.BlockSpec((1, w), lambda bi, li: (0, 0))
    h, h_last = pl.pallas_call(
        kern,
        grid=(b, l // tl),
        in_specs=[pl.BlockSpec((1, tl, w), lambda bi, li: (bi, li, 0)),
                  pl.BlockSpec((1, CONV_WIDTH - 1, w), lambda bi, li: (bi, 0, 0)),
                  pl.BlockSpec((1, 1, w), lambda bi, li: (bi, 0, 0)),
                  pl.BlockSpec((CONV_WIDTH, w), lambda bi, li: (0, 0)),
                  vec(),
                  pl.BlockSpec((w, w), lambda bi, li: (0, 0)), vec(),
                  pl.BlockSpec((w, w), lambda bi, li: (0, 0)), vec(), vec()],
        out_specs=[pl.BlockSpec((1, tl, w), lambda bi, li: (bi, li, 0)),
                   pl.BlockSpec((1, 1, w), lambda bi, li: (bi, 0, 0))],
        out_shape=[jax.ShapeDtypeStruct((b, l, w), F32), jax.ShapeDtypeStruct((b, 1, w), F32)],
        scratch_shapes=[pltpu.VMEM((8, w), F32), pltpu.VMEM((1, w), F32)],
        compiler_params=_cparams(("parallel", "arbitrary")),
        name="conv_rglru",
    )(x_b, hist, h0.reshape(b, 1, w), conv_w, conv_b.reshape(1, w),
      _block_diag(w_r).astype(BF16), b_r.reshape(1, w), _block_diag(w_i).astype(BF16), b_i.reshape(1, w),
      lam.reshape(1, w))
    return h, h_last.reshape(b, w)


def _out_kernel(oa_ref, za_ref, ob_ref, zb_ref, x_ref, gate_ref, w_ref, fg_ref, o_ref, *, final_norm):
    half = oa_ref.shape[-1]
    ma = (oa_ref[0] * jax.nn.silu(za_ref[0])).astype(BF16)
    mb = (ob_ref[0] * jax.nn.silu(zb_ref[0])).astype(BF16)
    y = _dot(ma, w_ref[0:half, :]) + _dot(mb, w_ref[half:2 * half, :])
    out = x_ref[0] + gate_ref[0] * y
    if final_norm:
        out = out * lax.rsqrt(jnp.mean(out * out, axis=-1, keepdims=True) + NORM_EPS) * fg_ref[...]
    o_ref[0] = out


def _out_project(o_a, z_a, o_b, z_b, x, gate, w_out, final_g, *, tl, final_norm):
    b, l, d = x.shape
    half = o_a.shape[-1]
    ts = gate.shape[1]
    tm = 1 if ts == 1 else tl
    mod_map = (lambda bi, li: (bi, 0, 0)) if ts == 1 else (lambda bi, li: (bi, li, 0))
    act = lambda: pl.BlockSpec((1, tl, half), lambda bi, li: (bi, li, 0))
    return pl.pallas_call(
        functools.partial(_out_kernel, final_norm=final_norm),
        grid=(b, l // tl),
        in_specs=[act(), act(), act(), act(),
                  pl.BlockSpec((1, tl, d), lambda bi, li: (bi, li, 0)),
                  pl.BlockSpec((1, tm, d), mod_map),
                  pl.BlockSpec((2 * half, d), lambda bi, li: (0, 0)),
                  pl.BlockSpec((1, d), lambda bi, li: (0, 0))],
        out_specs=pl.BlockSpec((1, tl, d), lambda bi, li: (bi, li, 0)),
        out_shape=jax.ShapeDtypeStruct((b, l, d), F32),
        compiler_params=_cparams(("parallel", "arbitrary")),
        name="out_project",
    )(o_a, z_a, o_b, z_b, x, gate, w_out, final_g.reshape(1, d))


def _diff_queries(q, tq):
    hpg = DIFF_HEADS // DIFF_KV_HEADS
    q = q * (DIFF_HALF ** -0.5 * LOG2E)
    zero = jnp.zeros((tq, DIFF_HALF), F32)
    parts = []
    for mp in range(2):
        for h in range(hpg):
            qh = q[:, (2 * h + mp) * DIFF_HALF:(2 * h + mp + 1) * DIFF_HALF]
            parts.append(jnp.concatenate([qh, zero] if mp == 0 else [zero, qh], axis=-1))
    return jnp.concatenate(parts, axis=0).astype(BF16)


def _diff_finish(l, acc, lamv, subg, tq):
    hpg = DIFF_HEADS // DIFF_KV_HEADS
    o = acc * (1.0 / jnp.maximum(l, 1e-30))
    lam = (jnp.exp(jnp.sum(lamv[0:1] * lamv[1:2], axis=-1, keepdims=True))
           - jnp.exp(jnp.sum(lamv[2:3] * lamv[3:4], axis=-1, keepdims=True)) + DIFF_LAMBDA_INIT)
    half = hpg * tq
    od = o[0:half] - lam * o[half:2 * half]
    od = od * lax.rsqrt(jnp.mean(od * od, axis=-1, keepdims=True) + NORM_EPS)
    od = od * subg * (1.0 - DIFF_LAMBDA_INIT)
    return jnp.concatenate([od[h * tq:(h + 1) * tq] for h in range(hpg)], axis=-1)


def _diff_kernel(q_ref, k_ref, v_ref, lamv_ref, subg_ref, o_ref, *, tq, tk, q_pos0):
    hpg = DIFF_HEADS // DIFF_KV_HEADS
    qi = pl.program_id(2)
    q0 = q_pos0 + qi * tq
    n_maps = 2 * hpg
    rows = n_maps * tq
    qpos = q0 + lax.broadcasted_iota(I32, (tq, 1), 0)
    qs = _diff_queries(q_ref[0], tq)

    def step(k0, w, diagonal, carry):
        s = _dot_nt(qs, k_ref[0, 0, pl.ds(k0, w), :])
        bias = None
        if diagonal:
            kpos = k0 + lax.broadcasted_iota(I32, (1, w), 1)
            bias = jnp.where(kpos <= qpos, 0.0, NEG)
        return _flash_tile(s, bias, v_ref[0, 0, pl.ds(k0, w), :], carry, n_maps, tq, False)

    m, l, acc = _causal_sweep(step, _flash_init(rows, 2 * DIFF_HALF), q0, tq, tk)
    o_ref[0] = _diff_finish(l, acc, lamv_ref[...], subg_ref[...], tq)


def _diff_decode_kernel(pt_ref, q_ref, new_ref, lamv_ref, subg_ref, *refs, pp, tq, n_new):
    page_refs = refs[:pp]
    o_ref, m_ref, l_ref, acc_ref = refs[pp:]
    G = DIFF_KV_HEADS
    n_maps = 2 * (DIFF_HEADS // G)
    gw = n_maps * DIFF_HALF
    j = pl.program_id(1)

    @pl.when(j == 0)
    def _():
        m_ref[...] = jnp.full(m_ref.shape, NEG, F32)
        l_ref[...] = jnp.zeros(l_ref.shape, F32)
        acc_ref[...] = jnp.zeros(acc_ref.shape, F32)

    def update(g, qs, k, v, bias):
        carry = (m_ref[g], l_ref[g], acc_ref[g])
        m, l, acc = _flash_tile(_dot_nt(qs, k), bias, v, carry, n_maps, tq, False)
        m_ref[g] = m
        l_ref[g] = l
        acc_ref[g] = acc

    qs = [_diff_queries(q_ref[0][:, g * gw:(g + 1) * gw], tq) for g in range(G)]
    for g in range(G):
        k = jnp.concatenate([r[0, pl.ds(g, PAGE_SIZE, stride=2 * G), :] for r in page_refs], axis=0)
        v = jnp.concatenate([r[0, pl.ds(G + g, PAGE_SIZE, stride=2 * G), :] for r in page_refs], axis=0)
        update(g, qs[g], k.astype(BF16), v.astype(BF16), None)

    @pl.when(j == pl.num_programs(1) - 1)
    def _():
        new = new_ref[0]
        pad = jnp.zeros((LANE - tq, 2 * DIFF_HALF), F32)
        row = lax.broadcasted_iota(I32, (tq, 1), 0)
        col = lax.broadcasted_iota(I32, (1, LANE), 1)
        bias = jnp.where((col <= row) & (col < n_new), 0.0, NEG)
        outs = []
        for g in range(G):
            k = jnp.concatenate([new[:, g * 2 * DIFF_HALF:(g + 1) * 2 * DIFF_HALF], pad], axis=0)
            v = jnp.concatenate([new[:, (G + g) * 2 * DIFF_HALF:(G + g + 1) * 2 * DIFF_HALF], pad], axis=0)
            update(g, qs[g], k.astype(BF16), v.astype(BF16), bias)
            outs.append(_diff_finish(l_ref[g], acc_ref[g], lamv_ref[...], subg_ref[...], tq))
        o_ref[0] = jnp.concatenate(outs, axis=-1)


def _diff_decode(q, pool, page_table, new, lamv, subln_g, *, n_new):
    bk, tq, qw = q.shape
    n_pages = page_table.shape[1]
    pp = min(PAGES_PER_STEP, n_pages)
    assert n_pages % pp == 0
    G = DIFF_KV_HEADS
    rows = 2 * (DIFF_HEADS // G) * tq

    def page_map(i):
        return lambda b, j, pt: (pt[b, j * pp + i], 0, 0)

    in_specs = [pl.BlockSpec((1, tq, qw), lambda b, j, pt: (b, 0, 0)),
                pl.BlockSpec((1, tq, new.shape[-1]), lambda b, j, pt: (b, 0, 0)),
                pl.BlockSpec((4, DIFF_HALF), lambda b, j, pt: (0, 0)),
                pl.BlockSpec((1, 2 * DIFF_HALF), lambda b, j, pt: (0, 0))]
    in_specs += [pl.BlockSpec((1,) + pool.shape[1:], page_map(i)) for i in range(pp)]
    return pl.pallas_call(
        functools.partial(_diff_decode_kernel, pp=pp, tq=tq, n_new=n_new),
        grid_spec=pltpu.PrefetchScalarGridSpec(
            num_scalar_prefetch=1, grid=(bk, n_pages // pp), in_specs=in_specs,
            out_specs=pl.BlockSpec((1, tq, qw), lambda b, j, pt: (b, 0, 0)),
            scratch_shapes=[pltpu.VMEM((G, rows, 1), F32), pltpu.VMEM((G, rows, 1), F32),
                            pltpu.VMEM((G, rows, 2 * DIFF_HALF), F32)]),
        out_shape=jax.ShapeDtypeStruct((bk, tq, qw), F32),
        compiler_params=_cparams(("parallel", "arbitrary")),
        name="diff_decode",
    )(page_table, q, new, lamv, subln_g.reshape(1, 2 * DIFF_HALF), *([pool] * pp))


def _diff_attention(q, kv, lamv, subln_g, *, tq, q_pos0):
    bk, lq, _ = q.shape
    G = DIFF_KV_HEADS
    lk_pad = kv.shape[2]
    tk = _kv_tile(lk_pad)
    gw = (DIFF_HEADS // G) * 2 * DIFF_HALF
    assert q_pos0 + lq <= lk_pad
    return pl.pallas_call(
        functools.partial(_diff_kernel, tq=tq, tk=tk, q_pos0=q_pos0),
        grid=(bk, G, lq // tq),
        in_specs=[pl.BlockSpec((1, tq, gw), lambda b, g, i: (b, i, g)),
                  pl.BlockSpec((1, 1, lk_pad, 2 * DIFF_HALF), lambda b, g, i: (b, g, 0, 0)),
                  pl.BlockSpec((1, 1, lk_pad, 2 * DIFF_HALF), lambda b, g, i: (b, G + g, 0, 0)),
                  pl.BlockSpec((4, DIFF_HALF), lambda b, g, i: (0, 0)),
                  pl.BlockSpec((1, 2 * DIFF_HALF), lambda b, g, i: (0, 0))],
        out_specs=pl.BlockSpec((1, tq, gw), lambda b, g, i: (b, i, g)),
        out_shape=jax.ShapeDtypeStruct((bk, lq, DIFF_HEADS * 2 * DIFF_HALF), F32),
        compiler_params=_cparams(("parallel", "parallel", "arbitrary")),
        name="diff_attention",
    )(q, kv, kv, lamv, subln_g.reshape(1, 2 * DIFF_HALF))


def _dsa_keys(s_all, wi, causal, tq):
    score = jnp.zeros((tq, s_all.shape[1]), F32)
    for h in range(IDX_HEADS):
        score = score + wi[:, h:h + 1] * jnp.maximum(s_all[h * tq:(h + 1) * tq], 0.0)
    bits = pltpu.bitcast(score, I32)
    key = jnp.where(bits < 0, bits ^ 0x7FFFFFFF, bits)
    key = jnp.where(score == 0.0, 0, key)
    key = jnp.where(score > 0.5 * NEG, key, INT_MIN)
    return key if causal is None else jnp.where(causal, key, INT_MIN)


def _dsa_threshold(read, n_tiles, kmax, *, tq, tk, n_sel, transposed, unroll):
    lanes = tk // LANE

    def count(*bounds):
        def f(j, accs):
            keys = read(j)
            out = []
            for bound, acc in zip(bounds, accs):
                hit = jnp.where(keys >= bound, 1.0, 0.0)
                if transposed:
                    acc = acc + jnp.sum(hit.reshape(tk // COUNT_ROWS, COUNT_ROWS, tq), axis=0)
                else:
                    for c in range(lanes):
                        acc = acc + hit[:, c * LANE:(c + 1) * LANE]
                out.append(acc)
            return tuple(out)
        acc0 = jnp.zeros((COUNT_ROWS, tq) if transposed else (tq, LANE), F32)
        accs = lax.fori_loop(0, n_tiles, f, tuple(acc0 for _ in bounds), unroll=unroll)
        return [jnp.sum(acc, axis=0 if transposed else -1, keepdims=True) for acc in accs]

    k_f = float(n_sel)
    probe = jnp.maximum(kmax - (PROBE_BINADES << 23), 1)
    c_adm, c_nn, c_pos, c_probe = count(INT_MIN + 1, 0, 1, probe)
    few, pos, zero, high = c_adm < k_f, c_pos >= k_f, c_nn >= k_f, c_probe >= k_f
    lo0 = jnp.where(pos, jnp.where(high, probe, 1), jnp.where(zero, 0, INT_MIN))
    hi0 = jnp.where(pos, jnp.where(high, kmax, probe - 1), jnp.where(zero, 0, jnp.where(few, INT_MIN, -1)))

    def unfinished(lo_hi):
        lo, hi = lo_hi
        return jnp.max(jnp.where(lo < hi, 1.0, 0.0)) > 0.0

    def bisect(lo_hi):
        lo, hi = lo_hi
        mid = (lo >> 1) + (hi >> 1) + ((lo | hi) & 1)
        cnt, = count(mid)
        lo = jnp.where(cnt >= k_f, mid, lo)
        hi = jnp.where(cnt > k_f, hi, jnp.where(cnt == k_f, mid, mid - 1))
        return lo, hi

    thr, _ = lax.while_loop(unfinished, lambda s: bisect(bisect(s)), (lo0, hi0))
    thr = jnp.maximum(thr, INT_MIN + 1)
    n_gt, = count(thr + 1)
    return thr, k_f - n_gt


def _dsa_bias(key, thr, need, seen, tri):
    tied = key == thr
    tied_b = jnp.where(tied, 1.0, 0.0).astype(BF16)
    t = tri.shape[0]
    ranks = []
    for c in range(key.shape[1] // t):
        r = _dot(tied_b[:, c * t:(c + 1) * t], tri) + seen
        ranks.append(r)
        seen = r[:, t - 1:t]
    rank = jnp.concatenate(ranks, axis=-1)
    return jnp.where(key > thr, 0.0, jnp.where(tied, jnp.where(rank <= need, 0.0, NEG), NEG)), seen


def _dsa_kernel(q_ref, qi_ref, kw_ref, kidx_ref, kv_ref, tri_ref, o_ref, key_ref, keyt_ref, *,
                tq, tk, q_pos0, n_sel):
    G = DSA_KV_HEADS
    hpg = DSA_HEADS // G
    q0 = q_pos0 + pl.program_id(1) * tq
    qpos = q0 + lax.broadcasted_iota(I32, (tq, 1), 0)
    n_tiles = (q0 + tq - 1) // tk + 1
    lanes = tk // LANE

    qidx = _pad_lanes(_stack_heads(qi_ref[0] * (IDX_DIM ** -0.5), IDX_HEADS, IDX_DIM)).astype(BF16)
    wi = kw_ref[0][:, IDX_DIM:IDX_DIM + IDX_HEADS] * (IDX_HEADS ** -0.5)

    def score_step(j, kmax, diagonal):
        k0 = pl.multiple_of(j * tk, tk)
        s_all = _dot_nt(qidx, kidx_ref[0, 0, pl.ds(k0, tk), :])
        causal = (k0 + lax.broadcasted_iota(I32, (1, tk), 1)) <= qpos if diagonal else None
        key = _dsa_keys(s_all, wi, causal, tq)
        key_ref[j] = key
        key_t = key.T
        keyt_ref[j] = key_t
        return jnp.maximum(kmax, jnp.max(key_t, axis=0, keepdims=True))

    kmax = lax.fori_loop(0, q0 // tk, functools.partial(score_step, diagonal=False), jnp.full((1, tq), INT_MIN, I32))
    kmax = lax.fori_loop(q0 // tk, n_tiles, functools.partial(score_step, diagonal=True), kmax)
    thr, need = _dsa_threshold(lambda j: keyt_ref[j], n_tiles, kmax,
                               tq=tq, tk=tk, n_sel=n_sel, transposed=True, unroll=False)

    def along_rows(v):
        rep = jnp.broadcast_to(v, (LANE, tq)).T
        return jnp.concatenate([rep] * lanes, axis=-1)

    thr = along_rows(thr)
    need = along_rows(need)

    q = q_ref[0] * (HEAD_DIM ** -0.5 * LOG2E)
    qs = [_pad_lanes(_stack_heads(q[:, g * hpg * HEAD_DIM:(g + 1) * hpg * HEAD_DIM], hpg, HEAD_DIM)).astype(BF16)
          for g in range(G)]
    rows = hpg * tq

    def att_step(k0, w, diagonal, carry):
        seen, flash = carry
        bias, seen = _dsa_bias(key_ref[k0 // tk, :, 0:w], thr[:, :w], need[:, :w], seen, tri_ref[...])
        out = []
        for g in range(G):
            s = _dot_nt(qs[g], kv_ref[0, g, pl.ds(k0, w), :])
            out.append(_flash_tile(s, bias, kv_ref[0, G + g, pl.ds(k0, w), :], flash[g], hpg, tq, True))
        return seen, tuple(out)

    init = (jnp.zeros((tq, 1), F32), tuple(_flash_init(rows, LANE) for _ in range(G)))
    _, res = _causal_sweep(att_step, init, q0, tq, tk)
    outs = []
    for g in range(G):
        o = _flash_out(res[g][2])
        outs.extend(o[h * tq:(h + 1) * tq] for h in range(hpg))
    o_ref[0] = jnp.concatenate(outs, axis=-1)


def _dsa_attention(q, qi, kw, kidx, kv, *, tq, q_pos0, lk):
    bk, lq, _ = q.shape
    lk_pad = kv.shape[2]
    tk = _kv_tile(lk_pad, DSA_KV_TILE)
    n_sel = min(DSA_TOPK_MAX, lk // 4)
    assert q_pos0 + lq <= lk_pad and tk >= n_sel and tq % LANE == 0
    kern = functools.partial(_dsa_kernel, tq=tq, tk=tk, q_pos0=q_pos0, n_sel=n_sel)
    tri = jnp.triu(jnp.ones((TRI_TILE, TRI_TILE), BF16))
    return pl.pallas_call(
        kern,
        grid=(bk, lq // tq),
        in_specs=[pl.BlockSpec((1, tq, DSA_HEADS * HEAD_DIM), lambda b, i: (b, i, 0)),
                  pl.BlockSpec((1, tq, IDX_HEADS * IDX_DIM), lambda b, i: (b, i, 0)),
                  pl.BlockSpec((1, tq, LANE), lambda b, i: (b, i, 0)),
                  pl.BlockSpec((1, 1, lk_pad, LANE), lambda b, i: (b, 0, 0, 0)),
                  pl.BlockSpec((1, 4, lk_pad, LANE), lambda b, i: (b, 0, 0, 0)),
                  pl.BlockSpec((TRI_TILE, TRI_TILE), lambda b, i: (0, 0))],
        out_specs=pl.BlockSpec((1, tq, DSA_HEADS * HEAD_DIM), lambda b, i: (b, i, 0)),
        out_shape=jax.ShapeDtypeStruct((bk, lq, DSA_HEADS * HEAD_DIM), F32),
        scratch_shapes=[pltpu.VMEM((lk_pad // tk, tq, tk), I32), pltpu.VMEM((lk_pad // tk, tk, tq), I32)],
        compiler_params=_cparams(("parallel", "arbitrary")),
        name="dsa_attention",
    )(q, qi, kw, kidx, kv, tri)


def _dsa_decode_kernel(pt_ref, q_ref, qi_ref, kw_ref, newi_ref, newkv_ref, tri_ref, *refs,
                       pp, n_steps, tq, n_new, n_sel):
    ipage_refs = refs[:pp]
    kvpage_refs = refs[pp:2 * pp]
    o_ref, key_ref, thr_ref, need_ref, seen_ref, m_ref, l_ref, acc_ref = refs[2 * pp:]
    G = DSA_KV_HEADS
    hpg = DSA_HEADS // G
    gw = hpg * HEAD_DIM
    j = pl.program_id(1)
    keys = pp * PAGE_SIZE
    qidx = _stack_heads(qi_ref[0] * (IDX_DIM ** -0.5), IDX_HEADS, IDX_DIM).astype(BF16)
    wi = kw_ref[0][:, IDX_DIM:IDX_DIM + IDX_HEADS] * (IDX_HEADS ** -0.5)
    q_all = q_ref[0] * (HEAD_DIM ** -0.5 * LOG2E)
    row = lax.broadcasted_iota(I32, (tq, 1), 0)
    col = lax.broadcasted_iota(I32, (1, LANE), 1)
    own = (col <= row) & (col < n_new)

    def queries(g):
        return _stack_heads(q_all[:, g * gw:(g + 1) * gw], hpg, HEAD_DIM).astype(BF16)

    @pl.when(j < n_steps)
    def _():
        k_t = jnp.concatenate([r[0, 0] for r in ipage_refs], axis=-1).astype(BF16)
        key_ref[j] = _dsa_keys(_dot(qidx, k_t), wi, None, tq)

    @pl.when(j == n_steps - 1)
    def _():
        pad = jnp.zeros((LANE - tq, IDX_DIM), F32)
        k_new = jnp.concatenate([newi_ref[0], pad], axis=0).astype(BF16)
        key_new = _dsa_keys(_dot_nt(qidx, k_new), wi, own, tq)
        key_ref[n_steps] = jnp.concatenate([key_new, jnp.full((tq, keys - LANE), INT_MIN, I32)], axis=-1)
        kmax = jnp.full((tq, 1), INT_MIN, I32)
        for t in range(n_steps + 1):
            kmax = jnp.maximum(kmax, jnp.max(key_ref[t], axis=-1, keepdims=True))
        thr, need = _dsa_threshold(lambda t: key_ref[t], n_steps + 1, kmax, tq=tq, tk=keys, n_sel=n_sel,
                                   transposed=False, unroll=True)
        thr_ref[...] = thr
        need_ref[...] = need
        seen_ref[...] = jnp.zeros(seen_ref.shape, F32)
        m_ref[...] = jnp.full(m_ref.shape, NEG, F32)
        l_ref[...] = jnp.zeros(l_ref.shape, F32)
        acc_ref[...] = jnp.zeros(acc_ref.shape, F32)

    def update(g, s, bias, v, v_t):
        carry = (m_ref[g], l_ref[g], acc_ref[g])
        m, l, acc = _flash_tile(s, bias, v, carry, hpg, tq, False, v_t=v_t)
        m_ref[g] = m
        l_ref[g] = l
        acc_ref[g] = acc

    @pl.when(j >= n_steps)
    def _():
        bias, seen = _dsa_bias(key_ref[j - n_steps], thr_ref[...], need_ref[...], seen_ref[...], tri_ref[...])
        seen_ref[...] = seen
        for g in range(G):
            k_t = jnp.concatenate([r[0, g] for r in kvpage_refs], axis=-1).astype(BF16)
            v_t = jnp.concatenate([r[0, G + g] for r in kvpage_refs], axis=-1).astype(BF16)
            update(g, _dot(queries(g), k_t), bias, v_t, True)

    @pl.when(j == 2 * n_steps - 1)
    def _():
        new = newkv_ref[0]
        pad = jnp.zeros((LANE - tq, HEAD_DIM), F32)
        bias, _ = _dsa_bias(key_ref[n_steps][:, :LANE], thr_ref[...], need_ref[...], seen_ref[...],
                            tri_ref[0:LANE, 0:LANE])
        outs = []
        for g in range(G):
            k = jnp.concatenate([new[:, g * HEAD_DIM:(g + 1) * HEAD_DIM], pad], axis=0).astype(BF16)
            v = jnp.concatenate([new[:, (G + g) * HEAD_DIM:(G + g + 1) * HEAD_DIM], pad], axis=0).astype(BF16)
            update(g, _dot_nt(queries(g), k), bias, v, False)
            o = acc_ref[g] * (1.0 / jnp.maximum(l_ref[g], 1e-30))
            outs.extend(o[h * tq:(h + 1) * tq] for h in range(hpg))
        o_ref[0] = jnp.concatenate(outs, axis=-1)


def _dsa_decode(q, qi, kw, kidx_pool, kv_pool, page_table, new_kidx, new_kv, *, n_new):
    bk, tq, qw = q.shape
    n_pages = page_table.shape[1]
    pp = min(HALF_PAGES_PER_STEP, n_pages)
    assert n_pages % pp == 0 and n_new <= tq
    n_steps = n_pages // pp
    keys = pp * PAGE_SIZE
    lk = n_pages * PAGE_SIZE + n_new
    n_sel = min(DSA_TOPK_MAX, lk // 4)
    G = DSA_KV_HEADS
    rows = (DSA_HEADS // G) * tq
    tri = jnp.triu(jnp.ones((TRI_TILE, TRI_TILE), BF16))
    assert keys % TRI_TILE == 0 and keys >= n_sel

    def ipage_map(i):
        return lambda b, j, pt: (pt[b, jnp.minimum(j, n_steps - 1) * pp + i], 0, 0, 0)

    def kvpage_map(i):
        return lambda b, j, pt: (pt[b, jnp.maximum(j - n_steps, 0) * pp + i], 0, 0, 0)

    const3 = lambda b, j, pt: (b, 0, 0)
    in_specs = [pl.BlockSpec((1, tq, qw), const3),
                pl.BlockSpec((1, tq, qi.shape[-1]), const3),
                pl.BlockSpec((1, tq, LANE), const3),
                pl.BlockSpec((1, tq, IDX_DIM), const3),
                pl.BlockSpec((1, tq, new_kv.shape[-1]), const3),
                pl.BlockSpec((TRI_TILE, TRI_TILE), lambda b, j, pt: (0, 0))]
    in_specs += [pl.BlockSpec((1,) + kidx_pool.shape[1:], ipage_map(i)) for i in range(pp)]
    in_specs += [pl.BlockSpec((1,) + kv_pool.shape[1:], kvpage_map(i)) for i in range(pp)]
    kern = functools.partial(_dsa_decode_kernel, pp=pp, n_steps=n_steps, tq=tq, n_new=n_new, n_sel=n_sel)
    return pl.pallas_call(
        kern,
        grid_spec=pltpu.PrefetchScalarGridSpec(
            num_scalar_prefetch=1, grid=(bk, 2 * n_steps), in_specs=in_specs,
            out_specs=pl.BlockSpec((1, tq, qw), const3),
            scratch_shapes=[pltpu.VMEM((n_steps + 1, tq, keys), I32),
                            pltpu.VMEM((tq, 1), I32), pltpu.VMEM((tq, 1), F32), pltpu.VMEM((tq, 1), F32),
                            pltpu.VMEM((G, rows, 1), F32), pltpu.VMEM((G, rows, 1), F32),
                            pltpu.VMEM((G, rows, HEAD_DIM), F32)]),
        out_shape=jax.ShapeDtypeStruct((bk, tq, qw), F32),
        compiler_params=_cparams(("parallel", "arbitrary")),
        name="dsa_decode",
    )(page_table, q, qi, kw, new_kidx, new_kv, tri, *([kidx_pool] * pp), *([kv_pool] * pp))


L0_SIZES = (512, 768, 24, 512, 512, 512)
L1_SIZES = (512, 256, 256, 512, 512, 128, 128, 256, 64, 4, 512)


def _l0_weight(w_in):
    d = w_in.shape[0]
    q, kv6, gl, z_a, x_b, z_b = jnp.split(w_in, np.cumsum(L0_SIZES)[:-1].tolist(), axis=1)
    pad = jnp.zeros((d, LANE - 12), w_in.dtype)
    w = jnp.concatenate([q, kv6, z_a, x_b, z_b, gl[:, :12], pad, gl[:, 12:], pad], axis=1)
    segs = [(0, 512), (512, 1024), (1024, 1280), (1280, 1792), (1792, 2304), (2304, 2816), (2816, 3072)]
    return w.astype(BF16), segs


def _l1_weight(w_in):
    d = w_in.shape[0]
    qc, kc, vc, z_c, qd, kd, vd, qi, ki, wi, z_d = jnp.split(w_in, np.cumsum(L1_SIZES)[:-1].tolist(), axis=1)
    pad = jnp.zeros((d, LANE - IDX_DIM - IDX_HEADS), w_in.dtype)
    w = jnp.concatenate([qc, kc, vc, z_c, qd, kd, vd, qi, z_d, ki, wi, pad], axis=1)
    segs = [(0, 512), (512, 1024), (1024, 1536), (1536, 2048), (2048, 2304), (2304, 2560), (2560, 3072),
            (3072, 3200)]
    return w.astype(BF16), segs


def _pad_rows(x, n):
    return jnp.pad(x, ((0, 0), (0, n - x.shape[1]), (0, 0)))


def _cols_pool(pool):
    npool, ps = pool.shape[:2]
    cw = pool.shape[-1]
    perm = (0,) + tuple(range(2, pool.ndim)) + (1,)
    return jnp.transpose(pool, perm).reshape(npool, -1, cw, ps)


def _layer0(x, mod, past, w, *, tl, tq):
    (norm_g, w_in, cmp_wk, cmp_wv, conv_w, conv_b, lru_wr, lru_br, lru_wi, lru_bi, lru_lambda, w_out) = w
    shift, scale, gate = mod
    b, l, d = x.shape
    w_p, segs = _l0_weight(w_in)
    flat = shift.shape[1] != 1
    xin = x.reshape(1, b * l, d) if flat else x
    q, kvp, kvw, z_a, x_b, z_b, gates = _project(xin, norm_g, shift, scale, w_p, segs, 6, tl)
    if flat:
        q, kvp, kvw, z_a, x_b, z_b, gates = (t.reshape(b, l, -1) for t in (q, kvp, kvw, z_a, x_b, z_b, gates))
    wk2 = jnp.concatenate([cmp_wk, cmp_wk], axis=0)
    wv2 = jnp.concatenate([cmp_wv, cmp_wv], axis=0)
    w2 = jnp.stack([wk2, wk2, wv2, wv2], axis=0)
    lq = _round_up(l, 8)
    if past is None:
        ksel, cmp = _split_chunks(kvp, cw=HEAD_DIM, n_out=4, n_cmp=4, cmp_w=w2)
        kwin = _split_chunks(kvw, cw=HEAD_DIM, n_out=4)
        kv_win = kvw
        hist = jnp.zeros((b, CONV_WIDTH - 1, x_b.shape[-1]), F32)
        h0 = jnp.zeros((b, x_b.shape[-1]), F32)
        nsp = _round_up(cmp.shape[3], LANE)
        cmp = jnp.pad(cmp, ((0, 0), (0, 0), (0, 0), (0, nsp - cmp.shape[3]), (0, 0)))
        cmp = cmp.reshape(b, 2, 2, 2, nsp, HEAD_DIM).transpose(0, 2, 1, 3, 4, 5).reshape(b, 4, 2, nsp, HEAD_DIM)
        o_a = _nsa_attention(_pad_rows(q, lq), _pad_rows(gates, lq), cmp, ksel, kwin,
                             tq=min(tq, lq), q_pos0=0, lk=l, win_pos0=0)[:, :l]
    else:
        pool, table, win_buf, hist, h0 = past
        assert lq <= tq
        kv_win = jnp.concatenate([win_buf.reshape(b, win_buf.shape[1], -1), kvw], axis=1)
        lw_pad = _win_span(lq)
        assert kv_win.shape[1] <= lw_pad
        kwin = _split_chunks(_pad_rows(kv_win, lw_pad), cw=HEAD_DIM, n_out=4)
        o_a = _nsa_decode(_pad_rows(q, lq), _pad_rows(gates, lq), _cols_pool(pool), table, _pad_rows(kvp, lq),
                          w2, kwin, n_new=l, win_pos0=table.shape[1] * PAGE_SIZE - win_buf.shape[1])[:, :l]
    o_b, h_last = _conv_rglru(_pad_rows(x_b, lq), hist, h0, conv_w, conv_b, lru_wr, lru_br, lru_wi, lru_bi,
                              lru_lambda, tl=min(256, lq), n_valid=l)
    o_b = o_b[:, :l]
    fl = (lambda t: t.reshape(1, b * l, -1)) if flat else (lambda t: t)
    x_new = _out_project(fl(o_a), fl(z_a), fl(o_b), fl(z_b), xin, gate, w_out.astype(BF16),
                         jnp.ones((d,), F32), tl=tl, final_norm=False).reshape(b, l, d)
    win_keep = min(NSA_WINDOW, kv_win.shape[1])
    conv_src = jnp.concatenate([hist, x_b], axis=1) if l < CONV_WIDTH - 1 else x_b
    states = (kvp.reshape(b, l, 4, NSA_KV_HEADS, HEAD_DIM),
              kv_win[:, -win_keep:].reshape(b, win_keep, 2, NSA_KV_HEADS, HEAD_DIM),
              conv_src[:, -(CONV_WIDTH - 1):], h_last)
    return x_new, states


def _layer1(x, mod, past, w, final_g, *, tl, tq):
    (norm_g, w_in, lam_q1, lam_k1, lam_q2, lam_k2, subln_g, w_out) = w
    shift, scale, gate = mod
    b, l, d = x.shape
    w_p, segs = _l1_weight(w_in)
    flat = shift.shape[1] != 1
    xin = x.reshape(1, b * l, d) if flat else x
    qc, kvc, z_c, qd, kvd, qi, z_d, kiw = _project(xin, norm_g, shift, scale, w_p, segs, -1, tl)
    if flat:
        qc, kvc, z_c, qd, kvd, qi, z_d, kiw = (t.reshape(b, l, -1) for t in (qc, kvc, z_c, qd, kvd, qi, z_d, kiw))
    lq = _round_up(l, 8)
    lamv = jnp.stack([lam_q1, lam_k1, lam_q2, lam_k2], axis=0)
    if past is None:
        diff_kv = _split_chunks(kvc, cw=2 * DIFF_HALF, n_out=4)
        dsa_kv = _split_chunks(kvd, cw=HEAD_DIM, n_out=4)
        kidx = _split_chunks(kiw, cw=IDX_DIM, n_out=1)
        o_c = _diff_attention(_pad_rows(qc, lq), diff_kv, lamv, subln_g, tq=min(tq, lq), q_pos0=0)[:, :l]
        o_d = _dsa_attention(_pad_rows(qd, lq), _pad_rows(qi, lq), _pad_rows(kiw, lq), kidx, dsa_kv,
                             tq=min(tq, lq), q_pos0=0, lk=l)[:, :l]
    else:
        diff_pool, dsa_pool, kidx_pool, table = past
        assert lq <= tq
        diff_rows = diff_pool.reshape(diff_pool.shape[0], PAGE_SIZE * 4, 2 * DIFF_HALF)
        o_c = _diff_decode(_pad_rows(qc, lq), diff_rows, table, _pad_rows(kvc, lq), lamv, subln_g, n_new=l)[:, :l]
        o_d = _dsa_decode(_pad_rows(qd, lq), _pad_rows(qi, lq), _pad_rows(kiw, lq), _cols_pool(kidx_pool),
                          _cols_pool(dsa_pool), table, _pad_rows(kiw[:, :, :IDX_DIM], lq), _pad_rows(kvd, lq),
                          n_new=l)[:, :l]
    fl = (lambda t: t.reshape(1, b * l, -1)) if flat else (lambda t: t)
    y = _out_project(fl(o_c), fl(z_c), fl(o_d), fl(z_d), xin, gate, w_out.astype(BF16), final_g,
                     tl=tl, final_norm=True).reshape(b, l, d)
    states = (kvc.reshape(b, l, 2, DIFF_KV_HEADS, 2 * DIFF_HALF),
              kvd.reshape(b, l, 2, DSA_KV_HEADS, HEAD_DIM), kiw[:, :, :IDX_DIM])
    return y, states


def kernel(x_prompt, x_sample, cache_l0_nsa_kv, state_l0_win_kv, state_l0_conv, state_l0_lru_h,
           cache_l1_diff_kv, cache_l1_dsa_kv, cache_l1_dsa_kidx, page_table, c_prompt, c_sample,
           l0_norm_g, l0_ada_w, l0_ada_b, l0_w_in, l0_cmp_wk, l0_cmp_wv, l0_conv_w, l0_conv_b,
           l0_lru_wr, l0_lru_br, l0_lru_wi, l0_lru_bi, l0_lru_lambda, l0_w_out,
           l1_norm_g, l1_ada_w, l1_ada_b, l1_w_in, l1_lam_q1, l1_lam_k1, l1_lam_q2, l1_lam_k2,
           l1_subln_g, l1_w_out, final_norm_g):
    bp, lp, d = x_prompt.shape
    bs, ls, _ = x_sample.shape
    c_all = jnp.concatenate([c_prompt, c_sample], axis=0)

    def mods(ada_w, ada_b):
        m = _modulation(c_all, ada_w, ada_b)
        mp = tuple(t[:, None] for t in jnp.split(m[:bp], 3, axis=-1))
        ms = tuple(jnp.repeat(t, ls, axis=0)[None] for t in jnp.split(m[bp:], 3, axis=-1))
        return mp, ms

    tl_p = min(512, lp)
    tl_s = bs * ls
    tq = min(Q_TILE, lp)
    w0 = (l0_norm_g, l0_w_in, l0_cmp_wk, l0_cmp_wv, l0_conv_w, l0_conv_b, l0_lru_wr, l0_lru_br,
          l0_lru_wi, l0_lru_bi, l0_lru_lambda, l0_w_out)
    mp0, ms0 = mods(l0_ada_w, l0_ada_b)
    xp, (nsa_kv_p, win_p, conv_p, h_p) = _layer0(x_prompt, mp0, None, w0, tl=tl_p, tq=tq)
    xs, (nsa_kv_s, win_s, conv_s, h_s) = _layer0(
        x_sample, ms0, (cache_l0_nsa_kv, page_table, state_l0_win_kv, state_l0_conv, state_l0_lru_h), w0,
        tl=tl_s, tq=tq)
    w1 = (l1_norm_g, l1_w_in, l1_lam_q1, l1_lam_k1, l1_lam_q2, l1_lam_k2, l1_subln_g, l1_w_out)
    mp1, ms1 = mods(l1_ada_w, l1_ada_b)
    y_p, (diff_kv_p, dsa_kv_p, kidx_p) = _layer1(xp, mp1, None, w1, final_norm_g, tl=tl_p, tq=tq)
    y_s, (diff_kv_s, dsa_kv_s, kidx_s) = _layer1(
        xs, ms1, (cache_l1_diff_kv, cache_l1_dsa_kv, cache_l1_dsa_kidx, page_table), w1, final_norm_g,
        tl=tl_s, tq=tq)
    return (y_p, y_s, nsa_kv_p, nsa_kv_s, win_p, win_s, conv_p, conv_s, h_p, h_s,
            diff_kv_p, diff_kv_s, dsa_kv_p, dsa_kv_s, kidx_p, kidx_s)
```
